```python
import jax, jax.numpy as jnp
from jax import lax
import numpy as np

D_MODEL = 2048
BATCH = 8
SEQ = 2048
DEPTH = 2

RMS_EPS = 1e-6

DN_HEADS = 8
DN_HEAD_DIM = 128
DN_WIDTH = DN_HEADS * DN_HEAD_DIM
DN_CONV = 4
DN_CHUNK = 64

SC_GROUPS = 8
SC_GROUP_DIM = 128
SC_WIDTH = SC_GROUPS * SC_GROUP_DIM
SC_CONV = 3

SWA_GROUPS = ((128, 1), (512, 4), (2048, 16))
SWA_N_GROUPS = 3
SWA_HEADS_PER_GROUP = 4
SWA_HEADS = SWA_N_GROUPS * SWA_HEADS_PER_GROUP
SWA_HEAD_DIM = 128
SWA_WIDTH = SWA_HEADS * SWA_HEAD_DIM
SWA_OUT_WIDTH = SWA_HEADS_PER_GROUP * SWA_HEAD_DIM
SWA_BLOCK = 128
ALIBI_MAX_BIAS = 8.0

N_BRANCHES = 3

IN_SIZES = (3 * DN_WIDTH, DN_WIDTH, DN_HEADS, DN_HEADS,
            SC_WIDTH, SC_WIDTH, SC_WIDTH,
            3 * SWA_WIDTH, N_BRANCHES * D_MODEL)
IN_COLS = sum(IN_SIZES)

MOE_GROUPS = 4
MOE_EXPERTS_PER_GROUP = 8
MOE_EXPERTS = MOE_GROUPS * MOE_EXPERTS_PER_GROUP
MOE_TOPK = 2
MOE_FF = 512

kernel_name = "hybrid_gdn_shortconv_dilswa_hmoe"


def rms_norm(x, g, eps=RMS_EPS):
    xf = x.astype(jnp.float32)
    y = xf * lax.rsqrt(jnp.mean(xf * xf, axis=-1, keepdims=True) + eps)
    return (y * g.astype(jnp.float32)).astype(x.dtype)


def l2norm(x, eps=1e-6):
    return x * lax.rsqrt(jnp.sum(x * x, axis=-1, keepdims=True) + eps)


def causal_dwconv(x, w):
    K = w.shape[0]
    return lax.conv_general_dilated(
        x, w[:, None, :].astype(x.dtype), window_strides=(1,), padding=[(K - 1, 0)],
        dimension_numbers=('NWC', 'WIO', 'NWC'), feature_group_count=x.shape[-1])


def alibi_slopes():
    return jnp.asarray(2.0 ** (-ALIBI_MAX_BIAS * np.arange(1, SWA_HEADS + 1) / SWA_HEADS), dtype=jnp.float32)


def gated_deltanet(q, k, v, a, b, z, a_log, dt_bias, norm_g):
    f32 = jnp.float32
    Bn, S, _ = q.shape
    H, Dh, C = DN_HEADS, DN_HEAD_DIM, DN_CHUNK
    N = S // C

    def heads(t):
        return t.astype(f32).reshape(Bn, N, C, H, Dh).transpose(0, 3, 1, 2, 4)

    def per_head(t):
        return t.reshape(Bn, N, C, H).transpose(0, 3, 1, 2)

    q = l2norm(heads(q)) * (Dh ** -0.5)
    k = l2norm(heads(k))
    v = heads(v)
    beta = per_head(jax.nn.sigmoid(b.astype(f32)))
    g = -jnp.exp(a_log.astype(f32))[:, None, None] * per_head(
        jax.nn.softplus(a.astype(f32) + dt_bias.astype(f32)))
    g = jnp.cumsum(g, axis=-1)

    causal = jnp.tril(jnp.ones((C, C), dtype=bool))
    strict = jnp.tril(jnp.ones((C, C), dtype=bool), -1)
    decay = jnp.exp(jnp.where(causal, g[..., :, None] - g[..., None, :], -jnp.inf))

    kb = k * beta[..., None]
    L = jnp.where(strict, jnp.einsum('bhnid,bhnjd->bhnij', kb, k) * decay, 0.0)
    rhs = jnp.concatenate([v * beta[..., None], kb * jnp.exp(g)[..., None]], axis=-1)
    uw = lax.linalg.triangular_solve(L + jnp.eye(C, dtype=f32), rhs, left_side=True,
                                     lower=True, unit_diagonal=True)
    u, w = uw[..., :Dh], uw[..., Dh:]
    attn_intra = jnp.where(causal, jnp.einsum('bhnid,bhnjd->bhnij', q, k) * decay, 0.0)
    q_dec = q * jnp.exp(g)[..., None]
    g_last = g[..., -1]
    k_dec = k * jnp.exp(g_last[..., None] - g)[..., None]

    def step(state, xs):
        q_c, k_c, u_c, w_c, a_c, gl = xs
        v_new = u_c - jnp.einsum('bhcd,bhde->bhce', w_c, state)
        o = jnp.einsum('bhcd,bhde->bhce', q_c, state) + jnp.einsum('bhij,bhje->bhie', a_c, v_new)
        state = state * jnp.exp(gl)[..., None, None] + jnp.einsum('bhcd,bhce->bhde', k_c, v_new)
        return state, o

    xs = tuple(jnp.moveaxis(t, 2, 0) for t in (q_dec, k_dec, u, w, attn_intra, g_last))
    _, o = lax.scan(step, jnp.zeros((Bn, H, Dh, Dh), f32), xs)
    o = o.transpose(1, 0, 3, 2, 4).reshape(Bn, S, H, Dh)
    o = o * lax.rsqrt(jnp.mean(o * o, axis=-1, keepdims=True) + RMS_EPS) * norm_g.astype(f32)
    o = o * jax.nn.silu(z.astype(f32).reshape(Bn, S, H, Dh))
    return o.reshape(Bn, S, DN_WIDTH).astype(z.dtype)


def short_conv_mixer(c_gate, b_gate, x_in, conv_w):
    return b_gate * causal_dwconv(c_gate * x_in, conv_w)


def strided_window_attn(q, k, v, dil, steps, slopes):
    f32 = jnp.float32
    Bn, S, H, Dh = q.shape
    Bk = SWA_BLOCK
    L = S // dil
    nb = -(-L // Bk)
    Lp = nb * Bk

    def sub(t):
        t = t.astype(f32).reshape(Bn, L, dil, H, Dh).transpose(0, 2, 1, 3, 4)
        t = jnp.pad(t, ((0, 0), (0, 0), (0, Lp - L), (0, 0), (0, 0)))
        return t.reshape(Bn, dil, nb, Bk, H, Dh)

    def with_prev(t):
        prev = jnp.pad(t, ((0, 0), (0, 0), (1, 0), (0, 0), (0, 0), (0, 0)))[:, :, :-1]
        return jnp.concatenate([prev, t], axis=3)

    qs, kk, vv = sub(q), with_prev(sub(k)), with_prev(sub(v))
    scores = jnp.einsum('brnqhd,brnkhd->brnhqk', qs, kk) * (Dh ** -0.5)
    qi = jnp.arange(Bk)[:, None]
    kj = jnp.arange(2 * Bk)[None, :]
    nstep = Bk + qi - kj
    blk = jnp.arange(nb)[:, None, None]
    valid = (nstep >= 0) & (nstep <= steps) & (blk * Bk + kj - Bk >= 0)
    bias = -slopes[:, None, None] * (nstep * dil).astype(f32)
    scores = jnp.where(valid[None, None, :, None], scores + bias[None, None, None], -jnp.inf)
    lse = jax.nn.logsumexp(scores, axis=-1)
    p = jnp.exp(scores - lse[..., None])
    o = jnp.einsum('brnhqk,brnkhd->brnqhd', p, vv)
    o = o.reshape(Bn, dil, Lp, H, Dh)[:, :, :L].transpose(0, 2, 1, 3, 4).reshape(Bn, S, H, Dh)
    lse = lse.transpose(0, 1, 2, 4, 3).reshape(Bn, dil, Lp, H)[:, :, :L]
    lse = lse.transpose(0, 2, 1, 3).reshape(Bn, S, H)
    return o, lse


def dilated_swa(q, k, v, slopes):
    Bn, S, _ = q.shape
    shp = (Bn, S, SWA_N_GROUPS, SWA_HEADS_PER_GROUP, SWA_HEAD_DIM)
    q, k, v = q.reshape(shp), k.reshape(shp), v.reshape(shp)
    outs, lses = [], []
    for gi, (window, dil) in enumerate(SWA_GROUPS):
        o, lse = strided_window_attn(q[:, :, gi], k[:, :, gi], v[:, :, gi], dil, window // dil, slopes[gi])
        outs.append(o)
        lses.append(lse)
    wts = jax.nn.softmax(jnp.stack(lses, axis=0), axis=0)
    o = jnp.einsum('gbsh,gbshd->bshd', wts, jnp.stack(outs, axis=0))
    return o.reshape(Bn, S, SWA_OUT_WIDTH).astype(q.dtype)


def hier_moe(h, wg, bg, we, be, w_gate, w_up, w_down):
    f32 = jnp.float32
    Bn, S, D = h.shape
    x = h.reshape(-1, D)
    n_tok = x.shape[0]
    G, E = MOE_GROUPS, MOE_EXPERTS_PER_GROUP
    g_logits = (x @ wg + bg).astype(f32)
    grp = jnp.argmax(g_logits, axis=-1)
    g_w = jnp.take_along_axis(jax.nn.softmax(g_logits, axis=-1), grp[:, None], axis=-1)
    e_logits = (x @ we + be).astype(f32).reshape(n_tok, G, E)
    e_sel = jnp.take_along_axis(e_logits, grp[:, None, None], axis=1)[:, 0]
    top_v, top_i = lax.top_k(e_sel, MOE_TOPK)
    top_w = jax.nn.softmax(top_v, axis=-1) * g_w
    comb = jnp.einsum('nk,nke->ne', top_w, jax.nn.one_hot(top_i, E, dtype=f32))
    comb = (comb[:, None, :] * jax.nn.one_hot(grp, G, dtype=f32)[:, :, None]).reshape(n_tok, G * E)
    y = jnp.zeros((n_tok, D), f32)
    for gi in range(G):
        sl = slice(gi * E, (gi + 1) * E)
        hid = jax.nn.silu(jnp.einsum('nd,edf->enf', x, w_gate[sl])) * jnp.einsum('nd,edf->enf', x, w_up[sl])
        hid = hid * comb[:, sl].T[:, :, None]
        y = y + jnp.einsum('enf,efd->nd', hid, w_down[sl])
    return y.reshape(Bn, S, D).astype(h.dtype)


def setup_inputs(seed: int = 0) -> dict:
    key = jax.random.key(seed)
    ks = jax.random.split(key, 24)
    f32 = jnp.float32
    nrm = lambda k, shape, scale: jax.random.normal(k, shape, f32) * scale
    dt = jnp.exp(jax.random.uniform(ks[6], (DEPTH, DN_HEADS), f32, np.log(1e-3), np.log(1e-1)))
    return {
        "x": nrm(ks[0], (BATCH, SEQ, D_MODEL), 1.0),
        "norm_mix_g": 1.0 + nrm(ks[1], (DEPTH, D_MODEL), 0.02),
        "w_in": nrm(ks[2], (DEPTH, D_MODEL, IN_COLS), D_MODEL ** -0.5),
        "dn_conv_w": nrm(ks[3], (DEPTH, DN_CONV, 3 * DN_WIDTH), DN_CONV ** -0.5),
        "dn_a_log": jnp.log(jax.random.uniform(ks[4], (DEPTH, DN_HEADS), f32, 1.0, 16.0)),
        "dn_dt_bias": dt + jnp.log(-jnp.expm1(-dt)),
        "dn_norm_g": 1.0 + nrm(ks[5], (DEPTH, DN_HEAD_DIM), 0.02),
        "sc_conv_w": nrm(ks[7], (DEPTH, SC_CONV, SC_WIDTH), SC_CONV ** -0.5),
        "w_branch_dn": nrm(ks[8], (DEPTH, DN_WIDTH, D_MODEL), DN_WIDTH ** -0.5),
        "w_branch_sc": nrm(ks[9], (DEPTH, SC_WIDTH, D_MODEL), SC_WIDTH ** -0.5),
        "w_branch_swa": nrm(ks[10], (DEPTH, SWA_OUT_WIDTH, D_MODEL), SWA_OUT_WIDTH ** -0.5),
        "w_out": nrm(ks[11], (DEPTH, D_MODEL, D_MODEL), D_MODEL ** -0.5),
        "norm_ffn_g": 1.0 + nrm(ks[12], (DEPTH, D_MODEL), 0.02),
        "router_group_w": nrm(ks[13], (DEPTH, D_MODEL, MOE_GROUPS), D_MODEL ** -0.5),
        "router_group_b": nrm(ks[14], (DEPTH, MOE_GROUPS), 0.01),
        "router_expert_w": nrm(ks[15], (DEPTH, D_MODEL, MOE_EXPERTS), D_MODEL ** -0.5),
        "router_expert_b": nrm(ks[16], (DEPTH, MOE_EXPERTS), 0.01),
        "expert_w_gate": nrm(ks[17], (DEPTH, MOE_EXPERTS, D_MODEL, MOE_FF), D_MODEL ** -0.5),
        "expert_w_up": nrm(ks[18], (DEPTH, MOE_EXPERTS, D_MODEL, MOE_FF), D_MODEL ** -0.5),
        "expert_w_down": nrm(ks[19], (DEPTH, MOE_EXPERTS, MOE_FF, D_MODEL), MOE_FF ** -0.5),
        "final_norm_g": 1.0 + nrm(ks[20], (D_MODEL,), 0.02),
    }


def reference(x, norm_mix_g, w_in, dn_conv_w, dn_a_log, dn_dt_bias, dn_norm_g, sc_conv_w,
              w_branch_dn, w_branch_sc, w_branch_swa, w_out, norm_ffn_g,
              router_group_w, router_group_b, router_expert_w, router_expert_b,
              expert_w_gate, expert_w_up, expert_w_down, final_norm_g):
    slopes = alibi_slopes().reshape(SWA_N_GROUPS, SWA_HEADS_PER_GROUP)
    offsets = [int(o) for o in np.cumsum(IN_SIZES)[:-1]]
    for l in range(DEPTH):
        h = rms_norm(x, norm_mix_g[l])
        proj = h @ w_in[l]
        (dn_qkv, dn_z, dn_a, dn_b, sc_c, sc_b, sc_x, swa_qkv, gate_logits) = jnp.split(proj, offsets, axis=-1)
        dn_qkv = jax.nn.silu(causal_dwconv(dn_qkv, dn_conv_w[l]))
        dq, dk, dv = jnp.split(dn_qkv, 3, axis=-1)
        y_dn = gated_deltanet(dq, dk, dv, dn_a, dn_b, dn_z, dn_a_log[l], dn_dt_bias[l], dn_norm_g[l])
        y_sc = short_conv_mixer(sc_c, sc_b, sc_x, sc_conv_w[l])
        sq, sk, sv = jnp.split(swa_qkv, 3, axis=-1)
        y_swa = dilated_swa(sq, sk, sv, slopes)
        g_dn, g_sc, g_swa = jnp.split(jax.nn.sigmoid(gate_logits), 3, axis=-1)
        merged = (g_dn * (y_dn @ w_branch_dn[l]) + g_sc * (y_sc @ w_branch_sc[l])
                  + g_swa * (y_swa @ w_branch_swa[l]))
        x = x + merged @ w_out[l]
        x = x + hier_moe(rms_norm(x, norm_ffn_g[l]), router_group_w[l], router_group_b[l],
                         router_expert_w[l], router_expert_b[l],
                         expert_w_gate[l], expert_w_up[l], expert_w_down[l])
    return rms_norm(x, final_norm_g)
```

```python
import functools

import jax
import jax.numpy as jnp
import numpy as np
from jax import lax
from jax.experimental import pallas as pl
from jax.experimental.pallas import tpu as pltpu

F32 = jnp.float32
BF16 = jnp.bfloat16

D_MODEL = 2048
RMS_EPS = 1e-6
L2_EPS = 1e-6

DN_HEADS = 8
HEAD_DIM = 128
DN_WIDTH = DN_HEADS * HEAD_DIM
DN_CONV = 4
SC_WIDTH = 1024
SC_CONV = 3
SWA_GROUPS = ((128, 1), (512, 4), (2048, 16))
SWA_HEADS_PER_GROUP = 4
SWA_HEADS = 12
SWA_GROUP_WIDTH = SWA_HEADS_PER_GROUP * HEAD_DIM
SWA_BLOCK = 128
ALIBI_MAX_BIAS = 8.0
MOE_GROUPS = 4
MOE_EPG = 8
MOE_EXPERTS = 32
MOE_FF = 512

OFF_GATE = 0
OFF_DNQ = 6144
OFF_DNK = OFF_DNQ + DN_WIDTH
OFF_DNV = OFF_DNK + DN_WIDTH
OFF_DNZ = OFF_DNV + DN_WIDTH
OFF_SCC = OFF_DNZ + DN_WIDTH
OFF_SCB = OFF_SCC + SC_WIDTH
OFF_SCX = OFF_SCB + SC_WIDTH
OFF_SWA = OFF_SCX + SC_WIDTH
OFF_AB = OFF_SWA + 3 * SWA_HEADS * HEAD_DIM
PROJ_COLS = 18432

LANES = 128
SWA_OUT_COLS = SWA_GROUP_WIDTH + LANES
CHUNK = 128
HALO = 16

VMEM_LIMIT = 56 * 1024 * 1024


def _cparams(sem):
    return pltpu.CompilerParams(dimension_semantics=sem, vmem_limit_bytes=VMEM_LIMIT)


def _sigmoid(x):
    return 1.0 / (1.0 + jnp.exp(-x))


def _silu(x):
    return x * _sigmoid(x)


def _softplus(x):
    return jnp.maximum(x, 0.0) + jnp.log1p(jnp.exp(-jnp.abs(x)))


IN_TM = 1024
IN_TN = 1024
NORM_ROWS = 64


def _rmsnorm_rows(x, g):
    ms = jnp.mean(x * x, axis=-1, keepdims=True)
    return x * lax.rsqrt(ms + RMS_EPS) * g


def _in_proj_kernel(x_ref, g_ref, w_ref, o_ref, h_ref):
    @pl.when(pl.program_id(1) == 0)
    def _():
        def body(r, c):
            rows = pl.ds(pl.multiple_of(r * NORM_ROWS, NORM_ROWS), NORM_ROWS)
            h_ref[rows, :] = _rmsnorm_rows(x_ref[rows, :], g_ref[...]).astype(BF16)
            return c
        lax.fori_loop(0, IN_TM // NORM_ROWS, body, 0)

    o_ref[...] = jnp.dot(h_ref[...], w_ref[...], preferred_element_type=F32).astype(o_ref.dtype)


def _in_proj(x2d, g_row, w_bf16):
    m = x2d.shape[0]
    return pl.pallas_call(
        _in_proj_kernel,
        grid=(m // IN_TM, PROJ_COLS // IN_TN),
        in_specs=[
            pl.BlockSpec((IN_TM, D_MODEL), lambda i, j: (i, 0)),
            pl.BlockSpec((1, D_MODEL), lambda i, j: (0, 0)),
            pl.BlockSpec((D_MODEL, IN_TN), lambda i, j: (0, j)),
        ],
        out_specs=pl.BlockSpec((IN_TM, IN_TN), lambda i, j: (i, j)),
        out_shape=jax.ShapeDtypeStruct((m, PROJ_COLS), BF16),
        scratch_shapes=[pltpu.VMEM((IN_TM, D_MODEL), BF16)],
        compiler_params=_cparams(("parallel", "arbitrary")),
        name="in_proj",
    )(x2d, g_row, w_bf16)


DN_HB = 4
DN_TS = 512
DN_W = DN_HB * HEAD_DIM
DN_DOUBLINGS = CHUNK.bit_length() - 2


def _dot_nt(a, b):
    return lax.dot_general(a, b, (((1,), (1,)), ((), ())), preferred_element_type=F32)


def _dot_tn(a, b):
    return lax.dot_general(a, b, (((0,), (0,)), ((), ())), preferred_element_type=F32)


def _causal_conv(xx, w, width):
    acc = None
    for j in range(width):
        shift = width - 1 - j
        xs = xx if shift == 0 else pltpu.roll(xx, shift, axis=0)
        term = xs[HALO:, :] * w[j:j + 1, :]
        acc = term if acc is None else acc + term
    return acc


def _deltanet_kernel(q_ref, k_ref, v_ref, z_ref, ab_ref, cwq_ref, cwk_ref, cwv_ref,
                     alog_ref, dtb_ref, ng_ref, o_ref,
                     xpad_ref, state_ref):
    hg = pl.program_id(1)
    st = pl.program_id(2)

    @pl.when(st == 0)
    def _():
        xpad_ref[:, 0:HALO, :] = jnp.zeros((3, HALO, DN_W), BF16)
        state_ref[...] = jnp.zeros_like(state_ref)

    xpad_ref[0, HALO:, :] = q_ref[0]
    xpad_ref[1, HALO:, :] = k_ref[0]
    xpad_ref[2, HALO:, :] = v_ref[0]

    row_i = lax.broadcasted_iota(jnp.int32, (CHUNK, CHUNK), 0)
    col_j = lax.broadcasted_iota(jnp.int32, (CHUNK, CHUNK), 1)
    causal = row_i >= col_j
    strict = row_i > col_j
    tri = causal.astype(F32)
    eye = (row_i == col_j).astype(F32)
    lane = lax.broadcasted_iota(jnp.int32, (CHUNK, LANES), 1)
    neg_decay_rate = -jnp.exp(alog_ref[...])
    cws = (cwq_ref[...], cwk_ref[...], cwv_ref[...])
    ng = ng_ref[...]

    def chunk_body(c, carry):
        r0 = pl.multiple_of(c * CHUNK, CHUNK)
        win = pl.ds(r0, CHUNK + HALO)
        rows = pl.ds(r0, CHUNK)
        conv = [_silu(_causal_conv(xpad_ref[t, win, :].astype(F32), cws[t], DN_CONV)) for t in range(3)]
        ab = ab_ref[0, rows, :].astype(F32)
        g_raw = neg_decay_rate * _softplus(ab + dtb_ref[...])
        g_cum = jnp.dot(tri, g_raw, preferred_element_type=F32, precision=lax.Precision.HIGHEST)
        beta_all = _sigmoid(ab)
        z = z_ref[0, rows, :].astype(F32)

        for i in range(DN_HB):
            h = hg * DN_HB + i
            sl = slice(i * HEAD_DIM, (i + 1) * HEAD_DIM)
            g_col = jnp.sum(jnp.where(lane == h, g_cum, 0.0), axis=-1, keepdims=True)
            beta = jnp.sum(jnp.where(lane == h + DN_HEADS, beta_all, 0.0), axis=-1, keepdims=True)
            g_last = g_col[CHUNK - 1:CHUNK, :]
            eg = jnp.exp(g_col)
            ek = jnp.exp(g_last - g_col)
            g_b = jnp.broadcast_to(g_col, (CHUNK, CHUNK))
            decay = jnp.exp(jnp.where(causal, g_b - g_b.T, -jnp.inf))

            qf, kf, vf = conv[0][:, sl], conv[1][:, sl], conv[2][:, sl]
            q = qf * lax.rsqrt(jnp.sum(qf * qf, axis=-1, keepdims=True) + L2_EPS) * (HEAD_DIM ** -0.5)
            k = kf * lax.rsqrt(jnp.sum(kf * kf, axis=-1, keepdims=True) + L2_EPS)
            kb = k * beta
            k16 = k.astype(BF16)
            a2 = _dot_nt(jnp.concatenate([q, kb], axis=0).astype(BF16), k16)
            attn = a2[:CHUNK] * decay
            n_mat = jnp.where(strict, -(a2[CHUNK:] * decay), 0.0)

            u_mat = eye + n_mat
            n16 = n_mat.astype(BF16)
            p_mat = jnp.dot(n16, n16, preferred_element_type=F32)
            for _ in range(DN_DOUBLINGS - 1):
                p16 = p_mat.astype(BF16)
                up = jnp.dot(jnp.concatenate([u_mat.astype(BF16), p16], axis=0), p16,
                             preferred_element_type=F32)
                u_mat = u_mat + up[:CHUNK]
                p_mat = up[CHUNK:]
            u_mat = u_mat + jnp.dot(u_mat.astype(BF16), p_mat.astype(BF16), preferred_element_type=F32)
            rhs = jnp.concatenate([vf * beta, kb * eg], axis=1).astype(BF16)
            uw = jnp.dot(u_mat.astype(BF16), rhs, preferred_element_type=F32)
            u, w = uw[:, :HEAD_DIM], uw[:, HEAD_DIM:]

            s_old = state_ref[i]
            qw = jnp.dot(jnp.concatenate([q * eg, w], axis=0).astype(BF16), s_old.astype(BF16),
                         preferred_element_type=F32)
            v_new = u - qw[CHUNK:]
            v16 = v_new.astype(BF16)
            o = qw[:CHUNK] + jnp.dot(attn.astype(BF16), v16, preferred_element_type=F32)
            state_ref[i] = s_old * jnp.exp(g_last) + _dot_tn((k * ek).astype(BF16), v16)

            o = o * lax.rsqrt(jnp.mean(o * o, axis=-1, keepdims=True) + RMS_EPS) * ng
            o_ref[0, rows, sl] = (o * _silu(z[:, sl])).astype(o_ref.dtype)
        return carry

    lax.fori_loop(0, DN_TS // CHUNK, chunk_body, 0)

    tail = pl.ds(DN_TS, HALO)
    head = pl.ds(0, HALO)
    for t in range(3):
        xpad_ref[t, head, :] = xpad_ref[t, tail, :]


def _deltanet(proj3, conv_w, alog_row, dtb_row, ng_row):
    b, s, _ = proj3.shape
    qb, kb_, vb, zb = (off // DN_W for off in (OFF_DNQ, OFF_DNK, OFF_DNV, OFF_DNZ))
    act = lambda base: pl.BlockSpec((1, DN_TS, DN_W), lambda bi, hg, st: (bi, st, base + hg))
    cw = lambda base: pl.BlockSpec((DN_CONV, DN_W), lambda bi, hg, st: (0, base + hg))
    row = pl.BlockSpec((1, LANES), lambda bi, hg, st: (0, 0))
    return pl.pallas_call(
        _deltanet_kernel,
        grid=(b, DN_HEADS // DN_HB, s // DN_TS),
        in_specs=[
            act(qb), act(kb_), act(vb), act(zb),
            pl.BlockSpec((1, DN_TS, LANES), lambda bi, hg, st: (bi, st, OFF_AB // LANES)),
            cw(0), cw(DN_WIDTH // DN_W), cw(2 * DN_WIDTH // DN_W),
            row, row, row,
        ],
        out_specs=pl.BlockSpec((1, DN_TS, DN_W), lambda bi, hg, st: (bi, st, hg)),
        out_shape=jax.ShapeDtypeStruct((b, s, DN_WIDTH), BF16),
        scratch_shapes=[
            pltpu.VMEM((3, DN_TS + HALO, DN_W), BF16),
            pltpu.VMEM((DN_HB, HEAD_DIM, HEAD_DIM), F32),
        ],
        compiler_params=_cparams(("parallel", "parallel", "arbitrary")),
        name="deltanet",
    )(proj3, proj3, proj3, proj3, proj3, conv_w, conv_w, conv_w, alog_row, dtb_row, ng_row)


def _swa_kernel(q_ref, k_ref, v_ref, o_ref, *, n_blocks, dil, slopes):
    qi = lax.broadcasted_iota(jnp.int32, (SWA_BLOCK, SWA_BLOCK), 0)
    kj = lax.broadcasted_iota(jnp.int32, (SWA_BLOCK, SWA_BLOCK), 1)
    step_cur = (qi - kj).astype(F32)
    step_prev = step_cur + float(SWA_BLOCK)
    ok_cur = qi >= kj
    ok_prev = kj >= qi
    lane = lax.broadcasted_iota(jnp.int32, (SWA_BLOCK, LANES), 1)
    scale = HEAD_DIM ** -0.5

    def block_body(n, carry):
        rows = pl.ds(pl.multiple_of(n * SWA_BLOCK, SWA_BLOCK), SWA_BLOCK)
        prev_rows = pl.ds(pl.multiple_of(jnp.maximum(n - 1, 0) * SWA_BLOCK, SWA_BLOCK), SWA_BLOCK)
        qn = q_ref[0, rows, :]
        kc, vc = k_ref[0, rows, :], v_ref[0, rows, :]
        kp, vp = k_ref[0, prev_rows, :], v_ref[0, prev_rows, :]
        has_prev = n > 0
        lse_tile = jnp.zeros((SWA_BLOCK, LANES), F32)
        for hh in range(SWA_HEADS_PER_GROUP):
            sl = slice(hh * HEAD_DIM, (hh + 1) * HEAD_DIM)
            m_h = float(slopes[hh]) * float(dil)
            s_cur = jnp.where(ok_cur, _dot_nt(qn[:, sl], kc[:, sl]) * scale - m_h * step_cur, -jnp.inf)
            s_prev = jnp.where(jnp.logical_and(ok_prev, has_prev),
                               _dot_nt(qn[:, sl], kp[:, sl]) * scale - m_h * step_prev, -jnp.inf)
            mx = jnp.maximum(jnp.max(s_cur, axis=-1, keepdims=True), jnp.max(s_prev, axis=-1, keepdims=True))
            e_cur = jnp.exp(s_cur - mx)
            e_prev = jnp.exp(s_prev - mx)
            den = jnp.sum(e_cur, axis=-1, keepdims=True) + jnp.sum(e_prev, axis=-1, keepdims=True)
            inv = 1.0 / den
            o = (jnp.dot(e_cur.astype(BF16), vc[:, sl], preferred_element_type=F32)
                 + jnp.dot(e_prev.astype(BF16), vp[:, sl], preferred_element_type=F32)) * inv
            o_ref[0, rows, sl] = o
            lse_tile = jnp.where(lane == hh, mx + jnp.log(den), lse_tile)
        o_ref[0, rows, SWA_GROUP_WIDTH:] = lse_tile
        return carry

    lax.fori_loop(0, n_blocks, block_body, 0)


def _swa_group(proj2, batch, seq, gi):
    window, dil = SWA_GROUPS[gi]
    assert window // dil == SWA_BLOCK
    length = seq // dil
    n_blocks = length // SWA_BLOCK
    slopes = 2.0 ** (-ALIBI_MAX_BIAS * np.arange(1, SWA_HEADS + 1) / SWA_HEADS)
    slopes = slopes.reshape(len(SWA_GROUPS), SWA_HEADS_PER_GROUP)[gi]
    view = proj2.reshape(batch, length, dil * PROJ_COLS)
    cols_per_res = PROJ_COLS // SWA_GROUP_WIDTH

    def spec(which):
        base = (OFF_SWA + which * SWA_HEADS * HEAD_DIM) // SWA_GROUP_WIDTH + gi
        return pl.BlockSpec((1, length, SWA_GROUP_WIDTH), lambda bi, r: (bi, 0, r * cols_per_res + base))

    out = pl.pallas_call(
        functools.partial(_swa_kernel, n_blocks=n_blocks, dil=dil, slopes=tuple(float(s) for s in slopes)),
        grid=(batch, dil),
        in_specs=[spec(0), spec(1), spec(2)],
        out_specs=pl.BlockSpec((1, length, SWA_OUT_COLS), lambda bi, r: (bi, 0, r)),
        out_shape=jax.ShapeDtypeStruct((batch, length, dil * SWA_OUT_COLS), F32),
        compiler_params=_cparams(("parallel", "parallel")),
        name=f"swa_g{gi}",
    )(view, view, view)
    return out.reshape(batch * seq, SWA_OUT_COLS)


MG_TM = 256


def _merge_kernel(x_ref, ydn_ref, gate_ref, scc_ref, scb_ref, scx_ref, scc_h_ref, scx_h_ref,
                  s0_ref, s1_ref, s2_ref, scw_ref, wdn_ref, wsc_ref, wswa_ref, wout_ref, o_ref,
                  *, tiles_per_seq):
    i = pl.program_id(0)
    first = (i % tiles_per_seq) == 0

    cx_cur = scc_ref[...].astype(F32) * scx_ref[...].astype(F32)
    cx_halo = scc_h_ref[...].astype(F32) * scx_h_ref[...].astype(F32)
    cx_halo = jnp.where(first, 0.0, cx_halo)
    cx = jnp.concatenate([cx_halo, cx_cur], axis=0)
    y_sc = scb_ref[...].astype(F32) * _causal_conv(cx, scw_ref[...], SC_CONV)

    outs = (s0_ref[...], s1_ref[...], s2_ref[...])
    lses = [o[:, SWA_GROUP_WIDTH:] for o in outs]
    mx = jnp.maximum(jnp.maximum(lses[0], lses[1]), lses[2])
    es = [jnp.exp(l - mx) for l in lses]
    inv = 1.0 / (es[0] + es[1] + es[2])
    heads = []
    for hh in range(SWA_HEADS_PER_GROUP):
        sl = slice(hh * HEAD_DIM, (hh + 1) * HEAD_DIM)
        acc = None
        for gi in range(3):
            wt = (es[gi] * inv)[:, hh:hh + 1]
            term = wt * outs[gi][:, sl]
            acc = term if acc is None else acc + term
        heads.append(acc)
    y_swa = jnp.concatenate(heads, axis=1)

    gates = gate_ref[...].astype(F32)
    merged = (_sigmoid(gates[:, 0:D_MODEL])
              * jnp.dot(ydn_ref[...], wdn_ref[...], preferred_element_type=F32)
              + _sigmoid(gates[:, D_MODEL:2 * D_MODEL])
              * jnp.dot(y_sc.astype(BF16), wsc_ref[...], preferred_element_type=F32)
              + _sigmoid(gates[:, 2 * D_MODEL:3 * D_MODEL])
              * jnp.dot(y_swa.astype(BF16), wswa_ref[...], preferred_element_type=F32))
    o_ref[...] = x_ref[...] + jnp.dot(merged.astype(BF16), wout_ref[...], preferred_element_type=F32)


def _merge(x2d, ydn2, proj2, swa_outs, sc_conv_w, wdn, wsc, wswa, wout, seq):
    m = x2d.shape[0]
    tiles_per_seq = seq // MG_TM
    halo_blocks = MG_TM // HALO
    rows = lambda w, cb: pl.BlockSpec((MG_TM, w), lambda i: (i, cb))
    halo = lambda cb: pl.BlockSpec((HALO, SC_WIDTH), lambda i: (jnp.maximum(i * halo_blocks - 1, 0), cb))
    full = lambda a: pl.BlockSpec(a.shape, lambda i: (0, 0), pipeline_mode=pl.Buffered(1))
    return pl.pallas_call(
        functools.partial(_merge_kernel, tiles_per_seq=tiles_per_seq),
        grid=(m // MG_TM,),
        in_specs=[
            rows(D_MODEL, 0),
            rows(DN_WIDTH, 0),
            rows(3 * D_MODEL, OFF_GATE // (3 * D_MODEL)),
            rows(SC_WIDTH, OFF_SCC // SC_WIDTH), rows(SC_WIDTH, OFF_SCB // SC_WIDTH),
            rows(SC_WIDTH, OFF_SCX // SC_WIDTH),
            halo(OFF_SCC // SC_WIDTH), halo(OFF_SCX // SC_WIDTH),
            rows(SWA_OUT_COLS, 0), rows(SWA_OUT_COLS, 0), rows(SWA_OUT_COLS, 0),
            full(sc_conv_w), full(wdn), full(wsc), full(wswa), full(wout),
        ],
        out_specs=rows(D_MODEL, 0),
        out_shape=jax.ShapeDtypeStruct((m, D_MODEL), F32),
        compiler_params=_cparams(("parallel",)),
        name="merge",
    )(x2d, ydn2, proj2, proj2, proj2, proj2, proj2, proj2, *swa_outs, sc_conv_w, wdn, wsc, wswa, wout)


RT_TM = 512
ROUTE_LANE0 = MOE_GROUPS


def _router_kernel(x_ref, g_ref, wr_ref, br_ref, h_ref, comb_ref):
    h = _rmsnorm_rows(x_ref[...], g_ref[...])
    h_ref[...] = h.astype(BF16)
    logits = jnp.dot(h, wr_ref[...], preferred_element_type=F32, precision=lax.Precision.HIGHEST) + br_ref[...]
    lane = lax.broadcasted_iota(jnp.int32, logits.shape, 1)
    big = jnp.int32(LANES)
    neg = -jnp.inf

    is_grp = lane < MOE_GROUPS
    gl = jnp.where(is_grp, logits, neg)
    gmax = jnp.max(gl, axis=-1, keepdims=True)
    grp = jnp.min(jnp.where(gl == gmax, lane, big), axis=-1, keepdims=True)
    g_w = 1.0 / jnp.sum(jnp.exp(gl - gmax), axis=-1, keepdims=True)

    lo = ROUTE_LANE0 + grp * MOE_EPG
    in_grp = jnp.logical_and(lane >= lo, lane < lo + MOE_EPG)
    el = jnp.where(in_grp, logits, neg)
    m1 = jnp.max(el, axis=-1, keepdims=True)
    i1 = jnp.min(jnp.where(el == m1, lane, big), axis=-1, keepdims=True)
    el2 = jnp.where(lane == i1, neg, el)
    m2 = jnp.max(el2, axis=-1, keepdims=True)
    i2 = jnp.min(jnp.where(el2 == m2, lane, big), axis=-1, keepdims=True)
    e2 = jnp.exp(m2 - m1)
    w1 = g_w / (1.0 + e2)
    w2 = g_w * e2 / (1.0 + e2)
    comb_ref[...] = jnp.where(lane == i1, w1, jnp.where(lane == i2, w2, 0.0))


def _router(x2d, g_row, w_route, b_route):
    m = x2d.shape[0]
    return pl.pallas_call(
        _router_kernel,
        grid=(m // RT_TM,),
        in_specs=[
            pl.BlockSpec((RT_TM, D_MODEL), lambda i: (i, 0)),
            pl.BlockSpec((1, D_MODEL), lambda i: (0, 0)),
            pl.BlockSpec((D_MODEL, LANES), lambda i: (0, 0)),
            pl.BlockSpec((1, LANES), lambda i: (0, 0)),
        ],
        out_specs=[pl.BlockSpec((RT_TM, D_MODEL), lambda i: (i, 0)),
                   pl.BlockSpec((RT_TM, LANES), lambda i: (i, 0))],
        out_shape=[jax.ShapeDtypeStruct((m, D_MODEL), BF16), jax.ShapeDtypeStruct((m, LANES), F32)],
        compiler_params=_cparams(("parallel",)),
        name="router",
    )(x2d, g_row, w_route, b_route)


MOE_TM = 512


def _moe_kernel(x_ref, h_ref, comb_ref, wg_ref, wu_ref, wd_ref, o_ref):
    e = pl.program_id(1)

    @pl.when(e == 0)
    def _():
        o_ref[...] = x_ref[...]

    h = h_ref[...]
    lane = lax.broadcasted_iota(jnp.int32, comb_ref.shape, 1)
    c = jnp.sum(jnp.where(lane == e + ROUTE_LANE0, comb_ref[...], 0.0), axis=-1, keepdims=True)
    gate = jnp.dot(h, wg_ref[0], preferred_element_type=F32)
    up = jnp.dot(h, wu_ref[0], preferred_element_type=F32)
    hid = (_silu(gate) * up * c).astype(BF16)
    o_ref[...] += jnp.dot(hid, wd_ref[0], preferred_element_type=F32)


def _moe(x2d, h2d, comb, wg, wu, wd):
    m = x2d.shape[0]
    return pl.pallas_call(
        _moe_kernel,
        grid=(m // MOE_TM, MOE_EXPERTS),
        in_specs=[
            pl.BlockSpec((MOE_TM, D_MODEL), lambda i, e: (i, 0)),
            pl.BlockSpec((MOE_TM, D_MODEL), lambda i, e: (i, 0)),
            pl.BlockSpec((MOE_TM, LANES), lambda i, e: (i, 0)),
            pl.BlockSpec((1, D_MODEL, MOE_FF), lambda i, e: (e, 0, 0)),
            pl.BlockSpec((1, D_MODEL, MOE_FF), lambda i, e: (e, 0, 0)),
            pl.BlockSpec((1, MOE_FF, D_MODEL), lambda i, e: (e, 0, 0)),
        ],
        out_specs=pl.BlockSpec((MOE_TM, D_MODEL), lambda i, e: (i, 0)),
        out_shape=jax.ShapeDtypeStruct((m, D_MODEL), F32),
        compiler_params=_cparams(("parallel", "arbitrary")),
        name="moe",
    )(x2d, h2d, comb, wg, wu, wd)


FN_TM = 512


def _final_norm_kernel(x_ref, g_ref, o_ref):
    o_ref[...] = _rmsnorm_rows(x_ref[...], g_ref[...])


def _final_norm(x2d, g_row):
    m = x2d.shape[0]
    return pl.pallas_call(
        _final_norm_kernel,
        grid=(m // FN_TM,),
        in_specs=[pl.BlockSpec((FN_TM, D_MODEL), lambda i: (i, 0)),
                  pl.BlockSpec((1, D_MODEL), lambda i: (0, 0))],
        out_specs=pl.BlockSpec((FN_TM, D_MODEL), lambda i: (i, 0)),
        out_shape=jax.ShapeDtypeStruct((m, D_MODEL), F32),
        compiler_params=_cparams(("parallel",)),
        name="final_norm",
    )(x2d, g_row)


def _prep_w_in(w):
    o_z_end, o_ab_end, o_sc_end, o_swa_end = 4096, 4112, 7184, 11792
    parts = [w[:, o_swa_end:], w[:, :o_z_end], w[:, o_ab_end:o_sc_end], w[:, o_sc_end:o_swa_end],
             w[:, o_z_end:o_ab_end]]
    used = sum(p.shape[1] for p in parts)
    parts.append(jnp.zeros((w.shape[0], PROJ_COLS - used), w.dtype))
    return jnp.concatenate(parts, axis=1).astype(BF16)


def _lane_row(v, offset=0):
    return jnp.zeros((1, LANES), F32).at[0, offset:offset + v.shape[0]].set(v.astype(F32))


def _layer(x2d, batch, seq, p):
    proj2 = _in_proj(x2d, p["norm_mix_g"], p["w_in"])
    proj3 = proj2.reshape(batch, seq, PROJ_COLS)
    ydn = _deltanet(proj3, p["dn_conv_w"], p["alog_row"], p["dtb_row"], p["dn_norm_g"])
    swa_outs = [_swa_group(proj2, batch, seq, gi) for gi in range(len(SWA_GROUPS))]
    x2d = _merge(x2d, ydn.reshape(batch * seq, DN_WIDTH), proj2, swa_outs, p["sc_conv_w"],
                 p["w_branch_dn"], p["w_branch_sc"], p["w_branch_swa"], p["w_out"], seq)
    h2d, comb = _router(x2d, p["norm_ffn_g"], p["w_route"], p["b_route"])
    return _moe(x2d, h2d, comb, p["expert_w_gate"], p["expert_w_up"], p["expert_w_down"])


def kernel(x, norm_mix_g, w_in, dn_conv_w, dn_a_log, dn_dt_bias, dn_norm_g, sc_conv_w, w_branch_dn, w_branch_sc, w_branch_swa, w_out, norm_ffn_g, router_group_w, router_group_b, router_expert_w, router_expert_b, expert_w_gate, expert_w_up, expert_w_down, final_norm_g):
    batch, seq, _ = x.shape
    depth = w_in.shape[0]
    x2d = x.reshape(batch * seq, D_MODEL)
    for l in range(depth):
        w_route = jnp.concatenate([router_group_w[l], router_expert_w[l]], axis=1)
        w_route = jnp.pad(w_route, ((0, 0), (0, LANES - w_route.shape[1])))
        p = dict(
            norm_mix_g=norm_mix_g[l].reshape(1, D_MODEL),
            w_in=_prep_w_in(w_in[l]),
            dn_conv_w=dn_conv_w[l],
            alog_row=_lane_row(dn_a_log[l]),
            dtb_row=_lane_row(dn_dt_bias[l]),
            dn_norm_g=dn_norm_g[l].reshape(1, HEAD_DIM),
            sc_conv_w=sc_conv_w[l],
            w_branch_dn=w_branch_dn[l].astype(BF16),
            w_branch_sc=w_branch_sc[l].astype(BF16),
            w_branch_swa=w_branch_swa[l].astype(BF16),
            w_out=w_out[l].astype(BF16),
            norm_ffn_g=norm_ffn_g[l].reshape(1, D_MODEL),
            w_route=w_route,
            b_route=_lane_row(jnp.concatenate([router_group_b[l], router_expert_b[l]])),
            expert_w_gate=expert_w_gate[l].astype(BF16),
            expert_w_up=expert_w_up[l].astype(BF16),
            expert_w_down=expert_w_down[l].astype(BF16),
        )
        x2d = _layer(x2d, batch, seq, p)
    return _final_norm(x2d, final_norm_g.reshape(1, D_MODEL)).reshape(batch, seq, D_MODEL)
```

```python
import functools

import jax
import jax.numpy as jnp
import numpy as np
from jax import lax
from jax.experimental import pallas as pl
from jax.experimental.pallas import tpu as pltpu

F32 = jnp.float32
BF16 = jnp.bfloat16

D_MODEL = 2048
RMS_EPS = 1e-6
L2_EPS = 1e-6

DN_HEADS = 8
HEAD_DIM = 128
DN_WIDTH = DN_HEADS * HEAD_DIM
DN_CONV = 4
SC_WIDTH = 1024
SC_CONV = 3
SWA_GROUPS = ((128, 1), (512, 4), (2048, 16))
SWA_HEADS_PER_GROUP = 4
SWA_HEADS = 12
SWA_GROUP_WIDTH = SWA_HEADS_PER_GROUP * HEAD_DIM
SWA_BLOCK = 128
ALIBI_MAX_BIAS = 8.0
MOE_GROUPS = 4
MOE_EPG = 8
MOE_EXPERTS = 32
MOE_FF = 512

OFF_GATE = 0
OFF_DNQ = 6144
OFF_DNK = OFF_DNQ + DN_WIDTH
OFF_DNV = OFF_DNK + DN_WIDTH
OFF_DNZ = OFF_DNV + DN_WIDTH
OFF_SCC = OFF_DNZ + DN_WIDTH
OFF_SCB = OFF_SCC + SC_WIDTH
OFF_SCX = OFF_SCB + SC_WIDTH
OFF_SWA = OFF_SCX + SC_WIDTH
OFF_AB = OFF_SWA + 3 * SWA_HEADS * HEAD_DIM
PROJ_COLS = 18432

LANES = 128
SWA_OUT_SLOTS = SWA_HEADS_PER_GROUP + 1
CHUNK = 128
HALO = 16

VMEM_LIMIT = 56 * 1024 * 1024


def _cparams(sem):
    return pltpu.CompilerParams(dimension_semantics=sem, vmem_limit_bytes=VMEM_LIMIT)


def _sigmoid(x):
    return 1.0 / (1.0 + jnp.exp(-x))


def _silu(x):
    return x * _sigmoid(x)


def _softplus(x):
    return jnp.maximum(x, 0.0) + jnp.log1p(jnp.exp(-jnp.abs(x)))


IN_TM = 1024
IN_TN = 1024
NORM_ROWS = 64


def _rmsnorm_rows(x, g):
    ms = jnp.mean(x * x, axis=-1, keepdims=True)
    return x * lax.rsqrt(ms + RMS_EPS) * g


def _in_proj_kernel(x_ref, g_ref, w_ref, o_ref, h_ref):
    @pl.when(pl.program_id(1) == 0)
    def _():
        def body(r, c):
            rows = pl.ds(pl.multiple_of(r * NORM_ROWS, NORM_ROWS), NORM_ROWS)
            h_ref[rows, :] = _rmsnorm_rows(x_ref[rows, :], g_ref[...]).astype(BF16)
            return c
        lax.fori_loop(0, IN_TM // NORM_ROWS, body, 0)

    o_ref[...] = jnp.dot(h_ref[...], w_ref[...], preferred_element_type=F32).astype(o_ref.dtype)


def _in_proj(x2d, g_row, w_bf16):
    m = x2d.shape[0]
    return pl.pallas_call(
        _in_proj_kernel,
        grid=(m // IN_TM, PROJ_COLS // IN_TN),
        in_specs=[
            pl.BlockSpec((IN_TM, D_MODEL), lambda i, j: (i, 0)),
            pl.BlockSpec((1, D_MODEL), lambda i, j: (0, 0)),
            pl.BlockSpec((D_MODEL, IN_TN), lambda i, j: (0, j)),
        ],
        out_specs=pl.BlockSpec((IN_TM, IN_TN), lambda i, j: (i, j)),
        out_shape=jax.ShapeDtypeStruct((m, PROJ_COLS), BF16),
        scratch_shapes=[pltpu.VMEM((IN_TM, D_MODEL), BF16)],
        compiler_params=_cparams(("parallel", "arbitrary")),
        name="in_proj",
    )(x2d, g_row, w_bf16)


DN_HB = 4
DN_TS = 512
DN_W = DN_HB * HEAD_DIM
DN_DOUBLINGS = CHUNK.bit_length() - 2


def _dot_nt(a, b):
    return lax.dot_general(a, b, (((1,), (1,)), ((), ())), preferred_element_type=F32)


def _dot_tn(a, b):
    return lax.dot_general(a, b, (((0,), (0,)), ((), ())), preferred_element_type=F32)


def _causal_conv(xx, w, width):
    acc = None
    for j in range(width):
        shift = width - 1 - j
        xs = xx if shift == 0 else pltpu.roll(xx, shift, axis=0)
        term = xs[HALO:, :] * w[j:j + 1, :]
        acc = term if acc is None else acc + term
    return acc


def _deltanet_kernel(q_ref, k_ref, v_ref, z_ref, ab_ref, cwq_ref, cwk_ref, cwv_ref,
                     alog_ref, dtb_ref, ng_ref, o_ref,
                     xpad_ref, state_ref):
    hg = pl.program_id(1)
    st = pl.program_id(2)

    @pl.when(st == 0)
    def _():
        xpad_ref[:, 0:HALO, :] = jnp.zeros((3, HALO, DN_W), BF16)
        state_ref[...] = jnp.zeros_like(state_ref)

    xpad_ref[0, HALO:, :] = q_ref[0]
    xpad_ref[1, HALO:, :] = k_ref[0]
    xpad_ref[2, HALO:, :] = v_ref[0]

    row_i = lax.broadcasted_iota(jnp.int32, (CHUNK, CHUNK), 0)
    col_j = lax.broadcasted_iota(jnp.int32, (CHUNK, CHUNK), 1)
    causal = row_i >= col_j
    strict = row_i > col_j
    tri = causal.astype(F32)
    eye = (row_i == col_j).astype(F32)
    lane = lax.broadcasted_iota(jnp.int32, (CHUNK, LANES), 1)
    neg_decay_rate = -jnp.exp(alog_ref[...])
    cws = (cwq_ref[...], cwk_ref[...], cwv_ref[...])
    ng = ng_ref[...]

    def chunk_body(c, carry):
        r0 = pl.multiple_of(c * CHUNK, CHUNK)
        win = pl.ds(r0, CHUNK + HALO)
        rows = pl.ds(r0, CHUNK)
        conv = [_silu(_causal_conv(xpad_ref[t, win, :].astype(F32), cws[t], DN_CONV)) for t in range(3)]
        ab = ab_ref[0, rows, :].astype(F32)
        g_raw = neg_decay_rate * _softplus(ab + dtb_ref[...])
        g_cum = jnp.dot(tri, g_raw, preferred_element_type=F32, precision=lax.Precision.HIGHEST)
        beta_all = _sigmoid(ab)
        z = z_ref[0, rows, :].astype(F32)

        heads = range(DN_HB)
        sls = [slice(i * HEAD_DIM, (i + 1) * HEAD_DIM) for i in heads]
        dot = functools.partial(jnp.dot, preferred_element_type=F32)

        g_col = [jnp.sum(jnp.where(lane == hg * DN_HB + i, g_cum, 0.0), axis=-1, keepdims=True) for i in heads]
        beta = [jnp.sum(jnp.where(lane == hg * DN_HB + i + DN_HEADS, beta_all, 0.0), axis=-1, keepdims=True)
                for i in heads]
        g_last = [g[CHUNK - 1:CHUNK, :] for g in g_col]
        eg = [jnp.exp(g) for g in g_col]
        ek = [jnp.exp(gl - g) for gl, g in zip(g_last, g_col)]
        g_b = [jnp.broadcast_to(g, (CHUNK, CHUNK)) for g in g_col]
        decay = [jnp.exp(jnp.where(causal, gb - gb.T, -jnp.inf)) for gb in g_b]

        qf = [conv[0][:, sl] for sl in sls]
        kf = [conv[1][:, sl] for sl in sls]
        vf = [conv[2][:, sl] for sl in sls]
        q = [x * lax.rsqrt(jnp.sum(x * x, axis=-1, keepdims=True) + L2_EPS) * (HEAD_DIM ** -0.5) for x in qf]
        k = [x * lax.rsqrt(jnp.sum(x * x, axis=-1, keepdims=True) + L2_EPS) for x in kf]
        kb = [ki * bi for ki, bi in zip(k, beta)]
        a2 = [_dot_nt(jnp.concatenate([qi, kbi], axis=0).astype(BF16), ki.astype(BF16))
              for qi, kbi, ki in zip(q, kb, k)]
        attn = [a[:CHUNK] * d for a, d in zip(a2, decay)]
        n_mat = [jnp.where(strict, -(a[CHUNK:] * d), 0.0) for a, d in zip(a2, decay)]

        u_mat = [eye + n for n in n_mat]
        p_mat = [dot(n.astype(BF16), n.astype(BF16)) for n in n_mat]
        for _ in range(DN_DOUBLINGS - 1):
            up = [dot(jnp.concatenate([u.astype(BF16), p.astype(BF16)], axis=0), p.astype(BF16))
                  for u, p in zip(u_mat, p_mat)]
            u_mat = [u + x[:CHUNK] for u, x in zip(u_mat, up)]
            p_mat = [x[CHUNK:] for x in up]
        u_mat = [u + dot(u.astype(BF16), p.astype(BF16)) for u, p in zip(u_mat, p_mat)]
        uw = [dot(u.astype(BF16), jnp.concatenate([vi * bi, kbi * egi], axis=1).astype(BF16))
              for u, vi, bi, kbi, egi in zip(u_mat, vf, beta, kb, eg)]

        s_old = [state_ref[i] for i in heads]
        qw = [dot(jnp.concatenate([qi * egi, x[:, HEAD_DIM:]], axis=0).astype(BF16), s.astype(BF16))
              for qi, egi, x, s in zip(q, eg, uw, s_old)]
        v16 = [(x[:, :HEAD_DIM] - y[CHUNK:]).astype(BF16) for x, y in zip(uw, qw)]
        o = [y[:CHUNK] + dot(a.astype(BF16), v) for y, a, v in zip(qw, attn, v16)]
        s_new = [s * jnp.exp(gl) + _dot_tn((ki * eki).astype(BF16), v)
                 for s, gl, ki, eki, v in zip(s_old, g_last, k, ek, v16)]
        for i in heads:
            state_ref[i] = s_new[i]
            on = o[i] * lax.rsqrt(jnp.mean(o[i] * o[i], axis=-1, keepdims=True) + RMS_EPS) * ng
            o_ref[0, rows, sls[i]] = (on * _silu(z[:, sls[i]])).astype(o_ref.dtype)
        return carry

    lax.fori_loop(0, DN_TS // CHUNK, chunk_body, 0)

    tail = pl.ds(DN_TS, HALO)
    head = pl.ds(0, HALO)
    for t in range(3):
        xpad_ref[t, head, :] = xpad_ref[t, tail, :]


def _deltanet(proj3, conv_w, alog_row, dtb_row, ng_row):
    b, s, _ = proj3.shape
    qb, kb_, vb, zb = (off // DN_W for off in (OFF_DNQ, OFF_DNK, OFF_DNV, OFF_DNZ))
    act = lambda base: pl.BlockSpec((1, DN_TS, DN_W), lambda bi, hg, st: (bi, st, base + hg))
    cw = lambda base: pl.BlockSpec((DN_CONV, DN_W), lambda bi, hg, st: (0, base + hg))
    row = pl.BlockSpec((1, LANES), lambda bi, hg, st: (0, 0))
    return pl.pallas_call(
        _deltanet_kernel,
        grid=(b, DN_HEADS // DN_HB, s // DN_TS),
        in_specs=[
            act(qb), act(kb_), act(vb), act(zb),
            pl.BlockSpec((1, DN_TS, LANES), lambda bi, hg, st: (bi, st, OFF_AB // LANES)),
            cw(0), cw(DN_WIDTH // DN_W), cw(2 * DN_WIDTH // DN_W),
            row, row, row,
        ],
        out_specs=pl.BlockSpec((1, DN_TS, DN_W), lambda bi, hg, st: (bi, st, hg)),
        out_shape=jax.ShapeDtypeStruct((b, s, DN_WIDTH), BF16),
        scratch_shapes=[
            pltpu.VMEM((3, DN_TS + HALO, DN_W), BF16),
            pltpu.VMEM((DN_HB, HEAD_DIM, HEAD_DIM), F32),
        ],
        compiler_params=_cparams(("parallel", "parallel", "arbitrary")),
        name="deltanet",
    )(proj3, proj3, proj3, proj3, proj3, conv_w, conv_w, conv_w, alog_row, dtb_row, ng_row)


SWA_CAST_ROWS = 256


def _swa_kernel(q_ref, k_ref, v_ref, o_ref, *scratch, seq, dil, slopes):
    n_blocks = seq // dil // SWA_BLOCK
    qi = lax.broadcasted_iota(jnp.int32, (SWA_BLOCK, SWA_BLOCK), 0)
    kj = lax.broadcasted_iota(jnp.int32, (SWA_BLOCK, SWA_BLOCK), 1)
    step_cur = (qi - kj).astype(F32)
    step_prev = step_cur + float(SWA_BLOCK)
    ok_cur = qi >= kj
    ok_prev = kj >= qi
    lane = lax.broadcasted_iota(jnp.int32, (SWA_BLOCK, LANES), 1)
    scale = HEAD_DIM ** -0.5
    heads = range(SWA_HEADS_PER_GROUP)
    sls = [slice(hh * HEAD_DIM, (hh + 1) * HEAD_DIM) for hh in heads]
    m_h = [float(slopes[hh]) * float(dil) for hh in heads]

    if dil > 1:
        (f32_ref,) = scratch

        def cast_body(c, carry):
            rows = pl.ds(pl.multiple_of(c * SWA_CAST_ROWS, SWA_CAST_ROWS), SWA_CAST_ROWS)
            for t, ref in enumerate((q_ref, k_ref, v_ref)):
                for hh in heads:
                    f32_ref[t * SWA_HEADS_PER_GROUP + hh, rows, :] = ref[0, rows, sls[hh]].astype(F32)
            return carry
        lax.fori_loop(0, seq // SWA_CAST_ROWS, cast_body, 0)

        def load(t, rows):
            return [f32_ref[t * SWA_HEADS_PER_GROUP + hh, rows, :].astype(BF16) for hh in heads]
    else:
        refs = (q_ref, k_ref, v_ref)

        def load(t, rows):
            return [refs[t][0, rows, sls[hh]] for hh in heads]

    def block_body(it, carry):
        r = it // n_blocks
        n = it % n_blocks
        base = n * (SWA_BLOCK * dil) + r
        prev_base = jnp.maximum(n - 1, 0) * (SWA_BLOCK * dil) + r
        if dil > 1:
            rows = pl.ds(base, SWA_BLOCK, stride=dil)
            prev_rows = pl.ds(prev_base, SWA_BLOCK, stride=dil)
        else:
            rows = pl.ds(pl.multiple_of(base, SWA_BLOCK), SWA_BLOCK)
            prev_rows = pl.ds(pl.multiple_of(prev_base, SWA_BLOCK), SWA_BLOCK)
        qn, kc, vc = load(0, rows), load(1, rows), load(2, rows)
        kp, vp = load(1, prev_rows), load(2, prev_rows)
        ok_p = jnp.logical_and(ok_prev, n > 0)

        s_cur = [jnp.where(ok_cur, _dot_nt(qh, kh) * scale - m * step_cur, -jnp.inf)
                 for qh, kh, m in zip(qn, kc, m_h)]
        s_prev = [jnp.where(ok_p, _dot_nt(qh, kh) * scale - m * step_prev, -jnp.inf)
                  for qh, kh, m in zip(qn, kp, m_h)]
        mx = [jnp.maximum(jnp.max(a, axis=-1, keepdims=True), jnp.max(b, axis=-1, keepdims=True))
              for a, b in zip(s_cur, s_prev)]
        e_cur = [jnp.exp(a - m) for a, m in zip(s_cur, mx)]
        e_prev = [jnp.exp(b - m) for b, m in zip(s_prev, mx)]
        den = [jnp.sum(a, axis=-1, keepdims=True) + jnp.sum(b, axis=-1, keepdims=True)
               for a, b in zip(e_cur, e_prev)]
        pv = [jnp.dot(a.astype(BF16), vch, preferred_element_type=F32)
              + jnp.dot(b.astype(BF16), vph, preferred_element_type=F32)
              for a, b, vch, vph in zip(e_cur, e_prev, vc, vp)]
        lse_tile = jnp.zeros((SWA_BLOCK, LANES), F32)
        for hh in heads:
            o_ref[0, hh, rows, :] = pv[hh] * (1.0 / den[hh])
            lse_tile = jnp.where(lane == hh, mx[hh] + jnp.log(den[hh]), lse_tile)
        o_ref[0, SWA_HEADS_PER_GROUP, rows, :] = lse_tile
        return carry

    lax.fori_loop(0, seq // SWA_BLOCK, block_body, 0)


def _swa_group(proj3, gi):
    batch, seq, _ = proj3.shape
    window, dil = SWA_GROUPS[gi]
    assert window // dil == SWA_BLOCK and seq % (dil * SWA_BLOCK) == 0
    slopes = 2.0 ** (-ALIBI_MAX_BIAS * np.arange(1, SWA_HEADS + 1) / SWA_HEADS)
    slopes = slopes.reshape(len(SWA_GROUPS), SWA_HEADS_PER_GROUP)[gi]

    def spec(which):
        cb = (OFF_SWA + which * SWA_HEADS * HEAD_DIM) // SWA_GROUP_WIDTH + gi
        return pl.BlockSpec((1, seq, SWA_GROUP_WIDTH), lambda bi: (bi, 0, cb))

    scratch = [pltpu.VMEM((3 * SWA_HEADS_PER_GROUP, seq, HEAD_DIM), F32)] if dil > 1 else []
    return pl.pallas_call(
        functools.partial(_swa_kernel, seq=seq, dil=dil, slopes=tuple(float(s) for s in slopes)),
        grid=(batch,),
        in_specs=[spec(0), spec(1), spec(2)],
        out_specs=pl.BlockSpec((1, SWA_OUT_SLOTS, seq, HEAD_DIM), lambda bi: (bi, 0, 0, 0)),
        out_shape=jax.ShapeDtypeStruct((batch, SWA_OUT_SLOTS, seq, HEAD_DIM), F32),
        scratch_shapes=scratch,
        compiler_params=_cparams(("parallel",)),
        name=f"swa_g{gi}",
    )(proj3, proj3, proj3)


MG_TM = 256


def _merge_kernel(x_ref, ydn_ref, gate_ref, scc_ref, scb_ref, scx_ref, scc_h_ref, scx_h_ref,
                  s0_ref, s1_ref, s2_ref, scw_ref, wdn_ref, wsc_ref, wswa_ref, wout_ref, o_ref,
                  *, tiles_per_seq):
    i = pl.program_id(0)
    first = (i % tiles_per_seq) == 0

    cx_cur = scc_ref[...].astype(F32) * scx_ref[...].astype(F32)
    cx_halo = scc_h_ref[...].astype(F32) * scx_h_ref[...].astype(F32)
    cx_halo = jnp.where(first, 0.0, cx_halo)
    cx = jnp.concatenate([cx_halo, cx_cur], axis=0)
    y_sc = scb_ref[...].astype(F32) * _causal_conv(cx, scw_ref[...], SC_CONV)

    outs = (s0_ref, s1_ref, s2_ref)
    lses = [o[0, SWA_HEADS_PER_GROUP] for o in outs]
    mx = jnp.maximum(jnp.maximum(lses[0], lses[1]), lses[2])
    es = [jnp.exp(l - mx) for l in lses]
    inv = 1.0 / (es[0] + es[1] + es[2])
    heads = []
    for hh in range(SWA_HEADS_PER_GROUP):
        acc = None
        for gi in range(3):
            wt = (es[gi] * inv)[:, hh:hh + 1]
            term = wt * outs[gi][0, hh]
            acc = term if acc is None else acc + term
        heads.append(acc)
    y_swa = jnp.concatenate(heads, axis=1)

    gates = gate_ref[...].astype(F32)
    merged = (_sigmoid(gates[:, 0:D_MODEL])
              * jnp.dot(ydn_ref[...], wdn_ref[...], preferred_element_type=F32)
              + _sigmoid(gates[:, D_MODEL:2 * D_MODEL])
              * jnp.dot(y_sc.astype(BF16), wsc_ref[...], preferred_element_type=F32)
              + _sigmoid(gates[:, 2 * D_MODEL:3 * D_MODEL])
              * jnp.dot(y_swa.astype(BF16), wswa_ref[...], preferred_element_type=F32))
    o_ref[...] = x_ref[...] + jnp.dot(merged.astype(BF16), wout_ref[...], preferred_element_type=F32)


def _merge(x2d, ydn2, proj2, swa_outs, sc_conv_w, wdn, wsc, wswa, wout, seq):
    m = x2d.shape[0]
    tiles_per_seq = seq // MG_TM
    halo_blocks = MG_TM // HALO
    rows = lambda w, cb: pl.BlockSpec((MG_TM, w), lambda i: (i, cb))
    halo = lambda cb: pl.BlockSpec((HALO, SC_WIDTH), lambda i: (jnp.maximum(i * halo_blocks - 1, 0), cb))
    full = lambda a: pl.BlockSpec(a.shape, lambda i: (0, 0), pipeline_mode=pl.Buffered(1))
    swa = pl.BlockSpec((1, SWA_OUT_SLOTS, MG_TM, HEAD_DIM),
                       lambda i: (i // tiles_per_seq, 0, i % tiles_per_seq, 0))
    return pl.pallas_call(
        functools.partial(_merge_kernel, tiles_per_seq=tiles_per_seq),
        grid=(m // MG_TM,),
        in_specs=[
            rows(D_MODEL, 0),
            rows(DN_WIDTH, 0),
            rows(3 * D_MODEL, OFF_GATE // (3 * D_MODEL)),
            rows(SC_WIDTH, OFF_SCC // SC_WIDTH), rows(SC_WIDTH, OFF_SCB // SC_WIDTH),
            rows(SC_WIDTH, OFF_SCX // SC_WIDTH),
            halo(OFF_SCC // SC_WIDTH), halo(OFF_SCX // SC_WIDTH),
            swa, swa, swa,
            full(sc_conv_w), full(wdn), full(wsc), full(wswa), full(wout),
        ],
        out_specs=rows(D_MODEL, 0),
        out_shape=jax.ShapeDtypeStruct((m, D_MODEL), F32),
        compiler_params=_cparams(("parallel",)),
        name="merge",
    )(x2d, ydn2, proj2, proj2, proj2, proj2, proj2, proj2, *swa_outs, sc_conv_w, wdn, wsc, wswa, wout)


RT_TM = 512
ROUTE_LANE0 = MOE_GROUPS
R_E1, R_E2, R_W1, R_W2, R_RANK1, R_RANK2 = range(6)
HALF = D_MODEL // 2


def _pack_pair(lo, hi):
    lo_b = lax.bitcast_convert_type(lo.astype(BF16).astype(F32), jnp.uint32)
    hi_b = lax.bitcast_convert_type(hi.astype(BF16).astype(F32), jnp.uint32)
    return (hi_b & jnp.uint32(0xFFFF0000)) | (lo_b >> jnp.uint32(16))


def _unpack_pair(p):
    lo = lax.bitcast_convert_type(p << jnp.uint32(16), F32)
    hi = lax.bitcast_convert_type(p & jnp.uint32(0xFFFF0000), F32)
    return lo, hi


def _router_kernel(x_ref, g_ref, wr_ref, br_ref, hp_ref, route_ref, cnt_ref, carry_ref):
    @pl.when(pl.program_id(0) == 0)
    def _():
        carry_ref[...] = jnp.zeros_like(carry_ref)

    h = _rmsnorm_rows(x_ref[...], g_ref[...])
    hp_ref[...] = _pack_pair(h[:, :HALF], h[:, HALF:])
    logits = jnp.dot(h, wr_ref[...], preferred_element_type=F32, precision=lax.Precision.HIGHEST) + br_ref[...]
    lane = lax.broadcasted_iota(jnp.int32, logits.shape, 1)
    big = jnp.int32(LANES)
    neg = -jnp.inf

    is_grp = lane < MOE_GROUPS
    gl = jnp.where(is_grp, logits, neg)
    gmax = jnp.max(gl, axis=-1, keepdims=True)
    grp = jnp.min(jnp.where(gl == gmax, lane, big), axis=-1, keepdims=True)
    g_w = 1.0 / jnp.sum(jnp.exp(gl - gmax), axis=-1, keepdims=True)

    lo = ROUTE_LANE0 + grp * MOE_EPG
    in_grp = jnp.logical_and(lane >= lo, lane < lo + MOE_EPG)
    el = jnp.where(in_grp, logits, neg)
    m1 = jnp.max(el, axis=-1, keepdims=True)
    i1 = jnp.min(jnp.where(el == m1, lane, big), axis=-1, keepdims=True)
    el2 = jnp.where(lane == i1, neg, el)
    m2 = jnp.max(el2, axis=-1, keepdims=True)
    i2 = jnp.min(jnp.where(el2 == m2, lane, big), axis=-1, keepdims=True)
    e2 = jnp.exp(m2 - m1)
    w1 = g_w / (1.0 + e2)
    w2 = g_w * e2 / (1.0 + e2)

    tm = logits.shape[0]
    onehot = jnp.where(jnp.logical_or(lane == i1, lane == i2), 1.0, 0.0)
    earlier = (lax.broadcasted_iota(jnp.int32, (tm, tm), 0) > lax.broadcasted_iota(jnp.int32, (tm, tm), 1))
    before = carry_ref[...] + jnp.dot(jnp.where(earlier, 1.0, 0.0).astype(BF16), onehot.astype(BF16),
                                      preferred_element_type=F32)
    rank1 = jnp.sum(jnp.where(lane == i1, before, 0.0), axis=-1, keepdims=True)
    rank2 = jnp.sum(jnp.where(lane == i2, before, 0.0), axis=-1, keepdims=True)
    carry_ref[...] += jnp.sum(onehot, axis=0, keepdims=True)
    cnt_ref[...] = carry_ref[...]

    rec = jnp.zeros_like(logits)
    for ln, val in ((R_E1, (i1 - ROUTE_LANE0).astype(F32)), (R_E2, (i2 - ROUTE_LANE0).astype(F32)),
                    (R_W1, w1), (R_W2, w2), (R_RANK1, rank1), (R_RANK2, rank2)):
        rec = jnp.where(lane == ln, val, rec)
    route_ref[...] = rec


def _router(x2d, g_row, w_route, b_route):
    m = x2d.shape[0]
    return pl.pallas_call(
        _router_kernel,
        grid=(m // RT_TM,),
        in_specs=[
            pl.BlockSpec((RT_TM, D_MODEL), lambda i: (i, 0)),
            pl.BlockSpec((1, D_MODEL), lambda i: (0, 0)),
            pl.BlockSpec((D_MODEL, LANES), lambda i: (0, 0)),
            pl.BlockSpec((1, LANES), lambda i: (0, 0)),
        ],
        out_specs=[pl.BlockSpec((RT_TM, HALF), lambda i: (i, 0)),
                   pl.BlockSpec((RT_TM, LANES), lambda i: (i, 0)),
                   pl.BlockSpec((1, LANES), lambda i: (0, 0))],
        out_shape=[jax.ShapeDtypeStruct((m, HALF), jnp.uint32), jax.ShapeDtypeStruct((m, LANES), F32),
                   jax.ShapeDtypeStruct((1, LANES), F32)],
        scratch_shapes=[pltpu.VMEM((1, LANES), F32)],
        compiler_params=_cparams(("arbitrary",)),
        name="router",
    )(x2d, g_row, w_route, b_route)


EXP_TM = 256


def _n_row_tiles(n_tokens):
    return 2 * n_tokens // EXP_TM + MOE_EXPERTS


def _moe_plan(route, cnt):
    n_tiles = _n_row_tiles(route.shape[0])
    counts = cnt[0, ROUTE_LANE0:ROUTE_LANE0 + MOE_EXPERTS].astype(jnp.int32)
    padded = (counts + EXP_TM - 1) // EXP_TM * EXP_TM
    ends = jnp.cumsum(padded)
    offs = ends - padded
    e1, e2 = route[:, R_E1].astype(jnp.int32), route[:, R_E2].astype(jnp.int32)
    pos1 = offs[e1] + route[:, R_RANK1].astype(jnp.int32)
    pos2 = offs[e2] + route[:, R_RANK2].astype(jnp.int32)
    n_active = ends[-1] // EXP_TM
    tile_row = jnp.minimum(jnp.arange(n_tiles), n_active - 1) * EXP_TM
    tile_expert = jnp.sum(ends[None, :] <= tile_row[:, None], axis=1)
    return pos1, pos2, tile_expert.astype(jnp.int32), n_active.astype(jnp.int32).reshape(1)


DSP_TB = 512


def _dispatch_kernel(pos1_ref, pos2_ref, hp_ref, xs_in_ref, xs_ref, sem):
    del xs_in_ref

    def row_copy(j, pos_ref):
        return pltpu.make_async_copy(hp_ref.at[pl.ds(j, 1), :], xs_ref.at[pl.ds(pos_ref[0, 0, j], 1), :], sem)

    def start(j, c):
        row_copy(j, pos1_ref).start()
        row_copy(j, pos2_ref).start()
        return c

    def wait(j, c):
        row_copy(j, pos1_ref).wait()
        row_copy(j, pos2_ref).wait()
        return c

    lax.fori_loop(0, DSP_TB, start, 0)
    lax.fori_loop(0, DSP_TB, wait, 0)


def _dispatch(hp, pos1, pos2):
    m = hp.shape[0]
    n_rows = _n_row_tiles(m) * EXP_TM
    pos_spec = pl.BlockSpec((1, 1, DSP_TB), lambda i: (i, 0, 0), memory_space=pltpu.SMEM)
    return pl.pallas_call(
        _dispatch_kernel,
        grid=(m // DSP_TB,),
        in_specs=[pos_spec, pos_spec,
                  pl.BlockSpec((DSP_TB, HALF), lambda i: (i, 0)),
                  pl.BlockSpec(memory_space=pl.ANY)],
        out_specs=pl.BlockSpec(memory_space=pl.ANY),
        out_shape=jax.ShapeDtypeStruct((n_rows, HALF), jnp.uint32),
        scratch_shapes=[pltpu.SemaphoreType.DMA(())],
        input_output_aliases={3: 0},
        compiler_params=_cparams(("arbitrary",)),
        name="moe_dispatch",
    )(pos1.reshape(m // DSP_TB, 1, DSP_TB), pos2.reshape(m // DSP_TB, 1, DSP_TB), hp,
      jnp.zeros((n_rows, HALF), jnp.uint32))


def _expert_kernel(te_ref, nact_ref, xs_ref, wg_ref, wu_ref, wd_ref, ys_ref):
    del te_ref

    @pl.when(pl.program_id(0) < nact_ref[0])
    def _():
        lo, hi = _unpack_pair(xs_ref[...])
        lo, hi = lo.astype(BF16), hi.astype(BF16)
        dot = functools.partial(jnp.dot, preferred_element_type=F32)
        gate = dot(lo, wg_ref[0, :HALF, :]) + dot(hi, wg_ref[0, HALF:, :])
        up = dot(lo, wu_ref[0, :HALF, :]) + dot(hi, wu_ref[0, HALF:, :])
        y = dot((_silu(gate) * up).astype(BF16), wd_ref[0])
        ys_ref[...] = _pack_pair(y[:, :HALF], y[:, HALF:])

    @pl.when(pl.program_id(0) >= nact_ref[0])
    def _():
        ys_ref[...] = jnp.zeros_like(ys_ref)


def _experts(xs, tile_expert, n_active, wg, wu, wd):
    n_tiles = xs.shape[0] // EXP_TM
    rows = pl.BlockSpec((EXP_TM, HALF), lambda i, te, na: (jnp.minimum(i, na[0] - 1), 0))
    out_rows = pl.BlockSpec((EXP_TM, HALF), lambda i, te, na: (i, 0))
    return pl.pallas_call(
        _expert_kernel,
        grid_spec=pltpu.PrefetchScalarGridSpec(
            num_scalar_prefetch=2,
            grid=(n_tiles,),
            in_specs=[rows,
                      pl.BlockSpec((1, D_MODEL, MOE_FF), lambda i, te, na: (te[i], 0, 0)),
                      pl.BlockSpec((1, D_MODEL, MOE_FF), lambda i, te, na: (te[i], 0, 0)),
                      pl.BlockSpec((1, MOE_FF, D_MODEL), lambda i, te, na: (te[i], 0, 0))],
            out_specs=out_rows,
        ),
        out_shape=jax.ShapeDtypeStruct(xs.shape, jnp.uint32),
        compiler_params=_cparams(("arbitrary",)),
        name="moe_experts",
    )(tile_expert, n_active, xs, wg, wu, wd)


CMB_TC = 256


def _combine_kernel(pos1_ref, pos2_ref, x_ref, route_ref, g_ref, ys_ref, o_ref, buf_ref, sem, *, final_norm):
    def row_copy(j, k, pos_ref):
        return pltpu.make_async_copy(ys_ref.at[pl.ds(pos_ref[0, 0, j], 1), :], buf_ref.at[k, pl.ds(j, 1), :], sem)

    def start(j, c):
        row_copy(j, 0, pos1_ref).start()
        row_copy(j, 1, pos2_ref).start()
        return c

    def wait(j, c):
        row_copy(j, 0, pos1_ref).wait()
        row_copy(j, 1, pos2_ref).wait()
        return c

    lax.fori_loop(0, CMB_TC, start, 0)
    lax.fori_loop(0, CMB_TC, wait, 0)

    rec = route_ref[...]
    w1, w2 = rec[:, R_W1:R_W1 + 1], rec[:, R_W2:R_W2 + 1]
    lo1, hi1 = _unpack_pair(buf_ref[0])
    lo2, hi2 = _unpack_pair(buf_ref[1])
    out_lo = x_ref[:, :HALF] + w1 * lo1 + w2 * lo2
    out_hi = x_ref[:, HALF:] + w1 * hi1 + w2 * hi2
    if final_norm:
        ms = (jnp.sum(out_lo * out_lo, axis=-1, keepdims=True)
              + jnp.sum(out_hi * out_hi, axis=-1, keepdims=True)) * (1.0 / D_MODEL)
        inv = lax.rsqrt(ms + RMS_EPS)
        out_lo = out_lo * inv * g_ref[:, :HALF]
        out_hi = out_hi * inv * g_ref[:, HALF:]
    o_ref[:, :HALF] = out_lo
    o_ref[:, HALF:] = out_hi


def _combine(x2d, route, ys, pos1, pos2, g_row, final_norm):
    m = x2d.shape[0]
    pos_spec = pl.BlockSpec((1, 1, CMB_TC), lambda i: (i, 0, 0), memory_space=pltpu.SMEM)
    return pl.pallas_call(
        functools.partial(_combine_kernel, final_norm=final_norm),
        grid=(m // CMB_TC,),
        in_specs=[pos_spec, pos_spec,
                  pl.BlockSpec((CMB_TC, D_MODEL), lambda i: (i, 0)),
                  pl.BlockSpec((CMB_TC, LANES), lambda i: (i, 0)),
                  pl.BlockSpec((1, D_MODEL), lambda i: (0, 0)),
                  pl.BlockSpec(memory_space=pl.ANY)],
        out_specs=pl.BlockSpec((CMB_TC, D_MODEL), lambda i: (i, 0)),
        out_shape=jax.ShapeDtypeStruct((m, D_MODEL), F32),
        scratch_shapes=[pltpu.VMEM((2, CMB_TC, HALF), jnp.uint32), pltpu.SemaphoreType.DMA(())],
        compiler_params=_cparams(("arbitrary",)),
        name="moe_combine",
    )(pos1.reshape(m // CMB_TC, 1, CMB_TC), pos2.reshape(m // CMB_TC, 1, CMB_TC), x2d, route, g_row, ys)


def _prep_w_in(w):
    o_z_end, o_ab_end, o_sc_end, o_swa_end = 4096, 4112, 7184, 11792
    parts = [w[:, o_swa_end:], w[:, :o_z_end], w[:, o_ab_end:o_sc_end], w[:, o_sc_end:o_swa_end],
             w[:, o_z_end:o_ab_end]]
    used = sum(p.shape[1] for p in parts)
    parts.append(jnp.zeros((w.shape[0], PROJ_COLS - used), w.dtype))
    return jnp.concatenate(parts, axis=1).astype(BF16)


def _lane_row(v, offset=0):
    return jnp.zeros((1, LANES), F32).at[0, offset:offset + v.shape[0]].set(v.astype(F32))


def _layer(x2d, batch, seq, p, final_g_row):
    proj2 = _in_proj(x2d, p["norm_mix_g"], p["w_in"])
    proj3 = proj2.reshape(batch, seq, PROJ_COLS)
    ydn = _deltanet(proj3, p["dn_conv_w"], p["alog_row"], p["dtb_row"], p["dn_norm_g"])
    swa_outs = [_swa_group(proj3, gi) for gi in range(len(SWA_GROUPS))]
    x2d = _merge(x2d, ydn.reshape(batch * seq, DN_WIDTH), proj2, swa_outs, p["sc_conv_w"],
                 p["w_branch_dn"], p["w_branch_sc"], p["w_branch_swa"], p["w_out"], seq)
    hp, route, cnt = _router(x2d, p["norm_ffn_g"], p["w_route"], p["b_route"])
    pos1, pos2, tile_expert, n_active = _moe_plan(route, cnt)
    xs = _dispatch(hp, pos1, pos2)
    ys = _experts(xs, tile_expert, n_active, p["expert_w_gate"], p["expert_w_up"], p["expert_w_down"])
    is_last = final_g_row is not None
    g_row = final_g_row if is_last else p["norm_ffn_g"]
    return _combine(x2d, route, ys, pos1, pos2, g_row, is_last)


def kernel(x, norm_mix_g, w_in, dn_conv_w, dn_a_log, dn_dt_bias, dn_norm_g, sc_conv_w, w_branch_dn, w_branch_sc, w_branch_swa, w_out, norm_ffn_g, router_group_w, router_group_b, router_expert_w, router_expert_b, expert_w_gate, expert_w_up, expert_w_down, final_norm_g):
    batch, seq, _ = x.shape
    depth = w_in.shape[0]
    x2d = x.reshape(batch * seq, D_MODEL)
    for l in range(depth):
        w_route = jnp.concatenate([router_group_w[l], router_expert_w[l]], axis=1)
        w_route = jnp.pad(w_route, ((0, 0), (0, LANES - w_route.shape[1])))
        p = dict(
            norm_mix_g=norm_mix_g[l].reshape(1, D_MODEL),
            w_in=_prep_w_in(w_in[l]),
            dn_conv_w=dn_conv_w[l],
            alog_row=_lane_row(dn_a_log[l]),
            dtb_row=_lane_row(dn_dt_bias[l]),
            dn_norm_g=dn_norm_g[l].reshape(1, HEAD_DIM),
            sc_conv_w=sc_conv_w[l],
            w_branch_dn=w_branch_dn[l].astype(BF16),
            w_branch_sc=w_branch_sc[l].astype(BF16),
            w_branch_swa=w_branch_swa[l].astype(BF16),
            w_out=w_out[l].astype(BF16),
            norm_ffn_g=norm_ffn_g[l].reshape(1, D_MODEL),
            w_route=w_route,
            b_route=_lane_row(jnp.concatenate([router_group_b[l], router_expert_b[l]])),
            expert_w_gate=expert_w_gate[l].astype(BF16),
            expert_w_up=expert_w_up[l].astype(BF16),
            expert_w_down=expert_w_down[l].astype(BF16),
        )
        final_g_row = final_norm_g.reshape(1, D_MODEL) if l == depth - 1 else None
        x2d = _layer(x2d, batch, seq, p, final_g_row)
    return x2d.reshape(batch, seq, D_MODEL)
```

```python
import functools

import jax
import jax.numpy as jnp
import numpy as np
from jax import lax
from jax.experimental import pallas as pl
from jax.experimental.pallas import tpu as pltpu

F32 = jnp.float32
BF16 = jnp.bfloat16

D_MODEL = 2048
RMS_EPS = 1e-6
L2_EPS = 1e-6

DN_HEADS = 8
HEAD_DIM = 128
DN_WIDTH = DN_HEADS * HEAD_DIM
DN_CONV = 4
SC_WIDTH = 1024
SC_CONV = 3
SWA_GROUPS = ((128, 1), (512, 4), (2048, 16))
SWA_HEADS_PER_GROUP = 4
SWA_HEADS = 12
SWA_GROUP_WIDTH = SWA_HEADS_PER_GROUP * HEAD_DIM
SWA_BLOCK = 128
ALIBI_MAX_BIAS = 8.0
MOE_GROUPS = 4
MOE_EPG = 8
MOE_EXPERTS = 32
MOE_FF = 512

OFF_GATE = 0
OFF_DNQ = 6144
OFF_DNK = OFF_DNQ + DN_WIDTH
OFF_DNV = OFF_DNK + DN_WIDTH
OFF_DNZ = OFF_DNV + DN_WIDTH
OFF_SCC = OFF_DNZ + DN_WIDTH
OFF_SCB = OFF_SCC + SC_WIDTH
OFF_SCX = OFF_SCB + SC_WIDTH
OFF_SWA = OFF_SCX + SC_WIDTH
OFF_AB = OFF_SWA + 3 * SWA_HEADS * HEAD_DIM
PROJ_COLS = 18432

LANES = 128
SWA_OUT_SLOTS = SWA_HEADS_PER_GROUP + 1
CHUNK = 128
HALO = 16

VMEM_LIMIT = 56 * 1024 * 1024


def _cparams(sem):
    return pltpu.CompilerParams(dimension_semantics=sem, vmem_limit_bytes=VMEM_LIMIT)


def _sigmoid(x):
    return 1.0 / (1.0 + jnp.exp(-x))


def _silu(x):
    return x * _sigmoid(x)


def _softplus(x):
    return jnp.maximum(x, 0.0) + jnp.log1p(jnp.exp(-jnp.abs(x)))


IN_TM = 1024
IN_TN = 1024
NORM_ROWS = 64


def _rmsnorm_rows(x, g):
    ms = jnp.mean(x * x, axis=-1, keepdims=True)
    return x * lax.rsqrt(ms + RMS_EPS) * g


def _in_proj_kernel(x_ref, g_ref, w_ref, o_ref, h_ref):
    @pl.when(pl.program_id(1) == 0)
    def _():
        def body(r, c):
            rows = pl.ds(pl.multiple_of(r * NORM_ROWS, NORM_ROWS), NORM_ROWS)
            h_ref[rows, :] = _rmsnorm_rows(x_ref[rows, :], g_ref[...]).astype(BF16)
            return c
        lax.fori_loop(0, IN_TM // NORM_ROWS, body, 0)

    o_ref[...] = jnp.dot(h_ref[...], w_ref[...], preferred_element_type=F32).astype(o_ref.dtype)


def _in_proj(x2d, g_row, w_bf16):
    m = x2d.shape[0]
    return pl.pallas_call(
        _in_proj_kernel,
        grid=(m // IN_TM, PROJ_COLS // IN_TN),
        in_specs=[
            pl.BlockSpec((IN_TM, D_MODEL), lambda i, j: (i, 0)),
            pl.BlockSpec((1, D_MODEL), lambda i, j: (0, 0)),
            pl.BlockSpec((D_MODEL, IN_TN), lambda i, j: (0, j)),
        ],
        out_specs=pl.BlockSpec((IN_TM, IN_TN), lambda i, j: (i, j)),
        out_shape=jax.ShapeDtypeStruct((m, PROJ_COLS), BF16),
        scratch_shapes=[pltpu.VMEM((IN_TM, D_MODEL), BF16)],
        compiler_params=_cparams(("parallel", "arbitrary")),
        name="in_proj",
    )(x2d, g_row, w_bf16)


DN_HB = 4
DN_TS = 512
DN_W = DN_HB * HEAD_DIM
DN_DOUBLINGS = CHUNK.bit_length() - 2


def _dot_nt(a, b):
    return lax.dot_general(a, b, (((1,), (1,)), ((), ())), preferred_element_type=F32)


def _dot_tn(a, b):
    return lax.dot_general(a, b, (((0,), (0,)), ((), ())), preferred_element_type=F32)


def _causal_conv(xx, w, width):
    acc = None
    for j in range(width):
        shift = width - 1 - j
        xs = xx if shift == 0 else pltpu.roll(xx, shift, axis=0)
        term = xs[HALO:, :] * w[j:j + 1, :]
        acc = term if acc is None else acc + term
    return acc


def _deltanet_kernel(q_ref, k_ref, v_ref, z_ref, ab_ref, cwq_ref, cwk_ref, cwv_ref,
                     alog_ref, dtb_ref, ng_ref, o_ref,
                     xpad_ref, state_ref):
    hg = pl.program_id(1)
    st = pl.program_id(2)

    @pl.when(st == 0)
    def _():
        xpad_ref[:, 0:HALO, :] = jnp.zeros((3, HALO, DN_W), BF16)
        state_ref[...] = jnp.zeros_like(state_ref)

    xpad_ref[0, HALO:, :] = q_ref[0]
    xpad_ref[1, HALO:, :] = k_ref[0]
    xpad_ref[2, HALO:, :] = v_ref[0]

    row_i = lax.broadcasted_iota(jnp.int32, (CHUNK, CHUNK), 0)
    col_j = lax.broadcasted_iota(jnp.int32, (CHUNK, CHUNK), 1)
    causal = row_i >= col_j
    strict = row_i > col_j
    tri = causal.astype(F32)
    eye = (row_i == col_j).astype(F32)
    lane = lax.broadcasted_iota(jnp.int32, (CHUNK, LANES), 1)
    neg_decay_rate = -jnp.exp(alog_ref[...])
    cws = (cwq_ref[...], cwk_ref[...], cwv_ref[...])
    ng = ng_ref[...]

    def chunk_body(c, carry):
        r0 = pl.multiple_of(c * CHUNK, CHUNK)
        win = pl.ds(r0, CHUNK + HALO)
        rows = pl.ds(r0, CHUNK)
        conv = [_silu(_causal_conv(xpad_ref[t, win, :].astype(F32), cws[t], DN_CONV)) for t in range(3)]
        ab = ab_ref[0, rows, :].astype(F32)
        g_raw = neg_decay_rate * _softplus(ab + dtb_ref[...])
        g_cum = jnp.dot(tri, g_raw, preferred_element_type=F32, precision=lax.Precision.HIGHEST)
        beta_all = _sigmoid(ab)
        z = z_ref[0, rows, :].astype(F32)

        heads = range(DN_HB)
        sls = [slice(i * HEAD_DIM, (i + 1) * HEAD_DIM) for i in heads]
        dot = functools.partial(jnp.dot, preferred_element_type=F32)

        g_col = [jnp.sum(jnp.where(lane == hg * DN_HB + i, g_cum, 0.0), axis=-1, keepdims=True) for i in heads]
        beta = [jnp.sum(jnp.where(lane == hg * DN_HB + i + DN_HEADS, beta_all, 0.0), axis=-1, keepdims=True)
                for i in heads]
        g_last = [g[CHUNK - 1:CHUNK, :] for g in g_col]
        eg = [jnp.exp(g) for g in g_col]
        ek = [jnp.exp(gl - g) for gl, g in zip(g_last, g_col)]
        g_b = [jnp.broadcast_to(g, (CHUNK, CHUNK)) for g in g_col]
        decay = [jnp.exp(jnp.where(causal, gb - gb.T, -jnp.inf)) for gb in g_b]

        qf = [conv[0][:, sl] for sl in sls]
        kf = [conv[1][:, sl] for sl in sls]
        vf = [conv[2][:, sl] for sl in sls]
        q = [x * lax.rsqrt(jnp.sum(x * x, axis=-1, keepdims=True) + L2_EPS) * (HEAD_DIM ** -0.5) for x in qf]
        k = [x * lax.rsqrt(jnp.sum(x * x, axis=-1, keepdims=True) + L2_EPS) for x in kf]
        kb = [ki * bi for ki, bi in zip(k, beta)]
        a2 = [_dot_nt(jnp.concatenate([qi, kbi], axis=0).astype(BF16), ki.astype(BF16))
              for qi, kbi, ki in zip(q, kb, k)]
        attn = [a[:CHUNK] * d for a, d in zip(a2, decay)]
        n_mat = [jnp.where(strict, -(a[CHUNK:] * d), 0.0) for a, d in zip(a2, decay)]

        u_mat = [eye + n for n in n_mat]
        p_mat = [dot(n.astype(BF16), n.astype(BF16)) for n in n_mat]
        for _ in range(DN_DOUBLINGS - 1):
            up = [dot(jnp.concatenate([u.astype(BF16), p.astype(BF16)], axis=0), p.astype(BF16))
                  for u, p in zip(u_mat, p_mat)]
            u_mat = [u + x[:CHUNK] for u, x in zip(u_mat, up)]
            p_mat = [x[CHUNK:] for x in up]
        u_mat = [u + dot(u.astype(BF16), p.astype(BF16)) for u, p in zip(u_mat, p_mat)]
        uw = [dot(u.astype(BF16), jnp.concatenate([vi * bi, kbi * egi], axis=1).astype(BF16))
              for u, vi, bi, kbi, egi in zip(u_mat, vf, beta, kb, eg)]

        s_old = [state_ref[i] for i in heads]
        qw = [dot(jnp.concatenate([qi * egi, x[:, HEAD_DIM:]], axis=0).astype(BF16), s.astype(BF16))
              for qi, egi, x, s in zip(q, eg, uw, s_old)]
        v16 = [(x[:, :HEAD_DIM] - y[CHUNK:]).astype(BF16) for x, y in zip(uw, qw)]
        o = [y[:CHUNK] + dot(a.astype(BF16), v) for y, a, v in zip(qw, attn, v16)]
        s_new = [s * jnp.exp(gl) + _dot_tn((ki * eki).astype(BF16), v)
                 for s, gl, ki, eki, v in zip(s_old, g_last, k, ek, v16)]
        for i in heads:
            state_ref[i] = s_new[i]
            on = o[i] * lax.rsqrt(jnp.mean(o[i] * o[i], axis=-1, keepdims=True) + RMS_EPS) * ng
            o_ref[0, rows, sls[i]] = (on * _silu(z[:, sls[i]])).astype(o_ref.dtype)
        return carry

    lax.fori_loop(0, DN_TS // CHUNK, chunk_body, 0)

    tail = pl.ds(DN_TS, HALO)
    head = pl.ds(0, HALO)
    for t in range(3):
        xpad_ref[t, head, :] = xpad_ref[t, tail, :]


def _deltanet(proj3, conv_w, alog_row, dtb_row, ng_row):
    b, s, _ = proj3.shape
    qb, kb_, vb, zb = (off // DN_W for off in (OFF_DNQ, OFF_DNK, OFF_DNV, OFF_DNZ))
    act = lambda base: pl.BlockSpec((1, DN_TS, DN_W), lambda bi, hg, st: (bi, st, base + hg))
    cw = lambda base: pl.BlockSpec((DN_CONV, DN_W), lambda bi, hg, st: (0, base + hg))
    row = pl.BlockSpec((1, LANES), lambda bi, hg, st: (0, 0))
    return pl.pallas_call(
        _deltanet_kernel,
        grid=(b, DN_HEADS // DN_HB, s // DN_TS),
        in_specs=[
            act(qb), act(kb_), act(vb), act(zb),
            pl.BlockSpec((1, DN_TS, LANES), lambda bi, hg, st: (bi, st, OFF_AB // LANES)),
            cw(0), cw(DN_WIDTH // DN_W), cw(2 * DN_WIDTH // DN_W),
            row, row, row,
        ],
        out_specs=pl.BlockSpec((1, DN_TS, DN_W), lambda bi, hg, st: (bi, st, hg)),
        out_shape=jax.ShapeDtypeStruct((b, s, DN_WIDTH), BF16),
        scratch_shapes=[
            pltpu.VMEM((3, DN_TS + HALO, DN_W), BF16),
            pltpu.VMEM((DN_HB, HEAD_DIM, HEAD_DIM), F32),
        ],
        compiler_params=_cparams(("parallel", "parallel", "arbitrary")),
        name="deltanet",
    )(proj3, proj3, proj3, proj3, proj3, conv_w, conv_w, conv_w, alog_row, dtb_row, ng_row)


SWA_CAST_ROWS = 256


def _swa_kernel(q_ref, k_ref, v_ref, o_ref, *scratch, seq, dil, slopes):
    n_blocks = seq // dil // SWA_BLOCK
    qi = lax.broadcasted_iota(jnp.int32, (SWA_BLOCK, SWA_BLOCK), 0)
    kj = lax.broadcasted_iota(jnp.int32, (SWA_BLOCK, SWA_BLOCK), 1)
    step_cur = (qi - kj).astype(F32)
    step_prev = step_cur + float(SWA_BLOCK)
    ok_cur = qi >= kj
    ok_prev = kj >= qi
    lane = lax.broadcasted_iota(jnp.int32, (SWA_BLOCK, LANES), 1)
    scale = HEAD_DIM ** -0.5
    heads = range(SWA_HEADS_PER_GROUP)
    sls = [slice(hh * HEAD_DIM, (hh + 1) * HEAD_DIM) for hh in heads]
    m_h = [float(slopes[hh]) * float(dil) for hh in heads]

    if dil > 1:
        (f32_ref,) = scratch

        def cast_body(c, carry):
            rows = pl.ds(pl.multiple_of(c * SWA_CAST_ROWS, SWA_CAST_ROWS), SWA_CAST_ROWS)
            for t, ref in enumerate((q_ref, k_ref, v_ref)):
                for hh in heads:
                    f32_ref[t * SWA_HEADS_PER_GROUP + hh, rows, :] = ref[0, rows, sls[hh]].astype(F32)
            return carry
        lax.fori_loop(0, seq // SWA_CAST_ROWS, cast_body, 0)

        def load(t, rows):
            return [f32_ref[t * SWA_HEADS_PER_GROUP + hh, rows, :].astype(BF16) for hh in heads]
    else:
        refs = (q_ref, k_ref, v_ref)

        def load(t, rows):
            return [refs[t][0, rows, sls[hh]] for hh in heads]

    def block_body(it, carry):
        r = it // n_blocks
        n = it % n_blocks
        base = n * (SWA_BLOCK * dil) + r
        prev_base = jnp.maximum(n - 1, 0) * (SWA_BLOCK * dil) + r
        if dil > 1:
            rows = pl.ds(base, SWA_BLOCK, stride=dil)
            prev_rows = pl.ds(prev_base, SWA_BLOCK, stride=dil)
        else:
            rows = pl.ds(pl.multiple_of(base, SWA_BLOCK), SWA_BLOCK)
            prev_rows = pl.ds(pl.multiple_of(prev_base, SWA_BLOCK), SWA_BLOCK)
        qn, kc, vc = load(0, rows), load(1, rows), load(2, rows)
        kp, vp = load(1, prev_rows), load(2, prev_rows)
        ok_p = jnp.logical_and(ok_prev, n > 0)

        s_cur = [jnp.where(ok_cur, _dot_nt(qh, kh) * scale - m * step_cur, -jnp.inf)
                 for qh, kh, m in zip(qn, kc, m_h)]
        s_prev = [jnp.where(ok_p, _dot_nt(qh, kh) * scale - m * step_prev, -jnp.inf)
                  for qh, kh, m in zip(qn, kp, m_h)]
        mx = [jnp.maximum(jnp.max(a, axis=-1, keepdims=True), jnp.max(b, axis=-1, keepdims=True))
              for a, b in zip(s_cur, s_prev)]
        e_cur = [jnp.exp(a - m) for a, m in zip(s_cur, mx)]
        e_prev = [jnp.exp(b - m) for b, m in zip(s_prev, mx)]
        den = [jnp.sum(a, axis=-1, keepdims=True) + jnp.sum(b, axis=-1, keepdims=True)
               for a, b in zip(e_cur, e_prev)]
        pv = [jnp.dot(a.astype(BF16), vch, preferred_element_type=F32)
              + jnp.dot(b.astype(BF16), vph, preferred_element_type=F32)
              for a, b, vch, vph in zip(e_cur, e_prev, vc, vp)]
        lse_tile = jnp.zeros((SWA_BLOCK, LANES), F32)
        for hh in heads:
            o_ref[0, hh, rows, :] = pv[hh] * (1.0 / den[hh])
            lse_tile = jnp.where(lane == hh, mx[hh] + jnp.log(den[hh]), lse_tile)
        o_ref[0, SWA_HEADS_PER_GROUP, rows, :] = lse_tile
        return carry

    lax.fori_loop(0, seq // SWA_BLOCK, block_body, 0)


def _swa_group(proj3, gi):
    batch, seq, _ = proj3.shape
    window, dil = SWA_GROUPS[gi]
    assert window // dil == SWA_BLOCK and seq % (dil * SWA_BLOCK) == 0
    slopes = 2.0 ** (-ALIBI_MAX_BIAS * np.arange(1, SWA_HEADS + 1) / SWA_HEADS)
    slopes = slopes.reshape(len(SWA_GROUPS), SWA_HEADS_PER_GROUP)[gi]

    def spec(which):
        cb = (OFF_SWA + which * SWA_HEADS * HEAD_DIM) // SWA_GROUP_WIDTH + gi
        return pl.BlockSpec((1, seq, SWA_GROUP_WIDTH), lambda bi: (bi, 0, cb))

    scratch = [pltpu.VMEM((3 * SWA_HEADS_PER_GROUP, seq, HEAD_DIM), F32)] if dil > 1 else []
    return pl.pallas_call(
        functools.partial(_swa_kernel, seq=seq, dil=dil, slopes=tuple(float(s) for s in slopes)),
        grid=(batch,),
        in_specs=[spec(0), spec(1), spec(2)],
        out_specs=pl.BlockSpec((1, SWA_OUT_SLOTS, seq, HEAD_DIM), lambda bi: (bi, 0, 0, 0)),
        out_shape=jax.ShapeDtypeStruct((batch, SWA_OUT_SLOTS, seq, HEAD_DIM), F32),
        scratch_shapes=scratch,
        compiler_params=_cparams(("parallel",)),
        name=f"swa_g{gi}",
    )(proj3, proj3, proj3)


MG_TM = 256


def _merge_kernel(x_ref, ydn_ref, gate_ref, scc_ref, scb_ref, scx_ref, scc_h_ref, scx_h_ref,
                  s0_ref, s1_ref, s2_ref, scw_ref, wdn_ref, wsc_ref, wswa_ref, wout_ref, o_ref,
                  *, tiles_per_seq):
    i = pl.program_id(0)
    first = (i % tiles_per_seq) == 0

    cx_cur = scc_ref[...].astype(F32) * scx_ref[...].astype(F32)
    cx_halo = scc_h_ref[...].astype(F32) * scx_h_ref[...].astype(F32)
    cx_halo = jnp.where(first, 0.0, cx_halo)
    cx = jnp.concatenate([cx_halo, cx_cur], axis=0)
    y_sc = scb_ref[...].astype(F32) * _causal_conv(cx, scw_ref[...], SC_CONV)

    outs = (s0_ref, s1_ref, s2_ref)
    lses = [o[0, SWA_HEADS_PER_GROUP] for o in outs]
    mx = jnp.maximum(jnp.maximum(lses[0], lses[1]), lses[2])
    es = [jnp.exp(l - mx) for l in lses]
    inv = 1.0 / (es[0] + es[1] + es[2])
    heads = []
    for hh in range(SWA_HEADS_PER_GROUP):
        acc = None
        for gi in range(3):
            wt = (es[gi] * inv)[:, hh:hh + 1]
            term = wt * outs[gi][0, hh]
            acc = term if acc is None else acc + term
        heads.append(acc)
    y_swa = jnp.concatenate(heads, axis=1)

    gates = gate_ref[...].astype(F32)
    merged = (_sigmoid(gates[:, 0:D_MODEL])
              * jnp.dot(ydn_ref[...], wdn_ref[...], preferred_element_type=F32)
              + _sigmoid(gates[:, D_MODEL:2 * D_MODEL])
              * jnp.dot(y_sc.astype(BF16), wsc_ref[...], preferred_element_type=F32)
              + _sigmoid(gates[:, 2 * D_MODEL:3 * D_MODEL])
              * jnp.dot(y_swa.astype(BF16), wswa_ref[...], preferred_element_type=F32))
    o_ref[...] = x_ref[...] + jnp.dot(merged.astype(BF16), wout_ref[...], preferred_element_type=F32)


def _merge(x2d, ydn2, proj2, swa_outs, sc_conv_w, wdn, wsc, wswa, wout, seq):
    m = x2d.shape[0]
    tiles_per_seq = seq // MG_TM
    halo_blocks = MG_TM // HALO
    rows = lambda w, cb: pl.BlockSpec((MG_TM, w), lambda i: (i, cb))
    halo = lambda cb: pl.BlockSpec((HALO, SC_WIDTH), lambda i: (jnp.maximum(i * halo_blocks - 1, 0), cb))
    full = lambda a: pl.BlockSpec(a.shape, lambda i: (0, 0), pipeline_mode=pl.Buffered(1))
    swa = pl.BlockSpec((1, SWA_OUT_SLOTS, MG_TM, HEAD_DIM),
                       lambda i: (i // tiles_per_seq, 0, i % tiles_per_seq, 0))
    return pl.pallas_call(
        functools.partial(_merge_kernel, tiles_per_seq=tiles_per_seq),
        grid=(m // MG_TM,),
        in_specs=[
            rows(D_MODEL, 0),
            rows(DN_WIDTH, 0),
            rows(3 * D_MODEL, OFF_GATE // (3 * D_MODEL)),
            rows(SC_WIDTH, OFF_SCC // SC_WIDTH), rows(SC_WIDTH, OFF_SCB // SC_WIDTH),
            rows(SC_WIDTH, OFF_SCX // SC_WIDTH),
            halo(OFF_SCC // SC_WIDTH), halo(OFF_SCX // SC_WIDTH),
            swa, swa, swa,
            full(sc_conv_w), full(wdn), full(wsc), full(wswa), full(wout),
        ],
        out_specs=rows(D_MODEL, 0),
        out_shape=jax.ShapeDtypeStruct((m, D_MODEL), F32),
        compiler_params=_cparams(("parallel",)),
        name="merge",
    )(x2d, ydn2, proj2, proj2, proj2, proj2, proj2, proj2, *swa_outs, sc_conv_w, wdn, wsc, wswa, wout)


RT_TM = 512
ROUTE_LANE0 = MOE_GROUPS
R_E1, R_E2, R_W1, R_W2, R_RANK1, R_RANK2 = range(6)
HALF = D_MODEL // 2


def _pack_pair(lo, hi):
    lo_b = lax.bitcast_convert_type(lo.astype(BF16).astype(F32), jnp.uint32)
    hi_b = lax.bitcast_convert_type(hi.astype(BF16).astype(F32), jnp.uint32)
    return (hi_b & jnp.uint32(0xFFFF0000)) | (lo_b >> jnp.uint32(16))


def _unpack_pair(p):
    lo = lax.bitcast_convert_type(p << jnp.uint32(16), F32)
    hi = lax.bitcast_convert_type(p & jnp.uint32(0xFFFF0000), F32)
    return lo, hi


def _router_kernel(x_ref, g_ref, wr_ref, br_ref, hp_ref, route_ref, cnt_ref, carry_ref):
    @pl.when(pl.program_id(0) == 0)
    def _():
        carry_ref[...] = jnp.zeros_like(carry_ref)

    h = _rmsnorm_rows(x_ref[...], g_ref[...])
    hp_ref[...] = _pack_pair(h[:, :HALF], h[:, HALF:])
    logits = jnp.dot(h, wr_ref[...], preferred_element_type=F32, precision=lax.Precision.HIGHEST) + br_ref[...]
    lane = lax.broadcasted_iota(jnp.int32, logits.shape, 1)
    big = jnp.int32(LANES)
    neg = -jnp.inf

    is_grp = lane < MOE_GROUPS
    gl = jnp.where(is_grp, logits, neg)
    gmax = jnp.max(gl, axis=-1, keepdims=True)
    grp = jnp.min(jnp.where(gl == gmax, lane, big), axis=-1, keepdims=True)
    g_w = 1.0 / jnp.sum(jnp.exp(gl - gmax), axis=-1, keepdims=True)

    lo = ROUTE_LANE0 + grp * MOE_EPG
    in_grp = jnp.logical_and(lane >= lo, lane < lo + MOE_EPG)
    el = jnp.where(in_grp, logits, neg)
    m1 = jnp.max(el, axis=-1, keepdims=True)
    i1 = jnp.min(jnp.where(el == m1, lane, big), axis=-1, keepdims=True)
    el2 = jnp.where(lane == i1, neg, el)
    m2 = jnp.max(el2, axis=-1, keepdims=True)
    i2 = jnp.min(jnp.where(el2 == m2, lane, big), axis=-1, keepdims=True)
    e2 = jnp.exp(m2 - m1)
    w1 = g_w / (1.0 + e2)
    w2 = g_w * e2 / (1.0 + e2)

    tm = logits.shape[0]
    onehot = jnp.where(jnp.logical_or(lane == i1, lane == i2), 1.0, 0.0)
    earlier = (lax.broadcasted_iota(jnp.int32, (tm, tm), 0) > lax.broadcasted_iota(jnp.int32, (tm, tm), 1))
    before = carry_ref[...] + jnp.dot(jnp.where(earlier, 1.0, 0.0).astype(BF16), onehot.astype(BF16),
                                      preferred_element_type=F32)
    rank1 = jnp.sum(jnp.where(lane == i1, before, 0.0), axis=-1, keepdims=True)
    rank2 = jnp.sum(jnp.where(lane == i2, before, 0.0), axis=-1, keepdims=True)
    carry_ref[...] += jnp.sum(onehot, axis=0, keepdims=True)
    cnt_ref[...] = carry_ref[...]

    rec = jnp.zeros_like(logits)
    for ln, val in ((R_E1, (i1 - ROUTE_LANE0).astype(F32)), (R_E2, (i2 - ROUTE_LANE0).astype(F32)),
                    (R_W1, w1), (R_W2, w2), (R_RANK1, rank1), (R_RANK2, rank2)):
        rec = jnp.where(lane == ln, val, rec)
    route_ref[...] = rec


def _router(x2d, g_row, w_route, b_route):
    m = x2d.shape[0]
    return pl.pallas_call(
        _router_kernel,
        grid=(m // RT_TM,),
        in_specs=[
            pl.BlockSpec((RT_TM, D_MODEL), lambda i: (i, 0)),
            pl.BlockSpec((1, D_MODEL), lambda i: (0, 0)),
            pl.BlockSpec((D_MODEL, LANES), lambda i: (0, 0)),
            pl.BlockSpec((1, LANES), lambda i: (0, 0)),
        ],
        out_specs=[pl.BlockSpec((RT_TM, HALF), lambda i: (i, 0)),
                   pl.BlockSpec((RT_TM, LANES), lambda i: (i, 0)),
                   pl.BlockSpec((1, LANES), lambda i: (0, 0))],
        out_shape=[jax.ShapeDtypeStruct((m, HALF), jnp.uint32), jax.ShapeDtypeStruct((m, LANES), F32),
                   jax.ShapeDtypeStruct((1, LANES), F32)],
        scratch_shapes=[pltpu.VMEM((1, LANES), F32)],
        compiler_params=_cparams(("arbitrary",)),
        name="router",
    )(x2d, g_row, w_route, b_route)


EXP_TM = 256


def _n_row_tiles(n_tokens):
    return 2 * n_tokens // EXP_TM + MOE_EXPERTS


def _positions_kernel(route_ref, offs_ref, pos_ref):
    rec = route_ref[...]
    lane = lax.broadcasted_iota(jnp.int32, rec.shape, 1)
    offs = offs_ref[...]

    def first_row(e):
        return jnp.sum(jnp.where(lane == e.astype(jnp.int32), offs, 0.0), axis=-1, keepdims=True)

    pos1 = first_row(rec[:, R_E1:R_E1 + 1]) + rec[:, R_RANK1:R_RANK1 + 1]
    pos2 = first_row(rec[:, R_E2:R_E2 + 1]) + rec[:, R_RANK2:R_RANK2 + 1]
    pos_ref[...] = jnp.where(lane == 0, pos1, jnp.where(lane == 1, pos2, 0.0)).astype(jnp.int32)


def _moe_plan(route, cnt):
    m = route.shape[0]
    n_tiles = _n_row_tiles(m)
    counts = cnt[0, ROUTE_LANE0:ROUTE_LANE0 + MOE_EXPERTS].astype(jnp.int32)
    padded = (counts + EXP_TM - 1) // EXP_TM * EXP_TM
    ends = jnp.cumsum(padded)
    offs = ends - padded
    pos = pl.pallas_call(
        _positions_kernel,
        grid=(m // RT_TM,),
        in_specs=[pl.BlockSpec((RT_TM, LANES), lambda i: (i, 0)), pl.BlockSpec((1, LANES), lambda i: (0, 0))],
        out_specs=pl.BlockSpec((RT_TM, LANES), lambda i: (i, 0)),
        out_shape=jax.ShapeDtypeStruct((m, LANES), jnp.int32),
        compiler_params=_cparams(("parallel",)),
        name="moe_positions",
    )(route, _lane_row(offs))
    n_active = ends[-1] // EXP_TM
    tile_row = jnp.maximum(jnp.minimum(jnp.arange(n_tiles), n_active - 1), 0) * EXP_TM
    tile_expert = jnp.minimum(jnp.sum(ends[None, :] <= tile_row[:, None], axis=1), MOE_EXPERTS - 1)
    return pos[:, 0], pos[:, 1], tile_expert.astype(jnp.int32), n_active.astype(jnp.int32).reshape(1)


DSP_TB = 512
ROW_UNROLL = 8


def _dispatch_kernel(pos1_ref, pos2_ref, hp_ref, xs_in_ref, xs_ref, sem):
    del xs_in_ref

    def start(j8, c):
        for u in range(ROW_UNROLL):
            j = j8 * ROW_UNROLL + u
            src = hp_ref.at[pl.ds(j, 1), :]
            pltpu.make_async_copy(src, xs_ref.at[pl.ds(pos1_ref[0, 0, j], 1), :], sem).start(priority=0)
            pltpu.make_async_copy(src, xs_ref.at[pl.ds(pos2_ref[0, 0, j], 1), :], sem).start(priority=1)
        return c

    lax.fori_loop(0, DSP_TB // ROW_UNROLL, start, 0)
    for _ in range(2):
        pltpu.make_async_copy(hp_ref, xs_ref.at[pl.ds(0, DSP_TB), :], sem).wait()


def _dispatch(hp, pos1, pos2):
    m = hp.shape[0]
    n_rows = _n_row_tiles(m) * EXP_TM
    pos_spec = pl.BlockSpec((1, 1, DSP_TB), lambda i: (i, 0, 0), memory_space=pltpu.SMEM)
    return pl.pallas_call(
        _dispatch_kernel,
        grid=(m // DSP_TB,),
        in_specs=[pos_spec, pos_spec,
                  pl.BlockSpec((DSP_TB, HALF), lambda i: (i, 0)),
                  pl.BlockSpec(memory_space=pl.ANY)],
        out_specs=pl.BlockSpec(memory_space=pl.ANY),
        out_shape=jax.ShapeDtypeStruct((n_rows, HALF), jnp.uint32),
        scratch_shapes=[pltpu.SemaphoreType.DMA(())],
        input_output_aliases={3: 0},
        compiler_params=_cparams(("arbitrary",)),
        name="moe_dispatch",
    )(pos1.reshape(m // DSP_TB, 1, DSP_TB), pos2.reshape(m // DSP_TB, 1, DSP_TB), hp,
      jnp.zeros((n_rows, HALF), jnp.uint32))


W_CAST_ROWS = 256


def _cast_rows(src_ref, dst_ref):
    n_rows = dst_ref.shape[0]
    slab = min(W_CAST_ROWS, n_rows)

    def body(r, c):
        rows = pl.ds(pl.multiple_of(r * slab, slab), slab)
        dst_ref[rows, :] = src_ref[0, 0, rows, :].astype(BF16)
        return c
    lax.fori_loop(0, n_rows // slab, body, 0)


def _expert_kernel(te_ref, nact_ref, xs_ref, wg_ref, wu_ref, wd_ref, ys_ref, wg16_ref, wu16_ref, wd16_ref):
    i = pl.program_id(0)
    active = i < nact_ref[0]
    new_expert = jnp.logical_or(i == 0, te_ref[i] != te_ref[jnp.maximum(i - 1, 0)])

    @pl.when(jnp.logical_and(active, new_expert))
    def _():
        _cast_rows(wg_ref, wg16_ref)
        _cast_rows(wu_ref, wu16_ref)
        _cast_rows(wd_ref, wd16_ref)

    @pl.when(active)
    def _():
        lo, hi = _unpack_pair(xs_ref[...])
        lo, hi = lo.astype(BF16), hi.astype(BF16)
        dot = functools.partial(jnp.dot, preferred_element_type=F32)
        gate = dot(lo, wg16_ref[:HALF, :]) + dot(hi, wg16_ref[HALF:, :])
        up = dot(lo, wu16_ref[:HALF, :]) + dot(hi, wu16_ref[HALF:, :])
        y = dot((_silu(gate) * up).astype(BF16), wd16_ref[...])
        ys_ref[...] = _pack_pair(y[:, :HALF], y[:, HALF:])

    @pl.when(jnp.logical_not(active))
    def _():
        ys_ref[...] = jnp.zeros_like(ys_ref)


def _experts(xs, tile_expert, n_active, wg, wu, wd, layer):
    n_tiles = xs.shape[0] // EXP_TM
    rows = pl.BlockSpec((EXP_TM, HALF), lambda i, te, na: (jnp.maximum(jnp.minimum(i, na[0] - 1), 0), 0))
    out_rows = pl.BlockSpec((EXP_TM, HALF), lambda i, te, na: (i, 0))
    w_in_spec = pl.BlockSpec((1, 1, D_MODEL, MOE_FF), lambda i, te, na: (layer, te[i], 0, 0))
    w_out_spec = pl.BlockSpec((1, 1, MOE_FF, D_MODEL), lambda i, te, na: (layer, te[i], 0, 0))
    return pl.pallas_call(
        _expert_kernel,
        grid_spec=pltpu.PrefetchScalarGridSpec(
            num_scalar_prefetch=2,
            grid=(n_tiles,),
            in_specs=[rows, w_in_spec, w_in_spec, w_out_spec],
            out_specs=out_rows,
            scratch_shapes=[pltpu.VMEM((D_MODEL, MOE_FF), BF16), pltpu.VMEM((D_MODEL, MOE_FF), BF16),
                            pltpu.VMEM((MOE_FF, D_MODEL), BF16)],
        ),
        out_shape=jax.ShapeDtypeStruct(xs.shape, jnp.uint32),
        compiler_params=_cparams(("arbitrary",)),
        name="moe_experts",
    )(tile_expert, n_active, xs, wg, wu, wd)


CMB_TC = 256


def _combine_kernel(pos1_ref, pos2_ref, pos1_next_ref, pos2_next_ref, x_ref, route_ref, g_ref, ys_ref,
                    o_ref, buf_ref, sem, *, final_norm):
    i = pl.program_id(0)
    slot = i % 2

    def gather(p1_ref, p2_ref, s):
        def start(j8, c):
            for u in range(ROW_UNROLL):
                j = j8 * ROW_UNROLL + u
                for k, p_ref in enumerate((p1_ref, p2_ref)):
                    pltpu.make_async_copy(ys_ref.at[pl.ds(p_ref[0, 0, j], 1), :],
                                          buf_ref.at[s, k, pl.ds(j, 1), :], sem.at[s]).start(priority=k)
            return c
        lax.fori_loop(0, CMB_TC // ROW_UNROLL, start, 0)

    @pl.when(i == 0)
    def _():
        gather(pos1_ref, pos2_ref, 0)

    @pl.when(i + 1 < pl.num_programs(0))
    def _():
        gather(pos1_next_ref, pos2_next_ref, 1 - slot)

    for k in range(2):
        pltpu.make_async_copy(ys_ref.at[pl.ds(0, CMB_TC), :], buf_ref.at[slot, k], sem.at[slot]).wait()

    rec = route_ref[...]
    w1, w2 = rec[:, R_W1:R_W1 + 1], rec[:, R_W2:R_W2 + 1]
    lo1, hi1 = _unpack_pair(buf_ref[slot, 0])
    lo2, hi2 = _unpack_pair(buf_ref[slot, 1])
    out_lo = x_ref[:, :HALF] + w1 * lo1 + w2 * lo2
    out_hi = x_ref[:, HALF:] + w1 * hi1 + w2 * hi2
    if final_norm:
        ms = (jnp.sum(out_lo * out_lo, axis=-1, keepdims=True)
              + jnp.sum(out_hi * out_hi, axis=-1, keepdims=True)) * (1.0 / D_MODEL)
        inv = lax.rsqrt(ms + RMS_EPS)
        out_lo = out_lo * inv * g_ref[:, :HALF]
        out_hi = out_hi * inv * g_ref[:, HALF:]
    o_ref[:, :HALF] = out_lo
    o_ref[:, HALF:] = out_hi


def _combine(x2d, route, ys, pos1, pos2, g_row, final_norm):
    m = x2d.shape[0]
    n_steps = m // CMB_TC
    pos_spec = pl.BlockSpec((1, 1, CMB_TC), lambda i: (i, 0, 0), memory_space=pltpu.SMEM)
    next_spec = pl.BlockSpec((1, 1, CMB_TC), lambda i: (jnp.minimum(i + 1, n_steps - 1), 0, 0),
                             memory_space=pltpu.SMEM)
    pos1, pos2 = pos1.reshape(n_steps, 1, CMB_TC), pos2.reshape(n_steps, 1, CMB_TC)
    return pl.pallas_call(
        functools.partial(_combine_kernel, final_norm=final_norm),
        grid=(n_steps,),
        in_specs=[pos_spec, pos_spec, next_spec, next_spec,
                  pl.BlockSpec((CMB_TC, D_MODEL), lambda i: (i, 0)),
                  pl.BlockSpec((CMB_TC, LANES), lambda i: (i, 0)),
                  pl.BlockSpec((1, D_MODEL), lambda i: (0, 0)),
                  pl.BlockSpec(memory_space=pl.ANY)],
        out_specs=pl.BlockSpec((CMB_TC, D_MODEL), lambda i: (i, 0)),
        out_shape=jax.ShapeDtypeStruct((m, D_MODEL), F32),
        scratch_shapes=[pltpu.VMEM((2, 2, CMB_TC, HALF), jnp.uint32), pltpu.SemaphoreType.DMA((2,))],
        compiler_params=_cparams(("arbitrary",)),
        name="moe_combine",
    )(pos1, pos2, pos1, pos2, x2d, route, g_row, ys)


def _prep_w_in(w):
    o_z_end, o_ab_end, o_sc_end, o_swa_end = 4096, 4112, 7184, 11792
    parts = [w[:, o_swa_end:], w[:, :o_z_end], w[:, o_ab_end:o_sc_end], w[:, o_sc_end:o_swa_end],
             w[:, o_z_end:o_ab_end]]
    used = sum(p.shape[1] for p in parts)
    parts.append(jnp.zeros((w.shape[0], PROJ_COLS - used), w.dtype))
    return jnp.concatenate(parts, axis=1).astype(BF16)


def _lane_row(v, offset=0):
    return jnp.zeros((1, LANES), F32).at[0, offset:offset + v.shape[0]].set(v.astype(F32))


def _layer(x2d, batch, seq, p, final_g_row):
    proj2 = _in_proj(x2d, p["norm_mix_g"], p["w_in"])
    proj3 = proj2.reshape(batch, seq, PROJ_COLS)
    ydn = _deltanet(proj3, p["dn_conv_w"], p["alog_row"], p["dtb_row"], p["dn_norm_g"])
    swa_outs = [_swa_group(proj3, gi) for gi in range(len(SWA_GROUPS))]
    x2d = _merge(x2d, ydn.reshape(batch * seq, DN_WIDTH), proj2, swa_outs, p["sc_conv_w"],
                 p["w_branch_dn"], p["w_branch_sc"], p["w_branch_swa"], p["w_out"], seq)
    hp, route, cnt = _router(x2d, p["norm_ffn_g"], p["w_route"], p["b_route"])
    pos1, pos2, tile_expert, n_active = _moe_plan(route, cnt)
    xs = _dispatch(hp, pos1, pos2)
    ys = _experts(xs, tile_expert, n_active, p["expert_w_gate"], p["expert_w_up"], p["expert_w_down"], p["layer"])
    is_last = final_g_row is not None
    g_row = final_g_row if is_last else p["norm_ffn_g"]
    return _combine(x2d, route, ys, pos1, pos2, g_row, is_last)


def kernel(x, norm_mix_g, w_in, dn_conv_w, dn_a_log, dn_dt_bias, dn_norm_g, sc_conv_w, w_branch_dn, w_branch_sc, w_branch_swa, w_out, norm_ffn_g, router_group_w, router_group_b, router_expert_w, router_expert_b, expert_w_gate, expert_w_up, expert_w_down, final_norm_g):
    batch, seq, _ = x.shape
    depth = w_in.shape[0]
    x2d = x.reshape(batch * seq, D_MODEL)
    for l in range(depth):
        w_route = jnp.concatenate([router_group_w[l], router_expert_w[l]], axis=1)
        w_route = jnp.pad(w_route, ((0, 0), (0, LANES - w_route.shape[1])))
        p = dict(
            norm_mix_g=norm_mix_g[l].reshape(1, D_MODEL),
            w_in=_prep_w_in(w_in[l]),
            dn_conv_w=dn_conv_w[l],
            alog_row=_lane_row(dn_a_log[l]),
            dtb_row=_lane_row(dn_dt_bias[l]),
            dn_norm_g=dn_norm_g[l].reshape(1, HEAD_DIM),
            sc_conv_w=sc_conv_w[l],
            w_branch_dn=w_branch_dn[l].astype(BF16),
            w_branch_sc=w_branch_sc[l].astype(BF16),
            w_branch_swa=w_branch_swa[l].astype(BF16),
            w_out=w_out[l].astype(BF16),
            norm_ffn_g=norm_ffn_g[l].reshape(1, D_MODEL),
            w_route=w_route,
            b_route=_lane_row(jnp.concatenate([router_group_b[l], router_expert_b[l]])),
            expert_w_gate=expert_w_gate,
            expert_w_up=expert_w_up,
            expert_w_down=expert_w_down,
            layer=l,
        )
        final_g_row = final_norm_g.reshape(1, D_MODEL) if l == depth - 1 else None
        x2d = _layer(x2d, batch, seq, p, final_g_row)
    return x2d.reshape(batch, seq, D_MODEL)
```

```python
import functools

import jax
import jax.numpy as jnp
import numpy as np
from jax import lax
from jax.experimental import pallas as pl
from jax.experimental.pallas import tpu as pltpu

F32 = jnp.float32
BF16 = jnp.bfloat16

D_MODEL = 2048
RMS_EPS = 1e-6
L2_EPS = 1e-6

DN_HEADS = 8
HEAD_DIM = 128
DN_WIDTH = DN_HEADS * HEAD_DIM
DN_CONV = 4
SC_WIDTH = 1024
SC_CONV = 3
SWA_GROUPS = ((128, 1), (512, 4), (2048, 16))
SWA_HEADS_PER_GROUP = 4
SWA_HEADS = 12
SWA_GROUP_WIDTH = SWA_HEADS_PER_GROUP * HEAD_DIM
SWA_BLOCK = 128
ALIBI_MAX_BIAS = 8.0
MOE_GROUPS = 4
MOE_EPG = 8
MOE_EXPERTS = 32
MOE_FF = 512

OFF_GATE = 0
OFF_DNQ = 6144
OFF_DNK = OFF_DNQ + DN_WIDTH
OFF_DNV = OFF_DNK + DN_WIDTH
OFF_DNZ = OFF_DNV + DN_WIDTH
OFF_SCC = OFF_DNZ + DN_WIDTH
OFF_SCB = OFF_SCC + SC_WIDTH
OFF_SCX = OFF_SCB + SC_WIDTH
OFF_SWA = OFF_SCX + SC_WIDTH
OFF_AB = OFF_SWA + 3 * SWA_HEADS * HEAD_DIM
PROJ_COLS = 18432

LANES = 128
SWA_OUT_SLOTS = SWA_HEADS_PER_GROUP + 1
CHUNK = 128
HALO = 16

VMEM_LIMIT = 56 * 1024 * 1024


def _cparams(sem):
    return pltpu.CompilerParams(dimension_semantics=sem, vmem_limit_bytes=VMEM_LIMIT)


def _sigmoid(x):
    return 1.0 / (1.0 + jnp.exp(-x))


def _silu(x):
    return x * _sigmoid(x)


def _softplus(x):
    return jnp.maximum(x, 0.0) + jnp.log(1.0 + jnp.exp(-jnp.abs(x)))


IN_TM = 1024
IN_TN = 1024
NORM_ROWS = 64


def _rmsnorm_rows(x, g):
    ms = jnp.mean(x * x, axis=-1, keepdims=True)
    return x * lax.rsqrt(ms + RMS_EPS) * g


def _in_proj_kernel(x_ref, g_ref, w_ref, o_ref, h_ref):
    @pl.when(pl.program_id(1) == 0)
    def _():
        def body(r, c):
            rows = pl.ds(pl.multiple_of(r * NORM_ROWS, NORM_ROWS), NORM_ROWS)
            h_ref[rows, :] = _rmsnorm_rows(x_ref[rows, :], g_ref[...]).astype(BF16)
            return c
        lax.fori_loop(0, IN_TM // NORM_ROWS, body, 0)

    o_ref[...] = jnp.dot(h_ref[...], w_ref[...], preferred_element_type=F32).astype(o_ref.dtype)


def _in_proj(x2d, g_row, w_bf16):
    m = x2d.shape[0]
    return pl.pallas_call(
        _in_proj_kernel,
        grid=(m // IN_TM, PROJ_COLS // IN_TN),
        in_specs=[
            pl.BlockSpec((IN_TM, D_MODEL), lambda i, j: (i, 0)),
            pl.BlockSpec((1, D_MODEL), lambda i, j: (0, 0)),
            pl.BlockSpec((D_MODEL, IN_TN), lambda i, j: (0, j)),
        ],
        out_specs=pl.BlockSpec((IN_TM, IN_TN), lambda i, j: (i, j)),
        out_shape=jax.ShapeDtypeStruct((m, PROJ_COLS), BF16),
        scratch_shapes=[pltpu.VMEM((IN_TM, D_MODEL), BF16)],
        compiler_params=_cparams(("parallel", "arbitrary")),
        name="in_proj",
    )(x2d, g_row, w_bf16)


DN_HB = 8
DN_TS = 512
DN_W = DN_HB * HEAD_DIM
DN_DOUBLINGS = CHUNK.bit_length() - 2


def _dot_nt(a, b):
    return lax.dot_general(a, b, (((1,), (1,)), ((), ())), preferred_element_type=F32)


def _dot_tn(a, b):
    return lax.dot_general(a, b, (((0,), (0,)), ((), ())), preferred_element_type=F32)


def _causal_conv(xx, w, width):
    acc = None
    for j in range(width):
        shift = width - 1 - j
        xs = xx if shift == 0 else pltpu.roll(xx, shift, axis=0)
        term = xs[HALO:, :] * w[j:j + 1, :]
        acc = term if acc is None else acc + term
    return acc


def _deltanet_kernel(q_ref, k_ref, v_ref, z_ref, ab_ref, cwq_ref, cwk_ref, cwv_ref,
                     alog_ref, dtb_ref, ng_ref, o_ref,
                     xpad_ref, state_ref):
    hg = pl.program_id(1)
    st = pl.program_id(2)

    @pl.when(st == 0)
    def _():
        xpad_ref[:, 0:HALO, :] = jnp.zeros((3, HALO, DN_W), BF16)
        state_ref[...] = jnp.zeros_like(state_ref)

    xpad_ref[0, HALO:, :] = q_ref[0]
    xpad_ref[1, HALO:, :] = k_ref[0]
    xpad_ref[2, HALO:, :] = v_ref[0]

    row_i = lax.broadcasted_iota(jnp.int32, (CHUNK, CHUNK), 0)
    col_j = lax.broadcasted_iota(jnp.int32, (CHUNK, CHUNK), 1)
    causal = row_i >= col_j
    strict = row_i > col_j
    tri = causal.astype(F32)
    eye = (row_i == col_j).astype(F32)
    lane = lax.broadcasted_iota(jnp.int32, (CHUNK, LANES), 1)
    neg_decay_rate = -jnp.exp(alog_ref[...])
    cws = (cwq_ref[...], cwk_ref[...], cwv_ref[...])
    ng = ng_ref[...]

    def chunk_body(c, carry):
        r0 = pl.multiple_of(c * CHUNK, CHUNK)
        win = pl.ds(r0, CHUNK + HALO)
        rows = pl.ds(r0, CHUNK)
        conv = [_silu(_causal_conv(xpad_ref[t, win, :].astype(F32), cws[t], DN_CONV)) for t in range(3)]
        ab = ab_ref[0, rows, :].astype(F32)
        g_raw = neg_decay_rate * _softplus(ab + dtb_ref[...])
        g_cum = jnp.dot(tri, g_raw, preferred_element_type=F32, precision=lax.Precision.HIGHEST)
        beta_all = _sigmoid(ab)
        z = z_ref[0, rows, :].astype(F32)

        heads = range(DN_HB)
        sls = [slice(i * HEAD_DIM, (i + 1) * HEAD_DIM) for i in heads]
        dot = functools.partial(jnp.dot, preferred_element_type=F32)

        g_col = [jnp.sum(jnp.where(lane == hg * DN_HB + i, g_cum, 0.0), axis=-1, keepdims=True) for i in heads]
        beta = [jnp.sum(jnp.where(lane == hg * DN_HB + i + DN_HEADS, beta_all, 0.0), axis=-1, keepdims=True)
                for i in heads]
        g_last = [g[CHUNK - 1:CHUNK, :] for g in g_col]
        eg = [jnp.exp(g) for g in g_col]
        ek = [jnp.exp(gl - g) for gl, g in zip(g_last, g_col)]
        g_b = [jnp.broadcast_to(g, (CHUNK, CHUNK)) for g in g_col]
        decay = [jnp.exp(jnp.where(causal, gb - gb.T, -jnp.inf)) for gb in g_b]

        qf = [conv[0][:, sl] for sl in sls]
        kf = [conv[1][:, sl] for sl in sls]
        vf = [conv[2][:, sl] for sl in sls]
        q = [x * lax.rsqrt(jnp.sum(x * x, axis=-1, keepdims=True) + L2_EPS) * (HEAD_DIM ** -0.5) for x in qf]
        k = [x * lax.rsqrt(jnp.sum(x * x, axis=-1, keepdims=True) + L2_EPS) for x in kf]
        kb = [ki * bi for ki, bi in zip(k, beta)]
        a2 = [_dot_nt(jnp.concatenate([qi, kbi], axis=0).astype(BF16), ki.astype(BF16))
              for qi, kbi, ki in zip(q, kb, k)]
        attn = [a[:CHUNK] * d for a, d in zip(a2, decay)]
        n_mat = [jnp.where(strict, -(a[CHUNK:] * d), 0.0) for a, d in zip(a2, decay)]

        u_mat = [eye + n for n in n_mat]
        p_mat = [dot(n.astype(BF16), n.astype(BF16)) for n in n_mat]
        for _ in range(DN_DOUBLINGS - 1):
            up = [dot(jnp.concatenate([u.astype(BF16), p.astype(BF16)], axis=0), p.astype(BF16))
                  for u, p in zip(u_mat, p_mat)]
            u_mat = [u + x[:CHUNK] for u, x in zip(u_mat, up)]
            p_mat = [x[CHUNK:] for x in up]
        u_mat = [u + dot(u.astype(BF16), p.astype(BF16)) for u, p in zip(u_mat, p_mat)]
        uw = [dot(u.astype(BF16), jnp.concatenate([vi * bi, kbi * egi], axis=1).astype(BF16))
              for u, vi, bi, kbi, egi in zip(u_mat, vf, beta, kb, eg)]

        s_old = [state_ref[i] for i in heads]
        qw = [dot(jnp.concatenate([qi * egi, x[:, HEAD_DIM:]], axis=0).astype(BF16), s.astype(BF16))
              for qi, egi, x, s in zip(q, eg, uw, s_old)]
        v16 = [(x[:, :HEAD_DIM] - y[CHUNK:]).astype(BF16) for x, y in zip(uw, qw)]
        o = [y[:CHUNK] + dot(a.astype(BF16), v) for y, a, v in zip(qw, attn, v16)]
        s_new = [s * jnp.exp(gl) + _dot_tn((ki * eki).astype(BF16), v)
                 for s, gl, ki, eki, v in zip(s_old, g_last, k, ek, v16)]
        for i in heads:
            state_ref[i] = s_new[i]
            on = o[i] * lax.rsqrt(jnp.mean(o[i] * o[i], axis=-1, keepdims=True) + RMS_EPS) * ng
            o_ref[0, rows, sls[i]] = (on * _silu(z[:, sls[i]])).astype(o_ref.dtype)
        return carry

    lax.fori_loop(0, DN_TS // CHUNK, chunk_body, 0)

    tail = pl.ds(DN_TS, HALO)
    head = pl.ds(0, HALO)
    for t in range(3):
        xpad_ref[t, head, :] = xpad_ref[t, tail, :]


def _deltanet(proj3, conv_w, alog_row, dtb_row, ng_row):
    b, s, _ = proj3.shape
    qb, kb_, vb, zb = (off // DN_W for off in (OFF_DNQ, OFF_DNK, OFF_DNV, OFF_DNZ))
    act = lambda base: pl.BlockSpec((1, DN_TS, DN_W), lambda bi, hg, st: (bi, st, base + hg))
    cw = lambda base: pl.BlockSpec((DN_CONV, DN_W), lambda bi, hg, st: (0, base + hg))
    row = pl.BlockSpec((1, LANES), lambda bi, hg, st: (0, 0))
    return pl.pallas_call(
        _deltanet_kernel,
        grid=(b, DN_HEADS // DN_HB, s // DN_TS),
        in_specs=[
            act(qb), act(kb_), act(vb), act(zb),
            pl.BlockSpec((1, DN_TS, LANES), lambda bi, hg, st: (bi, st, OFF_AB // LANES)),
            cw(0), cw(DN_WIDTH // DN_W), cw(2 * DN_WIDTH // DN_W),
            row, row, row,
        ],
        out_specs=pl.BlockSpec((1, DN_TS, DN_W), lambda bi, hg, st: (bi, st, hg)),
        out_shape=jax.ShapeDtypeStruct((b, s, DN_WIDTH), BF16),
        scratch_shapes=[
            pltpu.VMEM((3, DN_TS + HALO, DN_W), BF16),
            pltpu.VMEM((DN_HB, HEAD_DIM, HEAD_DIM), F32),
        ],
        compiler_params=_cparams(("parallel", "parallel", "arbitrary")),
        name="deltanet",
    )(proj3, proj3, proj3, proj3, proj3, conv_w, conv_w, conv_w, alog_row, dtb_row, ng_row)


SWA_CAST_ROWS = 256


def _swa_kernel(q_ref, k_ref, v_ref, o_ref, *scratch, seq, dil, slopes):
    n_blocks = seq // dil // SWA_BLOCK
    qi = lax.broadcasted_iota(jnp.int32, (SWA_BLOCK, SWA_BLOCK), 0)
    kj = lax.broadcasted_iota(jnp.int32, (SWA_BLOCK, SWA_BLOCK), 1)
    step_cur = (qi - kj).astype(F32)
    step_prev = step_cur + float(SWA_BLOCK)
    ok_cur = qi >= kj
    ok_prev = kj >= qi
    lane = lax.broadcasted_iota(jnp.int32, (SWA_BLOCK, LANES), 1)
    scale = HEAD_DIM ** -0.5
    heads = range(SWA_HEADS_PER_GROUP)
    sls = [slice(hh * HEAD_DIM, (hh + 1) * HEAD_DIM) for hh in heads]
    m_h = [float(slopes[hh]) * float(dil) for hh in heads]

    if dil > 1:
        (f32_ref,) = scratch

        def cast_body(c, carry):
            rows = pl.ds(pl.multiple_of(c * SWA_CAST_ROWS, SWA_CAST_ROWS), SWA_CAST_ROWS)
            for t, ref in enumerate((q_ref, k_ref, v_ref)):
                for hh in heads:
                    f32_ref[t * SWA_HEADS_PER_GROUP + hh, rows, :] = ref[0, rows, sls[hh]].astype(F32)
            return carry
        lax.fori_loop(0, seq // SWA_CAST_ROWS, cast_body, 0)

        def load(t, rows):
            return [f32_ref[t * SWA_HEADS_PER_GROUP + hh, rows, :].astype(BF16) for hh in heads]
    else:
        refs = (q_ref, k_ref, v_ref)

        def load(t, rows):
            return [refs[t][0, rows, sls[hh]] for hh in heads]

    def block_body(it, carry):
        r = it // n_blocks
        n = it % n_blocks
        base = n * (SWA_BLOCK * dil) + r
        prev_base = jnp.maximum(n - 1, 0) * (SWA_BLOCK * dil) + r
        if dil > 1:
            rows = pl.ds(base, SWA_BLOCK, stride=dil)
            prev_rows = pl.ds(prev_base, SWA_BLOCK, stride=dil)
        else:
            rows = pl.ds(pl.multiple_of(base, SWA_BLOCK), SWA_BLOCK)
            prev_rows = pl.ds(pl.multiple_of(prev_base, SWA_BLOCK), SWA_BLOCK)
        qn, kc, vc = load(0, rows), load(1, rows), load(2, rows)
        kp, vp = load(1, prev_rows), load(2, prev_rows)
        ok_p = jnp.logical_and(ok_prev, n > 0)

        s_cur = [jnp.where(ok_cur, _dot_nt(qh, kh) * scale - m * step_cur, -jnp.inf)
                 for qh, kh, m in zip(qn, kc, m_h)]
        s_prev = [jnp.where(ok_p, _dot_nt(qh, kh) * scale - m * step_prev, -jnp.inf)
                  for qh, kh, m in zip(qn, kp, m_h)]
        mx = [jnp.maximum(jnp.max(a, axis=-1, keepdims=True), jnp.max(b, axis=-1, keepdims=True))
              for a, b in zip(s_cur, s_prev)]
        e_cur = [jnp.exp(a - m) for a, m in zip(s_cur, mx)]
        e_prev = [jnp.exp(b - m) for b, m in zip(s_prev, mx)]
        den = [jnp.sum(a, axis=-1, keepdims=True) + jnp.sum(b, axis=-1, keepdims=True)
               for a, b in zip(e_cur, e_prev)]
        pv = [jnp.dot(a.astype(BF16), vch, preferred_element_type=F32)
              + jnp.dot(b.astype(BF16), vph, preferred_element_type=F32)
              for a, b, vch, vph in zip(e_cur, e_prev, vc, vp)]
        lse_tile = jnp.zeros((SWA_BLOCK, LANES), F32)
        for hh in heads:
            o_ref[0, hh, rows, :] = pv[hh] * (1.0 / den[hh])
            lse_tile = jnp.where(lane == hh, mx[hh] + jnp.log(den[hh]), lse_tile)
        o_ref[0, SWA_HEADS_PER_GROUP, rows, :] = lse_tile
        return carry

    lax.fori_loop(0, seq // SWA_BLOCK, block_body, 0)


def _swa_group(proj3, gi):
    batch, seq, _ = proj3.shape
    window, dil = SWA_GROUPS[gi]
    assert window // dil == SWA_BLOCK and seq % (dil * SWA_BLOCK) == 0
    slopes = 2.0 ** (-ALIBI_MAX_BIAS * np.arange(1, SWA_HEADS + 1) / SWA_HEADS)
    slopes = slopes.reshape(len(SWA_GROUPS), SWA_HEADS_PER_GROUP)[gi]

    def spec(which):
        cb = (OFF_SWA + which * SWA_HEADS * HEAD_DIM) // SWA_GROUP_WIDTH + gi
        return pl.BlockSpec((1, seq, SWA_GROUP_WIDTH), lambda bi: (bi, 0, cb))

    scratch = [pltpu.VMEM((3 * SWA_HEADS_PER_GROUP, seq, HEAD_DIM), F32)] if dil > 1 else []
    return pl.pallas_call(
        functools.partial(_swa_kernel, seq=seq, dil=dil, slopes=tuple(float(s) for s in slopes)),
        grid=(batch,),
        in_specs=[spec(0), spec(1), spec(2)],
        out_specs=pl.BlockSpec((1, SWA_OUT_SLOTS, seq, HEAD_DIM), lambda bi: (bi, 0, 0, 0)),
        out_shape=jax.ShapeDtypeStruct((batch, SWA_OUT_SLOTS, seq, HEAD_DIM), F32),
        scratch_shapes=scratch,
        compiler_params=_cparams(("parallel",)),
        name=f"swa_g{gi}",
    )(proj3, proj3, proj3)


MG_TM = 256


def _merge_kernel(x_ref, ydn_ref, gate_ref, scc_ref, scb_ref, scx_ref, scc_h_ref, scx_h_ref,
                  s0_ref, s1_ref, s2_ref, scw_ref, wdn_ref, wsc_ref, wswa_ref, wout_ref, o_ref,
                  *, tiles_per_seq):
    i = pl.program_id(0)
    first = (i % tiles_per_seq) == 0

    cx_cur = scc_ref[...].astype(F32) * scx_ref[...].astype(F32)
    cx_halo = scc_h_ref[...].astype(F32) * scx_h_ref[...].astype(F32)
    cx_halo = jnp.where(first, 0.0, cx_halo)
    cx = jnp.concatenate([cx_halo, cx_cur], axis=0)
    y_sc = scb_ref[...].astype(F32) * _causal_conv(cx, scw_ref[...], SC_CONV)

    outs = (s0_ref, s1_ref, s2_ref)
    lses = [o[0, SWA_HEADS_PER_GROUP] for o in outs]
    mx = jnp.maximum(jnp.maximum(lses[0], lses[1]), lses[2])
    es = [jnp.exp(l - mx) for l in lses]
    inv = 1.0 / (es[0] + es[1] + es[2])
    heads = []
    for hh in range(SWA_HEADS_PER_GROUP):
        acc = None
        for gi in range(3):
            wt = (es[gi] * inv)[:, hh:hh + 1]
            term = wt * outs[gi][0, hh]
            acc = term if acc is None else acc + term
        heads.append(acc)
    y_swa = jnp.concatenate(heads, axis=1)

    gates = gate_ref[...].astype(F32)
    merged = (_sigmoid(gates[:, 0:D_MODEL])
              * jnp.dot(ydn_ref[...], wdn_ref[...], preferred_element_type=F32)
              + _sigmoid(gates[:, D_MODEL:2 * D_MODEL])
              * jnp.dot(y_sc.astype(BF16), wsc_ref[...], preferred_element_type=F32)
              + _sigmoid(gates[:, 2 * D_MODEL:3 * D_MODEL])
              * jnp.dot(y_swa.astype(BF16), wswa_ref[...], preferred_element_type=F32))
    o_ref[...] = x_ref[...] + jnp.dot(merged.astype(BF16), wout_ref[...], preferred_element_type=F32)


def _merge(x2d, ydn2, proj2, swa_outs, sc_conv_w, wdn, wsc, wswa, wout, seq):
    m = x2d.shape[0]
    tiles_per_seq = seq // MG_TM
    halo_blocks = MG_TM // HALO
    rows = lambda w, cb: pl.BlockSpec((MG_TM, w), lambda i: (i, cb))
    halo = lambda cb: pl.BlockSpec((HALO, SC_WIDTH), lambda i: (jnp.maximum(i * halo_blocks - 1, 0), cb))
    full = lambda a: pl.BlockSpec(a.shape, lambda i: (0, 0), pipeline_mode=pl.Buffered(1))
    swa = pl.BlockSpec((1, SWA_OUT_SLOTS, MG_TM, HEAD_DIM),
                       lambda i: (i // tiles_per_seq, 0, i % tiles_per_seq, 0))
    return pl.pallas_call(
        functools.partial(_merge_kernel, tiles_per_seq=tiles_per_seq),
        grid=(m // MG_TM,),
        in_specs=[
            rows(D_MODEL, 0),
            rows(DN_WIDTH, 0),
            rows(3 * D_MODEL, OFF_GATE // (3 * D_MODEL)),
            rows(SC_WIDTH, OFF_SCC // SC_WIDTH), rows(SC_WIDTH, OFF_SCB // SC_WIDTH),
            rows(SC_WIDTH, OFF_SCX // SC_WIDTH),
            halo(OFF_SCC // SC_WIDTH), halo(OFF_SCX // SC_WIDTH),
            swa, swa, swa,
            full(sc_conv_w), full(wdn), full(wsc), full(wswa), full(wout),
        ],
        out_specs=rows(D_MODEL, 0),
        out_shape=jax.ShapeDtypeStruct((m, D_MODEL), F32),
        compiler_params=_cparams(("parallel",)),
        name="merge",
    )(x2d, ydn2, proj2, proj2, proj2, proj2, proj2, proj2, *swa_outs, sc_conv_w, wdn, wsc, wswa, wout)


RT_TM = 512
ROUTE_LANE0 = MOE_GROUPS
R_E1, R_E2, R_W1, R_W2, R_RANK1, R_RANK2 = range(6)
HALF = D_MODEL // 2


def _pack_pair(lo, hi):
    lo_b = lax.bitcast_convert_type(lo.astype(BF16).astype(F32), jnp.uint32)
    hi_b = lax.bitcast_convert_type(hi.astype(BF16).astype(F32), jnp.uint32)
    return (hi_b & jnp.uint32(0xFFFF0000)) | (lo_b >> jnp.uint32(16))


def _unpack_pair(p):
    lo = lax.bitcast_convert_type(p << jnp.uint32(16), F32)
    hi = lax.bitcast_convert_type(p & jnp.uint32(0xFFFF0000), F32)
    return lo, hi


def _router_kernel(x_ref, g_ref, wr_ref, br_ref, hp_ref, route_ref, cnt_ref, carry_ref):
    @pl.when(pl.program_id(0) == 0)
    def _():
        carry_ref[...] = jnp.zeros_like(carry_ref)

    h = _rmsnorm_rows(x_ref[...], g_ref[...])
    hp_ref[...] = _pack_pair(h[:, :HALF], h[:, HALF:])
    w = wr_ref[...]
    h_hi, w_hi = h.astype(BF16), w.astype(BF16)
    h_lo = (h - h_hi.astype(F32)).astype(BF16)
    w_lo = (w - w_hi.astype(F32)).astype(BF16)
    dot = functools.partial(jnp.dot, preferred_element_type=F32)
    logits = dot(h_hi, w_hi) + dot(h_lo, w_hi) + dot(h_hi, w_lo) + br_ref[...]
    lane = lax.broadcasted_iota(jnp.int32, logits.shape, 1)
    big = jnp.int32(LANES)
    neg = -jnp.inf

    is_grp = lane < MOE_GROUPS
    gl = jnp.where(is_grp, logits, neg)
    gmax = jnp.max(gl, axis=-1, keepdims=True)
    grp = jnp.min(jnp.where(gl == gmax, lane, big), axis=-1, keepdims=True)
    g_w = 1.0 / jnp.sum(jnp.exp(gl - gmax), axis=-1, keepdims=True)

    lo = ROUTE_LANE0 + grp * MOE_EPG
    in_grp = jnp.logical_and(lane >= lo, lane < lo + MOE_EPG)
    el = jnp.where(in_grp, logits, neg)
    m1 = jnp.max(el, axis=-1, keepdims=True)
    i1 = jnp.min(jnp.where(el == m1, lane, big), axis=-1, keepdims=True)
    el2 = jnp.where(lane == i1, neg, el)
    m2 = jnp.max(el2, axis=-1, keepdims=True)
    i2 = jnp.min(jnp.where(el2 == m2, lane, big), axis=-1, keepdims=True)
    e2 = jnp.exp(m2 - m1)
    w1 = g_w / (1.0 + e2)
    w2 = g_w * e2 / (1.0 + e2)

    tm = logits.shape[0]
    onehot = jnp.where(jnp.logical_or(lane == i1, lane == i2), 1.0, 0.0)
    earlier = (lax.broadcasted_iota(jnp.int32, (tm, tm), 0) > lax.broadcasted_iota(jnp.int32, (tm, tm), 1))
    before = carry_ref[...] + jnp.dot(jnp.where(earlier, 1.0, 0.0).astype(BF16), onehot.astype(BF16),
                                      preferred_element_type=F32)
    rank1 = jnp.sum(jnp.where(lane == i1, before, 0.0), axis=-1, keepdims=True)
    rank2 = jnp.sum(jnp.where(lane == i2, before, 0.0), axis=-1, keepdims=True)
    carry_ref[...] += jnp.sum(onehot, axis=0, keepdims=True)
    cnt_ref[...] = carry_ref[...]

    rec = jnp.zeros_like(logits)
    for ln, val in ((R_E1, (i1 - ROUTE_LANE0).astype(F32)), (R_E2, (i2 - ROUTE_LANE0).astype(F32)),
                    (R_W1, w1), (R_W2, w2), (R_RANK1, rank1), (R_RANK2, rank2)):
        rec = jnp.where(lane == ln, val, rec)
    route_ref[...] = rec


def _router(x2d, g_row, w_route, b_route):
    m = x2d.shape[0]
    return pl.pallas_call(
        _router_kernel,
        grid=(m // RT_TM,),
        in_specs=[
            pl.BlockSpec((RT_TM, D_MODEL), lambda i: (i, 0)),
            pl.BlockSpec((1, D_MODEL), lambda i: (0, 0)),
            pl.BlockSpec((D_MODEL, LANES), lambda i: (0, 0)),
            pl.BlockSpec((1, LANES), lambda i: (0, 0)),
        ],
        out_specs=[pl.BlockSpec((RT_TM, HALF), lambda i: (i, 0)),
                   pl.BlockSpec((RT_TM, LANES), lambda i: (i, 0)),
                   pl.BlockSpec((1, LANES), lambda i: (0, 0))],
        out_shape=[jax.ShapeDtypeStruct((m, HALF), jnp.uint32), jax.ShapeDtypeStruct((m, LANES), F32),
                   jax.ShapeDtypeStruct((1, LANES), F32)],
        scratch_shapes=[pltpu.VMEM((1, LANES), F32)],
        compiler_params=_cparams(("arbitrary",)),
        name="router",
    )(x2d, g_row, w_route, b_route)


EXP_TM = 512


def _n_row_tiles(n_tokens):
    return 2 * n_tokens // EXP_TM + MOE_EXPERTS


def _positions_kernel(route_ref, offs_ref, pos_ref):
    rec = route_ref[...]
    lane = lax.broadcasted_iota(jnp.int32, rec.shape, 1)
    offs = offs_ref[...]

    def first_row(e):
        return jnp.sum(jnp.where(lane == e.astype(jnp.int32), offs, 0.0), axis=-1, keepdims=True)

    pos1 = first_row(rec[:, R_E1:R_E1 + 1]) + rec[:, R_RANK1:R_RANK1 + 1]
    pos2 = first_row(rec[:, R_E2:R_E2 + 1]) + rec[:, R_RANK2:R_RANK2 + 1]
    pos_ref[...] = jnp.where(lane == 0, pos1, jnp.where(lane == 1, pos2, 0.0)).astype(jnp.int32)


def _moe_plan(route, cnt):
    m = route.shape[0]
    n_tiles = _n_row_tiles(m)
    counts = cnt[0, ROUTE_LANE0:ROUTE_LANE0 + MOE_EXPERTS].astype(jnp.int32)
    padded = (counts + EXP_TM - 1) // EXP_TM * EXP_TM
    ends = jnp.cumsum(padded)
    offs = ends - padded
    pos = pl.pallas_call(
        _positions_kernel,
        grid=(m // RT_TM,),
        in_specs=[pl.BlockSpec((RT_TM, LANES), lambda i: (i, 0)), pl.BlockSpec((1, LANES), lambda i: (0, 0))],
        out_specs=pl.BlockSpec((RT_TM, LANES), lambda i: (i, 0)),
        out_shape=jax.ShapeDtypeStruct((m, LANES), jnp.int32),
        compiler_params=_cparams(("parallel",)),
        name="moe_positions",
    )(route, _lane_row(offs))
    n_active = ends[-1] // EXP_TM
    tile_row = jnp.maximum(jnp.minimum(jnp.arange(n_tiles), n_active - 1), 0) * EXP_TM
    tile_expert = jnp.minimum(jnp.sum(ends[None, :] <= tile_row[:, None], axis=1), MOE_EXPERTS - 1)
    return pos[:, 0], pos[:, 1], tile_expert.astype(jnp.int32), n_active.astype(jnp.int32).reshape(1)


DSP_TB = 512
ROW_UNROLL = 8


def _dispatch_kernel(pos1_ref, pos2_ref, hp_ref, xs_in_ref, xs_ref, sem):
    del xs_in_ref

    def start(j8, c):
        for u in range(ROW_UNROLL):
            j = j8 * ROW_UNROLL + u
            src = hp_ref.at[pl.ds(j, 1), :]
            pltpu.make_async_copy(src, xs_ref.at[pl.ds(pos1_ref[0, 0, j], 1), :], sem).start(priority=0)
            pltpu.make_async_copy(src, xs_ref.at[pl.ds(pos2_ref[0, 0, j], 1), :], sem).start(priority=1)
        return c

    lax.fori_loop(0, DSP_TB // ROW_UNROLL, start, 0)
    for _ in range(2):
        pltpu.make_async_copy(hp_ref, xs_ref.at[pl.ds(0, DSP_TB), :], sem).wait()


def _dispatch(hp, pos1, pos2):
    m = hp.shape[0]
    n_rows = _n_row_tiles(m) * EXP_TM
    pos_spec = pl.BlockSpec((1, 1, DSP_TB), lambda i: (i, 0, 0), memory_space=pltpu.SMEM)
    return pl.pallas_call(
        _dispatch_kernel,
        grid=(m // DSP_TB,),
        in_specs=[pos_spec, pos_spec,
                  pl.BlockSpec((DSP_TB, HALF), lambda i: (i, 0)),
                  pl.BlockSpec(memory_space=pl.ANY)],
        out_specs=pl.BlockSpec(memory_space=pl.ANY),
        out_shape=jax.ShapeDtypeStruct((n_rows, HALF), jnp.uint32),
        scratch_shapes=[pltpu.SemaphoreType.DMA(())],
        input_output_aliases={3: 0},
        compiler_params=_cparams(("arbitrary",)),
        name="moe_dispatch",
    )(pos1.reshape(m // DSP_TB, 1, DSP_TB), pos2.reshape(m // DSP_TB, 1, DSP_TB), hp,
      jnp.zeros((n_rows, HALF), jnp.uint32))


W_CAST_ROWS = 256


def _cast_rows(src_ref, dst_ref):
    n_rows = dst_ref.shape[0]
    slab = min(W_CAST_ROWS, n_rows)

    def body(r, c):
        rows = pl.ds(pl.multiple_of(r * slab, slab), slab)
        dst_ref[rows, :] = src_ref[0, 0, rows, :].astype(BF16)
        return c
    lax.fori_loop(0, n_rows // slab, body, 0)


def _expert_kernel(te_ref, nact_ref, xs_ref, wg_ref, wu_ref, wd_ref, ys_ref, wg16_ref, wu16_ref, wd16_ref):
    i = pl.program_id(0)
    active = i < nact_ref[0]
    new_expert = jnp.logical_or(i == 0, te_ref[i] != te_ref[jnp.maximum(i - 1, 0)])

    @pl.when(jnp.logical_and(active, new_expert))
    def _():
        _cast_rows(wg_ref, wg16_ref)
        _cast_rows(wu_ref, wu16_ref)
        _cast_rows(wd_ref, wd16_ref)

    @pl.when(active)
    def _():
        lo, hi = _unpack_pair(xs_ref[...])
        lo, hi = lo.astype(BF16), hi.astype(BF16)
        dot = functools.partial(jnp.dot, preferred_element_type=F32)
        gate = dot(lo, wg16_ref[:HALF, :]) + dot(hi, wg16_ref[HALF:, :])
        up = dot(lo, wu16_ref[:HALF, :]) + dot(hi, wu16_ref[HALF:, :])
        y = dot((_silu(gate) * up).astype(BF16), wd16_ref[...])
        ys_ref[...] = _pack_pair(y[:, :HALF], y[:, HALF:])

    @pl.when(jnp.logical_not(active))
    def _():
        ys_ref[...] = jnp.zeros_like(ys_ref)


def _experts(xs, tile_expert, n_active, wg, wu, wd, layer):
    n_tiles = xs.shape[0] // EXP_TM
    rows = pl.BlockSpec((EXP_TM, HALF), lambda i, te, na: (jnp.maximum(jnp.minimum(i, na[0] - 1), 0), 0))
    out_rows = pl.BlockSpec((EXP_TM, HALF), lambda i, te, na: (i, 0))
    w_in_spec = pl.BlockSpec((1, 1, D_MODEL, MOE_FF), lambda i, te, na: (layer, te[i], 0, 0))
    w_out_spec = pl.BlockSpec((1, 1, MOE_FF, D_MODEL), lambda i, te, na: (layer, te[i], 0, 0))
    return pl.pallas_call(
        _expert_kernel,
        grid_spec=pltpu.PrefetchScalarGridSpec(
            num_scalar_prefetch=2,
            grid=(n_tiles,),
            in_specs=[rows, w_in_spec, w_in_spec, w_out_spec],
            out_specs=out_rows,
            scratch_shapes=[pltpu.VMEM((D_MODEL, MOE_FF), BF16), pltpu.VMEM((D_MODEL, MOE_FF), BF16),
                            pltpu.VMEM((MOE_FF, D_MODEL), BF16)],
        ),
        out_shape=jax.ShapeDtypeStruct(xs.shape, jnp.uint32),
        compiler_params=_cparams(("arbitrary",)),
        name="moe_experts",
    )(tile_expert, n_active, xs, wg, wu, wd)


CMB_TC = 256


def _combine_kernel(pos1_ref, pos2_ref, pos1_next_ref, pos2_next_ref, x_ref, route_ref, g_ref, ys_ref,
                    o_ref, buf_ref, sem, *, final_norm):
    i = pl.program_id(0)
    slot = i % 2

    def gather(p1_ref, p2_ref, s):
        def start(j8, c):
            for u in range(ROW_UNROLL):
                j = j8 * ROW_UNROLL + u
                for k, p_ref in enumerate((p1_ref, p2_ref)):
                    pltpu.make_async_copy(ys_ref.at[pl.ds(p_ref[0, 0, j], 1), :],
                                          buf_ref.at[s, k, pl.ds(j, 1), :], sem.at[s]).start(priority=k)
            return c
        lax.fori_loop(0, CMB_TC // ROW_UNROLL, start, 0)

    @pl.when(i == 0)
    def _():
        gather(pos1_ref, pos2_ref, 0)

    @pl.when(i + 1 < pl.num_programs(0))
    def _():
        gather(pos1_next_ref, pos2_next_ref, 1 - slot)

    for k in range(2):
        pltpu.make_async_copy(ys_ref.at[pl.ds(0, CMB_TC), :], buf_ref.at[slot, k], sem.at[slot]).wait()

    rec = route_ref[...]
    w1, w2 = rec[:, R_W1:R_W1 + 1], rec[:, R_W2:R_W2 + 1]
    lo1, hi1 = _unpack_pair(buf_ref[slot, 0])
    lo2, hi2 = _unpack_pair(buf_ref[slot, 1])
    out_lo = x_ref[:, :HALF] + w1 * lo1 + w2 * lo2
    out_hi = x_ref[:, HALF:] + w1 * hi1 + w2 * hi2
    if final_norm:
        ms = (jnp.sum(out_lo * out_lo, axis=-1, keepdims=True)
              + jnp.sum(out_hi * out_hi, axis=-1, keepdims=True)) * (1.0 / D_MODEL)
        inv = lax.rsqrt(ms + RMS_EPS)
        out_lo = out_lo * inv * g_ref[:, :HALF]
        out_hi = out_hi * inv * g_ref[:, HALF:]
    o_ref[:, :HALF] = out_lo
    o_ref[:, HALF:] = out_hi


def _combine(x2d, route, ys, pos1, pos2, g_row, final_norm):
    m = x2d.shape[0]
    n_steps = m // CMB_TC
    pos_spec = pl.BlockSpec((1, 1, CMB_TC), lambda i: (i, 0, 0), memory_space=pltpu.SMEM)
    next_spec = pl.BlockSpec((1, 1, CMB_TC), lambda i: (jnp.minimum(i + 1, n_steps - 1), 0, 0),
                             memory_space=pltpu.SMEM)
    pos1, pos2 = pos1.reshape(n_steps, 1, CMB_TC), pos2.reshape(n_steps, 1, CMB_TC)
    return pl.pallas_call(
        functools.partial(_combine_kernel, final_norm=final_norm),
        grid=(n_steps,),
        in_specs=[pos_spec, pos_spec, next_spec, next_spec,
                  pl.BlockSpec((CMB_TC, D_MODEL), lambda i: (i, 0)),
                  pl.BlockSpec((CMB_TC, LANES), lambda i: (i, 0)),
                  pl.BlockSpec((1, D_MODEL), lambda i: (0, 0)),
                  pl.BlockSpec(memory_space=pl.ANY)],
        out_specs=pl.BlockSpec((CMB_TC, D_MODEL), lambda i: (i, 0)),
        out_shape=jax.ShapeDtypeStruct((m, D_MODEL), F32),
        scratch_shapes=[pltpu.VMEM((2, 2, CMB_TC, HALF), jnp.uint32), pltpu.SemaphoreType.DMA((2,))],
        compiler_params=_cparams(("arbitrary",)),
        name="moe_combine",
    )(pos1, pos2, pos1, pos2, x2d, route, g_row, ys)


RL_TN = 512
RL_ROWS = 256
SRC_A_END = 4096
SRC_AB = 16
SRC_COLS = 17936
MODE_COPY, MODE_SHIFT, MODE_AB = 0, 1, 2


def _relayout_plan():
    n_shifted = (SRC_COLS - SRC_A_END - SRC_AB) // RL_TN
    first = SRC_A_END // RL_TN
    shifted = lambda s: (MODE_SHIFT, first + s, (SRC_A_END + RL_TN * (s + 1)) // LANES)
    tiles = ([shifted(s) for s in range(15, n_shifted)]
             + [(MODE_COPY, a, 0) for a in range(first)]
             + [shifted(s) for s in range(15)]
             + [(MODE_AB, first, 0)])
    assert len(tiles) * RL_TN == PROJ_COLS
    return np.asarray(tiles, np.int32).T


def _relayout_kernel(mode_ref, blk_ref, carry_blk_ref, a_ref, b_ref, o_ref):
    del blk_ref, carry_blk_ref
    mode = mode_ref[pl.program_id(0)]
    lane = lax.broadcasted_iota(jnp.int32, (RL_ROWS, RL_TN), 1)

    def body(r, c):
        rows = pl.ds(pl.multiple_of(r * RL_ROWS, RL_ROWS), RL_ROWS)
        a = a_ref[0, rows, :]

        @pl.when(mode == MODE_COPY)
        def _():
            o_ref[rows, :] = a.astype(BF16)

        @pl.when(mode == MODE_SHIFT)
        def _():
            o_ref[rows, :] = jnp.concatenate([a[:, SRC_AB:], b_ref[0, rows, :SRC_AB]], axis=1).astype(BF16)

        @pl.when(mode == MODE_AB)
        def _():
            o_ref[rows, :] = jnp.where(lane < SRC_AB, a, 0.0).astype(BF16)
        return c
    lax.fori_loop(0, D_MODEL // RL_ROWS, body, 0)


def _prep_w_in(w_all, layer):
    plan = _relayout_plan()
    return pl.pallas_call(
        _relayout_kernel,
        grid_spec=pltpu.PrefetchScalarGridSpec(
            num_scalar_prefetch=3,
            grid=(PROJ_COLS // RL_TN,),
            in_specs=[pl.BlockSpec((1, D_MODEL, RL_TN), lambda t, mode, blk, cblk: (layer, 0, blk[t])),
                      pl.BlockSpec((1, D_MODEL, LANES), lambda t, mode, blk, cblk: (layer, 0, cblk[t]))],
            out_specs=pl.BlockSpec((D_MODEL, RL_TN), lambda t, mode, blk, cblk: (0, t)),
        ),
        out_shape=jax.ShapeDtypeStruct((D_MODEL, PROJ_COLS), BF16),
        compiler_params=_cparams(("arbitrary",)),
        name="w_in_relayout",
    )(jnp.asarray(plan[0]), jnp.asarray(plan[1]), jnp.asarray(plan[2]), w_all, w_all)


def _lane_row(v, offset=0):
    return jnp.zeros((1, LANES), F32).at[0, offset:offset + v.shape[0]].set(v.astype(F32))


def _layer(x2d, batch, seq, p, final_g_row):
    proj2 = _in_proj(x2d, p["norm_mix_g"], p["w_in"])
    proj3 = proj2.reshape(batch, seq, PROJ_COLS)
    ydn = _deltanet(proj3, p["dn_conv_w"], p["alog_row"], p["dtb_row"], p["dn_norm_g"])
    swa_outs = [_swa_group(proj3, gi) for gi in range(len(SWA_GROUPS))]
    x2d = _merge(x2d, ydn.reshape(batch * seq, DN_WIDTH), proj2, swa_outs, p["sc_conv_w"],
                 p["w_branch_dn"], p["w_branch_sc"], p["w_branch_swa"], p["w_out"], seq)
    hp, route, cnt = _router(x2d, p["norm_ffn_g"], p["w_route"], p["b_route"])
    pos1, pos2, tile_expert, n_active = _moe_plan(route, cnt)
    xs = _dispatch(hp, pos1, pos2)
    ys = _experts(xs, tile_expert, n_active, p["expert_w_gate"], p["expert_w_up"], p["expert_w_down"], p["layer"])
    is_last = final_g_row is not None
    g_row = final_g_row if is_last else p["norm_ffn_g"]
    return _combine(x2d, route, ys, pos1, pos2, g_row, is_last)


def kernel(x, norm_mix_g, w_in, dn_conv_w, dn_a_log, dn_dt_bias, dn_norm_g, sc_conv_w, w_branch_dn, w_branch_sc, w_branch_swa, w_out, norm_ffn_g, router_group_w, router_group_b, router_expert_w, router_expert_b, expert_w_gate, expert_w_up, expert_w_down, final_norm_g):
    batch, seq, _ = x.shape
    depth = w_in.shape[0]
    x2d = x.reshape(batch * seq, D_MODEL)
    for l in range(depth):
        w_route = jnp.concatenate([router_group_w[l], router_expert_w[l]], axis=1)
        w_route = jnp.pad(w_route, ((0, 0), (0, LANES - w_route.shape[1])))
        p = dict(
            norm_mix_g=norm_mix_g[l].reshape(1, D_MODEL),
            w_in=_prep_w_in(w_in, l),
            dn_conv_w=dn_conv_w[l],
            alog_row=_lane_row(dn_a_log[l]),
            dtb_row=_lane_row(dn_dt_bias[l]),
            dn_norm_g=dn_norm_g[l].reshape(1, HEAD_DIM),
            sc_conv_w=sc_conv_w[l],
            w_branch_dn=w_branch_dn[l].astype(BF16),
            w_branch_sc=w_branch_sc[l].astype(BF16),
            w_branch_swa=w_branch_swa[l].astype(BF16),
            w_out=w_out[l].astype(BF16),
            norm_ffn_g=norm_ffn_g[l].reshape(1, D_MODEL),
            w_route=w_route,
            b_route=_lane_row(jnp.concatenate([router_group_b[l], router_expert_b[l]])),
            expert_w_gate=expert_w_gate,
            expert_w_up=expert_w_up,
            expert_w_down=expert_w_down,
            layer=l,
        )
        final_g_row = final_norm_g.reshape(1, D_MODEL) if l == depth - 1 else None
        x2d = _layer(x2d, batch, seq, p, final_g_row)
    return x2d.reshape(batch, seq, D_MODEL)
```

```python
import functools

import jax
import jax.numpy as jnp
import numpy as np
from jax import lax
from jax.experimental import pallas as pl
from jax.experimental.pallas import tpu as pltpu

F32 = jnp.float32
BF16 = jnp.bfloat16

D_MODEL = 2048
RMS_EPS = 1e-6
L2_EPS = 1e-6

DN_HEADS = 8
HEAD_DIM = 128
DN_WIDTH = DN_HEADS * HEAD_DIM
DN_CONV = 4
SC_WIDTH = 1024
SC_CONV = 3
SWA_GROUPS = ((128, 1), (512, 4), (2048, 16))
SWA_HEADS_PER_GROUP = 4
SWA_HEADS = 12
SWA_GROUP_WIDTH = SWA_HEADS_PER_GROUP * HEAD_DIM
SWA_BLOCK = 128
ALIBI_MAX_BIAS = 8.0
MOE_GROUPS = 4
MOE_EPG = 8
MOE_EXPERTS = 32
MOE_FF = 512

OFF_GATE = 0
OFF_DNQ = 6144
OFF_DNK = OFF_DNQ + DN_WIDTH
OFF_DNV = OFF_DNK + DN_WIDTH
OFF_DNZ = OFF_DNV + DN_WIDTH
OFF_SCC = OFF_DNZ + DN_WIDTH
OFF_SCB = OFF_SCC + SC_WIDTH
OFF_SCX = OFF_SCB + SC_WIDTH
OFF_SWA = OFF_SCX + SC_WIDTH
OFF_AB = OFF_SWA + 3 * SWA_HEADS * HEAD_DIM
PROJ_COLS = 18432

LANES = 128
SWA_OUT_SLOTS = SWA_HEADS_PER_GROUP + 1
CHUNK = 128
HALO = 16

VMEM_LIMIT = 56 * 1024 * 1024


def _cparams(sem):
    return pltpu.CompilerParams(dimension_semantics=sem, vmem_limit_bytes=VMEM_LIMIT)


def _sigmoid(x):
    return 1.0 / (1.0 + jnp.exp(-x))


def _silu(x):
    return x * _sigmoid(x)


def _softplus(x):
    return jnp.maximum(x, 0.0) + jnp.log(1.0 + jnp.exp(-jnp.abs(x)))


IN_TM = 1024
IN_TN = 1024
NORM_ROWS = 64


def _rmsnorm_rows(x, g):
    ms = jnp.mean(x * x, axis=-1, keepdims=True)
    return x * lax.rsqrt(ms + RMS_EPS) * g


def _in_proj_kernel(x_ref, g_ref, w_ref, o_ref, h_ref):
    @pl.when(pl.program_id(1) == 0)
    def _():
        def body(r, c):
            rows = pl.ds(pl.multiple_of(r * NORM_ROWS, NORM_ROWS), NORM_ROWS)
            h_ref[rows, :] = _rmsnorm_rows(x_ref[rows, :], g_ref[...]).astype(BF16)
            return c
        lax.fori_loop(0, IN_TM // NORM_ROWS, body, 0)

    o_ref[...] = jnp.dot(h_ref[...], w_ref[...], preferred_element_type=F32).astype(o_ref.dtype)


def _in_proj(x2d, g_row, w_bf16):
    m = x2d.shape[0]
    return pl.pallas_call(
        _in_proj_kernel,
        grid=(m // IN_TM, PROJ_COLS // IN_TN),
        in_specs=[
            pl.BlockSpec((IN_TM, D_MODEL), lambda i, j: (i, 0)),
            pl.BlockSpec((1, D_MODEL), lambda i, j: (0, 0)),
            pl.BlockSpec((D_MODEL, IN_TN), lambda i, j: (0, j)),
        ],
        out_specs=pl.BlockSpec((IN_TM, IN_TN), lambda i, j: (i, j)),
        out_shape=jax.ShapeDtypeStruct((m, PROJ_COLS), BF16),
        scratch_shapes=[pltpu.VMEM((IN_TM, D_MODEL), BF16)],
        compiler_params=_cparams(("parallel", "arbitrary")),
        name="in_proj",
    )(x2d, g_row, w_bf16)


DN_HB = 8
DN_TS = 512
DN_W = DN_HB * HEAD_DIM
DN_DOUBLINGS = CHUNK.bit_length() - 2


def _dot_nt(a, b):
    return lax.dot_general(a, b, (((1,), (1,)), ((), ())), preferred_element_type=F32)


def _dot_tn(a, b):
    return lax.dot_general(a, b, (((0,), (0,)), ((), ())), preferred_element_type=F32)


def _causal_conv(xx, w, width):
    acc = None
    for j in range(width):
        shift = width - 1 - j
        xs = xx if shift == 0 else pltpu.roll(xx, shift, axis=0)
        term = xs[HALO:, :] * w[j:j + 1, :]
        acc = term if acc is None else acc + term
    return acc


def _deltanet_kernel(q_ref, k_ref, v_ref, z_ref, ab_ref, cwq_ref, cwk_ref, cwv_ref,
                     alog_ref, dtb_ref, ng_ref, o_ref,
                     xpad_ref, state_ref):
    hg = pl.program_id(1)
    st = pl.program_id(2)

    @pl.when(st == 0)
    def _():
        xpad_ref[:, 0:HALO, :] = jnp.zeros((3, HALO, DN_W), BF16)
        state_ref[...] = jnp.zeros_like(state_ref)

    xpad_ref[0, HALO:, :] = q_ref[0]
    xpad_ref[1, HALO:, :] = k_ref[0]
    xpad_ref[2, HALO:, :] = v_ref[0]

    row_i = lax.broadcasted_iota(jnp.int32, (CHUNK, CHUNK), 0)
    col_j = lax.broadcasted_iota(jnp.int32, (CHUNK, CHUNK), 1)
    causal = row_i >= col_j
    strict = row_i > col_j
    tri = causal.astype(F32)
    eye = (row_i == col_j).astype(F32)
    lane = lax.broadcasted_iota(jnp.int32, (CHUNK, LANES), 1)
    neg_decay_rate = -jnp.exp(alog_ref[...])
    cws = (cwq_ref[...], cwk_ref[...], cwv_ref[...])
    ng = ng_ref[...]

    def chunk_body(c, carry):
        r0 = pl.multiple_of(c * CHUNK, CHUNK)
        win = pl.ds(r0, CHUNK + HALO)
        rows = pl.ds(r0, CHUNK)
        conv = [_silu(_causal_conv(xpad_ref[t, win, :].astype(F32), cws[t], DN_CONV)) for t in range(3)]
        ab = ab_ref[0, rows, :].astype(F32)
        g_raw = neg_decay_rate * _softplus(ab + dtb_ref[...])
        g_cum = jnp.dot(tri, g_raw, preferred_element_type=F32, precision=lax.Precision.HIGHEST)
        beta_all = _sigmoid(ab)
        z = z_ref[0, rows, :].astype(F32)

        heads = range(DN_HB)
        sls = [slice(i * HEAD_DIM, (i + 1) * HEAD_DIM) for i in heads]
        dot = functools.partial(jnp.dot, preferred_element_type=F32)

        g_col = [jnp.sum(jnp.where(lane == hg * DN_HB + i, g_cum, 0.0), axis=-1, keepdims=True) for i in heads]
        beta = [jnp.sum(jnp.where(lane == hg * DN_HB + i + DN_HEADS, beta_all, 0.0), axis=-1, keepdims=True)
                for i in heads]
        g_last = [g[CHUNK - 1:CHUNK, :] for g in g_col]
        eg = [jnp.exp(g) for g in g_col]
        ek = [jnp.exp(gl - g) for gl, g in zip(g_last, g_col)]
        g_b = [jnp.broadcast_to(g, (CHUNK, CHUNK)) for g in g_col]
        decay = [jnp.exp(jnp.where(causal, gb - gb.T, -jnp.inf)) for gb in g_b]

        qf = [conv[0][:, sl] for sl in sls]
        kf = [conv[1][:, sl] for sl in sls]
        vf = [conv[2][:, sl] for sl in sls]
        q = [x * lax.rsqrt(jnp.sum(x * x, axis=-1, keepdims=True) + L2_EPS) * (HEAD_DIM ** -0.5) for x in qf]
        k = [x * lax.rsqrt(jnp.sum(x * x, axis=-1, keepdims=True) + L2_EPS) for x in kf]
        kb = [ki * bi for ki, bi in zip(k, beta)]
        a2 = [_dot_nt(jnp.concatenate([qi, kbi], axis=0).astype(BF16), ki.astype(BF16))
              for qi, kbi, ki in zip(q, kb, k)]
        attn = [a[:CHUNK] * d for a, d in zip(a2, decay)]
        n_mat = [jnp.where(strict, -(a[CHUNK:] * d), 0.0) for a, d in zip(a2, decay)]

        u_mat = [eye + n for n in n_mat]
        p_mat = [dot(n.astype(BF16), n.astype(BF16)) for n in n_mat]
        for _ in range(DN_DOUBLINGS - 1):
            up = [dot(jnp.concatenate([u.astype(BF16), p.astype(BF16)], axis=0), p.astype(BF16))
                  for u, p in zip(u_mat, p_mat)]
            u_mat = [u + x[:CHUNK] for u, x in zip(u_mat, up)]
            p_mat = [x[CHUNK:] for x in up]
        u_mat = [u + dot(u.astype(BF16), p.astype(BF16)) for u, p in zip(u_mat, p_mat)]
        uw = [dot(u.astype(BF16), jnp.concatenate([vi * bi, kbi * egi], axis=1).astype(BF16))
              for u, vi, bi, kbi, egi in zip(u_mat, vf, beta, kb, eg)]

        s_old = [state_ref[i] for i in heads]
        qw = [dot(jnp.concatenate([qi * egi, x[:, HEAD_DIM:]], axis=0).astype(BF16), s.astype(BF16))
              for qi, egi, x, s in zip(q, eg, uw, s_old)]
        v16 = [(x[:, :HEAD_DIM] - y[CHUNK:]).astype(BF16) for x, y in zip(uw, qw)]
        o = [y[:CHUNK] + dot(a.astype(BF16), v) for y, a, v in zip(qw, attn, v16)]
        s_new = [s * jnp.exp(gl) + _dot_tn((ki * eki).astype(BF16), v)
                 for s, gl, ki, eki, v in zip(s_old, g_last, k, ek, v16)]
        for i in heads:
            state_ref[i] = s_new[i]
            on = o[i] * lax.rsqrt(jnp.mean(o[i] * o[i], axis=-1, keepdims=True) + RMS_EPS) * ng
            o_ref[0, rows, sls[i]] = (on * _silu(z[:, sls[i]])).astype(o_ref.dtype)
        return carry

    lax.fori_loop(0, DN_TS // CHUNK, chunk_body, 0)

    tail = pl.ds(DN_TS, HALO)
    head = pl.ds(0, HALO)
    for t in range(3):
        xpad_ref[t, head, :] = xpad_ref[t, tail, :]


def _deltanet(proj3, conv_w, alog_row, dtb_row, ng_row):
    b, s, _ = proj3.shape
    qb, kb_, vb, zb = (off // DN_W for off in (OFF_DNQ, OFF_DNK, OFF_DNV, OFF_DNZ))
    act = lambda base: pl.BlockSpec((1, DN_TS, DN_W), lambda bi, hg, st: (bi, st, base + hg))
    cw = lambda base: pl.BlockSpec((DN_CONV, DN_W), lambda bi, hg, st: (0, base + hg))
    row = pl.BlockSpec((1, LANES), lambda bi, hg, st: (0, 0))
    return pl.pallas_call(
        _deltanet_kernel,
        grid=(b, DN_HEADS // DN_HB, s // DN_TS),
        in_specs=[
            act(qb), act(kb_), act(vb), act(zb),
            pl.BlockSpec((1, DN_TS, LANES), lambda bi, hg, st: (bi, st, OFF_AB // LANES)),
            cw(0), cw(DN_WIDTH // DN_W), cw(2 * DN_WIDTH // DN_W),
            row, row, row,
        ],
        out_specs=pl.BlockSpec((1, DN_TS, DN_W), lambda bi, hg, st: (bi, st, hg)),
        out_shape=jax.ShapeDtypeStruct((b, s, DN_WIDTH), BF16),
        scratch_shapes=[
            pltpu.VMEM((3, DN_TS + HALO, DN_W), BF16),
            pltpu.VMEM((DN_HB, HEAD_DIM, HEAD_DIM), F32),
        ],
        compiler_params=_cparams(("parallel", "parallel", "arbitrary")),
        name="deltanet",
    )(proj3, proj3, proj3, proj3, proj3, conv_w, conv_w, conv_w, alog_row, dtb_row, ng_row)


SWA_CAST_ROWS = 256


def _swa_kernel(q_ref, k_ref, v_ref, o_ref, *scratch, seq, dil, slopes):
    n_blocks = seq // dil // SWA_BLOCK
    qi = lax.broadcasted_iota(jnp.int32, (SWA_BLOCK, SWA_BLOCK), 0)
    kj = lax.broadcasted_iota(jnp.int32, (SWA_BLOCK, SWA_BLOCK), 1)
    step_cur = (qi - kj).astype(F32)
    step_prev = step_cur + float(SWA_BLOCK)
    ok_cur = qi >= kj
    ok_prev = kj >= qi
    lane = lax.broadcasted_iota(jnp.int32, (SWA_BLOCK, LANES), 1)
    scale = HEAD_DIM ** -0.5
    heads = range(SWA_HEADS_PER_GROUP)
    sls = [slice(hh * HEAD_DIM, (hh + 1) * HEAD_DIM) for hh in heads]
    m_h = [float(slopes[hh]) * float(dil) for hh in heads]

    if dil > 1:
        (f32_ref,) = scratch

        def cast_body(c, carry):
            rows = pl.ds(pl.multiple_of(c * SWA_CAST_ROWS, SWA_CAST_ROWS), SWA_CAST_ROWS)
            for t, ref in enumerate((q_ref, k_ref, v_ref)):
                for hh in heads:
                    f32_ref[t * SWA_HEADS_PER_GROUP + hh, rows, :] = ref[0, rows, sls[hh]].astype(F32)
            return carry
        lax.fori_loop(0, seq // SWA_CAST_ROWS, cast_body, 0)

        def load(t, rows):
            return [f32_ref[t * SWA_HEADS_PER_GROUP + hh, rows, :].astype(BF16) for hh in heads]
    else:
        refs = (q_ref, k_ref, v_ref)

        def load(t, rows):
            return [refs[t][0, rows, sls[hh]] for hh in heads]

    def block_body(it, carry):
        r = it // n_blocks
        n = it % n_blocks
        base = n * (SWA_BLOCK * dil) + r
        prev_base = jnp.maximum(n - 1, 0) * (SWA_BLOCK * dil) + r
        if dil > 1:
            rows = pl.ds(base, SWA_BLOCK, stride=dil)
            prev_rows = pl.ds(prev_base, SWA_BLOCK, stride=dil)
        else:
            rows = pl.ds(pl.multiple_of(base, SWA_BLOCK), SWA_BLOCK)
            prev_rows = pl.ds(pl.multiple_of(prev_base, SWA_BLOCK), SWA_BLOCK)
        qn, kc, vc = load(0, rows), load(1, rows), load(2, rows)
        kp, vp = load(1, prev_rows), load(2, prev_rows)
        ok_p = jnp.logical_and(ok_prev, n > 0)

        s_cur = [jnp.where(ok_cur, _dot_nt(qh, kh) * scale - m * step_cur, -jnp.inf)
                 for qh, kh, m in zip(qn, kc, m_h)]
        s_prev = [jnp.where(ok_p, _dot_nt(qh, kh) * scale - m * step_prev, -jnp.inf)
                  for qh, kh, m in zip(qn, kp, m_h)]
        mx = [jnp.maximum(jnp.max(a, axis=-1, keepdims=True), jnp.max(b, axis=-1, keepdims=True))
              for a, b in zip(s_cur, s_prev)]
        e_cur = [jnp.exp(a - m) for a, m in zip(s_cur, mx)]
        e_prev = [jnp.exp(b - m) for b, m in zip(s_prev, mx)]
        den = [jnp.sum(a, axis=-1, keepdims=True) + jnp.sum(b, axis=-1, keepdims=True)
               for a, b in zip(e_cur, e_prev)]
        pv = [jnp.dot(a.astype(BF16), vch, preferred_element_type=F32)
              + jnp.dot(b.astype(BF16), vph, preferred_element_type=F32)
              for a, b, vch, vph in zip(e_cur, e_prev, vc, vp)]
        lse_tile = jnp.zeros((SWA_BLOCK, LANES), F32)
        for hh in heads:
            o_ref[0, hh, rows, :] = pv[hh] * (1.0 / den[hh])
            lse_tile = jnp.where(lane == hh, mx[hh] + jnp.log(den[hh]), lse_tile)
        o_ref[0, SWA_HEADS_PER_GROUP, rows, :] = lse_tile
        return carry

    lax.fori_loop(0, seq // SWA_BLOCK, block_body, 0)


def _swa_group(proj3, gi):
    batch, seq, _ = proj3.shape
    window, dil = SWA_GROUPS[gi]
    assert window // dil == SWA_BLOCK and seq % (dil * SWA_BLOCK) == 0
    slopes = 2.0 ** (-ALIBI_MAX_BIAS * np.arange(1, SWA_HEADS + 1) / SWA_HEADS)
    slopes = slopes.reshape(len(SWA_GROUPS), SWA_HEADS_PER_GROUP)[gi]

    def spec(which):
        cb = (OFF_SWA + which * SWA_HEADS * HEAD_DIM) // SWA_GROUP_WIDTH + gi
        return pl.BlockSpec((1, seq, SWA_GROUP_WIDTH), lambda bi: (bi, 0, cb))

    scratch = [pltpu.VMEM((3 * SWA_HEADS_PER_GROUP, seq, HEAD_DIM), F32)] if dil > 1 else []
    return pl.pallas_call(
        functools.partial(_swa_kernel, seq=seq, dil=dil, slopes=tuple(float(s) for s in slopes)),
        grid=(batch,),
        in_specs=[spec(0), spec(1), spec(2)],
        out_specs=pl.BlockSpec((1, SWA_OUT_SLOTS, seq, HEAD_DIM), lambda bi: (bi, 0, 0, 0)),
        out_shape=jax.ShapeDtypeStruct((batch, SWA_OUT_SLOTS, seq, HEAD_DIM), F32),
        scratch_shapes=scratch,
        compiler_params=_cparams(("parallel",)),
        name=f"swa_g{gi}",
    )(proj3, proj3, proj3)


MG_TM = 256


def _merge_kernel(x_ref, ydn_ref, gate_ref, scc_ref, scb_ref, scx_ref, scc_h_ref, scx_h_ref,
                  s0_ref, s1_ref, s2_ref, scw_ref, wdn_ref, wsc_ref, wswa_ref, wout_ref, o_ref,
                  *, tiles_per_seq):
    i = pl.program_id(0)
    first = (i % tiles_per_seq) == 0

    cx_cur = scc_ref[...].astype(F32) * scx_ref[...].astype(F32)
    cx_halo = scc_h_ref[...].astype(F32) * scx_h_ref[...].astype(F32)
    cx_halo = jnp.where(first, 0.0, cx_halo)
    cx = jnp.concatenate([cx_halo, cx_cur], axis=0)
    y_sc = scb_ref[...].astype(F32) * _causal_conv(cx, scw_ref[...], SC_CONV)

    outs = (s0_ref, s1_ref, s2_ref)
    lses = [o[0, SWA_HEADS_PER_GROUP] for o in outs]
    mx = jnp.maximum(jnp.maximum(lses[0], lses[1]), lses[2])
    es = [jnp.exp(l - mx) for l in lses]
    inv = 1.0 / (es[0] + es[1] + es[2])
    heads = []
    for hh in range(SWA_HEADS_PER_GROUP):
        acc = None
        for gi in range(3):
            wt = (es[gi] * inv)[:, hh:hh + 1]
            term = wt * outs[gi][0, hh]
            acc = term if acc is None else acc + term
        heads.append(acc)
    y_swa = jnp.concatenate(heads, axis=1)

    gates = gate_ref[...].astype(F32)
    merged = (_sigmoid(gates[:, 0:D_MODEL])
              * jnp.dot(ydn_ref[...], wdn_ref[...], preferred_element_type=F32)
              + _sigmoid(gates[:, D_MODEL:2 * D_MODEL])
              * jnp.dot(y_sc.astype(BF16), wsc_ref[...], preferred_element_type=F32)
              + _sigmoid(gates[:, 2 * D_MODEL:3 * D_MODEL])
              * jnp.dot(y_swa.astype(BF16), wswa_ref[...], preferred_element_type=F32))
    o_ref[...] = x_ref[...] + jnp.dot(merged.astype(BF16), wout_ref[...], preferred_element_type=F32)


def _merge(x2d, ydn2, proj2, swa_outs, sc_conv_w, wdn, wsc, wswa, wout, seq):
    m = x2d.shape[0]
    tiles_per_seq = seq // MG_TM
    halo_blocks = MG_TM // HALO
    rows = lambda w, cb: pl.BlockSpec((MG_TM, w), lambda i: (i, cb))
    halo = lambda cb: pl.BlockSpec((HALO, SC_WIDTH), lambda i: (jnp.maximum(i * halo_blocks - 1, 0), cb))
    full = lambda a: pl.BlockSpec(a.shape, lambda i: (0, 0), pipeline_mode=pl.Buffered(1))
    swa = pl.BlockSpec((1, SWA_OUT_SLOTS, MG_TM, HEAD_DIM),
                       lambda i: (i // tiles_per_seq, 0, i % tiles_per_seq, 0))
    return pl.pallas_call(
        functools.partial(_merge_kernel, tiles_per_seq=tiles_per_seq),
        grid=(m // MG_TM,),
        in_specs=[
            rows(D_MODEL, 0),
            rows(DN_WIDTH, 0),
            rows(3 * D_MODEL, OFF_GATE // (3 * D_MODEL)),
            rows(SC_WIDTH, OFF_SCC // SC_WIDTH), rows(SC_WIDTH, OFF_SCB // SC_WIDTH),
            rows(SC_WIDTH, OFF_SCX // SC_WIDTH),
            halo(OFF_SCC // SC_WIDTH), halo(OFF_SCX // SC_WIDTH),
            swa, swa, swa,
            full(sc_conv_w), full(wdn), full(wsc), full(wswa), full(wout),
        ],
        out_specs=rows(D_MODEL, 0),
        out_shape=jax.ShapeDtypeStruct((m, D_MODEL), F32),
        compiler_params=_cparams(("parallel",)),
        name="merge",
    )(x2d, ydn2, proj2, proj2, proj2, proj2, proj2, proj2, *swa_outs, sc_conv_w, wdn, wsc, wswa, wout)


RT_TM = 512
ROUTE_LANE0 = MOE_GROUPS
R_E1, R_E2, R_W1, R_W2, R_RANK1, R_RANK2 = range(6)
HALF = D_MODEL // 2


def _pack_pair(lo, hi):
    lo_b = lax.bitcast_convert_type(lo.astype(BF16).astype(F32), jnp.uint32)
    hi_b = lax.bitcast_convert_type(hi.astype(BF16).astype(F32), jnp.uint32)
    return (hi_b & jnp.uint32(0xFFFF0000)) | (lo_b >> jnp.uint32(16))


def _unpack_pair(p):
    lo = lax.bitcast_convert_type(p << jnp.uint32(16), F32)
    hi = lax.bitcast_convert_type(p & jnp.uint32(0xFFFF0000), F32)
    return lo, hi


def _router_kernel(x_ref, g_ref, wr_ref, br_ref, hp_ref, route_ref, cnt_ref, carry_ref):
    @pl.when(pl.program_id(0) == 0)
    def _():
        carry_ref[...] = jnp.zeros_like(carry_ref)

    h = _rmsnorm_rows(x_ref[...], g_ref[...])
    hp_ref[...] = _pack_pair(h[:, :HALF], h[:, HALF:])
    w = wr_ref[...]
    h_hi, w_hi = h.astype(BF16), w.astype(BF16)
    h_lo = (h - h_hi.astype(F32)).astype(BF16)
    w_lo = (w - w_hi.astype(F32)).astype(BF16)
    dot = functools.partial(jnp.dot, preferred_element_type=F32)
    logits = dot(h_hi, w_hi) + dot(h_lo, w_hi) + dot(h_hi, w_lo) + br_ref[...]
    lane = lax.broadcasted_iota(jnp.int32, logits.shape, 1)
    big = jnp.int32(LANES)
    neg = -jnp.inf

    is_grp = lane < MOE_GROUPS
    gl = jnp.where(is_grp, logits, neg)
    gmax = jnp.max(gl, axis=-1, keepdims=True)
    grp = jnp.min(jnp.where(gl == gmax, lane, big), axis=-1, keepdims=True)
    g_w = 1.0 / jnp.sum(jnp.exp(gl - gmax), axis=-1, keepdims=True)

    lo = ROUTE_LANE0 + grp * MOE_EPG
    in_grp = jnp.logical_and(lane >= lo, lane < lo + MOE_EPG)
    el = jnp.where(in_grp, logits, neg)
    m1 = jnp.max(el, axis=-1, keepdims=True)
    i1 = jnp.min(jnp.where(el == m1, lane, big), axis=-1, keepdims=True)
    el2 = jnp.where(lane == i1, neg, el)
    m2 = jnp.max(el2, axis=-1, keepdims=True)
    i2 = jnp.min(jnp.where(el2 == m2, lane, big), axis=-1, keepdims=True)
    e2 = jnp.exp(m2 - m1)
    w1 = g_w / (1.0 + e2)
    w2 = g_w * e2 / (1.0 + e2)

    tm = logits.shape[0]
    onehot = jnp.where(jnp.logical_or(lane == i1, lane == i2), 1.0, 0.0)
    earlier = (lax.broadcasted_iota(jnp.int32, (tm, tm), 0) > lax.broadcasted_iota(jnp.int32, (tm, tm), 1))
    before = carry_ref[...] + jnp.dot(jnp.where(earlier, 1.0, 0.0).astype(BF16), onehot.astype(BF16),
                                      preferred_element_type=F32)
    rank1 = jnp.sum(jnp.where(lane == i1, before, 0.0), axis=-1, keepdims=True)
    rank2 = jnp.sum(jnp.where(lane == i2, before, 0.0), axis=-1, keepdims=True)
    carry_ref[...] += jnp.sum(onehot, axis=0, keepdims=True)
    cnt_ref[...] = carry_ref[...]

    rec = jnp.zeros_like(logits)
    for ln, val in ((R_E1, (i1 - ROUTE_LANE0).astype(F32)), (R_E2, (i2 - ROUTE_LANE0).astype(F32)),
                    (R_W1, w1), (R_W2, w2), (R_RANK1, rank1), (R_RANK2, rank2)):
        rec = jnp.where(lane == ln, val, rec)
    route_ref[...] = rec


def _router(x2d, g_row, w_route, b_route):
    m = x2d.shape[0]
    return pl.pallas_call(
        _router_kernel,
        grid=(m // RT_TM,),
        in_specs=[
            pl.BlockSpec((RT_TM, D_MODEL), lambda i: (i, 0)),
            pl.BlockSpec((1, D_MODEL), lambda i: (0, 0)),
            pl.BlockSpec((D_MODEL, LANES), lambda i: (0, 0)),
            pl.BlockSpec((1, LANES), lambda i: (0, 0)),
        ],
        out_specs=[pl.BlockSpec((RT_TM, HALF), lambda i: (i, 0)),
                   pl.BlockSpec((RT_TM, LANES), lambda i: (i, 0)),
                   pl.BlockSpec((1, LANES), lambda i: (0, 0))],
        out_shape=[jax.ShapeDtypeStruct((m, HALF), jnp.uint32), jax.ShapeDtypeStruct((m, LANES), F32),
                   jax.ShapeDtypeStruct((1, LANES), F32)],
        scratch_shapes=[pltpu.VMEM((1, LANES), F32)],
        compiler_params=_cparams(("arbitrary",)),
        name="router",
    )(x2d, g_row, w_route, b_route)


EXP_TM = 512


def _n_row_tiles(n_tokens):
    return 2 * n_tokens // EXP_TM + MOE_EXPERTS


def _positions_kernel(route_ref, offs_ref, pos_ref):
    rec = route_ref[...]
    lane = lax.broadcasted_iota(jnp.int32, rec.shape, 1)
    offs = offs_ref[...]

    def first_row(e):
        return jnp.sum(jnp.where(lane == e.astype(jnp.int32), offs, 0.0), axis=-1, keepdims=True)

    pos1 = first_row(rec[:, R_E1:R_E1 + 1]) + rec[:, R_RANK1:R_RANK1 + 1]
    pos2 = first_row(rec[:, R_E2:R_E2 + 1]) + rec[:, R_RANK2:R_RANK2 + 1]
    pos_ref[...] = jnp.where(lane == 0, pos1, jnp.where(lane == 1, pos2, 0.0)).astype(jnp.int32)


def _moe_plan(route, cnt):
    m = route.shape[0]
    n_tiles = _n_row_tiles(m)
    counts = cnt[0, ROUTE_LANE0:ROUTE_LANE0 + MOE_EXPERTS].astype(jnp.int32)
    padded = (counts + EXP_TM - 1) // EXP_TM * EXP_TM
    ends = jnp.cumsum(padded)
    offs = ends - padded
    pos = pl.pallas_call(
        _positions_kernel,
        grid=(m // RT_TM,),
        in_specs=[pl.BlockSpec((RT_TM, LANES), lambda i: (i, 0)), pl.BlockSpec((1, LANES), lambda i: (0, 0))],
        out_specs=pl.BlockSpec((RT_TM, LANES), lambda i: (i, 0)),
        out_shape=jax.ShapeDtypeStruct((m, LANES), jnp.int32),
        compiler_params=_cparams(("parallel",)),
        name="moe_positions",
    )(route, _lane_row(offs))
    n_active = ends[-1] // EXP_TM
    tile_row = jnp.maximum(jnp.minimum(jnp.arange(n_tiles), n_active - 1), 0) * EXP_TM
    tile_expert = jnp.minimum(jnp.sum(ends[None, :] <= tile_row[:, None], axis=1), MOE_EXPERTS - 1)
    used = counts > 0
    ids = jnp.where(used, jnp.arange(MOE_EXPERTS), MOE_EXPERTS)
    next_used = jnp.concatenate([lax.cummin(ids, reverse=True)[1:], jnp.full((1,), MOE_EXPERTS, ids.dtype)])
    parity = (jnp.cumsum(used.astype(jnp.int32)) - 1) % 2
    i32 = lambda a: a.astype(jnp.int32)
    return (pos[:, 0], pos[:, 1], i32(tile_expert), i32(n_active).reshape(1),
            i32(next_used[tile_expert]), i32(parity[tile_expert]))


DSP_TB = 512
ROW_UNROLL = 8


def _dispatch_kernel(pos1_ref, pos2_ref, hp_ref, xs_in_ref, xs_ref, sem):
    del xs_in_ref

    def start(j8, c):
        for u in range(ROW_UNROLL):
            j = j8 * ROW_UNROLL + u
            src = hp_ref.at[pl.ds(j, 1), :]
            pltpu.make_async_copy(src, xs_ref.at[pl.ds(pos1_ref[0, 0, j], 1), :], sem).start(priority=0)
            pltpu.make_async_copy(src, xs_ref.at[pl.ds(pos2_ref[0, 0, j], 1), :], sem).start(priority=1)
        return c

    lax.fori_loop(0, DSP_TB // ROW_UNROLL, start, 0)
    for _ in range(2):
        pltpu.make_async_copy(hp_ref, xs_ref.at[pl.ds(0, DSP_TB), :], sem).wait()


def _dispatch(hp, pos1, pos2):
    m = hp.shape[0]
    n_rows = _n_row_tiles(m) * EXP_TM
    pos_spec = pl.BlockSpec((1, 1, DSP_TB), lambda i: (i, 0, 0), memory_space=pltpu.SMEM)
    return pl.pallas_call(
        _dispatch_kernel,
        grid=(m // DSP_TB,),
        in_specs=[pos_spec, pos_spec,
                  pl.BlockSpec((DSP_TB, HALF), lambda i: (i, 0)),
                  pl.BlockSpec(memory_space=pl.ANY)],
        out_specs=pl.BlockSpec(memory_space=pl.ANY),
        out_shape=jax.ShapeDtypeStruct((n_rows, HALF), jnp.uint32),
        scratch_shapes=[pltpu.SemaphoreType.DMA(())],
        input_output_aliases={3: 0},
        compiler_params=_cparams(("arbitrary",)),
        name="moe_dispatch",
    )(pos1.reshape(m // DSP_TB, 1, DSP_TB), pos2.reshape(m // DSP_TB, 1, DSP_TB), hp,
      jnp.zeros((n_rows, HALF), jnp.uint32))


W_CAST_ROWS = 256


def _cast_rows(src_ref, dst_ref):
    n_rows = dst_ref.shape[0]
    slab = min(W_CAST_ROWS, n_rows)

    def body(r, c):
        rows = pl.ds(pl.multiple_of(r * slab, slab), slab)
        dst_ref[rows, :] = src_ref[rows, :].astype(BF16)
        return c
    lax.fori_loop(0, n_rows // slab, body, 0)


def _expert_kernel(te_ref, nact_ref, next_ref, par_ref, xs_ref, wg_hbm, wu_hbm, wd_hbm, ys_ref,
                   sg_ref, su_ref, sd_ref, wg16_ref, wu16_ref, wd16_ref, sem, *, layer):
    i = pl.program_id(0)
    active = i < nact_ref[0]
    expert = te_ref[i]
    slot = par_ref[i]
    new_expert = jnp.logical_or(i == 0, expert != te_ref[jnp.maximum(i - 1, 0)])

    def copies(e, s):
        return [pltpu.make_async_copy(hbm.at[layer, e], stage.at[s], sem.at[s])
                for hbm, stage in ((wg_hbm, sg_ref), (wu_hbm, su_ref), (wd_hbm, sd_ref))]

    @pl.when(jnp.logical_and(active, i == 0))
    def _():
        for cp in copies(expert, slot):
            cp.start()

    @pl.when(jnp.logical_and(active, new_expert))
    def _():
        for cp in copies(expert, slot):
            cp.wait()

        @pl.when(next_ref[i] < MOE_EXPERTS)
        def _():
            for cp in copies(next_ref[i], 1 - slot):
                cp.start()

        _cast_rows(sg_ref.at[slot], wg16_ref)
        _cast_rows(su_ref.at[slot], wu16_ref)
        _cast_rows(sd_ref.at[slot], wd16_ref)

    @pl.when(active)
    def _():
        lo, hi = _unpack_pair(xs_ref[...])
        lo, hi = lo.astype(BF16), hi.astype(BF16)
        dot = functools.partial(jnp.dot, preferred_element_type=F32)
        gate = dot(lo, wg16_ref[:HALF, :]) + dot(hi, wg16_ref[HALF:, :])
        up = dot(lo, wu16_ref[:HALF, :]) + dot(hi, wu16_ref[HALF:, :])
        y = dot((_silu(gate) * up).astype(BF16), wd16_ref[...])
        ys_ref[...] = _pack_pair(y[:, :HALF], y[:, HALF:])

    @pl.when(jnp.logical_not(active))
    def _():
        ys_ref[...] = jnp.zeros_like(ys_ref)


def _experts(xs, tile_expert, n_active, next_expert, parity, wg, wu, wd, layer):
    n_tiles = xs.shape[0] // EXP_TM
    rows = pl.BlockSpec((EXP_TM, HALF),
                        lambda i, te, na, nx, par: (jnp.maximum(jnp.minimum(i, na[0] - 1), 0), 0))
    out_rows = pl.BlockSpec((EXP_TM, HALF), lambda i, te, na, nx, par: (i, 0))
    hbm = pl.BlockSpec(memory_space=pl.ANY)
    return pl.pallas_call(
        functools.partial(_expert_kernel, layer=layer),
        grid_spec=pltpu.PrefetchScalarGridSpec(
            num_scalar_prefetch=4,
            grid=(n_tiles,),
            in_specs=[rows, hbm, hbm, hbm],
            out_specs=out_rows,
            scratch_shapes=[pltpu.VMEM((2, D_MODEL, MOE_FF), F32), pltpu.VMEM((2, D_MODEL, MOE_FF), F32),
                            pltpu.VMEM((2, MOE_FF, D_MODEL), F32),
                            pltpu.VMEM((D_MODEL, MOE_FF), BF16), pltpu.VMEM((D_MODEL, MOE_FF), BF16),
                            pltpu.VMEM((MOE_FF, D_MODEL), BF16),
                            pltpu.SemaphoreType.DMA((2,))],
        ),
        out_shape=jax.ShapeDtypeStruct(xs.shape, jnp.uint32),
        compiler_params=_cparams(("arbitrary",)),
        name="moe_experts",
    )(tile_expert, n_active, next_expert, parity, xs, wg, wu, wd)


CMB_TC = 256


def _combine_kernel(pos1_ref, pos2_ref, pos1_next_ref, pos2_next_ref, x_ref, route_ref, g_ref, ys_ref,
                    o_ref, buf_ref, sem, *, final_norm):
    i = pl.program_id(0)
    slot = i % 2

    def gather(p1_ref, p2_ref, s):
        def start(j8, c):
            for u in range(ROW_UNROLL):
                j = j8 * ROW_UNROLL + u
                for k, p_ref in enumerate((p1_ref, p2_ref)):
                    pltpu.make_async_copy(ys_ref.at[pl.ds(p_ref[0, 0, j], 1), :],
                                          buf_ref.at[s, k, pl.ds(j, 1), :], sem.at[s]).start(priority=k)
            return c
        lax.fori_loop(0, CMB_TC // ROW_UNROLL, start, 0)

    @pl.when(i == 0)
    def _():
        gather(pos1_ref, pos2_ref, 0)

    @pl.when(i + 1 < pl.num_programs(0))
    def _():
        gather(pos1_next_ref, pos2_next_ref, 1 - slot)

    for k in range(2):
        pltpu.make_async_copy(ys_ref.at[pl.ds(0, CMB_TC), :], buf_ref.at[slot, k], sem.at[slot]).wait()

    rec = route_ref[...]
    w1, w2 = rec[:, R_W1:R_W1 + 1], rec[:, R_W2:R_W2 + 1]
    lo1, hi1 = _unpack_pair(buf_ref[slot, 0])
    lo2, hi2 = _unpack_pair(buf_ref[slot, 1])
    out_lo = x_ref[:, :HALF] + w1 * lo1 + w2 * lo2
    out_hi = x_ref[:, HALF:] + w1 * hi1 + w2 * hi2
    if final_norm:
        ms = (jnp.sum(out_lo * out_lo, axis=-1, keepdims=True)
              + jnp.sum(out_hi * out_hi, axis=-1, keepdims=True)) * (1.0 / D_MODEL)
        inv = lax.rsqrt(ms + RMS_EPS)
        out_lo = out_lo * inv * g_ref[:, :HALF]
        out_hi = out_hi * inv * g_ref[:, HALF:]
    o_ref[:, :HALF] = out_lo
    o_ref[:, HALF:] = out_hi


def _combine(x2d, route, ys, pos1, pos2, g_row, final_norm):
    m = x2d.shape[0]
    n_steps = m // CMB_TC
    pos_spec = pl.BlockSpec((1, 1, CMB_TC), lambda i: (i, 0, 0), memory_space=pltpu.SMEM)
    next_spec = pl.BlockSpec((1, 1, CMB_TC), lambda i: (jnp.minimum(i + 1, n_steps - 1), 0, 0),
                             memory_space=pltpu.SMEM)
    pos1, pos2 = pos1.reshape(n_steps, 1, CMB_TC), pos2.reshape(n_steps, 1, CMB_TC)
    return pl.pallas_call(
        functools.partial(_combine_kernel, final_norm=final_norm),
        grid=(n_steps,),
        in_specs=[pos_spec, pos_spec, next_spec, next_spec,
                  pl.BlockSpec((CMB_TC, D_MODEL), lambda i: (i, 0)),
                  pl.BlockSpec((CMB_TC, LANES), lambda i: (i, 0)),
                  pl.BlockSpec((1, D_MODEL), lambda i: (0, 0)),
                  pl.BlockSpec(memory_space=pl.ANY)],
        out_specs=pl.BlockSpec((CMB_TC, D_MODEL), lambda i: (i, 0)),
        out_shape=jax.ShapeDtypeStruct((m, D_MODEL), F32),
        scratch_shapes=[pltpu.VMEM((2, 2, CMB_TC, HALF), jnp.uint32), pltpu.SemaphoreType.DMA((2,))],
        compiler_params=_cparams(("arbitrary",)),
        name="moe_combine",
    )(pos1, pos2, pos1, pos2, x2d, route, g_row, ys)


RL_TN = 512
RL_PIECE = 256
SRC_A_END = 4096
SRC_AB = 16
SRC_COLS = 17936


def _relayout_plan():
    n_rest = (SRC_COLS - SRC_A_END - SRC_AB) // RL_TN
    rest = lambda s: SRC_A_END + SRC_AB + RL_TN * s
    starts = ([rest(s) for s in range(15, n_rest)]
              + [RL_TN * a for a in range(SRC_A_END // RL_TN)]
              + [rest(s) for s in range(15)]
              + [SRC_A_END])
    assert len(starts) * RL_TN == PROJ_COLS and all(s % 16 == 0 and s + RL_TN <= SRC_COLS for s in starts)
    return np.asarray(starts, np.int32)


def _relayout_kernel(start_ref, wt_ref, o_ref, buf_ref, sem, *, layer):
    t = pl.program_id(0)
    n_tiles = pl.num_programs(0)
    slot = t % 2

    def fetch(tile, s):
        first = pl.multiple_of(start_ref[tile], 16)
        pltpu.make_async_copy(wt_ref.at[layer, pl.ds(first, RL_TN), :], buf_ref.at[s], sem.at[s]).start()

    @pl.when(t == 0)
    def _():
        fetch(0, 0)

    @pl.when(t + 1 < n_tiles)
    def _():
        fetch(t + 1, 1 - slot)

    pltpu.make_async_copy(wt_ref.at[layer, pl.ds(0, RL_TN), :], buf_ref.at[slot], sem.at[slot]).wait()

    row = lax.broadcasted_iota(jnp.int32, (RL_TN, RL_PIECE), 0)
    keep = jnp.logical_or(t + 1 < n_tiles, row < SRC_AB)
    for c in range(D_MODEL // RL_PIECE):
        cols = slice(c * RL_PIECE, (c + 1) * RL_PIECE)
        piece = jnp.where(keep, buf_ref[slot, :, cols], 0.0)
        o_ref[cols, :] = piece.T.astype(BF16)


def _prep_w_in(w_all, layer):
    wt = jnp.swapaxes(w_all, 1, 2)
    return pl.pallas_call(
        functools.partial(_relayout_kernel, layer=layer),
        grid_spec=pltpu.PrefetchScalarGridSpec(
            num_scalar_prefetch=1,
            grid=(PROJ_COLS // RL_TN,),
            in_specs=[pl.BlockSpec(memory_space=pl.ANY)],
            out_specs=pl.BlockSpec((D_MODEL, RL_TN), lambda t, starts: (0, t)),
            scratch_shapes=[pltpu.VMEM((2, RL_TN, D_MODEL), F32), pltpu.SemaphoreType.DMA((2,))],
        ),
        out_shape=jax.ShapeDtypeStruct((D_MODEL, PROJ_COLS), BF16),
        compiler_params=_cparams(("arbitrary",)),
        name="w_in_relayout",
    )(jnp.asarray(_relayout_plan()), wt)


def _lane_row(v, offset=0):
    return jnp.zeros((1, LANES), F32).at[0, offset:offset + v.shape[0]].set(v.astype(F32))


def _layer(x2d, batch, seq, p, final_g_row):
    proj2 = _in_proj(x2d, p["norm_mix_g"], p["w_in"])
    proj3 = proj2.reshape(batch, seq, PROJ_COLS)
    ydn = _deltanet(proj3, p["dn_conv_w"], p["alog_row"], p["dtb_row"], p["dn_norm_g"])
    swa_outs = [_swa_group(proj3, gi) for gi in range(len(SWA_GROUPS))]
    x2d = _merge(x2d, ydn.reshape(batch * seq, DN_WIDTH), proj2, swa_outs, p["sc_conv_w"],
                 p["w_branch_dn"], p["w_branch_sc"], p["w_branch_swa"], p["w_out"], seq)
    hp, route, cnt = _router(x2d, p["norm_ffn_g"], p["w_route"], p["b_route"])
    pos1, pos2, tile_expert, n_active, next_expert, parity = _moe_plan(route, cnt)
    xs = _dispatch(hp, pos1, pos2)
    ys = _experts(xs, tile_expert, n_active, next_expert, parity,
                  p["expert_w_gate"], p["expert_w_up"], p["expert_w_down"], p["layer"])
    is_last = final_g_row is not None
    g_row = final_g_row if is_last else p["norm_ffn_g"]
    return _combine(x2d, route, ys, pos1, pos2, g_row, is_last)


def kernel(x, norm_mix_g, w_in, dn_conv_w, dn_a_log, dn_dt_bias, dn_norm_g, sc_conv_w, w_branch_dn, w_branch_sc, w_branch_swa, w_out, norm_ffn_g, router_group_w, router_group_b, router_expert_w, router_expert_b, expert_w_gate, expert_w_up, expert_w_down, final_norm_g):
    batch, seq, _ = x.shape
    depth = w_in.shape[0]
    x2d = x.reshape(batch * seq, D_MODEL)
    for l in range(depth):
        w_route = jnp.concatenate([router_group_w[l], router_expert_w[l]], axis=1)
        w_route = jnp.pad(w_route, ((0, 0), (0, LANES - w_route.shape[1])))
        p = dict(
            norm_mix_g=norm_mix_g[l].reshape(1, D_MODEL),
            w_in=_prep_w_in(w_in, l),
            dn_conv_w=dn_conv_w[l],
            alog_row=_lane_row(dn_a_log[l]),
            dtb_row=_lane_row(dn_dt_bias[l]),
            dn_norm_g=dn_norm_g[l].reshape(1, HEAD_DIM),
            sc_conv_w=sc_conv_w[l],
            w_branch_dn=w_branch_dn[l].astype(BF16),
            w_branch_sc=w_branch_sc[l].astype(BF16),
            w_branch_swa=w_branch_swa[l].astype(BF16),
            w_out=w_out[l].astype(BF16),
            norm_ffn_g=norm_ffn_g[l].reshape(1, D_MODEL),
            w_route=w_route,
            b_route=_lane_row(jnp.concatenate([router_group_b[l], router_expert_b[l]])),
            expert_w_gate=expert_w_gate,
            expert_w_up=expert_w_up,
            expert_w_down=expert_w_down,
            layer=l,
        )
        final_g_row = final_norm_g.reshape(1, D_MODEL) if l == depth - 1 else None
        x2d = _layer(x2d, batch, seq, p, final_g_row)
    return x2d.reshape(batch, seq, D_MODEL)
```

```python
import functools

import jax
import jax.numpy as jnp
import numpy as np
from jax import lax
from jax.experimental import pallas as pl
from jax.experimental.pallas import tpu as pltpu

F32 = jnp.float32
BF16 = jnp.bfloat16

D_MODEL = 2048
RMS_EPS = 1e-6
L2_EPS = 1e-6

DN_HEADS = 8
HEAD_DIM = 128
DN_WIDTH = DN_HEADS * HEAD_DIM
DN_CONV = 4
SC_WIDTH = 1024
SC_CONV = 3
SWA_GROUPS = ((128, 1), (512, 4), (2048, 16))
SWA_HEADS_PER_GROUP = 4
SWA_HEADS = 12
SWA_GROUP_WIDTH = SWA_HEADS_PER_GROUP * HEAD_DIM
SWA_BLOCK = 128
ALIBI_MAX_BIAS = 8.0
MOE_GROUPS = 4
MOE_EPG = 8
MOE_EXPERTS = 32
MOE_FF = 512

OFF_GATE = 0
OFF_DNQ = 6144
OFF_DNK = OFF_DNQ + DN_WIDTH
OFF_DNV = OFF_DNK + DN_WIDTH
OFF_DNZ = OFF_DNV + DN_WIDTH
OFF_SCC = OFF_DNZ + DN_WIDTH
OFF_SCB = OFF_SCC + SC_WIDTH
OFF_SCX = OFF_SCB + SC_WIDTH
OFF_SWA = OFF_SCX + SC_WIDTH
OFF_AB = OFF_SWA + 3 * SWA_HEADS * HEAD_DIM
PROJ_COLS = 18432

LANES = 128
SUBLANES = 8
SWA_OUT_SLOTS = SWA_HEADS_PER_GROUP + 1
CHUNK = 128
HALO = 16

VMEM_LIMIT = 56 * 1024 * 1024


def _cparams(sem):
    return pltpu.CompilerParams(dimension_semantics=sem, vmem_limit_bytes=VMEM_LIMIT)


def _sigmoid(x):
    return 1.0 / (1.0 + jnp.exp(-x))


def _silu(x):
    return x * _sigmoid(x)


def _softplus(x):
    return jnp.maximum(x, 0.0) + jnp.log(1.0 + jnp.exp(-jnp.abs(x)))


IN_TM = 1024
IN_TN = 1024
NORM_ROWS = 64


def _rmsnorm_rows(x, g):
    ms = jnp.mean(x * x, axis=-1, keepdims=True)
    return x * lax.rsqrt(ms + RMS_EPS) * g


def _in_proj_kernel(x_ref, g_ref, w_ref, o_ref, h_ref):
    @pl.when(pl.program_id(1) == 0)
    def _():
        def body(r, c):
            rows = pl.ds(pl.multiple_of(r * NORM_ROWS, NORM_ROWS), NORM_ROWS)
            h_ref[rows, :] = _rmsnorm_rows(x_ref[rows, :], g_ref[...]).astype(BF16)
            return c
        lax.fori_loop(0, IN_TM // NORM_ROWS, body, 0)

    o_ref[...] = jnp.dot(h_ref[...], w_ref[...], preferred_element_type=F32).astype(o_ref.dtype)


def _in_proj(x2d, g_row, w_bf16):
    m = x2d.shape[0]
    return pl.pallas_call(
        _in_proj_kernel,
        grid=(m // IN_TM, PROJ_COLS // IN_TN),
        in_specs=[
            pl.BlockSpec((IN_TM, D_MODEL), lambda i, j: (i, 0)),
            pl.BlockSpec((1, D_MODEL), lambda i, j: (0, 0)),
            pl.BlockSpec((D_MODEL, IN_TN), lambda i, j: (0, j)),
        ],
        out_specs=pl.BlockSpec((IN_TM, IN_TN), lambda i, j: (i, j)),
        out_shape=jax.ShapeDtypeStruct((m, PROJ_COLS), BF16),
        scratch_shapes=[pltpu.VMEM((IN_TM, D_MODEL), BF16)],
        compiler_params=_cparams(("parallel", "arbitrary")),
        name="in_proj",
    )(x2d, g_row, w_bf16)


DN_HB = 8
DN_TS = 512
DN_W = DN_HB * HEAD_DIM
DN_DOUBLINGS = CHUNK.bit_length() - 2


def _dot_nt(a, b):
    return lax.dot_general(a, b, (((1,), (1,)), ((), ())), preferred_element_type=F32)


def _dot_tn(a, b):
    return lax.dot_general(a, b, (((0,), (0,)), ((), ())), preferred_element_type=F32)


def _causal_conv(xx, w, width):
    acc = None
    for j in range(width):
        shift = width - 1 - j
        xs = xx if shift == 0 else pltpu.roll(xx, shift, axis=0)
        term = xs[HALO:, :] * w[j:j + 1, :]
        acc = term if acc is None else acc + term
    return acc


def _deltanet_kernel(q_ref, k_ref, v_ref, z_ref, ab_ref, cwq_ref, cwk_ref, cwv_ref,
                     alog_ref, dtb_ref, ng_ref, o_ref,
                     xpad_ref, state_ref):
    hg = pl.program_id(1)
    st = pl.program_id(2)

    @pl.when(st == 0)
    def _():
        xpad_ref[:, 0:HALO, :] = jnp.zeros((3, HALO, DN_W), BF16)
        state_ref[...] = jnp.zeros_like(state_ref)

    xpad_ref[0, HALO:, :] = q_ref[0]
    xpad_ref[1, HALO:, :] = k_ref[0]
    xpad_ref[2, HALO:, :] = v_ref[0]

    row_i = lax.broadcasted_iota(jnp.int32, (CHUNK, CHUNK), 0)
    col_j = lax.broadcasted_iota(jnp.int32, (CHUNK, CHUNK), 1)
    causal = row_i >= col_j
    strict = row_i > col_j
    tri = causal.astype(F32)
    eye = (row_i == col_j).astype(F32)
    lane = lax.broadcasted_iota(jnp.int32, (CHUNK, LANES), 1)
    neg_decay_rate = -jnp.exp(alog_ref[...])
    cws = (cwq_ref[...], cwk_ref[...], cwv_ref[...])
    ng = ng_ref[...]

    def chunk_body(c, carry):
        r0 = pl.multiple_of(c * CHUNK, CHUNK)
        win = pl.ds(r0, CHUNK + HALO)
        rows = pl.ds(r0, CHUNK)
        conv = [_silu(_causal_conv(xpad_ref[t, win, :].astype(F32), cws[t], DN_CONV)) for t in range(3)]
        ab = ab_ref[0, rows, :].astype(F32)
        g_raw = neg_decay_rate * _softplus(ab + dtb_ref[...])
        g_cum = jnp.dot(tri, g_raw, preferred_element_type=F32, precision=lax.Precision.HIGHEST)
        beta_all = _sigmoid(ab)
        z = z_ref[0, rows, :].astype(F32)

        heads = range(DN_HB)
        sls = [slice(i * HEAD_DIM, (i + 1) * HEAD_DIM) for i in heads]
        dot = functools.partial(jnp.dot, preferred_element_type=F32)

        g_col = [jnp.sum(jnp.where(lane == hg * DN_HB + i, g_cum, 0.0), axis=-1, keepdims=True) for i in heads]
        beta = [jnp.sum(jnp.where(lane == hg * DN_HB + i + DN_HEADS, beta_all, 0.0), axis=-1, keepdims=True)
                for i in heads]
        g_last = [g[CHUNK - 1:CHUNK, :] for g in g_col]
        eg = [jnp.exp(g) for g in g_col]
        ek = [jnp.exp(gl - g) for gl, g in zip(g_last, g_col)]
        g_b = [jnp.broadcast_to(g, (CHUNK, CHUNK)) for g in g_col]
        decay = [jnp.exp(jnp.where(causal, gb - gb.T, -jnp.inf)) for gb in g_b]

        qf = [conv[0][:, sl] for sl in sls]
        kf = [conv[1][:, sl] for sl in sls]
        vf = [conv[2][:, sl] for sl in sls]
        q = [x * lax.rsqrt(jnp.sum(x * x, axis=-1, keepdims=True) + L2_EPS) * (HEAD_DIM ** -0.5) for x in qf]
        k = [x * lax.rsqrt(jnp.sum(x * x, axis=-1, keepdims=True) + L2_EPS) for x in kf]
        kb = [ki * bi for ki, bi in zip(k, beta)]
        a2 = [_dot_nt(jnp.concatenate([qi, kbi], axis=0).astype(BF16), ki.astype(BF16))
              for qi, kbi, ki in zip(q, kb, k)]
        attn = [a[:CHUNK] * d for a, d in zip(a2, decay)]
        n_mat = [jnp.where(strict, -(a[CHUNK:] * d), 0.0) for a, d in zip(a2, decay)]

        u_mat = [eye + n for n in n_mat]
        p_mat = [dot(n.astype(BF16), n.astype(BF16)) for n in n_mat]
        for _ in range(DN_DOUBLINGS - 1):
            up = [dot(jnp.concatenate([u.astype(BF16), p.astype(BF16)], axis=0), p.astype(BF16))
                  for u, p in zip(u_mat, p_mat)]
            u_mat = [u + x[:CHUNK] for u, x in zip(u_mat, up)]
            p_mat = [x[CHUNK:] for x in up]
        u_mat = [u + dot(u.astype(BF16), p.astype(BF16)) for u, p in zip(u_mat, p_mat)]
        uw = [dot(u.astype(BF16), jnp.concatenate([vi * bi, kbi * egi], axis=1).astype(BF16))
              for u, vi, bi, kbi, egi in zip(u_mat, vf, beta, kb, eg)]

        s_old = [state_ref[i] for i in heads]
        qw = [dot(jnp.concatenate([qi * egi, x[:, HEAD_DIM:]], axis=0).astype(BF16), s.astype(BF16))
              for qi, egi, x, s in zip(q, eg, uw, s_old)]
        v16 = [(x[:, :HEAD_DIM] - y[CHUNK:]).astype(BF16) for x, y in zip(uw, qw)]
        o = [y[:CHUNK] + dot(a.astype(BF16), v) for y, a, v in zip(qw, attn, v16)]
        s_new = [s * jnp.exp(gl) + _dot_tn((ki * eki).astype(BF16), v)
                 for s, gl, ki, eki, v in zip(s_old, g_last, k, ek, v16)]
        for i in heads:
            state_ref[i] = s_new[i]
            on = o[i] * lax.rsqrt(jnp.mean(o[i] * o[i], axis=-1, keepdims=True) + RMS_EPS) * ng
            o_ref[0, rows, sls[i]] = (on * _silu(z[:, sls[i]])).astype(o_ref.dtype)
        return carry

    lax.fori_loop(0, DN_TS // CHUNK, chunk_body, 0)

    tail = pl.ds(DN_TS, HALO)
    head = pl.ds(0, HALO)
    for t in range(3):
        xpad_ref[t, head, :] = xpad_ref[t, tail, :]


def _deltanet(proj3, conv_w, alog_row, dtb_row, ng_row):
    b, s, _ = proj3.shape
    qb, kb_, vb, zb = (off // DN_W for off in (OFF_DNQ, OFF_DNK, OFF_DNV, OFF_DNZ))
    act = lambda base: pl.BlockSpec((1, DN_TS, DN_W), lambda bi, hg, st: (bi, st, base + hg))
    cw = lambda base: pl.BlockSpec((DN_CONV, DN_W), lambda bi, hg, st: (0, base + hg))
    row = pl.BlockSpec((1, LANES), lambda bi, hg, st: (0, 0))
    return pl.pallas_call(
        _deltanet_kernel,
        grid=(b, DN_HEADS // DN_HB, s // DN_TS),
        in_specs=[
            act(qb), act(kb_), act(vb), act(zb),
            pl.BlockSpec((1, DN_TS, LANES), lambda bi, hg, st: (bi, st, OFF_AB // LANES)),
            cw(0), cw(DN_WIDTH // DN_W), cw(2 * DN_WIDTH // DN_W),
            row, row, row,
        ],
        out_specs=pl.BlockSpec((1, DN_TS, DN_W), lambda bi, hg, st: (bi, st, hg)),
        out_shape=jax.ShapeDtypeStruct((b, s, DN_WIDTH), BF16),
        scratch_shapes=[
            pltpu.VMEM((3, DN_TS + HALO, DN_W), BF16),
            pltpu.VMEM((DN_HB, HEAD_DIM, HEAD_DIM), F32),
        ],
        compiler_params=_cparams(("parallel", "parallel", "arbitrary")),
        name="deltanet",
    )(proj3, proj3, proj3, proj3, proj3, conv_w, conv_w, conv_w, alog_row, dtb_row, ng_row)


SWA_CAST_ROWS = 256


def _swa_kernel(q_ref, k_ref, v_ref, o_ref, *scratch, seq, dil, slopes):
    n_blocks = seq // dil // SWA_BLOCK
    qi = lax.broadcasted_iota(jnp.int32, (SWA_BLOCK, SWA_BLOCK), 0)
    kj = lax.broadcasted_iota(jnp.int32, (SWA_BLOCK, SWA_BLOCK), 1)
    step_cur = (qi - kj).astype(F32)
    step_prev = step_cur + float(SWA_BLOCK)
    ok_cur = qi >= kj
    ok_prev = kj >= qi
    lane = lax.broadcasted_iota(jnp.int32, (SWA_BLOCK, LANES), 1)
    scale = HEAD_DIM ** -0.5
    heads = range(SWA_HEADS_PER_GROUP)
    sls = [slice(hh * HEAD_DIM, (hh + 1) * HEAD_DIM) for hh in heads]
    m_h = [float(slopes[hh]) * float(dil) for hh in heads]

    if dil > 1:
        (f32_ref,) = scratch

        def cast_body(c, carry):
            rows = pl.ds(pl.multiple_of(c * SWA_CAST_ROWS, SWA_CAST_ROWS), SWA_CAST_ROWS)
            for t, ref in enumerate((q_ref, k_ref, v_ref)):
                for hh in heads:
                    f32_ref[t * SWA_HEADS_PER_GROUP + hh, rows, :] = ref[0, rows, sls[hh]].astype(F32)
            return carry
        lax.fori_loop(0, seq // SWA_CAST_ROWS, cast_body, 0)

        def load(t, rows):
            return [f32_ref[t * SWA_HEADS_PER_GROUP + hh, rows, :].astype(BF16) for hh in heads]
    else:
        refs = (q_ref, k_ref, v_ref)

        def load(t, rows):
            return [refs[t][0, rows, sls[hh]] for hh in heads]

    def block_body(it, carry):
        r = it // n_blocks
        n = it % n_blocks
        base = n * (SWA_BLOCK * dil) + r
        prev_base = jnp.maximum(n - 1, 0) * (SWA_BLOCK * dil) + r
        if dil > 1:
            rows = pl.ds(base, SWA_BLOCK, stride=dil)
            prev_rows = pl.ds(prev_base, SWA_BLOCK, stride=dil)
        else:
            rows = pl.ds(pl.multiple_of(base, SWA_BLOCK), SWA_BLOCK)
            prev_rows = pl.ds(pl.multiple_of(prev_base, SWA_BLOCK), SWA_BLOCK)
        qn, kc, vc = load(0, rows), load(1, rows), load(2, rows)
        kp, vp = load(1, prev_rows), load(2, prev_rows)
        ok_p = jnp.logical_and(ok_prev, n > 0)

        s_cur = [jnp.where(ok_cur, _dot_nt(qh, kh) * scale - m * step_cur, -jnp.inf)
                 for qh, kh, m in zip(qn, kc, m_h)]
        s_prev = [jnp.where(ok_p, _dot_nt(qh, kh) * scale - m * step_prev, -jnp.inf)
                  for qh, kh, m in zip(qn, kp, m_h)]
        mx = [jnp.max(jnp.maximum(a, b), axis=-1, keepdims=True) for a, b in zip(s_cur, s_prev)]
        e_cur = [jnp.exp(a - m) for a, m in zip(s_cur, mx)]
        e_prev = [jnp.exp(b - m) for b, m in zip(s_prev, mx)]
        den = [jnp.sum(a + b, axis=-1, keepdims=True) for a, b in zip(e_cur, e_prev)]
        pv = [jnp.dot(a.astype(BF16), vch, preferred_element_type=F32)
              + jnp.dot(b.astype(BF16), vph, preferred_element_type=F32)
              for a, b, vch, vph in zip(e_cur, e_prev, vc, vp)]
        lse_tile = jnp.zeros((SWA_BLOCK, LANES), F32)
        for hh in heads:
            o_ref[0, hh, rows, :] = pv[hh] * (1.0 / den[hh])
            lse_tile = jnp.where(lane == hh, mx[hh] + jnp.log(den[hh]), lse_tile)
        o_ref[0, SWA_HEADS_PER_GROUP, rows, :] = lse_tile
        return carry

    lax.fori_loop(0, seq // SWA_BLOCK, block_body, 0)


def _swa_group(proj3, gi):
    batch, seq, _ = proj3.shape
    window, dil = SWA_GROUPS[gi]
    assert window // dil == SWA_BLOCK and seq % (dil * SWA_BLOCK) == 0
    slopes = 2.0 ** (-ALIBI_MAX_BIAS * np.arange(1, SWA_HEADS + 1) / SWA_HEADS)
    slopes = slopes.reshape(len(SWA_GROUPS), SWA_HEADS_PER_GROUP)[gi]

    def spec(which):
        cb = (OFF_SWA + which * SWA_HEADS * HEAD_DIM) // SWA_GROUP_WIDTH + gi
        return pl.BlockSpec((1, seq, SWA_GROUP_WIDTH), lambda bi: (bi, 0, cb))

    scratch = [pltpu.VMEM((3 * SWA_HEADS_PER_GROUP, seq, HEAD_DIM), F32)] if dil > 1 else []
    return pl.pallas_call(
        functools.partial(_swa_kernel, seq=seq, dil=dil, slopes=tuple(float(s) for s in slopes)),
        grid=(batch,),
        in_specs=[spec(0), spec(1), spec(2)],
        out_specs=pl.BlockSpec((1, SWA_OUT_SLOTS, seq, HEAD_DIM), lambda bi: (bi, 0, 0, 0)),
        out_shape=jax.ShapeDtypeStruct((batch, SWA_OUT_SLOTS, seq, HEAD_DIM), F32),
        scratch_shapes=scratch,
        compiler_params=_cparams(("parallel",)),
        name=f"swa_g{gi}",
    )(proj3, proj3, proj3)


MG_TM = 256


def _merge_kernel(x_ref, ydn_ref, gate_ref, scc_ref, scb_ref, scx_ref, scc_h_ref, scx_h_ref,
                  s0_ref, s1_ref, s2_ref, scw_ref, wdn_ref, wsc_ref, wswa_ref, wout_ref, o_ref,
                  *, tiles_per_seq):
    i = pl.program_id(0)
    first = (i % tiles_per_seq) == 0

    cx_cur = scc_ref[...].astype(F32) * scx_ref[...].astype(F32)
    cx_halo = scc_h_ref[...].astype(F32) * scx_h_ref[...].astype(F32)
    cx_halo = jnp.where(first, 0.0, cx_halo)
    cx = jnp.concatenate([cx_halo, cx_cur], axis=0)
    y_sc = scb_ref[...].astype(F32) * _causal_conv(cx, scw_ref[...], SC_CONV)

    outs = (s0_ref, s1_ref, s2_ref)
    lses = [o[0, SWA_HEADS_PER_GROUP] for o in outs]
    mx = jnp.maximum(jnp.maximum(lses[0], lses[1]), lses[2])
    es = [jnp.exp(l - mx) for l in lses]
    inv = 1.0 / (es[0] + es[1] + es[2])
    heads = []
    for hh in range(SWA_HEADS_PER_GROUP):
        acc = None
        for gi in range(3):
            wt = (es[gi] * inv)[:, hh:hh + 1]
            term = wt * outs[gi][0, hh]
            acc = term if acc is None else acc + term
        heads.append(acc)
    y_swa = jnp.concatenate(heads, axis=1)

    gates = gate_ref[...].astype(F32)
    merged = (_sigmoid(gates[:, 0:D_MODEL])
              * jnp.dot(ydn_ref[...], wdn_ref[...], preferred_element_type=F32)
              + _sigmoid(gates[:, D_MODEL:2 * D_MODEL])
              * jnp.dot(y_sc.astype(BF16), wsc_ref[...], preferred_element_type=F32)
              + _sigmoid(gates[:, 2 * D_MODEL:3 * D_MODEL])
              * jnp.dot(y_swa.astype(BF16), wswa_ref[...], preferred_element_type=F32))
    o_ref[...] = x_ref[...] + jnp.dot(merged.astype(BF16), wout_ref[...], preferred_element_type=F32)


def _merge(x2d, ydn2, proj2, swa_outs, sc_conv_w, wdn, wsc, wswa, wout, seq):
    m = x2d.shape[0]
    tiles_per_seq = seq // MG_TM
    halo_blocks = MG_TM // HALO
    rows = lambda w, cb: pl.BlockSpec((MG_TM, w), lambda i: (i, cb))
    halo = lambda cb: pl.BlockSpec((HALO, SC_WIDTH), lambda i: (jnp.maximum(i * halo_blocks - 1, 0), cb))
    full = lambda a: pl.BlockSpec(a.shape, lambda i: (0, 0), pipeline_mode=pl.Buffered(1))
    swa = pl.BlockSpec((1, SWA_OUT_SLOTS, MG_TM, HEAD_DIM),
                       lambda i: (i // tiles_per_seq, 0, i % tiles_per_seq, 0))
    return pl.pallas_call(
        functools.partial(_merge_kernel, tiles_per_seq=tiles_per_seq),
        grid=(m // MG_TM,),
        in_specs=[
            rows(D_MODEL, 0),
            rows(DN_WIDTH, 0),
            rows(3 * D_MODEL, OFF_GATE // (3 * D_MODEL)),
            rows(SC_WIDTH, OFF_SCC // SC_WIDTH), rows(SC_WIDTH, OFF_SCB // SC_WIDTH),
            rows(SC_WIDTH, OFF_SCX // SC_WIDTH),
            halo(OFF_SCC // SC_WIDTH), halo(OFF_SCX // SC_WIDTH),
            swa, swa, swa,
            full(sc_conv_w), full(wdn), full(wsc), full(wswa), full(wout),
        ],
        out_specs=rows(D_MODEL, 0),
        out_shape=jax.ShapeDtypeStruct((m, D_MODEL), F32),
        compiler_params=_cparams(("parallel",)),
        name="merge",
    )(x2d, ydn2, proj2, proj2, proj2, proj2, proj2, proj2, *swa_outs, sc_conv_w, wdn, wsc, wswa, wout)


RT_TM = 512
ROUTE_LANE0 = MOE_GROUPS
R_E1, R_E2, R_W1, R_W2, R_RANK1, R_RANK2 = range(6)
HALF = D_MODEL // 2


def _pack_pair(lo, hi):
    lo_b = lax.bitcast_convert_type(lo.astype(BF16).astype(F32), jnp.uint32)
    hi_b = lax.bitcast_convert_type(hi.astype(BF16).astype(F32), jnp.uint32)
    return (hi_b & jnp.uint32(0xFFFF0000)) | (lo_b >> jnp.uint32(16))


def _unpack_pair(p):
    lo = lax.bitcast_convert_type(p << jnp.uint32(16), F32)
    hi = lax.bitcast_convert_type(p & jnp.uint32(0xFFFF0000), F32)
    return lo, hi


def _router_kernel(x_ref, g_ref, wr_ref, br_ref, hp_ref, route_ref, cnt_ref, carry_ref):
    @pl.when(pl.program_id(0) == 0)
    def _():
        carry_ref[...] = jnp.zeros_like(carry_ref)

    h = _rmsnorm_rows(x_ref[...], g_ref[...])
    hp_ref[...] = _pack_pair(h[:, :HALF], h[:, HALF:])
    w = wr_ref[...]
    h_hi, w_hi = h.astype(BF16), w.astype(BF16)
    h_lo = (h - h_hi.astype(F32)).astype(BF16)
    w_lo = (w - w_hi.astype(F32)).astype(BF16)
    dot = functools.partial(jnp.dot, preferred_element_type=F32)
    logits = dot(h_hi, w_hi) + dot(h_lo, w_hi) + dot(h_hi, w_lo) + br_ref[...]
    lane = lax.broadcasted_iota(jnp.int32, logits.shape, 1)
    big = jnp.int32(LANES)
    neg = -jnp.inf

    is_grp = lane < MOE_GROUPS
    gl = jnp.where(is_grp, logits, neg)
    gmax = jnp.max(gl, axis=-1, keepdims=True)
    grp = jnp.min(jnp.where(gl == gmax, lane, big), axis=-1, keepdims=True)
    g_w = 1.0 / jnp.sum(jnp.exp(gl - gmax), axis=-1, keepdims=True)

    lo = ROUTE_LANE0 + grp * MOE_EPG
    in_grp = jnp.logical_and(lane >= lo, lane < lo + MOE_EPG)
    el = jnp.where(in_grp, logits, neg)
    m1 = jnp.max(el, axis=-1, keepdims=True)
    i1 = jnp.min(jnp.where(el == m1, lane, big), axis=-1, keepdims=True)
    el2 = jnp.where(lane == i1, neg, el)
    m2 = jnp.max(el2, axis=-1, keepdims=True)
    i2 = jnp.min(jnp.where(el2 == m2, lane, big), axis=-1, keepdims=True)
    e2 = jnp.exp(m2 - m1)
    w1 = g_w / (1.0 + e2)
    w2 = g_w * e2 / (1.0 + e2)

    tm = logits.shape[0]
    onehot = jnp.where(jnp.logical_or(lane == i1, lane == i2), 1.0, 0.0)
    earlier = (lax.broadcasted_iota(jnp.int32, (tm, tm), 0) > lax.broadcasted_iota(jnp.int32, (tm, tm), 1))
    before = carry_ref[...] + jnp.dot(jnp.where(earlier, 1.0, 0.0).astype(BF16), onehot.astype(BF16),
                                      preferred_element_type=F32)
    rank1 = jnp.sum(jnp.where(lane == i1, before, 0.0), axis=-1, keepdims=True)
    rank2 = jnp.sum(jnp.where(lane == i2, before, 0.0), axis=-1, keepdims=True)
    carry_ref[...] += jnp.sum(onehot, axis=0, keepdims=True)
    cnt_ref[...] = carry_ref[...]

    rec = jnp.zeros_like(logits)
    for ln, val in ((R_E1, (i1 - ROUTE_LANE0).astype(F32)), (R_E2, (i2 - ROUTE_LANE0).astype(F32)),
                    (R_W1, w1), (R_W2, w2), (R_RANK1, rank1), (R_RANK2, rank2)):
        rec = jnp.where(lane == ln, val, rec)
    route_ref[...] = rec


def _router(x2d, g_row, w_route, b_route):
    m = x2d.shape[0]
    return pl.pallas_call(
        _router_kernel,
        grid=(m // RT_TM,),
        in_specs=[
            pl.BlockSpec((RT_TM, D_MODEL), lambda i: (i, 0)),
            pl.BlockSpec((1, D_MODEL), lambda i: (0, 0)),
            pl.BlockSpec((D_MODEL, LANES), lambda i: (0, 0)),
            pl.BlockSpec((1, LANES), lambda i: (0, 0)),
        ],
        out_specs=[pl.BlockSpec((RT_TM, HALF), lambda i: (i, 0)),
                   pl.BlockSpec((RT_TM, LANES), lambda i: (i, 0)),
                   pl.BlockSpec((1, LANES), lambda i: (0, 0))],
        out_shape=[jax.ShapeDtypeStruct((m, HALF), jnp.uint32), jax.ShapeDtypeStruct((m, LANES), F32),
                   jax.ShapeDtypeStruct((1, LANES), F32)],
        scratch_shapes=[pltpu.VMEM((1, LANES), F32)],
        compiler_params=_cparams(("arbitrary",)),
        name="router",
    )(x2d, g_row, w_route, b_route)


EXP_TM = 512


def _n_row_tiles(n_tokens):
    return 2 * n_tokens // EXP_TM + MOE_EXPERTS


def _positions_kernel(route_ref, offs_ref, pos_ref):
    rec = route_ref[...]
    lane = lax.broadcasted_iota(jnp.int32, rec.shape, 1)
    offs = offs_ref[...]

    def first_row(e):
        return jnp.sum(jnp.where(lane == e.astype(jnp.int32), offs, 0.0), axis=-1, keepdims=True)

    pos1 = first_row(rec[:, R_E1:R_E1 + 1]) + rec[:, R_RANK1:R_RANK1 + 1]
    pos2 = first_row(rec[:, R_E2:R_E2 + 1]) + rec[:, R_RANK2:R_RANK2 + 1]
    pos_ref[...] = jnp.where(lane == 0, pos1, jnp.where(lane == 1, pos2, 0.0)).astype(jnp.int32)


def _moe_plan(route, cnt):
    m = route.shape[0]
    n_tiles = _n_row_tiles(m)
    counts = cnt[0, ROUTE_LANE0:ROUTE_LANE0 + MOE_EXPERTS].astype(jnp.int32)
    padded = (counts + EXP_TM - 1) // EXP_TM * EXP_TM
    ends = jnp.cumsum(padded)
    offs = ends - padded
    pos = pl.pallas_call(
        _positions_kernel,
        grid=(m // RT_TM,),
        in_specs=[pl.BlockSpec((RT_TM, LANES), lambda i: (i, 0)), pl.BlockSpec((1, LANES), lambda i: (0, 0))],
        out_specs=pl.BlockSpec((RT_TM, LANES), lambda i: (i, 0)),
        out_shape=jax.ShapeDtypeStruct((m, LANES), jnp.int32),
        compiler_params=_cparams(("parallel",)),
        name="moe_positions",
    )(route, _lane_row(offs))
    n_active = ends[-1] // EXP_TM
    tile_row = jnp.maximum(jnp.minimum(jnp.arange(n_tiles), n_active - 1), 0) * EXP_TM
    tile_expert = jnp.minimum(jnp.sum(ends[None, :] <= tile_row[:, None], axis=1), MOE_EXPERTS - 1)
    used = counts > 0
    ids = jnp.where(used, jnp.arange(MOE_EXPERTS), MOE_EXPERTS)
    next_used = jnp.concatenate([lax.cummin(ids, reverse=True)[1:], jnp.full((1,), MOE_EXPERTS, ids.dtype)])
    parity = (jnp.cumsum(used.astype(jnp.int32)) - 1) % 2
    i32 = lambda a: a.astype(jnp.int32)
    return dict(pos1=pos[:, 0], pos2=pos[:, 1], counts=counts, offs=i32(offs),
                tile_expert=i32(tile_expert), n_active=i32(n_active).reshape(1),
                next_expert=i32(next_used[tile_expert]), parity=i32(parity[tile_expert]))


DSP_TB = 512
ROW_UNROLL = 8


def _dispatch_kernel(cnt_ref, offs_ref, nact_ref, pos1_ref, pos2_ref, hp_ref, xs_ref, zero_ref, sem, zero_sem):
    n_tiles = xs_ref.shape[0] // EXP_TM

    @pl.when(pl.program_id(0) == 0)
    def _():
        zero_ref[...] = jnp.zeros_like(zero_ref)

        def pad_pieces(e):
            pad = (-cnt_ref[e]) & (EXP_TM - 1)
            first = offs_ref[e] + cnt_ref[e]
            head = jnp.minimum((-first) & (SUBLANES - 1), pad)
            for k in range(SUBLANES - 1):
                yield k < head, pltpu.make_async_copy(zero_ref.at[pl.ds(0, 1), :],
                                                      xs_ref.at[pl.ds(first + k, 1), :], zero_sem)
            body = pad - head
            b = EXP_TM // 2
            while b >= SUBLANES:
                row = pl.multiple_of(first + head + (body & ~(2 * b - 1)), SUBLANES)
                yield (body & b) != 0, pltpu.make_async_copy(zero_ref.at[pl.ds(0, b), :],
                                                             xs_ref.at[pl.ds(row, b), :], zero_sem)
                b //= 2

        def tile_copy(t):
            rows = pl.ds(pl.multiple_of(t * EXP_TM, EXP_TM), EXP_TM)
            return pltpu.make_async_copy(zero_ref, xs_ref.at[rows, :], zero_sem)

        def for_all(method):
            def per_expert(e, c):
                for cond, cp in pad_pieces(e):
                    pl.when(cond)(getattr(cp, method))
                return c
            lax.fori_loop(0, MOE_EXPERTS, per_expert, 0)

            def per_tile(t, c):
                getattr(tile_copy(t), method)()
                return c
            lax.fori_loop(nact_ref[0], n_tiles, per_tile, 0)

        for_all("start")
        for_all("wait")

    def start(j8, c):
        for u in range(ROW_UNROLL):
            j = j8 * ROW_UNROLL + u
            src = hp_ref.at[pl.ds(j, 1), :]
            pltpu.make_async_copy(src, xs_ref.at[pl.ds(pos1_ref[0, 0, j], 1), :], sem).start(priority=0)
            pltpu.make_async_copy(src, xs_ref.at[pl.ds(pos2_ref[0, 0, j], 1), :], sem).start(priority=1)
        return c

    lax.fori_loop(0, DSP_TB // ROW_UNROLL, start, 0)
    for _ in range(2):
        pltpu.make_async_copy(hp_ref, xs_ref.at[pl.ds(0, DSP_TB), :], sem).wait()


def _dispatch(hp, pos1, pos2, counts, offs, n_active):
    m = hp.shape[0]
    n_rows = _n_row_tiles(m) * EXP_TM
    pos_spec = pl.BlockSpec((1, 1, DSP_TB), lambda i, *_: (i, 0, 0), memory_space=pltpu.SMEM)
    return pl.pallas_call(
        _dispatch_kernel,
        grid_spec=pltpu.PrefetchScalarGridSpec(
            num_scalar_prefetch=3,
            grid=(m // DSP_TB,),
            in_specs=[pos_spec, pos_spec, pl.BlockSpec((DSP_TB, HALF), lambda i, *_: (i, 0))],
            out_specs=pl.BlockSpec(memory_space=pl.ANY),
            scratch_shapes=[pltpu.VMEM((EXP_TM, HALF), jnp.uint32), pltpu.SemaphoreType.DMA(()),
                            pltpu.SemaphoreType.DMA(())],
        ),
        out_shape=jax.ShapeDtypeStruct((n_rows, HALF), jnp.uint32),
        compiler_params=_cparams(("arbitrary",)),
        name="moe_dispatch",
    )(counts, offs, n_active, pos1.reshape(m // DSP_TB, 1, DSP_TB), pos2.reshape(m // DSP_TB, 1, DSP_TB), hp)


W_CAST_ROWS = 256


def _cast_rows(src_ref, dst_ref):
    n_rows = dst_ref.shape[0]
    slab = min(W_CAST_ROWS, n_rows)

    def body(r, c):
        rows = pl.ds(pl.multiple_of(r * slab, slab), slab)
        dst_ref[rows, :] = src_ref[rows, :].astype(BF16)
        return c
    lax.fori_loop(0, n_rows // slab, body, 0)


def _expert_kernel(te_ref, nact_ref, next_ref, par_ref, xs_ref, wg_hbm, wu_hbm, wd_hbm, ys_ref,
                   sg_ref, su_ref, sd_ref, wg16_ref, wu16_ref, wd16_ref, sem, *, layer):
    i = pl.program_id(0)
    active = i < nact_ref[0]
    expert = te_ref[i]
    slot = par_ref[i]
    new_expert = jnp.logical_or(i == 0, expert != te_ref[jnp.maximum(i - 1, 0)])

    def copies(e, s):
        return [pltpu.make_async_copy(hbm.at[layer, e], stage.at[s], sem.at[s])
                for hbm, stage in ((wg_hbm, sg_ref), (wu_hbm, su_ref), (wd_hbm, sd_ref))]

    @pl.when(jnp.logical_and(active, i == 0))
    def _():
        for cp in copies(expert, slot):
            cp.start()

    @pl.when(jnp.logical_and(active, new_expert))
    def _():
        for cp in copies(expert, slot):
            cp.wait()

        @pl.when(next_ref[i] < MOE_EXPERTS)
        def _():
            for cp in copies(next_ref[i], 1 - slot):
                cp.start()

        _cast_rows(sg_ref.at[slot], wg16_ref)
        _cast_rows(su_ref.at[slot], wu16_ref)
        _cast_rows(sd_ref.at[slot], wd16_ref)

    @pl.when(active)
    def _():
        lo, hi = _unpack_pair(xs_ref[...])
        lo, hi = lo.astype(BF16), hi.astype(BF16)
        dot = functools.partial(jnp.dot, preferred_element_type=F32)
        gate = dot(lo, wg16_ref[:HALF, :]) + dot(hi, wg16_ref[HALF:, :])
        up = dot(lo, wu16_ref[:HALF, :]) + dot(hi, wu16_ref[HALF:, :])
        y = dot((_silu(gate) * up).astype(BF16), wd16_ref[...])
        ys_ref[...] = _pack_pair(y[:, :HALF], y[:, HALF:])

    @pl.when(jnp.logical_not(active))
    def _():
        ys_ref[...] = jnp.zeros_like(ys_ref)


def _experts(xs, tile_expert, n_active, next_expert, parity, wg, wu, wd, layer):
    n_tiles = xs.shape[0] // EXP_TM
    rows = pl.BlockSpec((EXP_TM, HALF),
                        lambda i, te, na, nx, par: (jnp.maximum(jnp.minimum(i, na[0] - 1), 0), 0))
    out_rows = pl.BlockSpec((EXP_TM, HALF), lambda i, te, na, nx, par: (i, 0))
    hbm = pl.BlockSpec(memory_space=pl.ANY)
    return pl.pallas_call(
        functools.partial(_expert_kernel, layer=layer),
        grid_spec=pltpu.PrefetchScalarGridSpec(
            num_scalar_prefetch=4,
            grid=(n_tiles,),
            in_specs=[rows, hbm, hbm, hbm],
            out_specs=out_rows,
            scratch_shapes=[pltpu.VMEM((2, D_MODEL, MOE_FF), F32), pltpu.VMEM((2, D_MODEL, MOE_FF), F32),
                            pltpu.VMEM((2, MOE_FF, D_MODEL), F32),
                            pltpu.VMEM((D_MODEL, MOE_FF), BF16), pltpu.VMEM((D_MODEL, MOE_FF), BF16),
                            pltpu.VMEM((MOE_FF, D_MODEL), BF16),
                            pltpu.SemaphoreType.DMA((2,))],
        ),
        out_shape=jax.ShapeDtypeStruct(xs.shape, jnp.uint32),
        compiler_params=_cparams(("arbitrary",)),
        name="moe_experts",
    )(tile_expert, n_active, next_expert, parity, xs, wg, wu, wd)


CMB_TC = 256


def _combine_kernel(pos1_ref, pos2_ref, pos1_next_ref, pos2_next_ref, x_ref, route_ref, g_ref, ys_ref,
                    o_ref, buf_ref, sem, *, final_norm):
    i = pl.program_id(0)
    slot = i % 2

    def gather(p1_ref, p2_ref, s):
        def start(j8, c):
            for u in range(ROW_UNROLL):
                j = j8 * ROW_UNROLL + u
                for k, p_ref in enumerate((p1_ref, p2_ref)):
                    pltpu.make_async_copy(ys_ref.at[pl.ds(p_ref[0, 0, j], 1), :],
                                          buf_ref.at[s, k, pl.ds(j, 1), :], sem.at[s]).start(priority=k)
            return c
        lax.fori_loop(0, CMB_TC // ROW_UNROLL, start, 0)

    @pl.when(i == 0)
    def _():
        gather(pos1_ref, pos2_ref, 0)

    @pl.when(i + 1 < pl.num_programs(0))
    def _():
        gather(pos1_next_ref, pos2_next_ref, 1 - slot)

    for k in range(2):
        pltpu.make_async_copy(ys_ref.at[pl.ds(0, CMB_TC), :], buf_ref.at[slot, k], sem.at[slot]).wait()

    rec = route_ref[...]
    w1, w2 = rec[:, R_W1:R_W1 + 1], rec[:, R_W2:R_W2 + 1]
    lo1, hi1 = _unpack_pair(buf_ref[slot, 0])
    lo2, hi2 = _unpack_pair(buf_ref[slot, 1])
    out_lo = x_ref[:, :HALF] + w1 * lo1 + w2 * lo2
    out_hi = x_ref[:, HALF:] + w1 * hi1 + w2 * hi2
    if final_norm:
        ms = (jnp.sum(out_lo * out_lo, axis=-1, keepdims=True)
              + jnp.sum(out_hi * out_hi, axis=-1, keepdims=True)) * (1.0 / D_MODEL)
        inv = lax.rsqrt(ms + RMS_EPS)
        out_lo = out_lo * inv * g_ref[:, :HALF]
        out_hi = out_hi * inv * g_ref[:, HALF:]
    o_ref[:, :HALF] = out_lo
    o_ref[:, HALF:] = out_hi


def _combine(x2d, route, ys, pos1, pos2, g_row, final_norm):
    m = x2d.shape[0]
    n_steps = m // CMB_TC
    pos_spec = pl.BlockSpec((1, 1, CMB_TC), lambda i: (i, 0, 0), memory_space=pltpu.SMEM)
    next_spec = pl.BlockSpec((1, 1, CMB_TC), lambda i: (jnp.minimum(i + 1, n_steps - 1), 0, 0),
                             memory_space=pltpu.SMEM)
    pos1, pos2 = pos1.reshape(n_steps, 1, CMB_TC), pos2.reshape(n_steps, 1, CMB_TC)
    return pl.pallas_call(
        functools.partial(_combine_kernel, final_norm=final_norm),
        grid=(n_steps,),
        in_specs=[pos_spec, pos_spec, next_spec, next_spec,
                  pl.BlockSpec((CMB_TC, D_MODEL), lambda i: (i, 0)),
                  pl.BlockSpec((CMB_TC, LANES), lambda i: (i, 0)),
                  pl.BlockSpec((1, D_MODEL), lambda i: (0, 0)),
                  pl.BlockSpec(memory_space=pl.ANY)],
        out_specs=pl.BlockSpec((CMB_TC, D_MODEL), lambda i: (i, 0)),
        out_shape=jax.ShapeDtypeStruct((m, D_MODEL), F32),
        scratch_shapes=[pltpu.VMEM((2, 2, CMB_TC, HALF), jnp.uint32), pltpu.SemaphoreType.DMA((2,))],
        compiler_params=_cparams(("arbitrary",)),
        name="moe_combine",
    )(pos1, pos2, pos1, pos2, x2d, route, g_row, ys)


RL_TN = 512
RL_PIECE = 256
SRC_A_END = 4096
SRC_AB = 16
SRC_COLS = 17936


def _relayout_plan():
    n_rest = (SRC_COLS - SRC_A_END - SRC_AB) // RL_TN
    rest = lambda s: SRC_A_END + SRC_AB + RL_TN * s
    starts = ([rest(s) for s in range(15, n_rest)]
              + [RL_TN * a for a in range(SRC_A_END // RL_TN)]
              + [rest(s) for s in range(15)]
              + [SRC_A_END])
    assert len(starts) * RL_TN == PROJ_COLS and all(s % 16 == 0 and s + RL_TN <= SRC_COLS for s in starts)
    return np.asarray(starts, np.int32)


def _relayout_kernel(start_ref, wt_ref, o_ref, buf_ref, sem, *, layer):
    t = pl.program_id(0)
    n_tiles = pl.num_programs(0)
    slot = t % 2

    def fetch(tile, s):
        first = pl.multiple_of(start_ref[tile], 16)
        pltpu.make_async_copy(wt_ref.at[layer, pl.ds(first, RL_TN), :], buf_ref.at[s], sem.at[s]).start()

    @pl.when(t == 0)
    def _():
        fetch(0, 0)

    @pl.when(t + 1 < n_tiles)
    def _():
        fetch(t + 1, 1 - slot)

    pltpu.make_async_copy(wt_ref.at[layer, pl.ds(0, RL_TN), :], buf_ref.at[slot], sem.at[slot]).wait()

    row = lax.broadcasted_iota(jnp.int32, (RL_TN, RL_PIECE), 0)
    keep = jnp.logical_or(t + 1 < n_tiles, row < SRC_AB)
    for c in range(D_MODEL // RL_PIECE):
        cols = slice(c * RL_PIECE, (c + 1) * RL_PIECE)
        piece = jnp.where(keep, buf_ref[slot, :, cols], 0.0)
        o_ref[cols, :] = piece.T.astype(BF16)


def _prep_w_in(w_all, layer):
    wt = jnp.swapaxes(w_all, 1, 2)
    return pl.pallas_call(
        functools.partial(_relayout_kernel, layer=layer),
        grid_spec=pltpu.PrefetchScalarGridSpec(
            num_scalar_prefetch=1,
            grid=(PROJ_COLS // RL_TN,),
            in_specs=[pl.BlockSpec(memory_space=pl.ANY)],
            out_specs=pl.BlockSpec((D_MODEL, RL_TN), lambda t, starts: (0, t)),
            scratch_shapes=[pltpu.VMEM((2, RL_TN, D_MODEL), F32), pltpu.SemaphoreType.DMA((2,))],
        ),
        out_shape=jax.ShapeDtypeStruct((D_MODEL, PROJ_COLS), BF16),
        compiler_params=_cparams(("arbitrary",)),
        name="w_in_relayout",
    )(jnp.asarray(_relayout_plan()), wt)


def _lane_row(v, offset=0):
    return jnp.zeros((1, LANES), F32).at[0, offset:offset + v.shape[0]].set(v.astype(F32))


def _layer(x2d, batch, seq, p, final_g_row):
    proj2 = _in_proj(x2d, p["norm_mix_g"], p["w_in"])
    proj3 = proj2.reshape(batch, seq, PROJ_COLS)
    ydn = _deltanet(proj3, p["dn_conv_w"], p["alog_row"], p["dtb_row"], p["dn_norm_g"])
    swa_outs = [_swa_group(proj3, gi) for gi in range(len(SWA_GROUPS))]
    x2d = _merge(x2d, ydn.reshape(batch * seq, DN_WIDTH), proj2, swa_outs, p["sc_conv_w"],
                 p["w_branch_dn"], p["w_branch_sc"], p["w_branch_swa"], p["w_out"], seq)
    hp, route, cnt = _router(x2d, p["norm_ffn_g"], p["w_route"], p["b_route"])
    plan = _moe_plan(route, cnt)
    xs = _dispatch(hp, plan["pos1"], plan["pos2"], plan["counts"], plan["offs"], plan["n_active"])
    ys = _experts(xs, plan["tile_expert"], plan["n_active"], plan["next_expert"], plan["parity"],
                  p["expert_w_gate"], p["expert_w_up"], p["expert_w_down"], p["layer"])
    is_last = final_g_row is not None
    g_row = final_g_row if is_last else p["norm_ffn_g"]
    return _combine(x2d, route, ys, plan["pos1"], plan["pos2"], g_row, is_last)


def kernel(x, norm_mix_g, w_in, dn_conv_w, dn_a_log, dn_dt_bias, dn_norm_g, sc_conv_w, w_branch_dn, w_branch_sc, w_branch_swa, w_out, norm_ffn_g, router_group_w, router_group_b, router_expert_w, router_expert_b, expert_w_gate, expert_w_up, expert_w_down, final_norm_g):
    batch, seq, _ = x.shape
    depth = w_in.shape[0]
    x2d = x.reshape(batch * seq, D_MODEL)
    for l in range(depth):
        w_route = jnp.concatenate([router_group_w[l], router_expert_w[l]], axis=1)
        w_route = jnp.pad(w_route, ((0, 0), (0, LANES - w_route.shape[1])))
        p = dict(
            norm_mix_g=norm_mix_g[l].reshape(1, D_MODEL),
            w_in=_prep_w_in(w_in, l),
            dn_conv_w=dn_conv_w[l],
            alog_row=_lane_row(dn_a_log[l]),
            dtb_row=_lane_row(dn_dt_bias[l]),
            dn_norm_g=dn_norm_g[l].reshape(1, HEAD_DIM),
            sc_conv_w=sc_conv_w[l],
            w_branch_dn=w_branch_dn[l].astype(BF16),
            w_branch_sc=w_branch_sc[l].astype(BF16),
            w_branch_swa=w_branch_swa[l].astype(BF16),
            w_out=w_out[l].astype(BF16),
            norm_ffn_g=norm_ffn_g[l].reshape(1, D_MODEL),
            w_route=w_route,
            b_route=_lane_row(jnp.concatenate([router_group_b[l], router_expert_b[l]])),
            expert_w_gate=expert_w_gate,
            expert_w_up=expert_w_up,
            expert_w_down=expert_w_down,
            layer=l,
        )
        final_g_row = final_norm_g.reshape(1, D_MODEL) if l == depth - 1 else None
        x2d = _layer(x2d, batch, seq, p, final_g_row)
    return x2d.reshape(batch, seq, D_MODEL)
```

```python
import functools

import jax
import jax.numpy as jnp
import numpy as np
from jax import lax
from jax.experimental import pallas as pl
from jax.experimental.pallas import tpu as pltpu

F32 = jnp.float32
BF16 = jnp.bfloat16

D_MODEL = 2048
RMS_EPS = 1e-6
L2_EPS = 1e-6

DN_HEADS = 8
HEAD_DIM = 128
DN_WIDTH = DN_HEADS * HEAD_DIM
DN_CONV = 4
SC_WIDTH = 1024
SC_CONV = 3
SWA_GROUPS = ((128, 1), (512, 4), (2048, 16))
SWA_HEADS_PER_GROUP = 4
SWA_HEADS = 12
SWA_GROUP_WIDTH = SWA_HEADS_PER_GROUP * HEAD_DIM
SWA_BLOCK = 128
ALIBI_MAX_BIAS = 8.0
MOE_GROUPS = 4
MOE_EPG = 8
MOE_EXPERTS = 32
MOE_FF = 512

OFF_GATE = 0
OFF_DNQ = 6144
OFF_DNK = OFF_DNQ + DN_WIDTH
OFF_DNV = OFF_DNK + DN_WIDTH
OFF_DNZ = OFF_DNV + DN_WIDTH
OFF_SCC = OFF_DNZ + DN_WIDTH
OFF_SCB = OFF_SCC + SC_WIDTH
OFF_SCX = OFF_SCB + SC_WIDTH
OFF_SWA = OFF_SCX + SC_WIDTH
OFF_AB = OFF_SWA + 3 * SWA_HEADS * HEAD_DIM
PROJ_COLS = 18432

LANES = 128
SUBLANES = 8
SWA_OUT_SLOTS = SWA_HEADS_PER_GROUP + 1
CHUNK = 128
HALO = 16

VMEM_LIMIT = 56 * 1024 * 1024


def _cparams(sem):
    return pltpu.CompilerParams(dimension_semantics=sem, vmem_limit_bytes=VMEM_LIMIT)


def _sigmoid(x):
    return 1.0 / (1.0 + jnp.exp(-x))


def _silu(x):
    return x * _sigmoid(x)


def _softplus(x):
    return jnp.maximum(x, 0.0) + jnp.log(1.0 + jnp.exp(-jnp.abs(x)))


IN_TM = 1024
IN_TN = 1024
NORM_ROWS = 64


def _rmsnorm_rows(x, g):
    ms = jnp.mean(x * x, axis=-1, keepdims=True)
    return x * lax.rsqrt(ms + RMS_EPS) * g


def _in_proj_kernel(x_ref, g_ref, w_ref, o_ref, h_ref):
    @pl.when(pl.program_id(1) == 0)
    def _():
        def body(r, c):
            rows = pl.ds(pl.multiple_of(r * NORM_ROWS, NORM_ROWS), NORM_ROWS)
            h_ref[rows, :] = _rmsnorm_rows(x_ref[rows, :], g_ref[...]).astype(BF16)
            return c
        lax.fori_loop(0, IN_TM // NORM_ROWS, body, 0)

    o_ref[...] = jnp.dot(h_ref[...], w_ref[...], preferred_element_type=F32).astype(o_ref.dtype)


def _in_proj(x2d, g_row, w_bf16):
    m = x2d.shape[0]
    return pl.pallas_call(
        _in_proj_kernel,
        grid=(m // IN_TM, PROJ_COLS // IN_TN),
        in_specs=[
            pl.BlockSpec((IN_TM, D_MODEL), lambda i, j: (i, 0)),
            pl.BlockSpec((1, D_MODEL), lambda i, j: (0, 0)),
            pl.BlockSpec((D_MODEL, IN_TN), lambda i, j: (0, j)),
        ],
        out_specs=pl.BlockSpec((IN_TM, IN_TN), lambda i, j: (i, j)),
        out_shape=jax.ShapeDtypeStruct((m, PROJ_COLS), BF16),
        scratch_shapes=[pltpu.VMEM((IN_TM, D_MODEL), BF16)],
        compiler_params=_cparams(("parallel", "arbitrary")),
        name="in_proj",
    )(x2d, g_row, w_bf16)


DN_HB = 8
DN_TS = 512
DN_W = DN_HB * HEAD_DIM
DN_DOUBLINGS = CHUNK.bit_length() - 2


def _dot_nt(a, b):
    return lax.dot_general(a, b, (((1,), (1,)), ((), ())), preferred_element_type=F32)


def _dot_tn(a, b):
    return lax.dot_general(a, b, (((0,), (0,)), ((), ())), preferred_element_type=F32)


def _causal_conv(xx, w, width):
    acc = None
    for j in range(width):
        shift = width - 1 - j
        xs = xx if shift == 0 else pltpu.roll(xx, shift, axis=0)
        term = xs[HALO:, :] * w[j:j + 1, :]
        acc = term if acc is None else acc + term
    return acc


def _deltanet_kernel(q_ref, k_ref, v_ref, z_ref, ab_ref, cwq_ref, cwk_ref, cwv_ref,
                     alog_ref, dtb_ref, ng_ref, o_ref,
                     xpad_ref, state_ref):
    hg = pl.program_id(1)
    st = pl.program_id(2)

    @pl.when(st == 0)
    def _():
        xpad_ref[:, 0:HALO, :] = jnp.zeros((3, HALO, DN_W), BF16)
        state_ref[...] = jnp.zeros_like(state_ref)

    xpad_ref[0, HALO:, :] = q_ref[0]
    xpad_ref[1, HALO:, :] = k_ref[0]
    xpad_ref[2, HALO:, :] = v_ref[0]

    row_i = lax.broadcasted_iota(jnp.int32, (CHUNK, CHUNK), 0)
    col_j = lax.broadcasted_iota(jnp.int32, (CHUNK, CHUNK), 1)
    causal = row_i >= col_j
    strict = row_i > col_j
    tri = causal.astype(F32)
    eye = (row_i == col_j).astype(F32)
    lane = lax.broadcasted_iota(jnp.int32, (CHUNK, LANES), 1)
    neg_decay_rate = -jnp.exp(alog_ref[...])
    cws = (cwq_ref[...], cwk_ref[...], cwv_ref[...])
    ng = ng_ref[...]

    def chunk_body(c, carry):
        r0 = pl.multiple_of(c * CHUNK, CHUNK)
        win = pl.ds(r0, CHUNK + HALO)
        rows = pl.ds(r0, CHUNK)
        conv = [_silu(_causal_conv(xpad_ref[t, win, :].astype(F32), cws[t], DN_CONV)) for t in range(3)]
        ab = ab_ref[0, rows, :].astype(F32)
        g_raw = neg_decay_rate * _softplus(ab + dtb_ref[...])
        g_cum = jnp.dot(tri, g_raw, preferred_element_type=F32, precision=lax.Precision.HIGHEST)
        beta_all = _sigmoid(ab)
        z = z_ref[0, rows, :].astype(F32)

        heads = range(DN_HB)
        sls = [slice(i * HEAD_DIM, (i + 1) * HEAD_DIM) for i in heads]
        dot = functools.partial(jnp.dot, preferred_element_type=F32)

        g_col = [jnp.sum(jnp.where(lane == hg * DN_HB + i, g_cum, 0.0), axis=-1, keepdims=True) for i in heads]
        beta = [jnp.sum(jnp.where(lane == hg * DN_HB + i + DN_HEADS, beta_all, 0.0), axis=-1, keepdims=True)
                for i in heads]
        g_last = [g[CHUNK - 1:CHUNK, :] for g in g_col]
        eg = [jnp.exp(g) for g in g_col]
        ek = [jnp.exp(gl - g) for gl, g in zip(g_last, g_col)]
        g_b = [jnp.broadcast_to(g, (CHUNK, CHUNK)) for g in g_col]
        decay = [jnp.exp(jnp.where(causal, gb - gb.T, -jnp.inf)) for gb in g_b]

        qf = [conv[0][:, sl] for sl in sls]
        kf = [conv[1][:, sl] for sl in sls]
        vf = [conv[2][:, sl] for sl in sls]
        q = [x * lax.rsqrt(jnp.sum(x * x, axis=-1, keepdims=True) + L2_EPS) * (HEAD_DIM ** -0.5) for x in qf]
        k = [x * lax.rsqrt(jnp.sum(x * x, axis=-1, keepdims=True) + L2_EPS) for x in kf]
        kb = [ki * bi for ki, bi in zip(k, beta)]
        a2 = [_dot_nt(jnp.concatenate([qi, kbi], axis=0).astype(BF16), ki.astype(BF16))
              for qi, kbi, ki in zip(q, kb, k)]
        attn = [a[:CHUNK] * d for a, d in zip(a2, decay)]
        n_mat = [jnp.where(strict, -(a[CHUNK:] * d), 0.0) for a, d in zip(a2, decay)]

        u_mat = [eye + n for n in n_mat]
        p_mat = [dot(n.astype(BF16), n.astype(BF16)) for n in n_mat]
        for _ in range(DN_DOUBLINGS - 1):
            up = [dot(jnp.concatenate([u.astype(BF16), p.astype(BF16)], axis=0), p.astype(BF16))
                  for u, p in zip(u_mat, p_mat)]
            u_mat = [u + x[:CHUNK] for u, x in zip(u_mat, up)]
            p_mat = [x[CHUNK:] for x in up]
        u_mat = [u + dot(u.astype(BF16), p.astype(BF16)) for u, p in zip(u_mat, p_mat)]
        uw = [dot(u.astype(BF16), jnp.concatenate([vi * bi, kbi * egi], axis=1).astype(BF16))
              for u, vi, bi, kbi, egi in zip(u_mat, vf, beta, kb, eg)]

        s_old = [state_ref[i] for i in heads]
        qw = [dot(jnp.concatenate([qi * egi, x[:, HEAD_DIM:]], axis=0).astype(BF16), s.astype(BF16))
              for qi, egi, x, s in zip(q, eg, uw, s_old)]
        v16 = [(x[:, :HEAD_DIM] - y[CHUNK:]).astype(BF16) for x, y in zip(uw, qw)]
        o = [y[:CHUNK] + dot(a.astype(BF16), v) for y, a, v in zip(qw, attn, v16)]
        s_new = [s * jnp.exp(gl) + _dot_tn((ki * eki).astype(BF16), v)
                 for s, gl, ki, eki, v in zip(s_old, g_last, k, ek, v16)]
        for i in heads:
            state_ref[i] = s_new[i]
            on = o[i] * lax.rsqrt(jnp.mean(o[i] * o[i], axis=-1, keepdims=True) + RMS_EPS) * ng
            o_ref[0, rows, sls[i]] = (on * _silu(z[:, sls[i]])).astype(o_ref.dtype)
        return carry

    lax.fori_loop(0, DN_TS // CHUNK, chunk_body, 0)

    tail = pl.ds(DN_TS, HALO)
    head = pl.ds(0, HALO)
    for t in range(3):
        xpad_ref[t, head, :] = xpad_ref[t, tail, :]


def _deltanet(proj3, conv_w, alog_row, dtb_row, ng_row):
    b, s, _ = proj3.shape
    qb, kb_, vb, zb = (off // DN_W for off in (OFF_DNQ, OFF_DNK, OFF_DNV, OFF_DNZ))
    act = lambda base: pl.BlockSpec((1, DN_TS, DN_W), lambda bi, hg, st: (bi, st, base + hg))
    cw = lambda base: pl.BlockSpec((DN_CONV, DN_W), lambda bi, hg, st: (0, base + hg))
    row = pl.BlockSpec((1, LANES), lambda bi, hg, st: (0, 0))
    return pl.pallas_call(
        _deltanet_kernel,
        grid=(b, DN_HEADS // DN_HB, s // DN_TS),
        in_specs=[
            act(qb), act(kb_), act(vb), act(zb),
            pl.BlockSpec((1, DN_TS, LANES), lambda bi, hg, st: (bi, st, OFF_AB // LANES)),
            cw(0), cw(DN_WIDTH // DN_W), cw(2 * DN_WIDTH // DN_W),
            row, row, row,
        ],
        out_specs=pl.BlockSpec((1, DN_TS, DN_W), lambda bi, hg, st: (bi, st, hg)),
        out_shape=jax.ShapeDtypeStruct((b, s, DN_WIDTH), BF16),
        scratch_shapes=[
            pltpu.VMEM((3, DN_TS + HALO, DN_W), BF16),
            pltpu.VMEM((DN_HB, HEAD_DIM, HEAD_DIM), F32),
        ],
        compiler_params=_cparams(("parallel", "parallel", "arbitrary")),
        name="deltanet",
    )(proj3, proj3, proj3, proj3, proj3, conv_w, conv_w, conv_w, alog_row, dtb_row, ng_row)


SWA_CAST_ROWS = 256
SWA_UNROLL = 2


def _swa_kernel(q_ref, k_ref, v_ref, o_ref, *scratch, seq, dil, slopes):
    n_blocks = seq // dil // SWA_BLOCK
    qi = lax.broadcasted_iota(jnp.int32, (SWA_BLOCK, SWA_BLOCK), 0)
    kj = lax.broadcasted_iota(jnp.int32, (SWA_BLOCK, SWA_BLOCK), 1)
    step_cur = (qi - kj).astype(F32)
    step_prev = step_cur + float(SWA_BLOCK)
    ok_cur = qi >= kj
    ok_prev = kj >= qi
    lane = lax.broadcasted_iota(jnp.int32, (SWA_BLOCK, LANES), 1)
    scale = HEAD_DIM ** -0.5
    heads = range(SWA_HEADS_PER_GROUP)
    sls = [slice(hh * HEAD_DIM, (hh + 1) * HEAD_DIM) for hh in heads]
    m_h = [float(slopes[hh]) * float(dil) for hh in heads]

    if dil > 1:
        (f32_ref,) = scratch

        def cast_body(c, carry):
            rows = pl.ds(pl.multiple_of(c * SWA_CAST_ROWS, SWA_CAST_ROWS), SWA_CAST_ROWS)
            for t, ref in enumerate((q_ref, k_ref, v_ref)):
                for hh in heads:
                    f32_ref[t * SWA_HEADS_PER_GROUP + hh, rows, :] = ref[0, rows, sls[hh]].astype(F32)
            return carry
        lax.fori_loop(0, seq // SWA_CAST_ROWS, cast_body, 0)

        def load(t, rows):
            return [f32_ref[t * SWA_HEADS_PER_GROUP + hh, rows, :].astype(BF16) for hh in heads]
    else:
        refs = (q_ref, k_ref, v_ref)

        def load(t, rows):
            return [refs[t][0, rows, sls[hh]] for hh in heads]

    def block_rows(it):
        r = it // n_blocks
        n = it % n_blocks
        base = n * (SWA_BLOCK * dil) + r
        prev_base = jnp.maximum(n - 1, 0) * (SWA_BLOCK * dil) + r
        if dil > 1:
            return n, pl.ds(base, SWA_BLOCK, stride=dil), pl.ds(prev_base, SWA_BLOCK, stride=dil)
        return (n, pl.ds(pl.multiple_of(base, SWA_BLOCK), SWA_BLOCK),
                pl.ds(pl.multiple_of(prev_base, SWA_BLOCK), SWA_BLOCK))

    def blocks_body(it2, carry):
        qn, kc, kp, vc, vp, ok_p, slope, out_rows = [], [], [], [], [], [], [], []
        for u in range(SWA_UNROLL):
            n, rows, prev_rows = block_rows(it2 * SWA_UNROLL + u)
            qn += load(0, rows)
            kc += load(1, rows)
            vc += load(2, rows)
            kp += load(1, prev_rows)
            vp += load(2, prev_rows)
            ok_p += [jnp.logical_and(ok_prev, n > 0)] * len(heads)
            slope += m_h
            out_rows.append(rows)

        s_cur = [jnp.where(ok_cur, _dot_nt(qh, kh) * scale - m * step_cur, -jnp.inf)
                 for qh, kh, m in zip(qn, kc, slope)]
        s_prev = [jnp.where(ok, _dot_nt(qh, kh) * scale - m * step_prev, -jnp.inf)
                  for ok, qh, kh, m in zip(ok_p, qn, kp, slope)]
        mx = [jnp.max(jnp.maximum(a, b), axis=-1, keepdims=True) for a, b in zip(s_cur, s_prev)]
        e_cur = [jnp.exp(a - m) for a, m in zip(s_cur, mx)]
        e_prev = [jnp.exp(b - m) for b, m in zip(s_prev, mx)]
        den = [jnp.sum(a + b, axis=-1, keepdims=True) for a, b in zip(e_cur, e_prev)]
        pv = [jnp.dot(a.astype(BF16), vch, preferred_element_type=F32)
              + jnp.dot(b.astype(BF16), vph, preferred_element_type=F32)
              for a, b, vch, vph in zip(e_cur, e_prev, vc, vp)]
        for u, rows in enumerate(out_rows):
            lse_tile = jnp.zeros((SWA_BLOCK, LANES), F32)
            for hh in heads:
                i = u * len(heads) + hh
                o_ref[0, hh, rows, :] = pv[i] * (1.0 / den[i])
                lse_tile = jnp.where(lane == hh, mx[i] + jnp.log(den[i]), lse_tile)
            o_ref[0, SWA_HEADS_PER_GROUP, rows, :] = lse_tile
        return carry

    lax.fori_loop(0, seq // SWA_BLOCK // SWA_UNROLL, blocks_body, 0)


def _swa_group(proj3, gi):
    batch, seq, _ = proj3.shape
    window, dil = SWA_GROUPS[gi]
    assert window // dil == SWA_BLOCK and seq % (dil * SWA_BLOCK) == 0
    slopes = 2.0 ** (-ALIBI_MAX_BIAS * np.arange(1, SWA_HEADS + 1) / SWA_HEADS)
    slopes = slopes.reshape(len(SWA_GROUPS), SWA_HEADS_PER_GROUP)[gi]

    def spec(which):
        cb = (OFF_SWA + which * SWA_HEADS * HEAD_DIM) // SWA_GROUP_WIDTH + gi
        return pl.BlockSpec((1, seq, SWA_GROUP_WIDTH), lambda bi: (bi, 0, cb))

    scratch = [pltpu.VMEM((3 * SWA_HEADS_PER_GROUP, seq, HEAD_DIM), F32)] if dil > 1 else []
    return pl.pallas_call(
        functools.partial(_swa_kernel, seq=seq, dil=dil, slopes=tuple(float(s) for s in slopes)),
        grid=(batch,),
        in_specs=[spec(0), spec(1), spec(2)],
        out_specs=pl.BlockSpec((1, SWA_OUT_SLOTS, seq, HEAD_DIM), lambda bi: (bi, 0, 0, 0)),
        out_shape=jax.ShapeDtypeStruct((batch, SWA_OUT_SLOTS, seq, HEAD_DIM), F32),
        scratch_shapes=scratch,
        compiler_params=_cparams(("parallel",)),
        name=f"swa_g{gi}",
    )(proj3, proj3, proj3)


MG_TM = 256


def _merge_kernel(x_ref, ydn_ref, gate_ref, scc_ref, scb_ref, scx_ref, scc_h_ref, scx_h_ref,
                  s0_ref, s1_ref, s2_ref, scw_ref, wdn_ref, wsc_ref, wswa_ref, wout_ref, o_ref,
                  *, tiles_per_seq):
    i = pl.program_id(0)
    first = (i % tiles_per_seq) == 0

    cx_cur = scc_ref[...].astype(F32) * scx_ref[...].astype(F32)
    cx_halo = scc_h_ref[...].astype(F32) * scx_h_ref[...].astype(F32)
    cx_halo = jnp.where(first, 0.0, cx_halo)
    cx = jnp.concatenate([cx_halo, cx_cur], axis=0)
    y_sc = scb_ref[...].astype(F32) * _causal_conv(cx, scw_ref[...], SC_CONV)

    outs = (s0_ref, s1_ref, s2_ref)
    lses = [o[0, SWA_HEADS_PER_GROUP] for o in outs]
    mx = jnp.maximum(jnp.maximum(lses[0], lses[1]), lses[2])
    es = [jnp.exp(l - mx) for l in lses]
    inv = 1.0 / (es[0] + es[1] + es[2])
    heads = []
    for hh in range(SWA_HEADS_PER_GROUP):
        acc = None
        for gi in range(3):
            wt = (es[gi] * inv)[:, hh:hh + 1]
            term = wt * outs[gi][0, hh]
            acc = term if acc is None else acc + term
        heads.append(acc)
    y_swa = jnp.concatenate(heads, axis=1)

    gates = gate_ref[...].astype(F32)
    merged = (_sigmoid(gates[:, 0:D_MODEL])
              * jnp.dot(ydn_ref[...], wdn_ref[...], preferred_element_type=F32)
              + _sigmoid(gates[:, D_MODEL:2 * D_MODEL])
              * jnp.dot(y_sc.astype(BF16), wsc_ref[...], preferred_element_type=F32)
              + _sigmoid(gates[:, 2 * D_MODEL:3 * D_MODEL])
              * jnp.dot(y_swa.astype(BF16), wswa_ref[...], preferred_element_type=F32))
    o_ref[...] = x_ref[...] + jnp.dot(merged.astype(BF16), wout_ref[...], preferred_element_type=F32)


def _merge(x2d, ydn2, proj2, swa_outs, sc_conv_w, wdn, wsc, wswa, wout, seq):
    m = x2d.shape[0]
    tiles_per_seq = seq // MG_TM
    halo_blocks = MG_TM // HALO
    rows = lambda w, cb: pl.BlockSpec((MG_TM, w), lambda i: (i, cb))
    halo = lambda cb: pl.BlockSpec((HALO, SC_WIDTH), lambda i: (jnp.maximum(i * halo_blocks - 1, 0), cb))
    full = lambda a: pl.BlockSpec(a.shape, lambda i: (0, 0), pipeline_mode=pl.Buffered(1))
    swa = pl.BlockSpec((1, SWA_OUT_SLOTS, MG_TM, HEAD_DIM),
                       lambda i: (i // tiles_per_seq, 0, i % tiles_per_seq, 0))
    return pl.pallas_call(
        functools.partial(_merge_kernel, tiles_per_seq=tiles_per_seq),
        grid=(m // MG_TM,),
        in_specs=[
            rows(D_MODEL, 0),
            rows(DN_WIDTH, 0),
            rows(3 * D_MODEL, OFF_GATE // (3 * D_MODEL)),
            rows(SC_WIDTH, OFF_SCC // SC_WIDTH), rows(SC_WIDTH, OFF_SCB // SC_WIDTH),
            rows(SC_WIDTH, OFF_SCX // SC_WIDTH),
            halo(OFF_SCC // SC_WIDTH), halo(OFF_SCX // SC_WIDTH),
            swa, swa, swa,
            full(sc_conv_w), full(wdn), full(wsc), full(wswa), full(wout),
        ],
        out_specs=rows(D_MODEL, 0),
        out_shape=jax.ShapeDtypeStruct((m, D_MODEL), F32),
        compiler_params=_cparams(("parallel",)),
        name="merge",
    )(x2d, ydn2, proj2, proj2, proj2, proj2, proj2, proj2, *swa_outs, sc_conv_w, wdn, wsc, wswa, wout)


RT_TM = 512
ROUTE_LANE0 = MOE_GROUPS
R_E1, R_E2, R_W1, R_W2, R_RANK1, R_RANK2 = range(6)
HALF = D_MODEL // 2


def _pack_pair(lo, hi):
    lo_b = lax.bitcast_convert_type(lo.astype(BF16).astype(F32), jnp.uint32)
    hi_b = lax.bitcast_convert_type(hi.astype(BF16).astype(F32), jnp.uint32)
    return (hi_b & jnp.uint32(0xFFFF0000)) | (lo_b >> jnp.uint32(16))


def _unpack_pair(p):
    lo = lax.bitcast_convert_type(p << jnp.uint32(16), F32)
    hi = lax.bitcast_convert_type(p & jnp.uint32(0xFFFF0000), F32)
    return lo, hi


def _router_kernel(x_ref, g_ref, wrh_ref, wrl_ref, br_ref, hp_ref, route_ref, cnt_ref, carry_ref):
    @pl.when(pl.program_id(0) == 0)
    def _():
        carry_ref[...] = jnp.zeros_like(carry_ref)

    h = _rmsnorm_rows(x_ref[...], g_ref[...])
    hp_ref[...] = _pack_pair(h[:, :HALF], h[:, HALF:])
    h_hi = h.astype(BF16)
    h_lo = (h - h_hi.astype(F32)).astype(BF16)
    w_hi, w_lo = wrh_ref[...], wrl_ref[...]
    dot = functools.partial(jnp.dot, preferred_element_type=F32)
    logits = dot(h_hi, w_hi) + dot(h_lo, w_hi) + dot(h_hi, w_lo) + br_ref[...]
    lane = lax.broadcasted_iota(jnp.int32, logits.shape, 1)
    big = jnp.int32(LANES)
    neg = -jnp.inf

    is_grp = lane < MOE_GROUPS
    gl = jnp.where(is_grp, logits, neg)
    gmax = jnp.max(gl, axis=-1, keepdims=True)
    grp = jnp.min(jnp.where(gl == gmax, lane, big), axis=-1, keepdims=True)
    g_w = 1.0 / jnp.sum(jnp.exp(gl - gmax), axis=-1, keepdims=True)

    lo = ROUTE_LANE0 + grp * MOE_EPG
    in_grp = jnp.logical_and(lane >= lo, lane < lo + MOE_EPG)
    el = jnp.where(in_grp, logits, neg)
    m1 = jnp.max(el, axis=-1, keepdims=True)
    i1 = jnp.min(jnp.where(el == m1, lane, big), axis=-1, keepdims=True)
    el2 = jnp.where(lane == i1, neg, el)
    m2 = jnp.max(el2, axis=-1, keepdims=True)
    i2 = jnp.min(jnp.where(el2 == m2, lane, big), axis=-1, keepdims=True)
    e2 = jnp.exp(m2 - m1)
    w1 = g_w / (1.0 + e2)
    w2 = g_w * e2 / (1.0 + e2)

    tm = logits.shape[0]
    onehot = jnp.where(jnp.logical_or(lane == i1, lane == i2), 1.0, 0.0)
    earlier = (lax.broadcasted_iota(jnp.int32, (tm, tm), 0) > lax.broadcasted_iota(jnp.int32, (tm, tm), 1))
    before = carry_ref[...] + jnp.dot(jnp.where(earlier, 1.0, 0.0).astype(BF16), onehot.astype(BF16),
                                      preferred_element_type=F32)
    rank1 = jnp.sum(jnp.where(lane == i1, before, 0.0), axis=-1, keepdims=True)
    rank2 = jnp.sum(jnp.where(lane == i2, before, 0.0), axis=-1, keepdims=True)
    carry_ref[...] += jnp.sum(onehot, axis=0, keepdims=True)
    cnt_ref[...] = carry_ref[...]

    rec = jnp.zeros_like(logits)
    for ln, val in ((R_E1, (i1 - ROUTE_LANE0).astype(F32)), (R_E2, (i2 - ROUTE_LANE0).astype(F32)),
                    (R_W1, w1), (R_W2, w2), (R_RANK1, rank1), (R_RANK2, rank2)):
        rec = jnp.where(lane == ln, val, rec)
    route_ref[...] = rec


def _router(x2d, g_row, w_route_hi, w_route_lo, b_route):
    m = x2d.shape[0]
    return pl.pallas_call(
        _router_kernel,
        grid=(m // RT_TM,),
        in_specs=[
            pl.BlockSpec((RT_TM, D_MODEL), lambda i: (i, 0)),
            pl.BlockSpec((1, D_MODEL), lambda i: (0, 0)),
            pl.BlockSpec((D_MODEL, LANES), lambda i: (0, 0)),
            pl.BlockSpec((D_MODEL, LANES), lambda i: (0, 0)),
            pl.BlockSpec((1, LANES), lambda i: (0, 0)),
        ],
        out_specs=[pl.BlockSpec((RT_TM, HALF), lambda i: (i, 0)),
                   pl.BlockSpec((RT_TM, LANES), lambda i: (i, 0)),
                   pl.BlockSpec((1, LANES), lambda i: (0, 0))],
        out_shape=[jax.ShapeDtypeStruct((m, HALF), jnp.uint32), jax.ShapeDtypeStruct((m, LANES), F32),
                   jax.ShapeDtypeStruct((1, LANES), F32)],
        scratch_shapes=[pltpu.VMEM((1, LANES), F32)],
        compiler_params=_cparams(("arbitrary",)),
        name="router",
    )(x2d, g_row, w_route_hi, w_route_lo, b_route)


EXP_TM = 512


def _n_row_tiles(n_tokens):
    return 2 * n_tokens // EXP_TM + MOE_EXPERTS


def _positions_kernel(route_ref, offs_ref, pos_ref):
    rec = route_ref[...]
    lane = lax.broadcasted_iota(jnp.int32, rec.shape, 1)
    offs = offs_ref[...]

    def first_row(e):
        return jnp.sum(jnp.where(lane == e.astype(jnp.int32), offs, 0.0), axis=-1, keepdims=True)

    pos1 = first_row(rec[:, R_E1:R_E1 + 1]) + rec[:, R_RANK1:R_RANK1 + 1]
    pos2 = first_row(rec[:, R_E2:R_E2 + 1]) + rec[:, R_RANK2:R_RANK2 + 1]
    pos_ref[...] = jnp.where(lane == 0, pos1, jnp.where(lane == 1, pos2, 0.0)).astype(jnp.int32)


def _moe_plan(route, cnt):
    m = route.shape[0]
    n_tiles = _n_row_tiles(m)
    counts = cnt[0, ROUTE_LANE0:ROUTE_LANE0 + MOE_EXPERTS].astype(jnp.int32)
    padded = (counts + EXP_TM - 1) // EXP_TM * EXP_TM
    ends = jnp.cumsum(padded)
    offs = ends - padded
    pos = pl.pallas_call(
        _positions_kernel,
        grid=(m // RT_TM,),
        in_specs=[pl.BlockSpec((RT_TM, LANES), lambda i: (i, 0)), pl.BlockSpec((1, LANES), lambda i: (0, 0))],
        out_specs=pl.BlockSpec((RT_TM, LANES), lambda i: (i, 0)),
        out_shape=jax.ShapeDtypeStruct((m, LANES), jnp.int32),
        compiler_params=_cparams(("parallel",)),
        name="moe_positions",
    )(route, _lane_row(offs))
    n_active = ends[-1] // EXP_TM
    tile_row = jnp.maximum(jnp.minimum(jnp.arange(n_tiles), n_active - 1), 0) * EXP_TM
    tile_expert = jnp.minimum(jnp.sum(ends[None, :] <= tile_row[:, None], axis=1), MOE_EXPERTS - 1)
    used = counts > 0
    ids = jnp.where(used, jnp.arange(MOE_EXPERTS), MOE_EXPERTS)
    next_used = jnp.concatenate([lax.cummin(ids, reverse=True)[1:], jnp.full((1,), MOE_EXPERTS, ids.dtype)])
    parity = (jnp.cumsum(used.astype(jnp.int32)) - 1) % 2
    i32 = lambda a: a.astype(jnp.int32)
    return dict(pos1=pos[:, 0], pos2=pos[:, 1], counts=counts, offs=i32(offs),
                tile_expert=i32(tile_expert), n_active=i32(n_active).reshape(1),
                next_expert=i32(next_used[tile_expert]), parity=i32(parity[tile_expert]))


DSP_TB = 512
ROW_UNROLL = 8


def _dispatch_kernel(cnt_ref, offs_ref, nact_ref, pos1_ref, pos2_ref, hp_ref, xs_ref, zero_ref, sem, zero_sem):
    n_tiles = xs_ref.shape[0] // EXP_TM

    @pl.when(pl.program_id(0) == 0)
    def _():
        zero_ref[...] = jnp.zeros_like(zero_ref)

        def pad_pieces(e):
            pad = (-cnt_ref[e]) & (EXP_TM - 1)
            first = offs_ref[e] + cnt_ref[e]
            head = jnp.minimum((-first) & (SUBLANES - 1), pad)
            for k in range(SUBLANES - 1):
                yield k < head, pltpu.make_async_copy(zero_ref.at[pl.ds(0, 1), :],
                                                      xs_ref.at[pl.ds(first + k, 1), :], zero_sem)
            body = pad - head
            b = EXP_TM // 2
            while b >= SUBLANES:
                row = pl.multiple_of(first + head + (body & ~(2 * b - 1)), SUBLANES)
                yield (body & b) != 0, pltpu.make_async_copy(zero_ref.at[pl.ds(0, b), :],
                                                             xs_ref.at[pl.ds(row, b), :], zero_sem)
                b //= 2

        def tile_copy(t):
            rows = pl.ds(pl.multiple_of(t * EXP_TM, EXP_TM), EXP_TM)
            return pltpu.make_async_copy(zero_ref, xs_ref.at[rows, :], zero_sem)

        def for_all(method):
            def per_expert(e, c):
                for cond, cp in pad_pieces(e):
                    pl.when(cond)(getattr(cp, method))
                return c
            lax.fori_loop(0, MOE_EXPERTS, per_expert, 0)

            def per_tile(t, c):
                getattr(tile_copy(t), method)()
                return c
            lax.fori_loop(nact_ref[0], n_tiles, per_tile, 0)

        for_all("start")
        for_all("wait")

    def start(j8, c):
        for u in range(ROW_UNROLL):
            j = j8 * ROW_UNROLL + u
            src = hp_ref.at[pl.ds(j, 1), :]
            pltpu.make_async_copy(src, xs_ref.at[pl.ds(pos1_ref[0, 0, j], 1), :], sem).start(priority=0)
            pltpu.make_async_copy(src, xs_ref.at[pl.ds(pos2_ref[0, 0, j], 1), :], sem).start(priority=1)
        return c

    lax.fori_loop(0, DSP_TB // ROW_UNROLL, start, 0)
    for _ in range(2):
        pltpu.make_async_copy(hp_ref, xs_ref.at[pl.ds(0, DSP_TB), :], sem).wait()


def _dispatch(hp, pos1, pos2, counts, offs, n_active):
    m = hp.shape[0]
    n_rows = _n_row_tiles(m) * EXP_TM
    pos_spec = pl.BlockSpec((1, 1, DSP_TB), lambda i, *_: (i, 0, 0), memory_space=pltpu.SMEM)
    return pl.pallas_call(
        _dispatch_kernel,
        grid_spec=pltpu.PrefetchScalarGridSpec(
            num_scalar_prefetch=3,
            grid=(m // DSP_TB,),
            in_specs=[pos_spec, pos_spec, pl.BlockSpec((DSP_TB, HALF), lambda i, *_: (i, 0))],
            out_specs=pl.BlockSpec(memory_space=pl.ANY),
            scratch_shapes=[pltpu.VMEM((EXP_TM, HALF), jnp.uint32), pltpu.SemaphoreType.DMA(()),
                            pltpu.SemaphoreType.DMA(())],
        ),
        out_shape=jax.ShapeDtypeStruct((n_rows, HALF), jnp.uint32),
        compiler_params=_cparams(("arbitrary",)),
        name="moe_dispatch",
    )(counts, offs, n_active, pos1.reshape(m // DSP_TB, 1, DSP_TB), pos2.reshape(m // DSP_TB, 1, DSP_TB), hp)


W_CAST_ROWS = 256


def _cast_rows(src_ref, dst_ref):
    n_rows = dst_ref.shape[0]
    slab = min(W_CAST_ROWS, n_rows)

    def body(r, c):
        rows = pl.ds(pl.multiple_of(r * slab, slab), slab)
        dst_ref[rows, :] = src_ref[rows, :].astype(BF16)
        return c
    lax.fori_loop(0, n_rows // slab, body, 0)


def _expert_kernel(te_ref, nact_ref, next_ref, par_ref, xs_ref, wg_hbm, wu_hbm, wd_hbm, ys_ref,
                   sg_ref, su_ref, sd_ref, wg16_ref, wu16_ref, wd16_ref, sem, *, layer):
    i = pl.program_id(0)
    active = i < nact_ref[0]
    expert = te_ref[i]
    slot = par_ref[i]
    new_expert = jnp.logical_or(i == 0, expert != te_ref[jnp.maximum(i - 1, 0)])

    def copies(e, s):
        return [pltpu.make_async_copy(hbm.at[layer, e], stage.at[s], sem.at[s])
                for hbm, stage in ((wg_hbm, sg_ref), (wu_hbm, su_ref), (wd_hbm, sd_ref))]

    @pl.when(jnp.logical_and(active, i == 0))
    def _():
        for cp in copies(expert, slot):
            cp.start()

    @pl.when(jnp.logical_and(active, new_expert))
    def _():
        for cp in copies(expert, slot):
            cp.wait()

        @pl.when(next_ref[i] < MOE_EXPERTS)
        def _():
            for cp in copies(next_ref[i], 1 - slot):
                cp.start()

        _cast_rows(sg_ref.at[slot], wg16_ref)
        _cast_rows(su_ref.at[slot], wu16_ref)
        _cast_rows(sd_ref.at[slot], wd16_ref)

    @pl.when(active)
    def _():
        lo, hi = _unpack_pair(xs_ref[...])
        lo, hi = lo.astype(BF16), hi.astype(BF16)
        dot = functools.partial(jnp.dot, preferred_element_type=F32)
        gate = dot(lo, wg16_ref[:HALF, :]) + dot(hi, wg16_ref[HALF:, :])
        up = dot(lo, wu16_ref[:HALF, :]) + dot(hi, wu16_ref[HALF:, :])
        y = dot((_silu(gate) * up).astype(BF16), wd16_ref[...])
        ys_ref[...] = _pack_pair(y[:, :HALF], y[:, HALF:])

    @pl.when(jnp.logical_not(active))
    def _():
        ys_ref[...] = jnp.zeros_like(ys_ref)


def _experts(xs, tile_expert, n_active, next_expert, parity, wg, wu, wd, layer):
    n_tiles = xs.shape[0] // EXP_TM
    rows = pl.BlockSpec((EXP_TM, HALF),
                        lambda i, te, na, nx, par: (jnp.maximum(jnp.minimum(i, na[0] - 1), 0), 0))
    out_rows = pl.BlockSpec((EXP_TM, HALF), lambda i, te, na, nx, par: (i, 0))
    hbm = pl.BlockSpec(memory_space=pl.ANY)
    return pl.pallas_call(
        functools.partial(_expert_kernel, layer=layer),
        grid_spec=pltpu.PrefetchScalarGridSpec(
            num_scalar_prefetch=4,
            grid=(n_tiles,),
            in_specs=[rows, hbm, hbm, hbm],
            out_specs=out_rows,
            scratch_shapes=[pltpu.VMEM((2, D_MODEL, MOE_FF), F32), pltpu.VMEM((2, D_MODEL, MOE_FF), F32),
                            pltpu.VMEM((2, MOE_FF, D_MODEL), F32),
                            pltpu.VMEM((D_MODEL, MOE_FF), BF16), pltpu.VMEM((D_MODEL, MOE_FF), BF16),
                            pltpu.VMEM((MOE_FF, D_MODEL), BF16),
                            pltpu.SemaphoreType.DMA((2,))],
        ),
        out_shape=jax.ShapeDtypeStruct(xs.shape, jnp.uint32),
        compiler_params=_cparams(("arbitrary",)),
        name="moe_experts",
    )(tile_expert, n_active, next_expert, parity, xs, wg, wu, wd)


CMB_TC = 256


def _combine_kernel(pos1_ref, pos2_ref, pos1_next_ref, pos2_next_ref, x_ref, route_ref, g_ref, ys_ref,
                    o_ref, buf_ref, sem, *, final_norm):
    i = pl.program_id(0)
    slot = i % 2

    def gather(p1_ref, p2_ref, s):
        def start(j8, c):
            for u in range(ROW_UNROLL):
                j = j8 * ROW_UNROLL + u
                for k, p_ref in enumerate((p1_ref, p2_ref)):
                    pltpu.make_async_copy(ys_ref.at[pl.ds(p_ref[0, 0, j], 1), :],
                                          buf_ref.at[s, k, pl.ds(j, 1), :], sem.at[s]).start(priority=k)
            return c
        lax.fori_loop(0, CMB_TC // ROW_UNROLL, start, 0)

    @pl.when(i == 0)
    def _():
        gather(pos1_ref, pos2_ref, 0)

    @pl.when(i + 1 < pl.num_programs(0))
    def _():
        gather(pos1_next_ref, pos2_next_ref, 1 - slot)

    for k in range(2):
        pltpu.make_async_copy(ys_ref.at[pl.ds(0, CMB_TC), :], buf_ref.at[slot, k], sem.at[slot]).wait()

    rec = route_ref[...]
    w1, w2 = rec[:, R_W1:R_W1 + 1], rec[:, R_W2:R_W2 + 1]
    lo1, hi1 = _unpack_pair(buf_ref[slot, 0])
    lo2, hi2 = _unpack_pair(buf_ref[slot, 1])
    out_lo = x_ref[:, :HALF] + w1 * lo1 + w2 * lo2
    out_hi = x_ref[:, HALF:] + w1 * hi1 + w2 * hi2
    if final_norm:
        ms = (jnp.sum(out_lo * out_lo, axis=-1, keepdims=True)
              + jnp.sum(out_hi * out_hi, axis=-1, keepdims=True)) * (1.0 / D_MODEL)
        inv = lax.rsqrt(ms + RMS_EPS)
        out_lo = out_lo * inv * g_ref[:, :HALF]
        out_hi = out_hi * inv * g_ref[:, HALF:]
    o_ref[:, :HALF] = out_lo
    o_ref[:, HALF:] = out_hi


def _combine(x2d, route, ys, pos1, pos2, g_row, final_norm):
    m = x2d.shape[0]
    n_steps = m // CMB_TC
    pos_spec = pl.BlockSpec((1, 1, CMB_TC), lambda i: (i, 0, 0), memory_space=pltpu.SMEM)
    next_spec = pl.BlockSpec((1, 1, CMB_TC), lambda i: (jnp.minimum(i + 1, n_steps - 1), 0, 0),
                             memory_space=pltpu.SMEM)
    pos1, pos2 = pos1.reshape(n_steps, 1, CMB_TC), pos2.reshape(n_steps, 1, CMB_TC)
    return pl.pallas_call(
        functools.partial(_combine_kernel, final_norm=final_norm),
        grid=(n_steps,),
        in_specs=[pos_spec, pos_spec, next_spec, next_spec,
                  pl.BlockSpec((CMB_TC, D_MODEL), lambda i: (i, 0)),
                  pl.BlockSpec((CMB_TC, LANES), lambda i: (i, 0)),
                  pl.BlockSpec((1, D_MODEL), lambda i: (0, 0)),
                  pl.BlockSpec(memory_space=pl.ANY)],
        out_specs=pl.BlockSpec((CMB_TC, D_MODEL), lambda i: (i, 0)),
        out_shape=jax.ShapeDtypeStruct((m, D_MODEL), F32),
        scratch_shapes=[pltpu.VMEM((2, 2, CMB_TC, HALF), jnp.uint32), pltpu.SemaphoreType.DMA((2,))],
        compiler_params=_cparams(("arbitrary",)),
        name="moe_combine",
    )(pos1, pos2, pos1, pos2, x2d, route, g_row, ys)


RL_TN = 512
RL_PIECE = 256
SRC_A_END = 4096
SRC_AB = 16
SRC_COLS = 17936


def _relayout_plan():
    n_rest = (SRC_COLS - SRC_A_END - SRC_AB) // RL_TN
    rest = lambda s: SRC_A_END + SRC_AB + RL_TN * s
    starts = ([rest(s) for s in range(15, n_rest)]
              + [RL_TN * a for a in range(SRC_A_END // RL_TN)]
              + [rest(s) for s in range(15)]
              + [SRC_A_END])
    assert len(starts) * RL_TN == PROJ_COLS and all(s % 16 == 0 and s + RL_TN <= SRC_COLS for s in starts)
    return np.asarray(starts, np.int32)


def _relayout_kernel(start_ref, wt_ref, o_ref, buf_ref, sem, *, layer):
    t = pl.program_id(0)
    n_tiles = pl.num_programs(0)
    slot = t % 2

    def fetch(tile, s):
        first = pl.multiple_of(start_ref[tile], 16)
        pltpu.make_async_copy(wt_ref.at[layer, pl.ds(first, RL_TN), :], buf_ref.at[s], sem.at[s]).start()

    @pl.when(t == 0)
    def _():
        fetch(0, 0)

    @pl.when(t + 1 < n_tiles)
    def _():
        fetch(t + 1, 1 - slot)

    pltpu.make_async_copy(wt_ref.at[layer, pl.ds(0, RL_TN), :], buf_ref.at[slot], sem.at[slot]).wait()

    row = lax.broadcasted_iota(jnp.int32, (RL_TN, RL_PIECE), 0)
    keep = jnp.logical_or(t + 1 < n_tiles, row < SRC_AB)
    for c in range(D_MODEL // RL_PIECE):
        cols = slice(c * RL_PIECE, (c + 1) * RL_PIECE)
        piece = jnp.where(keep, buf_ref[slot, :, cols], 0.0)
        o_ref[cols, :] = piece.T.astype(BF16)


def _prep_w_in(w_all, layer):
    wt = jnp.swapaxes(w_all, 1, 2)
    return pl.pallas_call(
        functools.partial(_relayout_kernel, layer=layer),
        grid_spec=pltpu.PrefetchScalarGridSpec(
            num_scalar_prefetch=1,
            grid=(PROJ_COLS // RL_TN,),
            in_specs=[pl.BlockSpec(memory_space=pl.ANY)],
            out_specs=pl.BlockSpec((D_MODEL, RL_TN), lambda t, starts: (0, t)),
            scratch_shapes=[pltpu.VMEM((2, RL_TN, D_MODEL), F32), pltpu.SemaphoreType.DMA((2,))],
        ),
        out_shape=jax.ShapeDtypeStruct((D_MODEL, PROJ_COLS), BF16),
        compiler_params=_cparams(("arbitrary",)),
        name="w_in_relayout",
    )(jnp.asarray(_relayout_plan()), wt)


def _lane_row(v, offset=0):
    return jnp.zeros((1, LANES), F32).at[0, offset:offset + v.shape[0]].set(v.astype(F32))


def _layer(x2d, batch, seq, p, final_g_row):
    proj2 = _in_proj(x2d, p["norm_mix_g"], p["w_in"])
    proj3 = proj2.reshape(batch, seq, PROJ_COLS)
    ydn = _deltanet(proj3, p["dn_conv_w"], p["alog_row"], p["dtb_row"], p["dn_norm_g"])
    swa_outs = [_swa_group(proj3, gi) for gi in range(len(SWA_GROUPS))]
    x2d = _merge(x2d, ydn.reshape(batch * seq, DN_WIDTH), proj2, swa_outs, p["sc_conv_w"],
                 p["w_branch_dn"], p["w_branch_sc"], p["w_branch_swa"], p["w_out"], seq)
    hp, route, cnt = _router(x2d, p["norm_ffn_g"], p["w_route_hi"], p["w_route_lo"], p["b_route"])
    plan = _moe_plan(route, cnt)
    xs = _dispatch(hp, plan["pos1"], plan["pos2"], plan["counts"], plan["offs"], plan["n_active"])
    ys = _experts(xs, plan["tile_expert"], plan["n_active"], plan["next_expert"], plan["parity"],
                  p["expert_w_gate"], p["expert_w_up"], p["expert_w_down"], p["layer"])
    is_last = final_g_row is not None
    g_row = final_g_row if is_last else p["norm_ffn_g"]
    return _combine(x2d, route, ys, plan["pos1"], plan["pos2"], g_row, is_last)


def kernel(x, norm_mix_g, w_in, dn_conv_w, dn_a_log, dn_dt_bias, dn_norm_g, sc_conv_w, w_branch_dn, w_branch_sc, w_branch_swa, w_out, norm_ffn_g, router_group_w, router_group_b, router_expert_w, router_expert_b, expert_w_gate, expert_w_up, expert_w_down, final_norm_g):
    batch, seq, _ = x.shape
    depth = w_in.shape[0]
    x2d = x.reshape(batch * seq, D_MODEL)
    for l in range(depth):
        w_route = jnp.concatenate([router_group_w[l], router_expert_w[l]], axis=1)
        w_route = jnp.pad(w_route, ((0, 0), (0, LANES - w_route.shape[1])))
        p = dict(
            norm_mix_g=norm_mix_g[l].reshape(1, D_MODEL),
            w_in=_prep_w_in(w_in, l),
            dn_conv_w=dn_conv_w[l],
            alog_row=_lane_row(dn_a_log[l]),
            dtb_row=_lane_row(dn_dt_bias[l]),
            dn_norm_g=dn_norm_g[l].reshape(1, HEAD_DIM),
            sc_conv_w=sc_conv_w[l],
            w_branch_dn=w_branch_dn[l].astype(BF16),
            w_branch_sc=w_branch_sc[l].astype(BF16),
            w_branch_swa=w_branch_swa[l].astype(BF16),
            w_out=w_out[l].astype(BF16),
            norm_ffn_g=norm_ffn_g[l].reshape(1, D_MODEL),
            w_route_hi=w_route.astype(BF16),
            w_route_lo=(w_route - w_route.astype(BF16).astype(F32)).astype(BF16),
            b_route=_lane_row(jnp.concatenate([router_group_b[l], router_expert_b[l]])),
            expert_w_gate=expert_w_gate,
            expert_w_up=expert_w_up,
            expert_w_down=expert_w_down,
            layer=l,
        )
        final_g_row = final_norm_g.reshape(1, D_MODEL) if l == depth - 1 else None
        x2d = _layer(x2d, batch, seq, p, final_g_row)
    return x2d.reshape(batch, seq, D_MODEL)
```

```python
import functools

import jax
import jax.numpy as jnp
import numpy as np
from jax import lax
from jax.experimental import pallas as pl
from jax.experimental.pallas import tpu as pltpu

F32 = jnp.float32
BF16 = jnp.bfloat16

D_MODEL = 2048
RMS_EPS = 1e-6
L2_EPS = 1e-6

DN_HEADS = 8
HEAD_DIM = 128
DN_WIDTH = DN_HEADS * HEAD_DIM
DN_CONV = 4
SC_WIDTH = 1024
SC_CONV = 3
SWA_GROUPS = ((128, 1), (512, 4), (2048, 16))
SWA_HEADS_PER_GROUP = 4
SWA_HEADS = 12
SWA_GROUP_WIDTH = SWA_HEADS_PER_GROUP * HEAD_DIM
SWA_BLOCK = 128
ALIBI_MAX_BIAS = 8.0
MOE_GROUPS = 4
MOE_EPG = 8
MOE_EXPERTS = 32
MOE_FF = 512

OFF_GATE = 0
OFF_DNQ = 6144
OFF_DNK = OFF_DNQ + DN_WIDTH
OFF_DNV = OFF_DNK + DN_WIDTH
OFF_DNZ = OFF_DNV + DN_WIDTH
OFF_SCC = OFF_DNZ + DN_WIDTH
OFF_SCB = OFF_SCC + SC_WIDTH
OFF_SCX = OFF_SCB + SC_WIDTH
OFF_SWA = OFF_SCX + SC_WIDTH
OFF_AB = OFF_SWA + 3 * SWA_HEADS * HEAD_DIM
PROJ_COLS = 18432

LANES = 128
SUBLANES = 8
SWA_OUT_SLOTS = SWA_HEADS_PER_GROUP + 1
CHUNK = 128
HALO = 16

VMEM_LIMIT = 56 * 1024 * 1024


def _cparams(sem):
    return pltpu.CompilerParams(dimension_semantics=sem, vmem_limit_bytes=VMEM_LIMIT)


def _sigmoid(x):
    return 1.0 / (1.0 + jnp.exp(-x))


def _silu(x):
    return x * _sigmoid(x)


def _softplus(x):
    return jnp.maximum(x, 0.0) + jnp.log(1.0 + jnp.exp(-jnp.abs(x)))


IN_TM = 1024
IN_TN = 2048
NORM_ROWS = 64


def _rmsnorm_rows(x, g):
    ms = jnp.mean(x * x, axis=-1, keepdims=True)
    return x * lax.rsqrt(ms + RMS_EPS) * g


def _in_proj_kernel(x_ref, g_ref, w_ref, o_ref, h_ref):
    @pl.when(pl.program_id(1) == 0)
    def _():
        def body(r, c):
            rows = pl.ds(pl.multiple_of(r * NORM_ROWS, NORM_ROWS), NORM_ROWS)
            h_ref[rows, :] = _rmsnorm_rows(x_ref[rows, :], g_ref[...]).astype(BF16)
            return c
        lax.fori_loop(0, IN_TM // NORM_ROWS, body, 0)

    o_ref[...] = jnp.dot(h_ref[...], w_ref[...], preferred_element_type=F32).astype(o_ref.dtype)


def _in_proj(x2d, g_row, w_bf16):
    m = x2d.shape[0]
    return pl.pallas_call(
        _in_proj_kernel,
        grid=(m // IN_TM, PROJ_COLS // IN_TN),
        in_specs=[
            pl.BlockSpec((IN_TM, D_MODEL), lambda i, j: (i, 0)),
            pl.BlockSpec((1, D_MODEL), lambda i, j: (0, 0)),
            pl.BlockSpec((D_MODEL, IN_TN), lambda i, j: (0, j)),
        ],
        out_specs=pl.BlockSpec((IN_TM, IN_TN), lambda i, j: (i, j)),
        out_shape=jax.ShapeDtypeStruct((m, PROJ_COLS), BF16),
        scratch_shapes=[pltpu.VMEM((IN_TM, D_MODEL), BF16)],
        compiler_params=_cparams(("parallel", "arbitrary")),
        name="in_proj",
    )(x2d, g_row, w_bf16)


DN_HB = 8
DN_TS = 512
DN_W = DN_HB * HEAD_DIM
DN_DOUBLINGS = CHUNK.bit_length() - 2


def _dot_nt(a, b):
    return lax.dot_general(a, b, (((1,), (1,)), ((), ())), preferred_element_type=F32)


def _dot_tn(a, b):
    return lax.dot_general(a, b, (((0,), (0,)), ((), ())), preferred_element_type=F32)


def _causal_conv(xx, w, width):
    acc = None
    for j in range(width):
        shift = width - 1 - j
        xs = xx if shift == 0 else pltpu.roll(xx, shift, axis=0)
        term = xs[HALO:, :] * w[j:j + 1, :]
        acc = term if acc is None else acc + term
    return acc


def _deltanet_kernel(q_ref, k_ref, v_ref, z_ref, ab_ref, cwq_ref, cwk_ref, cwv_ref,
                     alog_ref, dtb_ref, ng_ref, o_ref,
                     xpad_ref, state_ref):
    hg = pl.program_id(1)
    st = pl.program_id(2)

    @pl.when(st == 0)
    def _():
        xpad_ref[:, 0:HALO, :] = jnp.zeros((3, HALO, DN_W), BF16)
        state_ref[...] = jnp.zeros_like(state_ref)

    xpad_ref[0, HALO:, :] = q_ref[0]
    xpad_ref[1, HALO:, :] = k_ref[0]
    xpad_ref[2, HALO:, :] = v_ref[0]

    row_i = lax.broadcasted_iota(jnp.int32, (CHUNK, CHUNK), 0)
    col_j = lax.broadcasted_iota(jnp.int32, (CHUNK, CHUNK), 1)
    causal = row_i >= col_j
    strict = row_i > col_j
    tri = causal.astype(F32)
    eye = (row_i == col_j).astype(F32)
    lane = lax.broadcasted_iota(jnp.int32, (CHUNK, LANES), 1)
    neg_decay_rate = -jnp.exp(alog_ref[...])
    cws = (cwq_ref[...], cwk_ref[...], cwv_ref[...])
    ng = ng_ref[...]

    def chunk_body(c, carry):
        r0 = pl.multiple_of(c * CHUNK, CHUNK)
        win = pl.ds(r0, CHUNK + HALO)
        rows = pl.ds(r0, CHUNK)
        conv = [_silu(_causal_conv(xpad_ref[t, win, :].astype(F32), cws[t], DN_CONV)) for t in range(3)]
        ab = ab_ref[0, rows, :].astype(F32)
        g_raw = neg_decay_rate * _softplus(ab + dtb_ref[...])
        g_cum = jnp.dot(tri, g_raw, preferred_element_type=F32, precision=lax.Precision.HIGHEST)
        beta_all = _sigmoid(ab)
        z = z_ref[0, rows, :].astype(F32)

        heads = range(DN_HB)
        sls = [slice(i * HEAD_DIM, (i + 1) * HEAD_DIM) for i in heads]
        dot = functools.partial(jnp.dot, preferred_element_type=F32)

        g_col = [jnp.sum(jnp.where(lane == hg * DN_HB + i, g_cum, 0.0), axis=-1, keepdims=True) for i in heads]
        beta = [jnp.sum(jnp.where(lane == hg * DN_HB + i + DN_HEADS, beta_all, 0.0), axis=-1, keepdims=True)
                for i in heads]
        g_last = [g[CHUNK - 1:CHUNK, :] for g in g_col]
        eg = [jnp.exp(g) for g in g_col]
        ek = [jnp.exp(gl - g) for gl, g in zip(g_last, g_col)]
        g_b = [jnp.broadcast_to(g, (CHUNK, CHUNK)) for g in g_col]
        decay = [jnp.exp(jnp.where(causal, gb - gb.T, -jnp.inf)) for gb in g_b]

        qf = [conv[0][:, sl] for sl in sls]
        kf = [conv[1][:, sl] for sl in sls]
        vf = [conv[2][:, sl] for sl in sls]
        q = [x * lax.rsqrt(jnp.sum(x * x, axis=-1, keepdims=True) + L2_EPS) * (HEAD_DIM ** -0.5) for x in qf]
        k = [x * lax.rsqrt(jnp.sum(x * x, axis=-1, keepdims=True) + L2_EPS) for x in kf]
        kb = [ki * bi for ki, bi in zip(k, beta)]
        a2 = [_dot_nt(jnp.concatenate([qi, kbi], axis=0).astype(BF16), ki.astype(BF16))
              for qi, kbi, ki in zip(q, kb, k)]
        attn = [a[:CHUNK] * d for a, d in zip(a2, decay)]
        n_mat = [jnp.where(strict, -(a[CHUNK:] * d), 0.0) for a, d in zip(a2, decay)]

        u_mat = [eye + n for n in n_mat]
        p_mat = [dot(n.astype(BF16), n.astype(BF16)) for n in n_mat]
        for _ in range(DN_DOUBLINGS - 1):
            up = [dot(jnp.concatenate([u.astype(BF16), p.astype(BF16)], axis=0), p.astype(BF16))
                  for u, p in zip(u_mat, p_mat)]
            u_mat = [u + x[:CHUNK] for u, x in zip(u_mat, up)]
            p_mat = [x[CHUNK:] for x in up]
        u_mat = [u + dot(u.astype(BF16), p.astype(BF16)) for u, p in zip(u_mat, p_mat)]
        uw = [dot(u.astype(BF16), jnp.concatenate([vi * bi, kbi * egi], axis=1).astype(BF16))
              for u, vi, bi, kbi, egi in zip(u_mat, vf, beta, kb, eg)]

        s_old = [state_ref[i] for i in heads]
        qw = [dot(jnp.concatenate([qi * egi, x[:, HEAD_DIM:]], axis=0).astype(BF16), s.astype(BF16))
              for qi, egi, x, s in zip(q, eg, uw, s_old)]
        v16 = [(x[:, :HEAD_DIM] - y[CHUNK:]).astype(BF16) for x, y in zip(uw, qw)]
        o = [y[:CHUNK] + dot(a.astype(BF16), v) for y, a, v in zip(qw, attn, v16)]
        s_new = [s * jnp.exp(gl) + _dot_tn((ki * eki).astype(BF16), v)
                 for s, gl, ki, eki, v in zip(s_old, g_last, k, ek, v16)]
        for i in heads:
            state_ref[i] = s_new[i]
            on = o[i] * lax.rsqrt(jnp.mean(o[i] * o[i], axis=-1, keepdims=True) + RMS_EPS) * ng
            o_ref[0, rows, sls[i]] = (on * _silu(z[:, sls[i]])).astype(o_ref.dtype)
        return carry

    lax.fori_loop(0, DN_TS // CHUNK, chunk_body, 0)

    tail = pl.ds(DN_TS, HALO)
    head = pl.ds(0, HALO)
    for t in range(3):
        xpad_ref[t, head, :] = xpad_ref[t, tail, :]


def _deltanet(proj3, conv_w, alog_row, dtb_row, ng_row):
    b, s, _ = proj3.shape
    qb, kb_, vb, zb = (off // DN_W for off in (OFF_DNQ, OFF_DNK, OFF_DNV, OFF_DNZ))
    act = lambda base: pl.BlockSpec((1, DN_TS, DN_W), lambda bi, hg, st: (bi, st, base + hg))
    cw = lambda base: pl.BlockSpec((DN_CONV, DN_W), lambda bi, hg, st: (0, base + hg))
    row = pl.BlockSpec((1, LANES), lambda bi, hg, st: (0, 0))
    return pl.pallas_call(
        _deltanet_kernel,
        grid=(b, DN_HEADS // DN_HB, s // DN_TS),
        in_specs=[
            act(qb), act(kb_), act(vb), act(zb),
            pl.BlockSpec((1, DN_TS, LANES), lambda bi, hg, st: (bi, st, OFF_AB // LANES)),
            cw(0), cw(DN_WIDTH // DN_W), cw(2 * DN_WIDTH // DN_W),
            row, row, row,
        ],
        out_specs=pl.BlockSpec((1, DN_TS, DN_W), lambda bi, hg, st: (bi, st, hg)),
        out_shape=jax.ShapeDtypeStruct((b, s, DN_WIDTH), BF16),
        scratch_shapes=[
            pltpu.VMEM((3, DN_TS + HALO, DN_W), BF16),
            pltpu.VMEM((DN_HB, HEAD_DIM, HEAD_DIM), F32),
        ],
        compiler_params=_cparams(("parallel", "parallel", "arbitrary")),
        name="deltanet",
    )(proj3, proj3, proj3, proj3, proj3, conv_w, conv_w, conv_w, alog_row, dtb_row, ng_row)


SWA_CAST_ROWS = 256
SWA_UNROLL = 4


def _swa_kernel(q_ref, k_ref, v_ref, o_ref, *scratch, seq, dil, slopes):
    n_blocks = seq // dil // SWA_BLOCK
    qi = lax.broadcasted_iota(jnp.int32, (SWA_BLOCK, SWA_BLOCK), 0)
    kj = lax.broadcasted_iota(jnp.int32, (SWA_BLOCK, SWA_BLOCK), 1)
    step_cur = (qi - kj).astype(F32)
    step_prev = step_cur + float(SWA_BLOCK)
    ok_cur = qi >= kj
    ok_prev = kj >= qi
    lane = lax.broadcasted_iota(jnp.int32, (SWA_BLOCK, LANES), 1)
    scale = HEAD_DIM ** -0.5
    heads = range(SWA_HEADS_PER_GROUP)
    sls = [slice(hh * HEAD_DIM, (hh + 1) * HEAD_DIM) for hh in heads]
    m_h = [float(slopes[hh]) * float(dil) for hh in heads]

    if dil > 1:
        (f32_ref,) = scratch

        def cast_body(c, carry):
            rows = pl.ds(pl.multiple_of(c * SWA_CAST_ROWS, SWA_CAST_ROWS), SWA_CAST_ROWS)
            for t, ref in enumerate((q_ref, k_ref, v_ref)):
                for hh in heads:
                    f32_ref[t * SWA_HEADS_PER_GROUP + hh, rows, :] = ref[0, rows, sls[hh]].astype(F32)
            return carry
        lax.fori_loop(0, seq // SWA_CAST_ROWS, cast_body, 0)

        def load(t, rows):
            return [f32_ref[t * SWA_HEADS_PER_GROUP + hh, rows, :].astype(BF16) for hh in heads]
    else:
        refs = (q_ref, k_ref, v_ref)

        def load(t, rows):
            return [refs[t][0, rows, sls[hh]] for hh in heads]

    def block_rows(it):
        r = it // n_blocks
        n = it % n_blocks
        base = n * (SWA_BLOCK * dil) + r
        prev_base = jnp.maximum(n - 1, 0) * (SWA_BLOCK * dil) + r
        if dil > 1:
            return n, pl.ds(base, SWA_BLOCK, stride=dil), pl.ds(prev_base, SWA_BLOCK, stride=dil)
        return (n, pl.ds(pl.multiple_of(base, SWA_BLOCK), SWA_BLOCK),
                pl.ds(pl.multiple_of(prev_base, SWA_BLOCK), SWA_BLOCK))

    def blocks_body(it2, carry):
        qn, kc, kp, vc, vp, ok_p, slope, out_rows = [], [], [], [], [], [], [], []
        for u in range(SWA_UNROLL):
            n, rows, prev_rows = block_rows(it2 * SWA_UNROLL + u)
            qn += load(0, rows)
            kc += load(1, rows)
            vc += load(2, rows)
            kp += load(1, prev_rows)
            vp += load(2, prev_rows)
            ok_p += [jnp.logical_and(ok_prev, n > 0)] * len(heads)
            slope += m_h
            out_rows.append(rows)

        s_cur = [jnp.where(ok_cur, _dot_nt(qh, kh) * scale - m * step_cur, -jnp.inf)
                 for qh, kh, m in zip(qn, kc, slope)]
        s_prev = [jnp.where(ok, _dot_nt(qh, kh) * scale - m * step_prev, -jnp.inf)
                  for ok, qh, kh, m in zip(ok_p, qn, kp, slope)]
        mx = [jnp.max(jnp.maximum(a, b), axis=-1, keepdims=True) for a, b in zip(s_cur, s_prev)]
        e_cur = [jnp.exp(a - m) for a, m in zip(s_cur, mx)]
        e_prev = [jnp.exp(b - m) for b, m in zip(s_prev, mx)]
        den = [jnp.sum(a + b, axis=-1, keepdims=True) for a, b in zip(e_cur, e_prev)]
        pv = [jnp.dot(a.astype(BF16), vch, preferred_element_type=F32)
              + jnp.dot(b.astype(BF16), vph, preferred_element_type=F32)
              for a, b, vch, vph in zip(e_cur, e_prev, vc, vp)]
        for u, rows in enumerate(out_rows):
            lse_tile = jnp.zeros((SWA_BLOCK, LANES), F32)
            for hh in heads:
                i = u * len(heads) + hh
                o_ref[0, hh, rows, :] = pv[i] * (1.0 / den[i])
                lse_tile = jnp.where(lane == hh, mx[i] + jnp.log(den[i]), lse_tile)
            o_ref[0, SWA_HEADS_PER_GROUP, rows, :] = lse_tile
        return carry

    lax.fori_loop(0, seq // SWA_BLOCK // SWA_UNROLL, blocks_body, 0)


def _swa_group(proj3, gi):
    batch, seq, _ = proj3.shape
    window, dil = SWA_GROUPS[gi]
    assert window // dil == SWA_BLOCK and seq % (dil * SWA_BLOCK) == 0
    slopes = 2.0 ** (-ALIBI_MAX_BIAS * np.arange(1, SWA_HEADS + 1) / SWA_HEADS)
    slopes = slopes.reshape(len(SWA_GROUPS), SWA_HEADS_PER_GROUP)[gi]

    def spec(which):
        cb = (OFF_SWA + which * SWA_HEADS * HEAD_DIM) // SWA_GROUP_WIDTH + gi
        return pl.BlockSpec((1, seq, SWA_GROUP_WIDTH), lambda bi: (bi, 0, cb))

    scratch = [pltpu.VMEM((3 * SWA_HEADS_PER_GROUP, seq, HEAD_DIM), F32)] if dil > 1 else []
    return pl.pallas_call(
        functools.partial(_swa_kernel, seq=seq, dil=dil, slopes=tuple(float(s) for s in slopes)),
        grid=(batch,),
        in_specs=[spec(0), spec(1), spec(2)],
        out_specs=pl.BlockSpec((1, SWA_OUT_SLOTS, seq, HEAD_DIM), lambda bi: (bi, 0, 0, 0)),
        out_shape=jax.ShapeDtypeStruct((batch, SWA_OUT_SLOTS, seq, HEAD_DIM), F32),
        scratch_shapes=scratch,
        compiler_params=_cparams(("parallel",)),
        name=f"swa_g{gi}",
    )(proj3, proj3, proj3)


MG_TM = 256


def _merge_kernel(x_ref, ydn_ref, gate_ref, scc_ref, scb_ref, scx_ref, scc_h_ref, scx_h_ref,
                  s0_ref, s1_ref, s2_ref, scw_ref, wdn_ref, wsc_ref, wswa_ref, wout_ref, o_ref,
                  *, tiles_per_seq):
    i = pl.program_id(0)
    first = (i % tiles_per_seq) == 0

    cx_cur = scc_ref[...].astype(F32) * scx_ref[...].astype(F32)
    cx_halo = scc_h_ref[...].astype(F32) * scx_h_ref[...].astype(F32)
    cx_halo = jnp.where(first, 0.0, cx_halo)
    cx = jnp.concatenate([cx_halo, cx_cur], axis=0)
    y_sc = scb_ref[...].astype(F32) * _causal_conv(cx, scw_ref[...], SC_CONV)

    outs = (s0_ref, s1_ref, s2_ref)
    lses = [o[0, SWA_HEADS_PER_GROUP] for o in outs]
    mx = jnp.maximum(jnp.maximum(lses[0], lses[1]), lses[2])
    es = [jnp.exp(l - mx) for l in lses]
    inv = 1.0 / (es[0] + es[1] + es[2])
    heads = []
    for hh in range(SWA_HEADS_PER_GROUP):
        acc = None
        for gi in range(3):
            wt = (es[gi] * inv)[:, hh:hh + 1]
            term = wt * outs[gi][0, hh]
            acc = term if acc is None else acc + term
        heads.append(acc)
    y_swa = jnp.concatenate(heads, axis=1)

    gates = gate_ref[...].astype(F32)
    merged = (_sigmoid(gates[:, 0:D_MODEL])
              * jnp.dot(ydn_ref[...], wdn_ref[...], preferred_element_type=F32)
              + _sigmoid(gates[:, D_MODEL:2 * D_MODEL])
              * jnp.dot(y_sc.astype(BF16), wsc_ref[...], preferred_element_type=F32)
              + _sigmoid(gates[:, 2 * D_MODEL:3 * D_MODEL])
              * jnp.dot(y_swa.astype(BF16), wswa_ref[...], preferred_element_type=F32))
    o_ref[...] = x_ref[...] + jnp.dot(merged.astype(BF16), wout_ref[...], preferred_element_type=F32)


def _merge(x2d, ydn2, proj2, swa_outs, sc_conv_w, wdn, wsc, wswa, wout, seq):
    m = x2d.shape[0]
    tiles_per_seq = seq // MG_TM
    halo_blocks = MG_TM // HALO
    rows = lambda w, cb: pl.BlockSpec((MG_TM, w), lambda i: (i, cb))
    halo = lambda cb: pl.BlockSpec((HALO, SC_WIDTH), lambda i: (jnp.maximum(i * halo_blocks - 1, 0), cb))
    full = lambda a: pl.BlockSpec(a.shape, lambda i: (0, 0), pipeline_mode=pl.Buffered(1))
    swa = pl.BlockSpec((1, SWA_OUT_SLOTS, MG_TM, HEAD_DIM),
                       lambda i: (i // tiles_per_seq, 0, i % tiles_per_seq, 0))
    return pl.pallas_call(
        functools.partial(_merge_kernel, tiles_per_seq=tiles_per_seq),
        grid=(m // MG_TM,),
        in_specs=[
            rows(D_MODEL, 0),
            rows(DN_WIDTH, 0),
            rows(3 * D_MODEL, OFF_GATE // (3 * D_MODEL)),
            rows(SC_WIDTH, OFF_SCC // SC_WIDTH), rows(SC_WIDTH, OFF_SCB // SC_WIDTH),
            rows(SC_WIDTH, OFF_SCX // SC_WIDTH),
            halo(OFF_SCC // SC_WIDTH), halo(OFF_SCX // SC_WIDTH),
            swa, swa, swa,
            full(sc_conv_w), full(wdn), full(wsc), full(wswa), full(wout),
        ],
        out_specs=rows(D_MODEL, 0),
        out_shape=jax.ShapeDtypeStruct((m, D_MODEL), F32),
        compiler_params=_cparams(("parallel",)),
        name="merge",
    )(x2d, ydn2, proj2, proj2, proj2, proj2, proj2, proj2, *swa_outs, sc_conv_w, wdn, wsc, wswa, wout)


RT_TM = 512
ROUTE_LANE0 = MOE_GROUPS
R_E1, R_E2, R_W1, R_W2, R_RANK1, R_RANK2 = range(6)
HALF = D_MODEL // 2


def _pack_pair(lo, hi):
    lo_b = lax.bitcast_convert_type(lo.astype(BF16).astype(F32), jnp.uint32)
    hi_b = lax.bitcast_convert_type(hi.astype(BF16).astype(F32), jnp.uint32)
    return (hi_b & jnp.uint32(0xFFFF0000)) | (lo_b >> jnp.uint32(16))


def _unpack_pair(p):
    lo = lax.bitcast_convert_type(p << jnp.uint32(16), F32)
    hi = lax.bitcast_convert_type(p & jnp.uint32(0xFFFF0000), F32)
    return lo, hi


def _router_kernel(x_ref, g_ref, wrh_ref, wrl_ref, br_ref, hp_ref, route_ref, cnt_ref, carry_ref):
    @pl.when(pl.program_id(0) == 0)
    def _():
        carry_ref[...] = jnp.zeros_like(carry_ref)

    h = _rmsnorm_rows(x_ref[...], g_ref[...])
    hp_ref[...] = _pack_pair(h[:, :HALF], h[:, HALF:])
    h_hi = h.astype(BF16)
    h_lo = (h - h_hi.astype(F32)).astype(BF16)
    w_hi, w_lo = wrh_ref[...], wrl_ref[...]
    dot = functools.partial(jnp.dot, preferred_element_type=F32)
    logits = dot(h_hi, w_hi) + dot(h_lo, w_hi) + dot(h_hi, w_lo) + br_ref[...]
    lane = lax.broadcasted_iota(jnp.int32, logits.shape, 1)
    big = jnp.int32(LANES)
    neg = -jnp.inf

    is_grp = lane < MOE_GROUPS
    gl = jnp.where(is_grp, logits, neg)
    gmax = jnp.max(gl, axis=-1, keepdims=True)
    grp = jnp.min(jnp.where(gl == gmax, lane, big), axis=-1, keepdims=True)
    g_w = 1.0 / jnp.sum(jnp.exp(gl - gmax), axis=-1, keepdims=True)

    lo = ROUTE_LANE0 + grp * MOE_EPG
    in_grp = jnp.logical_and(lane >= lo, lane < lo + MOE_EPG)
    el = jnp.where(in_grp, logits, neg)
    m1 = jnp.max(el, axis=-1, keepdims=True)
    i1 = jnp.min(jnp.where(el == m1, lane, big), axis=-1, keepdims=True)
    el2 = jnp.where(lane == i1, neg, el)
    m2 = jnp.max(el2, axis=-1, keepdims=True)
    i2 = jnp.min(jnp.where(el2 == m2, lane, big), axis=-1, keepdims=True)
    e2 = jnp.exp(m2 - m1)
    w1 = g_w / (1.0 + e2)
    w2 = g_w * e2 / (1.0 + e2)

    tm = logits.shape[0]
    onehot = jnp.where(jnp.logical_or(lane == i1, lane == i2), 1.0, 0.0)
    earlier = (lax.broadcasted_iota(jnp.int32, (tm, tm), 0) > lax.broadcasted_iota(jnp.int32, (tm, tm), 1))
    before = carry_ref[...] + jnp.dot(jnp.where(earlier, 1.0, 0.0).astype(BF16), onehot.astype(BF16),
                                      preferred_element_type=F32)
    rank1 = jnp.sum(jnp.where(lane == i1, before, 0.0), axis=-1, keepdims=True)
    rank2 = jnp.sum(jnp.where(lane == i2, before, 0.0), axis=-1, keepdims=True)
    carry_ref[...] += jnp.sum(onehot, axis=0, keepdims=True)
    cnt_ref[...] = carry_ref[...]

    rec = jnp.zeros_like(logits)
    for ln, val in ((R_E1, (i1 - ROUTE_LANE0).astype(F32)), (R_E2, (i2 - ROUTE_LANE0).astype(F32)),
                    (R_W1, w1), (R_W2, w2), (R_RANK1, rank1), (R_RANK2, rank2)):
        rec = jnp.where(lane == ln, val, rec)
    route_ref[...] = rec


def _router(x2d, g_row, w_route_hi, w_route_lo, b_route):
    m = x2d.shape[0]
    return pl.pallas_call(
        _router_kernel,
        grid=(m // RT_TM,),
        in_specs=[
            pl.BlockSpec((RT_TM, D_MODEL), lambda i: (i, 0)),
            pl.BlockSpec((1, D_MODEL), lambda i: (0, 0)),
            pl.BlockSpec((D_MODEL, LANES), lambda i: (0, 0)),
            pl.BlockSpec((D_MODEL, LANES), lambda i: (0, 0)),
            pl.BlockSpec((1, LANES), lambda i: (0, 0)),
        ],
        out_specs=[pl.BlockSpec((RT_TM, HALF), lambda i: (i, 0)),
                   pl.BlockSpec((RT_TM, LANES), lambda i: (i, 0)),
                   pl.BlockSpec((1, LANES), lambda i: (0, 0))],
        out_shape=[jax.ShapeDtypeStruct((m, HALF), jnp.uint32), jax.ShapeDtypeStruct((m, LANES), F32),
                   jax.ShapeDtypeStruct((1, LANES), F32)],
        scratch_shapes=[pltpu.VMEM((1, LANES), F32)],
        compiler_params=_cparams(("arbitrary",)),
        name="router",
    )(x2d, g_row, w_route_hi, w_route_lo, b_route)


EXP_TM = 512


def _n_row_tiles(n_tokens):
    return 2 * n_tokens // EXP_TM + MOE_EXPERTS


def _positions_kernel(route_ref, offs_ref, pos_ref):
    rec = route_ref[...]
    lane = lax.broadcasted_iota(jnp.int32, rec.shape, 1)
    offs = offs_ref[...]

    def first_row(e):
        return jnp.sum(jnp.where(lane == e.astype(jnp.int32), offs, 0.0), axis=-1, keepdims=True)

    pos1 = first_row(rec[:, R_E1:R_E1 + 1]) + rec[:, R_RANK1:R_RANK1 + 1]
    pos2 = first_row(rec[:, R_E2:R_E2 + 1]) + rec[:, R_RANK2:R_RANK2 + 1]
    pos_ref[...] = jnp.where(lane == 0, pos1, jnp.where(lane == 1, pos2, 0.0)).astype(jnp.int32)


def _moe_plan(route, cnt):
    m = route.shape[0]
    n_tiles = _n_row_tiles(m)
    counts = cnt[0, ROUTE_LANE0:ROUTE_LANE0 + MOE_EXPERTS].astype(jnp.int32)
    padded = (counts + EXP_TM - 1) // EXP_TM * EXP_TM
    ends = jnp.cumsum(padded)
    offs = ends - padded
    pos = pl.pallas_call(
        _positions_kernel,
        grid=(m // RT_TM,),
        in_specs=[pl.BlockSpec((RT_TM, LANES), lambda i: (i, 0)), pl.BlockSpec((1, LANES), lambda i: (0, 0))],
        out_specs=pl.BlockSpec((RT_TM, LANES), lambda i: (i, 0)),
        out_shape=jax.ShapeDtypeStruct((m, LANES), jnp.int32),
        compiler_params=_cparams(("parallel",)),
        name="moe_positions",
    )(route, _lane_row(offs))
    n_active = ends[-1] // EXP_TM
    tile_row = jnp.maximum(jnp.minimum(jnp.arange(n_tiles), n_active - 1), 0) * EXP_TM
    tile_expert = jnp.minimum(jnp.sum(ends[None, :] <= tile_row[:, None], axis=1), MOE_EXPERTS - 1)
    used = counts > 0
    ids = jnp.where(used, jnp.arange(MOE_EXPERTS), MOE_EXPERTS)
    next_used = jnp.concatenate([lax.cummin(ids, reverse=True)[1:], jnp.full((1,), MOE_EXPERTS, ids.dtype)])
    parity = (jnp.cumsum(used.astype(jnp.int32)) - 1) % 2
    i32 = lambda a: a.astype(jnp.int32)
    return dict(pos1=pos[:, 0], pos2=pos[:, 1], counts=counts, offs=i32(offs),
                tile_expert=i32(tile_expert), n_active=i32(n_active).reshape(1),
                next_expert=i32(next_used[tile_expert]), parity=i32(parity[tile_expert]))


DSP_TB = 512
ROW_UNROLL = 8


def _dispatch_kernel(cnt_ref, offs_ref, nact_ref, pos1_ref, pos2_ref, hp_ref, xs_ref, zero_ref, sem, zero_sem):
    n_tiles = xs_ref.shape[0] // EXP_TM

    @pl.when(pl.program_id(0) == 0)
    def _():
        zero_ref[...] = jnp.zeros_like(zero_ref)

        def pad_pieces(e):
            pad = (-cnt_ref[e]) & (EXP_TM - 1)
            first = offs_ref[e] + cnt_ref[e]
            head = jnp.minimum((-first) & (SUBLANES - 1), pad)
            for k in range(SUBLANES - 1):
                yield k < head, pltpu.make_async_copy(zero_ref.at[pl.ds(0, 1), :],
                                                      xs_ref.at[pl.ds(first + k, 1), :], zero_sem)
            body = pad - head
            b = EXP_TM // 2
            while b >= SUBLANES:
                row = pl.multiple_of(first + head + (body & ~(2 * b - 1)), SUBLANES)
                yield (body & b) != 0, pltpu.make_async_copy(zero_ref.at[pl.ds(0, b), :],
                                                             xs_ref.at[pl.ds(row, b), :], zero_sem)
                b //= 2

        def tile_copy(t):
            rows = pl.ds(pl.multiple_of(t * EXP_TM, EXP_TM), EXP_TM)
            return pltpu.make_async_copy(zero_ref, xs_ref.at[rows, :], zero_sem)

        def for_all(method):
            def per_expert(e, c):
                for cond, cp in pad_pieces(e):
                    pl.when(cond)(getattr(cp, method))
                return c
            lax.fori_loop(0, MOE_EXPERTS, per_expert, 0)

            def per_tile(t, c):
                getattr(tile_copy(t), method)()
                return c
            lax.fori_loop(nact_ref[0], n_tiles, per_tile, 0)

        for_all("start")
        for_all("wait")

    def start(j8, c):
        for u in range(ROW_UNROLL):
            j = j8 * ROW_UNROLL + u
            src = hp_ref.at[pl.ds(j, 1), :]
            pltpu.make_async_copy(src, xs_ref.at[pl.ds(pos1_ref[0, 0, j], 1), :], sem).start(priority=0)
            pltpu.make_async_copy(src, xs_ref.at[pl.ds(pos2_ref[0, 0, j], 1), :], sem).start(priority=1)
        return c

    lax.fori_loop(0, DSP_TB // ROW_UNROLL, start, 0)
    for _ in range(2):
        pltpu.make_async_copy(hp_ref, xs_ref.at[pl.ds(0, DSP_TB), :], sem).wait()


def _dispatch(hp, pos1, pos2, counts, offs, n_active):
    m = hp.shape[0]
    n_rows = _n_row_tiles(m) * EXP_TM
    pos_spec = pl.BlockSpec((1, 1, DSP_TB), lambda i, *_: (i, 0, 0), memory_space=pltpu.SMEM)
    return pl.pallas_call(
        _dispatch_kernel,
        grid_spec=pltpu.PrefetchScalarGridSpec(
            num_scalar_prefetch=3,
            grid=(m // DSP_TB,),
            in_specs=[pos_spec, pos_spec, pl.BlockSpec((DSP_TB, HALF), lambda i, *_: (i, 0))],
            out_specs=pl.BlockSpec(memory_space=pl.ANY),
            scratch_shapes=[pltpu.VMEM((EXP_TM, HALF), jnp.uint32), pltpu.SemaphoreType.DMA(()),
                            pltpu.SemaphoreType.DMA(())],
        ),
        out_shape=jax.ShapeDtypeStruct((n_rows, HALF), jnp.uint32),
        compiler_params=_cparams(("arbitrary",)),
        name="moe_dispatch",
    )(counts, offs, n_active, pos1.reshape(m // DSP_TB, 1, DSP_TB), pos2.reshape(m // DSP_TB, 1, DSP_TB), hp)


W_CAST_ROWS = 256


def _cast_rows(src_ref, dst_ref):
    n_rows = dst_ref.shape[0]
    slab = min(W_CAST_ROWS, n_rows)

    def body(r, c):
        rows = pl.ds(pl.multiple_of(r * slab, slab), slab)
        dst_ref[rows, :] = src_ref[rows, :].astype(BF16)
        return c
    lax.fori_loop(0, n_rows // slab, body, 0)


def _expert_kernel(te_ref, nact_ref, next_ref, par_ref, xs_ref, wg_hbm, wu_hbm, wd_hbm, ys_ref,
                   sg_ref, su_ref, sd_ref, wg16_ref, wu16_ref, wd16_ref, sem, *, layer):
    i = pl.program_id(0)
    active = i < nact_ref[0]
    expert = te_ref[i]
    slot = par_ref[i]
    new_expert = jnp.logical_or(i == 0, expert != te_ref[jnp.maximum(i - 1, 0)])

    def copies(e, s):
        return [pltpu.make_async_copy(hbm.at[layer, e], stage.at[s], sem.at[s])
                for hbm, stage in ((wg_hbm, sg_ref), (wu_hbm, su_ref), (wd_hbm, sd_ref))]

    @pl.when(jnp.logical_and(active, i == 0))
    def _():
        for cp in copies(expert, slot):
            cp.start()

    @pl.when(jnp.logical_and(active, new_expert))
    def _():
        for cp in copies(expert, slot):
            cp.wait()

        @pl.when(next_ref[i] < MOE_EXPERTS)
        def _():
            for cp in copies(next_ref[i], 1 - slot):
                cp.start()

        _cast_rows(sg_ref.at[slot], wg16_ref)
        _cast_rows(su_ref.at[slot], wu16_ref)
        _cast_rows(sd_ref.at[slot], wd16_ref)

    @pl.when(active)
    def _():
        lo, hi = _unpack_pair(xs_ref[...])
        lo, hi = lo.astype(BF16), hi.astype(BF16)
        dot = functools.partial(jnp.dot, preferred_element_type=F32)
        gate = dot(lo, wg16_ref[:HALF, :]) + dot(hi, wg16_ref[HALF:, :])
        up = dot(lo, wu16_ref[:HALF, :]) + dot(hi, wu16_ref[HALF:, :])
        y = dot((_silu(gate) * up).astype(BF16), wd16_ref[...])
        ys_ref[...] = _pack_pair(y[:, :HALF], y[:, HALF:])

    @pl.when(jnp.logical_not(active))
    def _():
        ys_ref[...] = jnp.zeros_like(ys_ref)


def _experts(xs, tile_expert, n_active, next_expert, parity, wg, wu, wd, layer):
    n_tiles = xs.shape[0] // EXP_TM
    rows = pl.BlockSpec((EXP_TM, HALF),
                        lambda i, te, na, nx, par: (jnp.maximum(jnp.minimum(i, na[0] - 1), 0), 0))
    out_rows = pl.BlockSpec((EXP_TM, HALF), lambda i, te, na, nx, par: (i, 0))
    hbm = pl.BlockSpec(memory_space=pl.ANY)
    return pl.pallas_call(
        functools.partial(_expert_kernel, layer=layer),
        grid_spec=pltpu.PrefetchScalarGridSpec(
            num_scalar_prefetch=4,
            grid=(n_tiles,),
            in_specs=[rows, hbm, hbm, hbm],
            out_specs=out_rows,
            scratch_shapes=[pltpu.VMEM((2, D_MODEL, MOE_FF), F32), pltpu.VMEM((2, D_MODEL, MOE_FF), F32),
                            pltpu.VMEM((2, MOE_FF, D_MODEL), F32),
                            pltpu.VMEM((D_MODEL, MOE_FF), BF16), pltpu.VMEM((D_MODEL, MOE_FF), BF16),
                            pltpu.VMEM((MOE_FF, D_MODEL), BF16),
                            pltpu.SemaphoreType.DMA((2,))],
        ),
        out_shape=jax.ShapeDtypeStruct(xs.shape, jnp.uint32),
        compiler_params=_cparams(("arbitrary",)),
        name="moe_experts",
    )(tile_expert, n_active, next_expert, parity, xs, wg, wu, wd)


CMB_TC = 512


def _combine_kernel(pos1_ref, pos2_ref, pos1_next_ref, pos2_next_ref, x_ref, route_ref, g_ref, ys_ref,
                    o_ref, buf_ref, sem, *, final_norm):
    i = pl.program_id(0)
    slot = i % 2

    def gather(p1_ref, p2_ref, s):
        def start(j8, c):
            for u in range(ROW_UNROLL):
                j = j8 * ROW_UNROLL + u
                for k, p_ref in enumerate((p1_ref, p2_ref)):
                    pltpu.make_async_copy(ys_ref.at[pl.ds(p_ref[0, 0, j], 1), :],
                                          buf_ref.at[s, k, pl.ds(j, 1), :], sem.at[s]).start(priority=k)
            return c
        lax.fori_loop(0, CMB_TC // ROW_UNROLL, start, 0)

    @pl.when(i == 0)
    def _():
        gather(pos1_ref, pos2_ref, 0)

    @pl.when(i + 1 < pl.num_programs(0))
    def _():
        gather(pos1_next_ref, pos2_next_ref, 1 - slot)

    for k in range(2):
        pltpu.make_async_copy(ys_ref.at[pl.ds(0, CMB_TC), :], buf_ref.at[slot, k], sem.at[slot]).wait()

    rec = route_ref[...]
    w1, w2 = rec[:, R_W1:R_W1 + 1], rec[:, R_W2:R_W2 + 1]
    lo1, hi1 = _unpack_pair(buf_ref[slot, 0])
    lo2, hi2 = _unpack_pair(buf_ref[slot, 1])
    out_lo = x_ref[:, :HALF] + w1 * lo1 + w2 * lo2
    out_hi = x_ref[:, HALF:] + w1 * hi1 + w2 * hi2
    if final_norm:
        ms = (jnp.sum(out_lo * out_lo, axis=-1, keepdims=True)
              + jnp.sum(out_hi * out_hi, axis=-1, keepdims=True)) * (1.0 / D_MODEL)
        inv = lax.rsqrt(ms + RMS_EPS)
        out_lo = out_lo * inv * g_ref[:, :HALF]
        out_hi = out_hi * inv * g_ref[:, HALF:]
    o_ref[:, :HALF] = out_lo
    o_ref[:, HALF:] = out_hi


def _combine(x2d, route, ys, pos1, pos2, g_row, final_norm):
    m = x2d.shape[0]
    n_steps = m // CMB_TC
    pos_spec = pl.BlockSpec((1, 1, CMB_TC), lambda i: (i, 0, 0), memory_space=pltpu.SMEM)
    next_spec = pl.BlockSpec((1, 1, CMB_TC), lambda i: (jnp.minimum(i + 1, n_steps - 1), 0, 0),
                             memory_space=pltpu.SMEM)
    pos1, pos2 = pos1.reshape(n_steps, 1, CMB_TC), pos2.reshape(n_steps, 1, CMB_TC)
    return pl.pallas_call(
        functools.partial(_combine_kernel, final_norm=final_norm),
        grid=(n_steps,),
        in_specs=[pos_spec, pos_spec, next_spec, next_spec,
                  pl.BlockSpec((CMB_TC, D_MODEL), lambda i: (i, 0)),
                  pl.BlockSpec((CMB_TC, LANES), lambda i: (i, 0)),
                  pl.BlockSpec((1, D_MODEL), lambda i: (0, 0)),
                  pl.BlockSpec(memory_space=pl.ANY)],
        out_specs=pl.BlockSpec((CMB_TC, D_MODEL), lambda i: (i, 0)),
        out_shape=jax.ShapeDtypeStruct((m, D_MODEL), F32),
        scratch_shapes=[pltpu.VMEM((2, 2, CMB_TC, HALF), jnp.uint32), pltpu.SemaphoreType.DMA((2,))],
        compiler_params=_cparams(("arbitrary",)),
        name="moe_combine",
    )(pos1, pos2, pos1, pos2, x2d, route, g_row, ys)


RL_TN = 512
RL_PIECE = 256
SRC_A_END = 4096
SRC_AB = 16
SRC_COLS = 17936


def _relayout_plan():
    n_rest = (SRC_COLS - SRC_A_END - SRC_AB) // RL_TN
    rest = lambda s: SRC_A_END + SRC_AB + RL_TN * s
    starts = ([rest(s) for s in range(15, n_rest)]
              + [RL_TN * a for a in range(SRC_A_END // RL_TN)]
              + [rest(s) for s in range(15)]
              + [SRC_A_END])
    assert len(starts) * RL_TN == PROJ_COLS and all(s % 16 == 0 and s + RL_TN <= SRC_COLS for s in starts)
    return np.asarray(starts, np.int32)


def _relayout_kernel(start_ref, wt_ref, o_ref, buf_ref, sem, *, layer):
    t = pl.program_id(0)
    n_tiles = pl.num_programs(0)
    slot = t % 2

    def fetch(tile, s):
        first = pl.multiple_of(start_ref[tile], 16)
        pltpu.make_async_copy(wt_ref.at[layer, pl.ds(first, RL_TN), :], buf_ref.at[s], sem.at[s]).start()

    @pl.when(t == 0)
    def _():
        fetch(0, 0)

    @pl.when(t + 1 < n_tiles)
    def _():
        fetch(t + 1, 1 - slot)

    pltpu.make_async_copy(wt_ref.at[layer, pl.ds(0, RL_TN), :], buf_ref.at[slot], sem.at[slot]).wait()

    row = lax.broadcasted_iota(jnp.int32, (RL_TN, RL_PIECE), 0)
    keep = jnp.logical_or(t + 1 < n_tiles, row < SRC_AB)
    for c in range(D_MODEL // RL_PIECE):
        cols = slice(c * RL_PIECE, (c + 1) * RL_PIECE)
        piece = jnp.where(keep, buf_ref[slot, :, cols], 0.0)
        o_ref[cols, :] = piece.T.astype(BF16)


def _prep_w_in(w_all, layer):
    wt = jnp.swapaxes(w_all, 1, 2)
    return pl.pallas_call(
        functools.partial(_relayout_kernel, layer=layer),
        grid_spec=pltpu.PrefetchScalarGridSpec(
            num_scalar_prefetch=1,
            grid=(PROJ_COLS // RL_TN,),
            in_specs=[pl.BlockSpec(memory_space=pl.ANY)],
            out_specs=pl.BlockSpec((D_MODEL, RL_TN), lambda t, starts: (0, t)),
            scratch_shapes=[pltpu.VMEM((2, RL_TN, D_MODEL), F32), pltpu.SemaphoreType.DMA((2,))],
        ),
        out_shape=jax.ShapeDtypeStruct((D_MODEL, PROJ_COLS), BF16),
        compiler_params=_cparams(("arbitrary",)),
        name="w_in_relayout",
    )(jnp.asarray(_relayout_plan()), wt)


def _lane_row(v, offset=0):
    return jnp.zeros((1, LANES), F32).at[0, offset:offset + v.shape[0]].set(v.astype(F32))


def _layer(x2d, batch, seq, p, final_g_row):
    proj2 = _in_proj(x2d, p["norm_mix_g"], p["w_in"])
    proj3 = proj2.reshape(batch, seq, PROJ_COLS)
    ydn = _deltanet(proj3, p["dn_conv_w"], p["alog_row"], p["dtb_row"], p["dn_norm_g"])
    swa_outs = [_swa_group(proj3, gi) for gi in range(len(SWA_GROUPS))]
    x2d = _merge(x2d, ydn.reshape(batch * seq, DN_WIDTH), proj2, swa_outs, p["sc_conv_w"],
                 p["w_branch_dn"], p["w_branch_sc"], p["w_branch_swa"], p["w_out"], seq)
    hp, route, cnt = _router(x2d, p["norm_ffn_g"], p["w_route_hi"], p["w_route_lo"], p["b_route"])
    plan = _moe_plan(route, cnt)
    xs = _dispatch(hp, plan["pos1"], plan["pos2"], plan["counts"], plan["offs"], plan["n_active"])
    ys = _experts(xs, plan["tile_expert"], plan["n_active"], plan["next_expert"], plan["parity"],
                  p["expert_w_gate"], p["expert_w_up"], p["expert_w_down"], p["layer"])
    is_last = final_g_row is not None
    g_row = final_g_row if is_last else p["norm_ffn_g"]
    return _combine(x2d, route, ys, plan["pos1"], plan["pos2"], g_row, is_last)


def kernel(x, norm_mix_g, w_in, dn_conv_w, dn_a_log, dn_dt_bias, dn_norm_g, sc_conv_w, w_branch_dn, w_branch_sc, w_branch_swa, w_out, norm_ffn_g, router_group_w, router_group_b, router_expert_w, router_expert_b, expert_w_gate, expert_w_up, expert_w_down, final_norm_g):
    batch, seq, _ = x.shape
    depth = w_in.shape[0]
    x2d = x.reshape(batch * seq, D_MODEL)
    for l in range(depth):
        w_route = jnp.concatenate([router_group_w[l], router_expert_w[l]], axis=1)
        w_route = jnp.pad(w_route, ((0, 0), (0, LANES - w_route.shape[1])))
        p = dict(
            norm_mix_g=norm_mix_g[l].reshape(1, D_MODEL),
            w_in=_prep_w_in(w_in, l),
            dn_conv_w=dn_conv_w[l],
            alog_row=_lane_row(dn_a_log[l]),
            dtb_row=_lane_row(dn_dt_bias[l]),
            dn_norm_g=dn_norm_g[l].reshape(1, HEAD_DIM),
            sc_conv_w=sc_conv_w[l],
            w_branch_dn=w_branch_dn[l].astype(BF16),
            w_branch_sc=w_branch_sc[l].astype(BF16),
            w_branch_swa=w_branch_swa[l].astype(BF16),
            w_out=w_out[l].astype(BF16),
            norm_ffn_g=norm_ffn_g[l].reshape(1, D_MODEL),
            w_route_hi=w_route.astype(BF16),
            w_route_lo=(w_route - w_route.astype(BF16).astype(F32)).astype(BF16),
            b_route=_lane_row(jnp.concatenate([router_group_b[l], router_expert_b[l]])),
            expert_w_gate=expert_w_gate,
            expert_w_up=expert_w_up,
            expert_w_down=expert_w_down,
            layer=l,
        )
        final_g_row = final_norm_g.reshape(1, D_MODEL) if l == depth - 1 else None
        x2d = _layer(x2d, batch, seq, p, final_g_row)
    return x2d.reshape(batch, seq, D_MODEL)
```

```python
import functools

import jax
import jax.numpy as jnp
import numpy as np
from jax import lax
from jax.experimental import pallas as pl
from jax.experimental.pallas import tpu as pltpu

F32 = jnp.float32
BF16 = jnp.bfloat16

D_MODEL = 2048
RMS_EPS = 1e-6
L2_EPS = 1e-6

DN_HEADS = 8
HEAD_DIM = 128
DN_WIDTH = DN_HEADS * HEAD_DIM
DN_CONV = 4
SC_WIDTH = 1024
SC_CONV = 3
SWA_GROUPS = ((128, 1), (512, 4), (2048, 16))
SWA_HEADS_PER_GROUP = 4
SWA_HEADS = 12
SWA_GROUP_WIDTH = SWA_HEADS_PER_GROUP * HEAD_DIM
SWA_BLOCK = 128
ALIBI_MAX_BIAS = 8.0
MOE_GROUPS = 4
MOE_EPG = 8
MOE_EXPERTS = 32
MOE_FF = 512

OFF_GATE = 0
OFF_DNQ = 6144
OFF_DNK = OFF_DNQ + DN_WIDTH
OFF_DNV = OFF_DNK + DN_WIDTH
OFF_DNZ = OFF_DNV + DN_WIDTH
OFF_SCC = OFF_DNZ + DN_WIDTH
OFF_SCB = OFF_SCC + SC_WIDTH
OFF_SCX = OFF_SCB + SC_WIDTH
OFF_SWA = OFF_SCX + SC_WIDTH
OFF_AB = OFF_SWA + 3 * SWA_HEADS * HEAD_DIM
PROJ_COLS = 18432

LANES = 128
SUBLANES = 8
SWA_OUT_SLOTS = SWA_HEADS_PER_GROUP + 1
CHUNK = 128
HALO = 16

VMEM_LIMIT = 56 * 1024 * 1024


def _cparams(sem):
    return pltpu.CompilerParams(dimension_semantics=sem, vmem_limit_bytes=VMEM_LIMIT)


def _sigmoid(x):
    return 1.0 / (1.0 + jnp.exp(-x))


def _silu(x):
    return x * _sigmoid(x)


def _softplus(x):
    return jnp.maximum(x, 0.0) + jnp.log(1.0 + jnp.exp(-jnp.abs(x)))


IN_TM = 1024
IN_TN = 2048
NORM_ROWS = 64


def _rmsnorm_rows(x, g):
    ms = jnp.mean(x * x, axis=-1, keepdims=True)
    return x * lax.rsqrt(ms + RMS_EPS) * g


def _in_proj_kernel(x_ref, g_ref, w_ref, o_ref, h_ref):
    @pl.when(pl.program_id(1) == 0)
    def _():
        def body(r, c):
            rows = pl.ds(pl.multiple_of(r * NORM_ROWS, NORM_ROWS), NORM_ROWS)
            h_ref[rows, :] = _rmsnorm_rows(x_ref[rows, :], g_ref[...]).astype(BF16)
            return c
        lax.fori_loop(0, IN_TM // NORM_ROWS, body, 0)

    o_ref[...] = jnp.dot(h_ref[...], w_ref[...], preferred_element_type=F32).astype(o_ref.dtype)


def _in_proj(x2d, g_row, w_bf16):
    m = x2d.shape[0]
    return pl.pallas_call(
        _in_proj_kernel,
        grid=(m // IN_TM, PROJ_COLS // IN_TN),
        in_specs=[
            pl.BlockSpec((IN_TM, D_MODEL), lambda i, j: (i, 0)),
            pl.BlockSpec((1, D_MODEL), lambda i, j: (0, 0)),
            pl.BlockSpec((D_MODEL, IN_TN), lambda i, j: (0, j)),
        ],
        out_specs=pl.BlockSpec((IN_TM, IN_TN), lambda i, j: (i, j)),
        out_shape=jax.ShapeDtypeStruct((m, PROJ_COLS), BF16),
        scratch_shapes=[pltpu.VMEM((IN_TM, D_MODEL), BF16)],
        compiler_params=_cparams(("parallel", "arbitrary")),
        name="in_proj",
    )(x2d, g_row, w_bf16)


DN_HB = 8
DN_BB = 2
DN_TS = 512
DN_W = DN_HB * HEAD_DIM
DN_DOUBLINGS = CHUNK.bit_length() - 2
GATE_ROWS = 2 * DN_HEADS


def _dot_nt(a, b):
    return lax.dot_general(a, b, (((1,), (1,)), ((), ())), preferred_element_type=F32)


def _dot_tn(a, b):
    return lax.dot_general(a, b, (((0,), (0,)), ((), ())), preferred_element_type=F32)


def _causal_conv(xx, w, width):
    if width == 4:
        x1 = pltpu.roll(xx, 1, axis=0)
        near = xx * w[3:4, :] + x1 * w[2:3, :]
        far = xx * w[1:2, :] + x1 * w[0:1, :]
        return (near + pltpu.roll(far, 2, axis=0))[HALO:, :]
    acc = None
    for j in range(width):
        shift = width - 1 - j
        xs = xx if shift == 0 else pltpu.roll(xx, shift, axis=0)
        term = xs[HALO:, :] * w[j:j + 1, :]
        acc = term if acc is None else acc + term
    return acc


def _deltanet_kernel(q_ref, k_ref, v_ref, z_ref, ab_ref, cwq_ref, cwk_ref, cwv_ref,
                     alog_ref, dtb_ref, ng_ref, o_ref,
                     xpad_ref, state_ref):
    st = pl.program_id(2)

    @pl.when(st == 0)
    def _():
        xpad_ref[:, 0:HALO, :] = jnp.zeros((3 * DN_BB, HALO, DN_W), BF16)
        state_ref[...] = jnp.zeros_like(state_ref)

    for bb in range(DN_BB):
        for t, ref in enumerate((q_ref, k_ref, v_ref)):
            xpad_ref[3 * bb + t, HALO:, :] = ref[bb]

    row_i = lax.broadcasted_iota(jnp.int32, (CHUNK, CHUNK), 0)
    col_j = lax.broadcasted_iota(jnp.int32, (CHUNK, CHUNK), 1)
    causal = row_i >= col_j
    strict = row_i > col_j
    upper = (row_i <= col_j).astype(F32)
    eye = (row_i == col_j).astype(F32)
    lane = lax.broadcasted_iota(jnp.int32, (CHUNK, LANES), 1)
    gate_row = lax.broadcasted_iota(jnp.int32, (GATE_ROWS, CHUNK), 0)
    neg_decay_rate = -jnp.exp(alog_ref[...])
    cws = (cwq_ref[...], cwk_ref[...], cwv_ref[...])
    ng = ng_ref[...]

    def chunk_body(c, carry):
        r0 = pl.multiple_of(c * CHUNK, CHUNK)
        win = pl.ds(r0, CHUNK + HALO)
        rows = pl.ds(r0, CHUNK)
        conv, gates, z = [], [], []
        for bb in range(DN_BB):
            conv.append([_silu(_causal_conv(xpad_ref[3 * bb + t, win, :].astype(F32), cws[t], DN_CONV))
                         for t in range(3)])
            ab_t = ab_ref[bb, rows, :].astype(F32).T[:GATE_ROWS, :]
            g_cum_t = jnp.dot(neg_decay_rate * _softplus(ab_t + dtb_ref[...]), upper,
                              preferred_element_type=F32, precision=lax.Precision.HIGHEST)
            gates_t = jnp.where(gate_row < DN_HEADS, g_cum_t, _sigmoid(ab_t))
            gates.append(jnp.concatenate([gates_t, jnp.zeros((CHUNK - GATE_ROWS, CHUNK), F32)], axis=0).T)
            z.append(z_ref[bb, rows, :].astype(F32))

        units = [(bb, i) for bb in range(DN_BB) for i in range(DN_HB)]
        sls = [slice(i * HEAD_DIM, (i + 1) * HEAD_DIM) for i in range(DN_HB)]
        dot = functools.partial(jnp.dot, preferred_element_type=F32)

        g_col = [jnp.sum(jnp.where(lane == i, gates[bb], 0.0), axis=-1, keepdims=True) for bb, i in units]
        beta = [jnp.sum(jnp.where(lane == i + DN_HEADS, gates[bb], 0.0), axis=-1, keepdims=True)
                for bb, i in units]
        g_last = [g[CHUNK - 1:CHUNK, :] for g in g_col]
        eg = [jnp.exp(g) for g in g_col]
        ek = [jnp.exp(gl - g) for gl, g in zip(g_last, g_col)]
        g_b = [jnp.broadcast_to(g, (CHUNK, CHUNK)) for g in g_col]
        decay = [jnp.exp(jnp.where(causal, gb - gb.T, -jnp.inf)) for gb in g_b]

        qf = [conv[bb][0][:, sls[i]] for bb, i in units]
        kf = [conv[bb][1][:, sls[i]] for bb, i in units]
        vf = [conv[bb][2][:, sls[i]] for bb, i in units]
        q = [x * lax.rsqrt(jnp.sum(x * x, axis=-1, keepdims=True) + L2_EPS) * (HEAD_DIM ** -0.5) for x in qf]
        k = [x * lax.rsqrt(jnp.sum(x * x, axis=-1, keepdims=True) + L2_EPS) for x in kf]
        kb = [ki * bi for ki, bi in zip(k, beta)]
        a2 = [_dot_nt(jnp.concatenate([qi, kbi], axis=0).astype(BF16), ki.astype(BF16))
              for qi, kbi, ki in zip(q, kb, k)]
        attn = [a[:CHUNK] * d for a, d in zip(a2, decay)]
        n_mat = [jnp.where(strict, -(a[CHUNK:] * d), 0.0) for a, d in zip(a2, decay)]

        u_mat = [eye + n for n in n_mat]
        p_mat = [dot(n.astype(BF16), n.astype(BF16)) for n in n_mat]
        for _ in range(DN_DOUBLINGS - 1):
            up = [dot(jnp.concatenate([u.astype(BF16), p.astype(BF16)], axis=0), p.astype(BF16))
                  for u, p in zip(u_mat, p_mat)]
            u_mat = [u + x[:CHUNK] for u, x in zip(u_mat, up)]
            p_mat = [x[CHUNK:] for x in up]
        u_mat = [u + dot(u.astype(BF16), p.astype(BF16)) for u, p in zip(u_mat, p_mat)]
        uw = [dot(u.astype(BF16), jnp.concatenate([vi * bi, kbi * egi], axis=1).astype(BF16))
              for u, vi, bi, kbi, egi in zip(u_mat, vf, beta, kb, eg)]

        s_old = [state_ref[u] for u in range(len(units))]
        qw = [dot(jnp.concatenate([qi * egi, x[:, HEAD_DIM:]], axis=0).astype(BF16), s.astype(BF16))
              for qi, egi, x, s in zip(q, eg, uw, s_old)]
        v16 = [(x[:, :HEAD_DIM] - y[CHUNK:]).astype(BF16) for x, y in zip(uw, qw)]
        o = [y[:CHUNK] + dot(a.astype(BF16), v) for y, a, v in zip(qw, attn, v16)]
        s_new = [s * jnp.exp(gl) + _dot_tn((ki * eki).astype(BF16), v)
                 for s, gl, ki, eki, v in zip(s_old, g_last, k, ek, v16)]
        for u, (bb, i) in enumerate(units):
            state_ref[u] = s_new[u]
            on = o[u] * lax.rsqrt(jnp.mean(o[u] * o[u], axis=-1, keepdims=True) + RMS_EPS) * ng
            o_ref[bb, rows, sls[i]] = (on * _silu(z[bb][:, sls[i]])).astype(o_ref.dtype)
        return carry

    lax.fori_loop(0, DN_TS // CHUNK, chunk_body, 0)

    tail = pl.ds(DN_TS, HALO)
    head = pl.ds(0, HALO)
    for t in range(3 * DN_BB):
        xpad_ref[t, head, :] = xpad_ref[t, tail, :]


def _gate_rows(v):
    rows = jnp.zeros((GATE_ROWS, LANES), F32)
    return rows.at[:v.shape[0], :].set(jnp.broadcast_to(v.astype(F32)[:, None], (v.shape[0], LANES)))


def _deltanet(proj3, conv_w, a_log, dt_bias, ng_row):
    assert DN_HB == DN_HEADS
    b, s, _ = proj3.shape
    assert b % DN_BB == 0
    gate = pl.BlockSpec((GATE_ROWS, LANES), lambda bi, hg, st: (0, 0))
    qb, kb_, vb, zb = (off // DN_W for off in (OFF_DNQ, OFF_DNK, OFF_DNV, OFF_DNZ))
    act = lambda base: pl.BlockSpec((DN_BB, DN_TS, DN_W), lambda bi, hg, st: (bi, st, base + hg))
    cw = lambda base: pl.BlockSpec((DN_CONV, DN_W), lambda bi, hg, st: (0, base + hg))
    row = pl.BlockSpec((1, LANES), lambda bi, hg, st: (0, 0))
    return pl.pallas_call(
        _deltanet_kernel,
        grid=(b // DN_BB, DN_HEADS // DN_HB, s // DN_TS),
        in_specs=[
            act(qb), act(kb_), act(vb), act(zb),
            pl.BlockSpec((DN_BB, DN_TS, LANES), lambda bi, hg, st: (bi, st, OFF_AB // LANES)),
            cw(0), cw(DN_WIDTH // DN_W), cw(2 * DN_WIDTH // DN_W),
            gate, gate, row,
        ],
        out_specs=pl.BlockSpec((DN_BB, DN_TS, DN_W), lambda bi, hg, st: (bi, st, hg)),
        out_shape=jax.ShapeDtypeStruct((b, s, DN_WIDTH), BF16),
        scratch_shapes=[
            pltpu.VMEM((3 * DN_BB, DN_TS + HALO, DN_W), BF16),
            pltpu.VMEM((DN_BB * DN_HB, HEAD_DIM, HEAD_DIM), F32),
        ],
        compiler_params=_cparams(("parallel", "parallel", "arbitrary")),
        name="deltanet",
    )(proj3, proj3, proj3, proj3, proj3, conv_w, conv_w, conv_w, _gate_rows(a_log), _gate_rows(dt_bias), ng_row)


SWA_CAST_ROWS = 256
SWA_UNROLL = 4


def _swa_kernel(q_ref, k_ref, v_ref, o_ref, *scratch, seq, dil, slopes):
    n_blocks = seq // dil // SWA_BLOCK
    qi = lax.broadcasted_iota(jnp.int32, (SWA_BLOCK, SWA_BLOCK), 0)
    kj = lax.broadcasted_iota(jnp.int32, (SWA_BLOCK, SWA_BLOCK), 1)
    step_cur = (qi - kj).astype(F32)
    step_prev = step_cur + float(SWA_BLOCK)
    ok_cur = qi >= kj
    ok_prev = kj >= qi
    lane = lax.broadcasted_iota(jnp.int32, (SWA_BLOCK, LANES), 1)
    scale = HEAD_DIM ** -0.5
    heads = range(SWA_HEADS_PER_GROUP)
    sls = [slice(hh * HEAD_DIM, (hh + 1) * HEAD_DIM) for hh in heads]
    m_h = [float(slopes[hh]) * float(dil) for hh in heads]

    if dil > 1:
        (f32_ref,) = scratch

        def cast_body(c, carry):
            rows = pl.ds(pl.multiple_of(c * SWA_CAST_ROWS, SWA_CAST_ROWS), SWA_CAST_ROWS)
            for t, ref in enumerate((q_ref, k_ref, v_ref)):
                for hh in heads:
                    f32_ref[t * SWA_HEADS_PER_GROUP + hh, rows, :] = ref[0, rows, sls[hh]].astype(F32)
            return carry
        lax.fori_loop(0, seq // SWA_CAST_ROWS, cast_body, 0)

        def load(t, rows):
            return [f32_ref[t * SWA_HEADS_PER_GROUP + hh, rows, :].astype(BF16) for hh in heads]
    else:
        refs = (q_ref, k_ref, v_ref)

        def load(t, rows):
            return [refs[t][0, rows, sls[hh]] for hh in heads]

    def block_rows(it):
        r = it // n_blocks
        n = it % n_blocks
        base = n * (SWA_BLOCK * dil) + r
        prev_base = jnp.maximum(n - 1, 0) * (SWA_BLOCK * dil) + r
        if dil > 1:
            return n, pl.ds(base, SWA_BLOCK, stride=dil), pl.ds(prev_base, SWA_BLOCK, stride=dil)
        return (n, pl.ds(pl.multiple_of(base, SWA_BLOCK), SWA_BLOCK),
                pl.ds(pl.multiple_of(prev_base, SWA_BLOCK), SWA_BLOCK))

    def blocks_body(it2, carry):
        qn, kc, kp, vc, vp, ok_p, slope, out_rows = [], [], [], [], [], [], [], []
        for u in range(SWA_UNROLL):
            n, rows, prev_rows = block_rows(it2 * SWA_UNROLL + u)
            qn += load(0, rows)
            kc += load(1, rows)
            vc += load(2, rows)
            kp += load(1, prev_rows)
            vp += load(2, prev_rows)
            ok_p += [jnp.logical_and(ok_prev, n > 0)] * len(heads)
            slope += m_h
            out_rows.append(rows)

        s_cur = [jnp.where(ok_cur, _dot_nt(qh, kh) * scale - m * step_cur, -jnp.inf)
                 for qh, kh, m in zip(qn, kc, slope)]
        s_prev = [jnp.where(ok, _dot_nt(qh, kh) * scale - m * step_prev, -jnp.inf)
                  for ok, qh, kh, m in zip(ok_p, qn, kp, slope)]
        mx = [jnp.max(jnp.maximum(a, b), axis=-1, keepdims=True) for a, b in zip(s_cur, s_prev)]
        e_cur = [jnp.exp(a - m) for a, m in zip(s_cur, mx)]
        e_prev = [jnp.exp(b - m) for b, m in zip(s_prev, mx)]
        den = [jnp.sum(a + b, axis=-1, keepdims=True) for a, b in zip(e_cur, e_prev)]
        pv = [jnp.dot(a.astype(BF16), vch, preferred_element_type=F32)
              + jnp.dot(b.astype(BF16), vph, preferred_element_type=F32)
              for a, b, vch, vph in zip(e_cur, e_prev, vc, vp)]
        for u, rows in enumerate(out_rows):
            lse_tile = jnp.zeros((SWA_BLOCK, LANES), F32)
            for hh in heads:
                i = u * len(heads) + hh
                o_ref[0, hh, rows, :] = pv[i] * (1.0 / den[i])
                lse_tile = jnp.where(lane == hh, mx[i] + jnp.log(den[i]), lse_tile)
            o_ref[0, SWA_HEADS_PER_GROUP, rows, :] = lse_tile
        return carry

    lax.fori_loop(0, seq // SWA_BLOCK // SWA_UNROLL, blocks_body, 0)


def _swa_group(proj3, gi):
    batch, seq, _ = proj3.shape
    window, dil = SWA_GROUPS[gi]
    assert window // dil == SWA_BLOCK and seq % (dil * SWA_BLOCK) == 0
    slopes = 2.0 ** (-ALIBI_MAX_BIAS * np.arange(1, SWA_HEADS + 1) / SWA_HEADS)
    slopes = slopes.reshape(len(SWA_GROUPS), SWA_HEADS_PER_GROUP)[gi]

    def spec(which):
        cb = (OFF_SWA + which * SWA_HEADS * HEAD_DIM) // SWA_GROUP_WIDTH + gi
        return pl.BlockSpec((1, seq, SWA_GROUP_WIDTH), lambda bi: (bi, 0, cb))

    scratch = [pltpu.VMEM((3 * SWA_HEADS_PER_GROUP, seq, HEAD_DIM), F32)] if dil > 1 else []
    return pl.pallas_call(
        functools.partial(_swa_kernel, seq=seq, dil=dil, slopes=tuple(float(s) for s in slopes)),
        grid=(batch,),
        in_specs=[spec(0), spec(1), spec(2)],
        out_specs=pl.BlockSpec((1, SWA_OUT_SLOTS, seq, HEAD_DIM), lambda bi: (bi, 0, 0, 0)),
        out_shape=jax.ShapeDtypeStruct((batch, SWA_OUT_SLOTS, seq, HEAD_DIM), F32),
        scratch_shapes=scratch,
        compiler_params=_cparams(("parallel",)),
        name=f"swa_g{gi}",
    )(proj3, proj3, proj3)


MG_TM = 256


def _merge_kernel(x_ref, ydn_ref, gate_ref, scc_ref, scb_ref, scx_ref, scc_h_ref, scx_h_ref,
                  s0_ref, s1_ref, s2_ref, scw_ref, wdn_ref, wsc_ref, wswa_ref, wout_ref, o_ref,
                  *, tiles_per_seq):
    i = pl.program_id(0)
    first = (i % tiles_per_seq) == 0

    cx_cur = scc_ref[...].astype(F32) * scx_ref[...].astype(F32)
    cx_halo = scc_h_ref[...].astype(F32) * scx_h_ref[...].astype(F32)
    cx_halo = jnp.where(first, 0.0, cx_halo)
    cx = jnp.concatenate([cx_halo, cx_cur], axis=0)
    y_sc = scb_ref[...].astype(F32) * _causal_conv(cx, scw_ref[...], SC_CONV)

    outs = (s0_ref, s1_ref, s2_ref)
    lses = [o[0, SWA_HEADS_PER_GROUP] for o in outs]
    mx = jnp.maximum(jnp.maximum(lses[0], lses[1]), lses[2])
    es = [jnp.exp(l - mx) for l in lses]
    inv = 1.0 / (es[0] + es[1] + es[2])
    heads = []
    for hh in range(SWA_HEADS_PER_GROUP):
        acc = None
        for gi in range(3):
            wt = (es[gi] * inv)[:, hh:hh + 1]
            term = wt * outs[gi][0, hh]
            acc = term if acc is None else acc + term
        heads.append(acc)
    y_swa = jnp.concatenate(heads, axis=1)

    gates = gate_ref[...].astype(F32)
    merged = (_sigmoid(gates[:, 0:D_MODEL])
              * jnp.dot(ydn_ref[...], wdn_ref[...], preferred_element_type=F32)
              + _sigmoid(gates[:, D_MODEL:2 * D_MODEL])
              * jnp.dot(y_sc.astype(BF16), wsc_ref[...], preferred_element_type=F32)
              + _sigmoid(gates[:, 2 * D_MODEL:3 * D_MODEL])
              * jnp.dot(y_swa.astype(BF16), wswa_ref[...], preferred_element_type=F32))
    o_ref[...] = x_ref[...] + jnp.dot(merged.astype(BF16), wout_ref[...], preferred_element_type=F32)


def _merge(x2d, ydn2, proj2, swa_outs, sc_conv_w, wdn, wsc, wswa, wout, seq):
    m = x2d.shape[0]
    tiles_per_seq = seq // MG_TM
    halo_blocks = MG_TM // HALO
    rows = lambda w, cb: pl.BlockSpec((MG_TM, w), lambda i: (i, cb))
    halo = lambda cb: pl.BlockSpec((HALO, SC_WIDTH), lambda i: (jnp.maximum(i * halo_blocks - 1, 0), cb))
    full = lambda a: pl.BlockSpec(a.shape, lambda i: (0, 0), pipeline_mode=pl.Buffered(1))
    swa = pl.BlockSpec((1, SWA_OUT_SLOTS, MG_TM, HEAD_DIM),
                       lambda i: (i // tiles_per_seq, 0, i % tiles_per_seq, 0))
    return pl.pallas_call(
        functools.partial(_merge_kernel, tiles_per_seq=tiles_per_seq),
        grid=(m // MG_TM,),
        in_specs=[
            rows(D_MODEL, 0),
            rows(DN_WIDTH, 0),
            rows(3 * D_MODEL, OFF_GATE // (3 * D_MODEL)),
            rows(SC_WIDTH, OFF_SCC // SC_WIDTH), rows(SC_WIDTH, OFF_SCB // SC_WIDTH),
            rows(SC_WIDTH, OFF_SCX // SC_WIDTH),
            halo(OFF_SCC // SC_WIDTH), halo(OFF_SCX // SC_WIDTH),
            swa, swa, swa,
            full(sc_conv_w), full(wdn), full(wsc), full(wswa), full(wout),
        ],
        out_specs=rows(D_MODEL, 0),
        out_shape=jax.ShapeDtypeStruct((m, D_MODEL), F32),
        compiler_params=_cparams(("parallel",)),
        name="merge",
    )(x2d, ydn2, proj2, proj2, proj2, proj2, proj2, proj2, *swa_outs, sc_conv_w, wdn, wsc, wswa, wout)


RT_TM = 512
ROUTE_LANE0 = MOE_GROUPS
R_E1, R_E2, R_W1, R_W2, R_RANK1, R_RANK2 = range(6)
HALF = D_MODEL // 2


def _pack_pair(lo, hi):
    lo_b = lax.bitcast_convert_type(lo.astype(BF16).astype(F32), jnp.uint32)
    hi_b = lax.bitcast_convert_type(hi.astype(BF16).astype(F32), jnp.uint32)
    return (hi_b & jnp.uint32(0xFFFF0000)) | (lo_b >> jnp.uint32(16))


def _unpack_pair(p):
    lo = lax.bitcast_convert_type(p << jnp.uint32(16), F32)
    hi = lax.bitcast_convert_type(p & jnp.uint32(0xFFFF0000), F32)
    return lo, hi


def _router_kernel(x_ref, g_ref, wrh_ref, wrl_ref, br_ref, hp_ref, route_ref, cnt_ref, carry_ref):
    @pl.when(pl.program_id(0) == 0)
    def _():
        carry_ref[...] = jnp.zeros_like(carry_ref)

    h = _rmsnorm_rows(x_ref[...], g_ref[...])
    hp_ref[...] = _pack_pair(h[:, :HALF], h[:, HALF:])
    h_hi = h.astype(BF16)
    h_lo = (h - h_hi.astype(F32)).astype(BF16)
    w_hi, w_lo = wrh_ref[...], wrl_ref[...]
    dot = functools.partial(jnp.dot, preferred_element_type=F32)
    logits = dot(h_hi, w_hi) + dot(h_lo, w_hi) + dot(h_hi, w_lo) + br_ref[...]
    lane = lax.broadcasted_iota(jnp.int32, logits.shape, 1)
    big = jnp.int32(LANES)
    neg = -jnp.inf

    is_grp = lane < MOE_GROUPS
    gl = jnp.where(is_grp, logits, neg)
    gmax = jnp.max(gl, axis=-1, keepdims=True)
    grp = jnp.min(jnp.where(gl == gmax, lane, big), axis=-1, keepdims=True)
    g_w = 1.0 / jnp.sum(jnp.exp(gl - gmax), axis=-1, keepdims=True)

    lo = ROUTE_LANE0 + grp * MOE_EPG
    in_grp = jnp.logical_and(lane >= lo, lane < lo + MOE_EPG)
    el = jnp.where(in_grp, logits, neg)
    m1 = jnp.max(el, axis=-1, keepdims=True)
    i1 = jnp.min(jnp.where(el == m1, lane, big), axis=-1, keepdims=True)
    el2 = jnp.where(lane == i1, neg, el)
    m2 = jnp.max(el2, axis=-1, keepdims=True)
    i2 = jnp.min(jnp.where(el2 == m2, lane, big), axis=-1, keepdims=True)
    e2 = jnp.exp(m2 - m1)
    w1 = g_w / (1.0 + e2)
    w2 = g_w * e2 / (1.0 + e2)

    tm = logits.shape[0]
    onehot = jnp.where(jnp.logical_or(lane == i1, lane == i2), 1.0, 0.0)
    earlier = (lax.broadcasted_iota(jnp.int32, (tm, tm), 0) > lax.broadcasted_iota(jnp.int32, (tm, tm), 1))
    before = carry_ref[...] + jnp.dot(jnp.where(earlier, 1.0, 0.0).astype(BF16), onehot.astype(BF16),
                                      preferred_element_type=F32)
    rank1 = jnp.sum(jnp.where(lane == i1, before, 0.0), axis=-1, keepdims=True)
    rank2 = jnp.sum(jnp.where(lane == i2, before, 0.0), axis=-1, keepdims=True)
    carry_ref[...] += jnp.sum(onehot, axis=0, keepdims=True)
    cnt_ref[...] = carry_ref[...]

    rec = jnp.zeros_like(logits)
    for ln, val in ((R_E1, (i1 - ROUTE_LANE0).astype(F32)), (R_E2, (i2 - ROUTE_LANE0).astype(F32)),
                    (R_W1, w1), (R_W2, w2), (R_RANK1, rank1), (R_RANK2, rank2)):
        rec = jnp.where(lane == ln, val, rec)
    route_ref[...] = rec


def _router(x2d, g_row, w_route_hi, w_route_lo, b_route):
    m = x2d.shape[0]
    return pl.pallas_call(
        _router_kernel,
        grid=(m // RT_TM,),
        in_specs=[
            pl.BlockSpec((RT_TM, D_MODEL), lambda i: (i, 0)),
            pl.BlockSpec((1, D_MODEL), lambda i: (0, 0)),
            pl.BlockSpec((D_MODEL, LANES), lambda i: (0, 0)),
            pl.BlockSpec((D_MODEL, LANES), lambda i: (0, 0)),
            pl.BlockSpec((1, LANES), lambda i: (0, 0)),
        ],
        out_specs=[pl.BlockSpec((RT_TM, HALF), lambda i: (i, 0)),
                   pl.BlockSpec((RT_TM, LANES), lambda i: (i, 0)),
                   pl.BlockSpec((1, LANES), lambda i: (0, 0))],
        out_shape=[jax.ShapeDtypeStruct((m, HALF), jnp.uint32), jax.ShapeDtypeStruct((m, LANES), F32),
                   jax.ShapeDtypeStruct((1, LANES), F32)],
        scratch_shapes=[pltpu.VMEM((1, LANES), F32)],
        compiler_params=_cparams(("arbitrary",)),
        name="router",
    )(x2d, g_row, w_route_hi, w_route_lo, b_route)


EXP_TM = 512


def _n_row_tiles(n_tokens):
    return 2 * n_tokens // EXP_TM + MOE_EXPERTS


def _positions_kernel(route_ref, offs_ref, pos_ref):
    rec = route_ref[...]
    lane = lax.broadcasted_iota(jnp.int32, rec.shape, 1)
    offs = offs_ref[...]

    def first_row(e):
        return jnp.sum(jnp.where(lane == e.astype(jnp.int32), offs, 0.0), axis=-1, keepdims=True)

    pos1 = first_row(rec[:, R_E1:R_E1 + 1]) + rec[:, R_RANK1:R_RANK1 + 1]
    pos2 = first_row(rec[:, R_E2:R_E2 + 1]) + rec[:, R_RANK2:R_RANK2 + 1]
    pos_ref[...] = jnp.where(lane == 0, pos1, jnp.where(lane == 1, pos2, 0.0)).astype(jnp.int32)


def _moe_plan(route, cnt):
    m = route.shape[0]
    n_tiles = _n_row_tiles(m)
    counts = cnt[0, ROUTE_LANE0:ROUTE_LANE0 + MOE_EXPERTS].astype(jnp.int32)
    padded = (counts + EXP_TM - 1) // EXP_TM * EXP_TM
    ends = jnp.cumsum(padded)
    offs = ends - padded
    pos = pl.pallas_call(
        _positions_kernel,
        grid=(m // RT_TM,),
        in_specs=[pl.BlockSpec((RT_TM, LANES), lambda i: (i, 0)), pl.BlockSpec((1, LANES), lambda i: (0, 0))],
        out_specs=pl.BlockSpec((RT_TM, LANES), lambda i: (i, 0)),
        out_shape=jax.ShapeDtypeStruct((m, LANES), jnp.int32),
        compiler_params=_cparams(("parallel",)),
        name="moe_positions",
    )(route, _lane_row(offs))
    n_active = ends[-1] // EXP_TM
    tile_row = jnp.maximum(jnp.minimum(jnp.arange(n_tiles), n_active - 1), 0) * EXP_TM
    tile_expert = jnp.minimum(jnp.sum(ends[None, :] <= tile_row[:, None], axis=1), MOE_EXPERTS - 1)
    used = counts > 0
    ids = jnp.where(used, jnp.arange(MOE_EXPERTS), MOE_EXPERTS)
    next_used = jnp.concatenate([lax.cummin(ids, reverse=True)[1:], jnp.full((1,), MOE_EXPERTS, ids.dtype)])
    parity = (jnp.cumsum(used.astype(jnp.int32)) - 1) % 2
    i32 = lambda a: a.astype(jnp.int32)
    return dict(pos1=pos[:, 0], pos2=pos[:, 1], counts=counts, offs=i32(offs),
                tile_expert=i32(tile_expert), n_active=i32(n_active).reshape(1),
                next_expert=i32(next_used[tile_expert]), parity=i32(parity[tile_expert]))


DSP_TB = 512
ROW_UNROLL = 8


def _dispatch_kernel(cnt_ref, offs_ref, nact_ref, pos1_ref, pos2_ref, hp_ref, xs_ref, zero_ref, sem, zero_sem):
    n_tiles = xs_ref.shape[0] // EXP_TM

    @pl.when(pl.program_id(0) == 0)
    def _():
        zero_ref[...] = jnp.zeros_like(zero_ref)

        def pad_pieces(e):
            pad = (-cnt_ref[e]) & (EXP_TM - 1)
            first = offs_ref[e] + cnt_ref[e]
            head = jnp.minimum((-first) & (SUBLANES - 1), pad)
            for k in range(SUBLANES - 1):
                yield k < head, pltpu.make_async_copy(zero_ref.at[pl.ds(0, 1), :],
                                                      xs_ref.at[pl.ds(first + k, 1), :], zero_sem)
            body = pad - head
            b = EXP_TM // 2
            while b >= SUBLANES:
                row = pl.multiple_of(first + head + (body & ~(2 * b - 1)), SUBLANES)
                yield (body & b) != 0, pltpu.make_async_copy(zero_ref.at[pl.ds(0, b), :],
                                                             xs_ref.at[pl.ds(row, b), :], zero_sem)
                b //= 2

        def tile_copy(t):
            rows = pl.ds(pl.multiple_of(t * EXP_TM, EXP_TM), EXP_TM)
            return pltpu.make_async_copy(zero_ref, xs_ref.at[rows, :], zero_sem)

        def for_all(method):
            def per_expert(e, c):
                for cond, cp in pad_pieces(e):
                    pl.when(cond)(getattr(cp, method))
                return c
            lax.fori_loop(0, MOE_EXPERTS, per_expert, 0)

            def per_tile(t, c):
                getattr(tile_copy(t), method)()
                return c
            lax.fori_loop(nact_ref[0], n_tiles, per_tile, 0)

        for_all("start")
        for_all("wait")

    def start(j8, c):
        for u in range(ROW_UNROLL):
            j = j8 * ROW_UNROLL + u
            src = hp_ref.at[pl.ds(j, 1), :]
            pltpu.make_async_copy(src, xs_ref.at[pl.ds(pos1_ref[0, 0, j], 1), :], sem).start(priority=0)
            pltpu.make_async_copy(src, xs_ref.at[pl.ds(pos2_ref[0, 0, j], 1), :], sem).start(priority=1)
        return c

    lax.fori_loop(0, DSP_TB // ROW_UNROLL, start, 0)
    for _ in range(2):
        pltpu.make_async_copy(hp_ref, xs_ref.at[pl.ds(0, DSP_TB), :], sem).wait()


def _dispatch(hp, pos1, pos2, counts, offs, n_active):
    m = hp.shape[0]
    n_rows = _n_row_tiles(m) * EXP_TM
    pos_spec = pl.BlockSpec((1, 1, DSP_TB), lambda i, *_: (i, 0, 0), memory_space=pltpu.SMEM)
    return pl.pallas_call(
        _dispatch_kernel,
        grid_spec=pltpu.PrefetchScalarGridSpec(
            num_scalar_prefetch=3,
            grid=(m // DSP_TB,),
            in_specs=[pos_spec, pos_spec, pl.BlockSpec((DSP_TB, HALF), lambda i, *_: (i, 0))],
            out_specs=pl.BlockSpec(memory_space=pl.ANY),
            scratch_shapes=[pltpu.VMEM((EXP_TM, HALF), jnp.uint32), pltpu.SemaphoreType.DMA(()),
                            pltpu.SemaphoreType.DMA(())],
        ),
        out_shape=jax.ShapeDtypeStruct((n_rows, HALF), jnp.uint32),
        compiler_params=_cparams(("arbitrary",)),
        name="moe_dispatch",
    )(counts, offs, n_active, pos1.reshape(m // DSP_TB, 1, DSP_TB), pos2.reshape(m // DSP_TB, 1, DSP_TB), hp)


W_CAST_ROWS = 256


def _cast_rows(src_ref, dst_ref):
    n_rows = dst_ref.shape[0]
    slab = min(W_CAST_ROWS, n_rows)

    def body(r, c):
        rows = pl.ds(pl.multiple_of(r * slab, slab), slab)
        dst_ref[rows, :] = src_ref[rows, :].astype(BF16)
        return c
    lax.fori_loop(0, n_rows // slab, body, 0)


def _expert_kernel(te_ref, nact_ref, next_ref, par_ref, xs_ref, wg_hbm, wu_hbm, wd_hbm, ys_ref,
                   sg_ref, su_ref, sd_ref, wg16_ref, wu16_ref, wd16_ref, sem, *, layer):
    i = pl.program_id(0)
    active = i < nact_ref[0]
    expert = te_ref[i]
    slot = par_ref[i]
    new_expert = jnp.logical_or(i == 0, expert != te_ref[jnp.maximum(i - 1, 0)])

    def copies(e, s):
        return [pltpu.make_async_copy(hbm.at[layer, e], stage.at[s], sem.at[s])
                for hbm, stage in ((wg_hbm, sg_ref), (wu_hbm, su_ref), (wd_hbm, sd_ref))]

    @pl.when(jnp.logical_and(active, i == 0))
    def _():
        for cp in copies(expert, slot):
            cp.start()

    @pl.when(jnp.logical_and(active, new_expert))
    def _():
        for cp in copies(expert, slot):
            cp.wait()

        @pl.when(next_ref[i] < MOE_EXPERTS)
        def _():
            for cp in copies(next_ref[i], 1 - slot):
                cp.start()

        _cast_rows(sg_ref.at[slot], wg16_ref)
        _cast_rows(su_ref.at[slot], wu16_ref)
        _cast_rows(sd_ref.at[slot], wd16_ref)

    @pl.when(active)
    def _():
        lo, hi = _unpack_pair(xs_ref[...])
        lo, hi = lo.astype(BF16), hi.astype(BF16)
        dot = functools.partial(jnp.dot, preferred_element_type=F32)
        gate = dot(lo, wg16_ref[:HALF, :]) + dot(hi, wg16_ref[HALF:, :])
        up = dot(lo, wu16_ref[:HALF, :]) + dot(hi, wu16_ref[HALF:, :])
        y = dot((_silu(gate) * up).astype(BF16), wd16_ref[...])
        ys_ref[...] = _pack_pair(y[:, :HALF], y[:, HALF:])

    @pl.when(jnp.logical_not(active))
    def _():
        ys_ref[...] = jnp.zeros_like(ys_ref)


def _experts(xs, tile_expert, n_active, next_expert, parity, wg, wu, wd, layer):
    n_tiles = xs.shape[0] // EXP_TM
    rows = pl.BlockSpec((EXP_TM, HALF),
                        lambda i, te, na, nx, par: (jnp.maximum(jnp.minimum(i, na[0] - 1), 0), 0))
    out_rows = pl.BlockSpec((EXP_TM, HALF), lambda i, te, na, nx, par: (i, 0))
    hbm = pl.BlockSpec(memory_space=pl.ANY)
    return pl.pallas_call(
        functools.partial(_expert_kernel, layer=layer),
        grid_spec=pltpu.PrefetchScalarGridSpec(
            num_scalar_prefetch=4,
            grid=(n_tiles,),
            in_specs=[rows, hbm, hbm, hbm],
            out_specs=out_rows,
            scratch_shapes=[pltpu.VMEM((2, D_MODEL, MOE_FF), F32), pltpu.VMEM((2, D_MODEL, MOE_FF), F32),
                            pltpu.VMEM((2, MOE_FF, D_MODEL), F32),
                            pltpu.VMEM((D_MODEL, MOE_FF), BF16), pltpu.VMEM((D_MODEL, MOE_FF), BF16),
                            pltpu.VMEM((MOE_FF, D_MODEL), BF16),
                            pltpu.SemaphoreType.DMA((2,))],
        ),
        out_shape=jax.ShapeDtypeStruct(xs.shape, jnp.uint32),
        compiler_params=_cparams(("arbitrary",)),
        name="moe_experts",
    )(tile_expert, n_active, next_expert, parity, xs, wg, wu, wd)


CMB_TC = 512


def _combine_kernel(pos1_ref, pos2_ref, pos1_next_ref, pos2_next_ref, x_ref, route_ref, g_ref, ys_ref,
                    o_ref, buf_ref, sem, *, final_norm):
    i = pl.program_id(0)
    slot = i % 2

    def gather(p1_ref, p2_ref, s):
        def start(j8, c):
            for u in range(ROW_UNROLL):
                j = j8 * ROW_UNROLL + u
                for k, p_ref in enumerate((p1_ref, p2_ref)):
                    pltpu.make_async_copy(ys_ref.at[pl.ds(p_ref[0, 0, j], 1), :],
                                          buf_ref.at[s, k, pl.ds(j, 1), :], sem.at[s]).start(priority=k)
            return c
        lax.fori_loop(0, CMB_TC // ROW_UNROLL, start, 0)

    @pl.when(i == 0)
    def _():
        gather(pos1_ref, pos2_ref, 0)

    @pl.when(i + 1 < pl.num_programs(0))
    def _():
        gather(pos1_next_ref, pos2_next_ref, 1 - slot)

    for k in range(2):
        pltpu.make_async_copy(ys_ref.at[pl.ds(0, CMB_TC), :], buf_ref.at[slot, k], sem.at[slot]).wait()

    rec = route_ref[...]
    w1, w2 = rec[:, R_W1:R_W1 + 1], rec[:, R_W2:R_W2 + 1]
    lo1, hi1 = _unpack_pair(buf_ref[slot, 0])
    lo2, hi2 = _unpack_pair(buf_ref[slot, 1])
    out_lo = x_ref[:, :HALF] + w1 * lo1 + w2 * lo2
    out_hi = x_ref[:, HALF:] + w1 * hi1 + w2 * hi2
    if final_norm:
        ms = (jnp.sum(out_lo * out_lo, axis=-1, keepdims=True)
              + jnp.sum(out_hi * out_hi, axis=-1, keepdims=True)) * (1.0 / D_MODEL)
        inv = lax.rsqrt(ms + RMS_EPS)
        out_lo = out_lo * inv * g_ref[:, :HALF]
        out_hi = out_hi * inv * g_ref[:, HALF:]
    o_ref[:, :HALF] = out_lo
    o_ref[:, HALF:] = out_hi


def _combine(x2d, route, ys, pos1, pos2, g_row, final_norm):
    m = x2d.shape[0]
    n_steps = m // CMB_TC
    pos_spec = pl.BlockSpec((1, 1, CMB_TC), lambda i: (i, 0, 0), memory_space=pltpu.SMEM)
    next_spec = pl.BlockSpec((1, 1, CMB_TC), lambda i: (jnp.minimum(i + 1, n_steps - 1), 0, 0),
                             memory_space=pltpu.SMEM)
    pos1, pos2 = pos1.reshape(n_steps, 1, CMB_TC), pos2.reshape(n_steps, 1, CMB_TC)
    return pl.pallas_call(
        functools.partial(_combine_kernel, final_norm=final_norm),
        grid=(n_steps,),
        in_specs=[pos_spec, pos_spec, next_spec, next_spec,
                  pl.BlockSpec((CMB_TC, D_MODEL), lambda i: (i, 0)),
                  pl.BlockSpec((CMB_TC, LANES), lambda i: (i, 0)),
                  pl.BlockSpec((1, D_MODEL), lambda i: (0, 0)),
                  pl.BlockSpec(memory_space=pl.ANY)],
        out_specs=pl.BlockSpec((CMB_TC, D_MODEL), lambda i: (i, 0)),
        out_shape=jax.ShapeDtypeStruct((m, D_MODEL), F32),
        scratch_shapes=[pltpu.VMEM((2, 2, CMB_TC, HALF), jnp.uint32), pltpu.SemaphoreType.DMA((2,))],
        compiler_params=_cparams(("arbitrary",)),
        name="moe_combine",
    )(pos1, pos2, pos1, pos2, x2d, route, g_row, ys)


RL_TN = 512
RL_PIECE = 256
SRC_A_END = 4096
SRC_AB = 16
SRC_COLS = 17936


def _relayout_plan():
    n_rest = (SRC_COLS - SRC_A_END - SRC_AB) // RL_TN
    rest = lambda s: SRC_A_END + SRC_AB + RL_TN * s
    starts = ([rest(s) for s in range(15, n_rest)]
              + [RL_TN * a for a in range(SRC_A_END // RL_TN)]
              + [rest(s) for s in range(15)]
              + [SRC_A_END])
    assert len(starts) * RL_TN == PROJ_COLS and all(s % 16 == 0 and s + RL_TN <= SRC_COLS for s in starts)
    return np.asarray(starts, np.int32)


def _relayout_kernel(start_ref, wt_ref, o_ref, buf_ref, sem, *, layer):
    t = pl.program_id(0)
    n_tiles = pl.num_programs(0)
    slot = t % 2

    def fetch(tile, s):
        first = pl.multiple_of(start_ref[tile], 16)
        pltpu.make_async_copy(wt_ref.at[layer, pl.ds(first, RL_TN), :], buf_ref.at[s], sem.at[s]).start()

    @pl.when(t == 0)
    def _():
        fetch(0, 0)

    @pl.when(t + 1 < n_tiles)
    def _():
        fetch(t + 1, 1 - slot)

    pltpu.make_async_copy(wt_ref.at[layer, pl.ds(0, RL_TN), :], buf_ref.at[slot], sem.at[slot]).wait()

    row = lax.broadcasted_iota(jnp.int32, (RL_TN, RL_PIECE), 0)
    keep = jnp.logical_or(t + 1 < n_tiles, row < SRC_AB)
    for c in range(D_MODEL // RL_PIECE):
        cols = slice(c * RL_PIECE, (c + 1) * RL_PIECE)
        piece = jnp.where(keep, buf_ref[slot, :, cols], 0.0)
        o_ref[cols, :] = piece.T.astype(BF16)


def _prep_w_in(w_all, layer):
    wt = jnp.swapaxes(w_all, 1, 2)
    return pl.pallas_call(
        functools.partial(_relayout_kernel, layer=layer),
        grid_spec=pltpu.PrefetchScalarGridSpec(
            num_scalar_prefetch=1,
            grid=(PROJ_COLS // RL_TN,),
            in_specs=[pl.BlockSpec(memory_space=pl.ANY)],
            out_specs=pl.BlockSpec((D_MODEL, RL_TN), lambda t, starts: (0, t)),
            scratch_shapes=[pltpu.VMEM((2, RL_TN, D_MODEL), F32), pltpu.SemaphoreType.DMA((2,))],
        ),
        out_shape=jax.ShapeDtypeStruct((D_MODEL, PROJ_COLS), BF16),
        compiler_params=_cparams(("arbitrary",)),
        name="w_in_relayout",
    )(jnp.asarray(_relayout_plan()), wt)


def _lane_row(v, offset=0):
    return jnp.zeros((1, LANES), F32).at[0, offset:offset + v.shape[0]].set(v.astype(F32))


def _layer(x2d, batch, seq, p, final_g_row):
    proj2 = _in_proj(x2d, p["norm_mix_g"], p["w_in"])
    proj3 = proj2.reshape(batch, seq, PROJ_COLS)
    ydn = _deltanet(proj3, p["dn_conv_w"], p["dn_a_log"], p["dn_dt_bias"], p["dn_norm_g"])
    swa_outs = [_swa_group(proj3, gi) for gi in range(len(SWA_GROUPS))]
    x2d = _merge(x2d, ydn.reshape(batch * seq, DN_WIDTH), proj2, swa_outs, p["sc_conv_w"],
                 p["w_branch_dn"], p["w_branch_sc"], p["w_branch_swa"], p["w_out"], seq)
    hp, route, cnt = _router(x2d, p["norm_ffn_g"], p["w_route_hi"], p["w_route_lo"], p["b_route"])
    plan = _moe_plan(route, cnt)
    xs = _dispatch(hp, plan["pos1"], plan["pos2"], plan["counts"], plan["offs"], plan["n_active"])
    ys = _experts(xs, plan["tile_expert"], plan["n_active"], plan["next_expert"], plan["parity"],
                  p["expert_w_gate"], p["expert_w_up"], p["expert_w_down"], p["layer"])
    is_last = final_g_row is not None
    g_row = final_g_row if is_last else p["norm_ffn_g"]
    return _combine(x2d, route, ys, plan["pos1"], plan["pos2"], g_row, is_last)


def kernel(x, norm_mix_g, w_in, dn_conv_w, dn_a_log, dn_dt_bias, dn_norm_g, sc_conv_w, w_branch_dn, w_branch_sc, w_branch_swa, w_out, norm_ffn_g, router_group_w, router_group_b, router_expert_w, router_expert_b, expert_w_gate, expert_w_up, expert_w_down, final_norm_g):
    batch, seq, _ = x.shape
    depth = w_in.shape[0]
    x2d = x.reshape(batch * seq, D_MODEL)
    for l in range(depth):
        w_route = jnp.concatenate([router_group_w[l], router_expert_w[l]], axis=1)
        w_route = jnp.pad(w_route, ((0, 0), (0, LANES - w_route.shape[1])))
        p = dict(
            norm_mix_g=norm_mix_g[l].reshape(1, D_MODEL),
            w_in=_prep_w_in(w_in, l),
            dn_conv_w=dn_conv_w[l],
            dn_a_log=dn_a_log[l],
            dn_dt_bias=dn_dt_bias[l],
            dn_norm_g=dn_norm_g[l].reshape(1, HEAD_DIM),
            sc_conv_w=sc_conv_w[l],
            w_branch_dn=w_branch_dn[l].astype(BF16),
            w_branch_sc=w_branch_sc[l].astype(BF16),
            w_branch_swa=w_branch_swa[l].astype(BF16),
            w_out=w_out[l].astype(BF16),
            norm_ffn_g=norm_ffn_g[l].reshape(1, D_MODEL),
            w_route_hi=w_route.astype(BF16),
            w_route_lo=(w_route - w_route.astype(BF16).astype(F32)).astype(BF16),
            b_route=_lane_row(jnp.concatenate([router_group_b[l], router_expert_b[l]])),
            expert_w_gate=expert_w_gate,
            expert_w_up=expert_w_up,
            expert_w_down=expert_w_down,
            layer=l,
        )
        final_g_row = final_norm_g.reshape(1, D_MODEL) if l == depth - 1 else None
        x2d = _layer(x2d, batch, seq, p, final_g_row)
    return x2d.reshape(batch, seq, D_MODEL)
```

```python
import functools

import jax
import jax.numpy as jnp
import numpy as np
from jax import lax
from jax.experimental import pallas as pl
from jax.experimental.pallas import tpu as pltpu

F32 = jnp.float32
BF16 = jnp.bfloat16

D_MODEL = 2048
RMS_EPS = 1e-6
L2_EPS = 1e-6

DN_HEADS = 8
HEAD_DIM = 128
DN_WIDTH = DN_HEADS * HEAD_DIM
DN_CONV = 4
SC_WIDTH = 1024
SC_CONV = 3
SWA_GROUPS = ((128, 1), (512, 4), (2048, 16))
SWA_HEADS_PER_GROUP = 4
SWA_HEADS = 12
SWA_GROUP_WIDTH = SWA_HEADS_PER_GROUP * HEAD_DIM
SWA_BLOCK = 128
ALIBI_MAX_BIAS = 8.0
MOE_GROUPS = 4
MOE_EPG = 8
MOE_EXPERTS = 32
MOE_FF = 512

OFF_GATE = 0
OFF_DNQ = 6144
OFF_DNK = OFF_DNQ + DN_WIDTH
OFF_DNV = OFF_DNK + DN_WIDTH
OFF_DNZ = OFF_DNV + DN_WIDTH
OFF_SCC = OFF_DNZ + DN_WIDTH
OFF_SCB = OFF_SCC + SC_WIDTH
OFF_SCX = OFF_SCB + SC_WIDTH
OFF_SWA = OFF_SCX + SC_WIDTH
OFF_AB = OFF_SWA + 3 * SWA_HEADS * HEAD_DIM
PROJ_COLS = 18432

LANES = 128
SUBLANES = 8
SWA_OUT_SLOTS = SWA_HEADS_PER_GROUP + 1
CHUNK = 128
HALO = 16

V7X_VMEM_BYTES = 64 * 1024 * 1024
VMEM_LIMIT = V7X_VMEM_BYTES - 8 * 1024 * 1024


def _cparams(sem):
    return pltpu.CompilerParams(dimension_semantics=sem, vmem_limit_bytes=VMEM_LIMIT)


def _sigmoid(x):
    return 1.0 / (1.0 + jnp.exp(-x))


def _silu(x):
    return x * _sigmoid(x)


def _softplus(x):
    return jnp.maximum(x, 0.0) + jnp.log(1.0 + jnp.exp(-jnp.abs(x)))


IN_TM = 1024
IN_TN = 2048
NORM_ROWS = 64


def _rmsnorm_rows(x, g):
    ms = jnp.mean(x * x, axis=-1, keepdims=True)
    return x * lax.rsqrt(ms + RMS_EPS) * g


def _in_proj_kernel(x_ref, g_ref, w_ref, o_ref, h_ref):
    @pl.when(pl.program_id(1) == 0)
    def _():
        def body(r, c):
            rows = pl.ds(pl.multiple_of(r * NORM_ROWS, NORM_ROWS), NORM_ROWS)
            h_ref[rows, :] = _rmsnorm_rows(x_ref[rows, :], g_ref[...]).astype(BF16)
            return c
        lax.fori_loop(0, IN_TM // NORM_ROWS, body, 0)

    o_ref[...] = jnp.dot(h_ref[...], w_ref[...], preferred_element_type=F32).astype(o_ref.dtype)


def _in_proj(x2d, g_row, w_bf16):
    m = x2d.shape[0]
    return pl.pallas_call(
        _in_proj_kernel,
        grid=(m // IN_TM, PROJ_COLS // IN_TN),
        in_specs=[
            pl.BlockSpec((IN_TM, D_MODEL), lambda i, j: (i, 0)),
            pl.BlockSpec((1, D_MODEL), lambda i, j: (0, 0)),
            pl.BlockSpec((D_MODEL, IN_TN), lambda i, j: (0, j)),
        ],
        out_specs=pl.BlockSpec((IN_TM, IN_TN), lambda i, j: (i, j)),
        out_shape=jax.ShapeDtypeStruct((m, PROJ_COLS), BF16),
        scratch_shapes=[pltpu.VMEM((IN_TM, D_MODEL), BF16)],
        compiler_params=_cparams(("parallel", "arbitrary")),
        name="in_proj",
    )(x2d, g_row, w_bf16)


DN_HB = 8
DN_BB = 2
DN_TS = 512
DN_W = DN_HB * HEAD_DIM
DN_DOUBLINGS = CHUNK.bit_length() - 2
GATE_ROWS = 2 * DN_HEADS


def _dot_nt(a, b):
    return lax.dot_general(a, b, (((1,), (1,)), ((), ())), preferred_element_type=F32)


def _dot_tn(a, b):
    return lax.dot_general(a, b, (((0,), (0,)), ((), ())), preferred_element_type=F32)


def _causal_conv(xx, w, width):
    if width == 4:
        x1 = pltpu.roll(xx, 1, axis=0)
        near = xx * w[3:4, :] + x1 * w[2:3, :]
        far = xx * w[1:2, :] + x1 * w[0:1, :]
        return (near + pltpu.roll(far, 2, axis=0))[HALO:, :]
    acc = None
    for j in range(width):
        shift = width - 1 - j
        xs = xx if shift == 0 else pltpu.roll(xx, shift, axis=0)
        term = xs[HALO:, :] * w[j:j + 1, :]
        acc = term if acc is None else acc + term
    return acc


def _deltanet_kernel(q_ref, k_ref, v_ref, z_ref, ab_ref, cwq_ref, cwk_ref, cwv_ref,
                     alog_ref, dtb_ref, ng_ref, o_ref,
                     xpad_ref, state_ref):
    st = pl.program_id(2)

    @pl.when(st == 0)
    def _():
        xpad_ref[:, 0:HALO, :] = jnp.zeros((3 * DN_BB, HALO, DN_W), BF16)
        state_ref[...] = jnp.zeros_like(state_ref)

    for bb in range(DN_BB):
        for t, ref in enumerate((q_ref, k_ref, v_ref)):
            xpad_ref[3 * bb + t, HALO:, :] = ref[bb]

    row_i = lax.broadcasted_iota(jnp.int32, (CHUNK, CHUNK), 0)
    col_j = lax.broadcasted_iota(jnp.int32, (CHUNK, CHUNK), 1)
    causal = row_i >= col_j
    strict = row_i > col_j
    upper = (row_i <= col_j).astype(F32)
    eye = (row_i == col_j).astype(F32)
    lane = lax.broadcasted_iota(jnp.int32, (CHUNK, LANES), 1)
    gate_row = lax.broadcasted_iota(jnp.int32, (GATE_ROWS, CHUNK), 0)
    neg_decay_rate = -jnp.exp(alog_ref[...])
    cws = (cwq_ref[...], cwk_ref[...], cwv_ref[...])
    ng = ng_ref[...]

    def chunk_body(c, carry):
        r0 = pl.multiple_of(c * CHUNK, CHUNK)
        win = pl.ds(r0, CHUNK + HALO)
        rows = pl.ds(r0, CHUNK)
        conv, gates, z = [], [], []
        for bb in range(DN_BB):
            conv.append([_silu(_causal_conv(xpad_ref[3 * bb + t, win, :].astype(F32), cws[t], DN_CONV))
                         for t in range(3)])
            ab_t = ab_ref[bb, rows, :].astype(F32).T[:GATE_ROWS, :]
            g_cum_t = jnp.dot(neg_decay_rate * _softplus(ab_t + dtb_ref[...]), upper,
                              preferred_element_type=F32, precision=lax.Precision.HIGHEST)
            gates_t = jnp.where(gate_row < DN_HEADS, g_cum_t, _sigmoid(ab_t))
            gates.append(jnp.concatenate([gates_t, jnp.zeros((CHUNK - GATE_ROWS, CHUNK), F32)], axis=0).T)
            z.append(z_ref[bb, rows, :].astype(F32))

        units = [(bb, i) for bb in range(DN_BB) for i in range(DN_HB)]
        sls = [slice(i * HEAD_DIM, (i + 1) * HEAD_DIM) for i in range(DN_HB)]
        dot = functools.partial(jnp.dot, preferred_element_type=F32)

        g_col = [jnp.sum(jnp.where(lane == i, gates[bb], 0.0), axis=-1, keepdims=True) for bb, i in units]
        beta = [jnp.sum(jnp.where(lane == i + DN_HEADS, gates[bb], 0.0), axis=-1, keepdims=True)
                for bb, i in units]
        g_last = [g[CHUNK - 1:CHUNK, :] for g in g_col]
        eg = [jnp.exp(g) for g in g_col]
        ek = [jnp.exp(gl - g) for gl, g in zip(g_last, g_col)]
        g_b = [jnp.broadcast_to(g, (CHUNK, CHUNK)) for g in g_col]
        decay = [jnp.exp(jnp.where(causal, gb - gb.T, -jnp.inf)) for gb in g_b]

        qf = [conv[bb][0][:, sls[i]] for bb, i in units]
        kf = [conv[bb][1][:, sls[i]] for bb, i in units]
        vf = [conv[bb][2][:, sls[i]] for bb, i in units]
        q = [x * lax.rsqrt(jnp.sum(x * x, axis=-1, keepdims=True) + L2_EPS) * (HEAD_DIM ** -0.5) for x in qf]
        k = [x * lax.rsqrt(jnp.sum(x * x, axis=-1, keepdims=True) + L2_EPS) for x in kf]
        kb = [ki * bi for ki, bi in zip(k, beta)]
        a2 = [_dot_nt(jnp.concatenate([qi, kbi], axis=0).astype(BF16), ki.astype(BF16))
              for qi, kbi, ki in zip(q, kb, k)]
        attn = [a[:CHUNK] * d for a, d in zip(a2, decay)]
        n_mat = [jnp.where(strict, -(a[CHUNK:] * d), 0.0) for a, d in zip(a2, decay)]

        u_mat = [eye + n for n in n_mat]
        p_mat = [dot(n.astype(BF16), n.astype(BF16)) for n in n_mat]
        for _ in range(DN_DOUBLINGS - 1):
            up = [dot(jnp.concatenate([u.astype(BF16), p.astype(BF16)], axis=0), p.astype(BF16))
                  for u, p in zip(u_mat, p_mat)]
            u_mat = [u + x[:CHUNK] for u, x in zip(u_mat, up)]
            p_mat = [x[CHUNK:] for x in up]
        u_mat = [u + dot(u.astype(BF16), p.astype(BF16)) for u, p in zip(u_mat, p_mat)]
        uw = [dot(u.astype(BF16), jnp.concatenate([vi * bi, kbi * egi], axis=1).astype(BF16))
              for u, vi, bi, kbi, egi in zip(u_mat, vf, beta, kb, eg)]

        s_old = [state_ref[u] for u in range(len(units))]
        qw = [dot(jnp.concatenate([qi * egi, x[:, HEAD_DIM:]], axis=0).astype(BF16), s.astype(BF16))
              for qi, egi, x, s in zip(q, eg, uw, s_old)]
        v16 = [(x[:, :HEAD_DIM] - y[CHUNK:]).astype(BF16) for x, y in zip(uw, qw)]
        o = [y[:CHUNK] + dot(a.astype(BF16), v) for y, a, v in zip(qw, attn, v16)]
        s_new = [s * jnp.exp(gl) + _dot_tn((ki * eki).astype(BF16), v)
                 for s, gl, ki, eki, v in zip(s_old, g_last, k, ek, v16)]
        for u, (bb, i) in enumerate(units):
            state_ref[u] = s_new[u]
            on = o[u] * lax.rsqrt(jnp.mean(o[u] * o[u], axis=-1, keepdims=True) + RMS_EPS) * ng
            o_ref[bb, rows, sls[i]] = (on * _silu(z[bb][:, sls[i]])).astype(o_ref.dtype)
        return carry

    lax.fori_loop(0, DN_TS // CHUNK, chunk_body, 0)

    tail = pl.ds(DN_TS, HALO)
    head = pl.ds(0, HALO)
    for t in range(3 * DN_BB):
        xpad_ref[t, head, :] = xpad_ref[t, tail, :]


def _gate_rows(v):
    rows = jnp.zeros((GATE_ROWS, LANES), F32)
    return rows.at[:v.shape[0], :].set(jnp.broadcast_to(v.astype(F32)[:, None], (v.shape[0], LANES)))


def _deltanet(proj3, conv_w, a_log, dt_bias, ng_row):
    assert DN_HB == DN_HEADS
    b, s, _ = proj3.shape
    assert b % DN_BB == 0
    gate = pl.BlockSpec((GATE_ROWS, LANES), lambda bi, hg, st: (0, 0))
    qb, kb_, vb, zb = (off // DN_W for off in (OFF_DNQ, OFF_DNK, OFF_DNV, OFF_DNZ))
    act = lambda base: pl.BlockSpec((DN_BB, DN_TS, DN_W), lambda bi, hg, st: (bi, st, base + hg))
    cw = lambda base: pl.BlockSpec((DN_CONV, DN_W), lambda bi, hg, st: (0, base + hg))
    row = pl.BlockSpec((1, LANES), lambda bi, hg, st: (0, 0))
    return pl.pallas_call(
        _deltanet_kernel,
        grid=(b // DN_BB, DN_HEADS // DN_HB, s // DN_TS),
        in_specs=[
            act(qb), act(kb_), act(vb), act(zb),
            pl.BlockSpec((DN_BB, DN_TS, LANES), lambda bi, hg, st: (bi, st, OFF_AB // LANES)),
            cw(0), cw(DN_WIDTH // DN_W), cw(2 * DN_WIDTH // DN_W),
            gate, gate, row,
        ],
        out_specs=pl.BlockSpec((DN_BB, DN_TS, DN_W), lambda bi, hg, st: (bi, st, hg)),
        out_shape=jax.ShapeDtypeStruct((b, s, DN_WIDTH), BF16),
        scratch_shapes=[
            pltpu.VMEM((3 * DN_BB, DN_TS + HALO, DN_W), BF16),
            pltpu.VMEM((DN_BB * DN_HB, HEAD_DIM, HEAD_DIM), F32),
        ],
        compiler_params=_cparams(("parallel", "parallel", "arbitrary")),
        name="deltanet",
    )(proj3, proj3, proj3, proj3, proj3, conv_w, conv_w, conv_w, _gate_rows(a_log), _gate_rows(dt_bias), ng_row)


SWA_CAST_ROWS = 256
SWA_UNROLL = 4


def _swa_kernel(q_ref, k_ref, v_ref, o_ref, *scratch, seq, dil, slopes):
    n_blocks = seq // dil // SWA_BLOCK
    qi = lax.broadcasted_iota(jnp.int32, (SWA_BLOCK, SWA_BLOCK), 0)
    kj = lax.broadcasted_iota(jnp.int32, (SWA_BLOCK, SWA_BLOCK), 1)
    step_cur = (qi - kj).astype(F32)
    step_prev = step_cur + float(SWA_BLOCK)
    ok_cur = qi >= kj
    ok_prev = kj >= qi
    lane = lax.broadcasted_iota(jnp.int32, (SWA_BLOCK, LANES), 1)
    scale = HEAD_DIM ** -0.5
    heads = range(SWA_HEADS_PER_GROUP)
    sls = [slice(hh * HEAD_DIM, (hh + 1) * HEAD_DIM) for hh in heads]
    m_h = [float(slopes[hh]) * float(dil) for hh in heads]

    if dil > 1:
        (f32_ref,) = scratch

        def cast_body(c, carry):
            rows = pl.ds(pl.multiple_of(c * SWA_CAST_ROWS, SWA_CAST_ROWS), SWA_CAST_ROWS)
            for t, ref in enumerate((q_ref, k_ref, v_ref)):
                for hh in heads:
                    f32_ref[t * SWA_HEADS_PER_GROUP + hh, rows, :] = ref[0, rows, sls[hh]].astype(F32)
            return carry
        lax.fori_loop(0, seq // SWA_CAST_ROWS, cast_body, 0)

        def load(t, rows):
            return [f32_ref[t * SWA_HEADS_PER_GROUP + hh, rows, :].astype(BF16) for hh in heads]
    else:
        refs = (q_ref, k_ref, v_ref)

        def load(t, rows):
            return [refs[t][0, rows, sls[hh]] for hh in heads]

    def block_rows(it):
        r = it // n_blocks
        n = it % n_blocks
        base = n * (SWA_BLOCK * dil) + r
        prev_base = jnp.maximum(n - 1, 0) * (SWA_BLOCK * dil) + r
        if dil > 1:
            return n, pl.ds(base, SWA_BLOCK, stride=dil), pl.ds(prev_base, SWA_BLOCK, stride=dil)
        return (n, pl.ds(pl.multiple_of(base, SWA_BLOCK), SWA_BLOCK),
                pl.ds(pl.multiple_of(prev_base, SWA_BLOCK), SWA_BLOCK))

    def blocks_body(it2, carry):
        qn, kc, kp, vc, vp, ok_p, slope, out_rows = [], [], [], [], [], [], [], []
        for u in range(SWA_UNROLL):
            n, rows, prev_rows = block_rows(it2 * SWA_UNROLL + u)
            qn += load(0, rows)
            kc += load(1, rows)
            vc += load(2, rows)
            kp += load(1, prev_rows)
            vp += load(2, prev_rows)
            ok_p += [jnp.logical_and(ok_prev, n > 0)] * len(heads)
            slope += m_h
            out_rows.append(rows)

        s_cur = [jnp.where(ok_cur, _dot_nt(qh, kh) * scale - m * step_cur, -jnp.inf)
                 for qh, kh, m in zip(qn, kc, slope)]
        s_prev = [jnp.where(ok, _dot_nt(qh, kh) * scale - m * step_prev, -jnp.inf)
                  for ok, qh, kh, m in zip(ok_p, qn, kp, slope)]
        mx = [jnp.max(jnp.maximum(a, b), axis=-1, keepdims=True) for a, b in zip(s_cur, s_prev)]
        e_cur = [jnp.exp(a - m) for a, m in zip(s_cur, mx)]
        e_prev = [jnp.exp(b - m) for b, m in zip(s_prev, mx)]
        den = [jnp.sum(a + b, axis=-1, keepdims=True) for a, b in zip(e_cur, e_prev)]
        pv = [jnp.dot(a.astype(BF16), vch, preferred_element_type=F32)
              + jnp.dot(b.astype(BF16), vph, preferred_element_type=F32)
              for a, b, vch, vph in zip(e_cur, e_prev, vc, vp)]
        for u, rows in enumerate(out_rows):
            lse_tile = jnp.zeros((SWA_BLOCK, LANES), F32)
            for hh in heads:
                i = u * len(heads) + hh
                o_ref[0, hh, rows, :] = pv[i] * (1.0 / den[i])
                lse_tile = jnp.where(lane == hh, mx[i] + jnp.log(den[i]), lse_tile)
            o_ref[0, SWA_HEADS_PER_GROUP, rows, :] = lse_tile
        return carry

    lax.fori_loop(0, seq // SWA_BLOCK // SWA_UNROLL, blocks_body, 0)


def _swa_group(proj3, gi):
    batch, seq, _ = proj3.shape
    window, dil = SWA_GROUPS[gi]
    assert window // dil == SWA_BLOCK and seq % (dil * SWA_BLOCK) == 0
    slopes = 2.0 ** (-ALIBI_MAX_BIAS * np.arange(1, SWA_HEADS + 1) / SWA_HEADS)
    slopes = slopes.reshape(len(SWA_GROUPS), SWA_HEADS_PER_GROUP)[gi]

    def spec(which):
        cb = (OFF_SWA + which * SWA_HEADS * HEAD_DIM) // SWA_GROUP_WIDTH + gi
        return pl.BlockSpec((1, seq, SWA_GROUP_WIDTH), lambda bi: (bi, 0, cb))

    scratch = [pltpu.VMEM((3 * SWA_HEADS_PER_GROUP, seq, HEAD_DIM), F32)] if dil > 1 else []
    return pl.pallas_call(
        functools.partial(_swa_kernel, seq=seq, dil=dil, slopes=tuple(float(s) for s in slopes)),
        grid=(batch,),
        in_specs=[spec(0), spec(1), spec(2)],
        out_specs=pl.BlockSpec((1, SWA_OUT_SLOTS, seq, HEAD_DIM), lambda bi: (bi, 0, 0, 0)),
        out_shape=jax.ShapeDtypeStruct((batch, SWA_OUT_SLOTS, seq, HEAD_DIM), F32),
        scratch_shapes=scratch,
        compiler_params=_cparams(("parallel",)),
        name=f"swa_g{gi}",
    )(proj3, proj3, proj3)


MG_TM = 256


def _merge_kernel(x_ref, ydn_ref, gate_ref, scc_ref, scb_ref, scx_ref, scc_h_ref, scx_h_ref,
                  s0_ref, s1_ref, s2_ref, scw_ref, wdn_ref, wsc_ref, wswa_ref, wout_ref, o_ref,
                  *, tiles_per_seq):
    i = pl.program_id(0)
    first = (i % tiles_per_seq) == 0

    cx_cur = scc_ref[...].astype(F32) * scx_ref[...].astype(F32)
    cx_halo = scc_h_ref[...].astype(F32) * scx_h_ref[...].astype(F32)
    cx_halo = jnp.where(first, 0.0, cx_halo)
    cx = jnp.concatenate([cx_halo, cx_cur], axis=0)
    y_sc = scb_ref[...].astype(F32) * _causal_conv(cx, scw_ref[...], SC_CONV)

    outs = (s0_ref, s1_ref, s2_ref)
    lses = [o[0, SWA_HEADS_PER_GROUP] for o in outs]
    mx = jnp.maximum(jnp.maximum(lses[0], lses[1]), lses[2])
    es = [jnp.exp(l - mx) for l in lses]
    inv = 1.0 / (es[0] + es[1] + es[2])
    heads = []
    for hh in range(SWA_HEADS_PER_GROUP):
        acc = None
        for gi in range(3):
            wt = (es[gi] * inv)[:, hh:hh + 1]
            term = wt * outs[gi][0, hh]
            acc = term if acc is None else acc + term
        heads.append(acc)
    y_swa = jnp.concatenate(heads, axis=1)

    gates = gate_ref[...].astype(F32)
    merged = (_sigmoid(gates[:, 0:D_MODEL])
              * jnp.dot(ydn_ref[...], wdn_ref[...], preferred_element_type=F32)
              + _sigmoid(gates[:, D_MODEL:2 * D_MODEL])
              * jnp.dot(y_sc.astype(BF16), wsc_ref[...], preferred_element_type=F32)
              + _sigmoid(gates[:, 2 * D_MODEL:3 * D_MODEL])
              * jnp.dot(y_swa.astype(BF16), wswa_ref[...], preferred_element_type=F32))
    o_ref[...] = x_ref[...] + jnp.dot(merged.astype(BF16), wout_ref[...], preferred_element_type=F32)


def _merge(x2d, ydn2, proj2, swa_outs, sc_conv_w, wdn, wsc, wswa, wout, seq):
    m = x2d.shape[0]
    tiles_per_seq = seq // MG_TM
    halo_blocks = MG_TM // HALO
    rows = lambda w, cb: pl.BlockSpec((MG_TM, w), lambda i: (i, cb))
    halo = lambda cb: pl.BlockSpec((HALO, SC_WIDTH), lambda i: (jnp.maximum(i * halo_blocks - 1, 0), cb))
    full = lambda a: pl.BlockSpec(a.shape, lambda i: (0, 0), pipeline_mode=pl.Buffered(1))
    swa = pl.BlockSpec((1, SWA_OUT_SLOTS, MG_TM, HEAD_DIM),
                       lambda i: (i // tiles_per_seq, 0, i % tiles_per_seq, 0))
    return pl.pallas_call(
        functools.partial(_merge_kernel, tiles_per_seq=tiles_per_seq),
        grid=(m // MG_TM,),
        in_specs=[
            rows(D_MODEL, 0),
            rows(DN_WIDTH, 0),
            rows(3 * D_MODEL, OFF_GATE // (3 * D_MODEL)),
            rows(SC_WIDTH, OFF_SCC // SC_WIDTH), rows(SC_WIDTH, OFF_SCB // SC_WIDTH),
            rows(SC_WIDTH, OFF_SCX // SC_WIDTH),
            halo(OFF_SCC // SC_WIDTH), halo(OFF_SCX // SC_WIDTH),
            swa, swa, swa,
            full(sc_conv_w), full(wdn), full(wsc), full(wswa), full(wout),
        ],
        out_specs=rows(D_MODEL, 0),
        out_shape=jax.ShapeDtypeStruct((m, D_MODEL), F32),
        compiler_params=_cparams(("parallel",)),
        name="merge",
    )(x2d, ydn2, proj2, proj2, proj2, proj2, proj2, proj2, *swa_outs, sc_conv_w, wdn, wsc, wswa, wout)


RT_TM = 1024
ROUTE_LANE0 = MOE_GROUPS
R_E1, R_E2, R_W1, R_W2, R_RANK1, R_RANK2 = range(6)
HALF = D_MODEL // 2


def _pack_pair(lo, hi):
    lo_b = lax.bitcast_convert_type(lo.astype(BF16).astype(F32), jnp.uint32)
    hi_b = lax.bitcast_convert_type(hi.astype(BF16).astype(F32), jnp.uint32)
    return (hi_b & jnp.uint32(0xFFFF0000)) | (lo_b >> jnp.uint32(16))


def _unpack_pair(p):
    lo = lax.bitcast_convert_type(p << jnp.uint32(16), F32)
    hi = lax.bitcast_convert_type(p & jnp.uint32(0xFFFF0000), F32)
    return lo, hi


def _router_kernel(x_ref, g_ref, wrh_ref, wrl_ref, br_ref, hp_ref, route_ref, cnt_ref, carry_ref):
    @pl.when(pl.program_id(0) == 0)
    def _():
        carry_ref[...] = jnp.zeros_like(carry_ref)

    h = _rmsnorm_rows(x_ref[...], g_ref[...])
    hp_ref[...] = _pack_pair(h[:, :HALF], h[:, HALF:])
    h_hi = h.astype(BF16)
    h_lo = (h - h_hi.astype(F32)).astype(BF16)
    w_hi, w_lo = wrh_ref[...], wrl_ref[...]
    dot = functools.partial(jnp.dot, preferred_element_type=F32)
    logits = dot(h_hi, w_hi) + dot(h_lo, w_hi) + dot(h_hi, w_lo) + br_ref[...]
    lane = lax.broadcasted_iota(jnp.int32, logits.shape, 1)
    big = jnp.int32(LANES)
    neg = -jnp.inf

    is_grp = lane < MOE_GROUPS
    gl = jnp.where(is_grp, logits, neg)
    gmax = jnp.max(gl, axis=-1, keepdims=True)
    grp = jnp.min(jnp.where(gl == gmax, lane, big), axis=-1, keepdims=True)
    g_w = 1.0 / jnp.sum(jnp.exp(gl - gmax), axis=-1, keepdims=True)

    lo = ROUTE_LANE0 + grp * MOE_EPG
    in_grp = jnp.logical_and(lane >= lo, lane < lo + MOE_EPG)
    el = jnp.where(in_grp, logits, neg)
    m1 = jnp.max(el, axis=-1, keepdims=True)
    i1 = jnp.min(jnp.where(el == m1, lane, big), axis=-1, keepdims=True)
    el2 = jnp.where(lane == i1, neg, el)
    m2 = jnp.max(el2, axis=-1, keepdims=True)
    i2 = jnp.min(jnp.where(el2 == m2, lane, big), axis=-1, keepdims=True)
    e2 = jnp.exp(m2 - m1)
    w1 = g_w / (1.0 + e2)
    w2 = g_w * e2 / (1.0 + e2)

    tm = logits.shape[0]
    onehot = jnp.where(jnp.logical_or(lane == i1, lane == i2), 1.0, 0.0)
    earlier = (lax.broadcasted_iota(jnp.int32, (tm, tm), 0) > lax.broadcasted_iota(jnp.int32, (tm, tm), 1))
    before = carry_ref[...] + jnp.dot(jnp.where(earlier, 1.0, 0.0).astype(BF16), onehot.astype(BF16),
                                      preferred_element_type=F32)
    rank1 = jnp.sum(jnp.where(lane == i1, before, 0.0), axis=-1, keepdims=True)
    rank2 = jnp.sum(jnp.where(lane == i2, before, 0.0), axis=-1, keepdims=True)
    carry_ref[...] += jnp.sum(onehot, axis=0, keepdims=True)
    cnt_ref[...] = carry_ref[...]

    rec = jnp.zeros_like(logits)
    for ln, val in ((R_E1, (i1 - ROUTE_LANE0).astype(F32)), (R_E2, (i2 - ROUTE_LANE0).astype(F32)),
                    (R_W1, w1), (R_W2, w2), (R_RANK1, rank1), (R_RANK2, rank2)):
        rec = jnp.where(lane == ln, val, rec)
    route_ref[...] = rec


def _router(x2d, g_row, w_route_hi, w_route_lo, b_route):
    m = x2d.shape[0]
    return pl.pallas_call(
        _router_kernel,
        grid=(m // RT_TM,),
        in_specs=[
            pl.BlockSpec((RT_TM, D_MODEL), lambda i: (i, 0)),
            pl.BlockSpec((1, D_MODEL), lambda i: (0, 0)),
            pl.BlockSpec((D_MODEL, LANES), lambda i: (0, 0)),
            pl.BlockSpec((D_MODEL, LANES), lambda i: (0, 0)),
            pl.BlockSpec((1, LANES), lambda i: (0, 0)),
        ],
        out_specs=[pl.BlockSpec((RT_TM, HALF), lambda i: (i, 0)),
                   pl.BlockSpec((RT_TM, LANES), lambda i: (i, 0)),
                   pl.BlockSpec((1, LANES), lambda i: (0, 0))],
        out_shape=[jax.ShapeDtypeStruct((m, HALF), jnp.uint32), jax.ShapeDtypeStruct((m, LANES), F32),
                   jax.ShapeDtypeStruct((1, LANES), F32)],
        scratch_shapes=[pltpu.VMEM((1, LANES), F32)],
        compiler_params=_cparams(("arbitrary",)),
        name="router",
    )(x2d, g_row, w_route_hi, w_route_lo, b_route)


EXP_TM = 512


def _n_row_tiles(n_tokens):
    return 2 * n_tokens // EXP_TM + MOE_EXPERTS


def _positions_kernel(route_ref, offs_ref, pos_ref):
    rec = route_ref[...]
    lane = lax.broadcasted_iota(jnp.int32, rec.shape, 1)
    offs = offs_ref[...]

    def first_row(e):
        return jnp.sum(jnp.where(lane == e.astype(jnp.int32), offs, 0.0), axis=-1, keepdims=True)

    pos1 = first_row(rec[:, R_E1:R_E1 + 1]) + rec[:, R_RANK1:R_RANK1 + 1]
    pos2 = first_row(rec[:, R_E2:R_E2 + 1]) + rec[:, R_RANK2:R_RANK2 + 1]
    pos_ref[...] = jnp.where(lane == 0, pos1, jnp.where(lane == 1, pos2, 0.0)).astype(jnp.int32)


def _moe_plan(route, cnt):
    m = route.shape[0]
    n_tiles = _n_row_tiles(m)
    counts = cnt[0, ROUTE_LANE0:ROUTE_LANE0 + MOE_EXPERTS].astype(jnp.int32)
    padded = (counts + EXP_TM - 1) // EXP_TM * EXP_TM
    ends = jnp.cumsum(padded)
    offs = ends - padded
    pos = pl.pallas_call(
        _positions_kernel,
        grid=(m // RT_TM,),
        in_specs=[pl.BlockSpec((RT_TM, LANES), lambda i: (i, 0)), pl.BlockSpec((1, LANES), lambda i: (0, 0))],
        out_specs=pl.BlockSpec((RT_TM, LANES), lambda i: (i, 0)),
        out_shape=jax.ShapeDtypeStruct((m, LANES), jnp.int32),
        compiler_params=_cparams(("parallel",)),
        name="moe_positions",
    )(route, _lane_row(offs))
    n_active = ends[-1] // EXP_TM
    tile_row = jnp.maximum(jnp.minimum(jnp.arange(n_tiles), n_active - 1), 0) * EXP_TM
    tile_expert = jnp.minimum(jnp.sum(ends[None, :] <= tile_row[:, None], axis=1), MOE_EXPERTS - 1)
    used = counts > 0
    ids = jnp.where(used, jnp.arange(MOE_EXPERTS), MOE_EXPERTS)
    next_used = jnp.concatenate([lax.cummin(ids, reverse=True)[1:], jnp.full((1,), MOE_EXPERTS, ids.dtype)])
    parity = (jnp.cumsum(used.astype(jnp.int32)) - 1) % 2
    i32 = lambda a: a.astype(jnp.int32)
    return dict(pos1=pos[:, 0], pos2=pos[:, 1], counts=counts, offs=i32(offs),
                tile_expert=i32(tile_expert), n_active=i32(n_active).reshape(1),
                next_expert=i32(next_used[tile_expert]), parity=i32(parity[tile_expert]))


DSP_TB = 512
ROW_UNROLL = 8


def _dispatch_kernel(cnt_ref, offs_ref, nact_ref, pos1_ref, pos2_ref, hp_ref, xs_ref, zero_ref, sem, zero_sem):
    n_tiles = xs_ref.shape[0] // EXP_TM

    @pl.when(pl.program_id(0) == 0)
    def _():
        zero_ref[...] = jnp.zeros_like(zero_ref)

        def pad_pieces(e):
            pad = (-cnt_ref[e]) & (EXP_TM - 1)
            first = offs_ref[e] + cnt_ref[e]
            head = jnp.minimum((-first) & (SUBLANES - 1), pad)
            for k in range(SUBLANES - 1):
                yield k < head, pltpu.make_async_copy(zero_ref.at[pl.ds(0, 1), :],
                                                      xs_ref.at[pl.ds(first + k, 1), :], zero_sem)
            body = pad - head
            b = EXP_TM // 2
            while b >= SUBLANES:
                row = pl.multiple_of(first + head + (body & ~(2 * b - 1)), SUBLANES)
                yield (body & b) != 0, pltpu.make_async_copy(zero_ref.at[pl.ds(0, b), :],
                                                             xs_ref.at[pl.ds(row, b), :], zero_sem)
                b //= 2

        def tile_copy(t):
            rows = pl.ds(pl.multiple_of(t * EXP_TM, EXP_TM), EXP_TM)
            return pltpu.make_async_copy(zero_ref, xs_ref.at[rows, :], zero_sem)

        def for_all(method):
            def per_expert(e, c):
                for cond, cp in pad_pieces(e):
                    pl.when(cond)(getattr(cp, method))
                return c
            lax.fori_loop(0, MOE_EXPERTS, per_expert, 0)

            def per_tile(t, c):
                getattr(tile_copy(t), method)()
                return c
            lax.fori_loop(nact_ref[0], n_tiles, per_tile, 0)

        for_all("start")
        for_all("wait")

    def start(j8, c):
        for u in range(ROW_UNROLL):
            j = j8 * ROW_UNROLL + u
            src = hp_ref.at[pl.ds(j, 1), :]
            pltpu.make_async_copy(src, xs_ref.at[pl.ds(pos1_ref[0, 0, j], 1), :], sem).start(priority=0)
            pltpu.make_async_copy(src, xs_ref.at[pl.ds(pos2_ref[0, 0, j], 1), :], sem).start(priority=1)
        return c

    lax.fori_loop(0, DSP_TB // ROW_UNROLL, start, 0)
    for _ in range(2):
        pltpu.make_async_copy(hp_ref, xs_ref.at[pl.ds(0, DSP_TB), :], sem).wait()


def _dispatch(hp, pos1, pos2, counts, offs, n_active):
    m = hp.shape[0]
    n_rows = _n_row_tiles(m) * EXP_TM
    pos_spec = pl.BlockSpec((1, 1, DSP_TB), lambda i, *_: (i, 0, 0), memory_space=pltpu.SMEM)
    return pl.pallas_call(
        _dispatch_kernel,
        grid_spec=pltpu.PrefetchScalarGridSpec(
            num_scalar_prefetch=3,
            grid=(m // DSP_TB,),
            in_specs=[pos_spec, pos_spec, pl.BlockSpec((DSP_TB, HALF), lambda i, *_: (i, 0))],
            out_specs=pl.BlockSpec(memory_space=pl.ANY),
            scratch_shapes=[pltpu.VMEM((EXP_TM, HALF), jnp.uint32), pltpu.SemaphoreType.DMA(()),
                            pltpu.SemaphoreType.DMA(())],
        ),
        out_shape=jax.ShapeDtypeStruct((n_rows, HALF), jnp.uint32),
        compiler_params=_cparams(("arbitrary",)),
        name="moe_dispatch",
    )(counts, offs, n_active, pos1.reshape(m // DSP_TB, 1, DSP_TB), pos2.reshape(m // DSP_TB, 1, DSP_TB), hp)


W_CAST_ROWS = 256


def _cast_rows(src_ref, dst_ref):
    n_rows = dst_ref.shape[0]
    slab = min(W_CAST_ROWS, n_rows)

    def body(r, c):
        rows = pl.ds(pl.multiple_of(r * slab, slab), slab)
        dst_ref[rows, :] = src_ref[rows, :].astype(BF16)
        return c
    lax.fori_loop(0, n_rows // slab, body, 0)


def _expert_kernel(te_ref, nact_ref, next_ref, par_ref, xs_ref, wg_hbm, wu_hbm, wd_hbm, ys_ref,
                   sg_ref, su_ref, sd_ref, wg16_ref, wu16_ref, wd16_ref, sem, *, layer):
    i = pl.program_id(0)
    active = i < nact_ref[0]
    expert = te_ref[i]
    slot = par_ref[i]
    new_expert = jnp.logical_or(i == 0, expert != te_ref[jnp.maximum(i - 1, 0)])

    def copies(e, s):
        return [pltpu.make_async_copy(hbm.at[layer, e], stage.at[s], sem.at[s])
                for hbm, stage in ((wg_hbm, sg_ref), (wu_hbm, su_ref), (wd_hbm, sd_ref))]

    @pl.when(jnp.logical_and(active, i == 0))
    def _():
        for cp in copies(expert, slot):
            cp.start()

    @pl.when(jnp.logical_and(active, new_expert))
    def _():
        for cp in copies(expert, slot):
            cp.wait()

        @pl.when(next_ref[i] < MOE_EXPERTS)
        def _():
            for cp in copies(next_ref[i], 1 - slot):
                cp.start()

        _cast_rows(sg_ref.at[slot], wg16_ref)
        _cast_rows(su_ref.at[slot], wu16_ref)
        _cast_rows(sd_ref.at[slot], wd16_ref)

    @pl.when(active)
    def _():
        lo, hi = _unpack_pair(xs_ref[...])
        lo, hi = lo.astype(BF16), hi.astype(BF16)
        dot = functools.partial(jnp.dot, preferred_element_type=F32)
        gate = dot(lo, wg16_ref[:HALF, :]) + dot(hi, wg16_ref[HALF:, :])
        up = dot(lo, wu16_ref[:HALF, :]) + dot(hi, wu16_ref[HALF:, :])
        y = dot((_silu(gate) * up).astype(BF16), wd16_ref[...])
        ys_ref[...] = _pack_pair(y[:, :HALF], y[:, HALF:])

    @pl.when(jnp.logical_not(active))
    def _():
        ys_ref[...] = jnp.zeros_like(ys_ref)


def _experts(xs, tile_expert, n_active, next_expert, parity, wg, wu, wd, layer):
    n_tiles = xs.shape[0] // EXP_TM
    rows = pl.BlockSpec((EXP_TM, HALF),
                        lambda i, te, na, nx, par: (jnp.maximum(jnp.minimum(i, na[0] - 1), 0), 0))
    out_rows = pl.BlockSpec((EXP_TM, HALF), lambda i, te, na, nx, par: (i, 0))
    hbm = pl.BlockSpec(memory_space=pl.ANY)
    return pl.pallas_call(
        functools.partial(_expert_kernel, layer=layer),
        grid_spec=pltpu.PrefetchScalarGridSpec(
            num_scalar_prefetch=4,
            grid=(n_tiles,),
            in_specs=[rows, hbm, hbm, hbm],
            out_specs=out_rows,
            scratch_shapes=[pltpu.VMEM((2, D_MODEL, MOE_FF), F32), pltpu.VMEM((2, D_MODEL, MOE_FF), F32),
                            pltpu.VMEM((2, MOE_FF, D_MODEL), F32),
                            pltpu.VMEM((D_MODEL, MOE_FF), BF16), pltpu.VMEM((D_MODEL, MOE_FF), BF16),
                            pltpu.VMEM((MOE_FF, D_MODEL), BF16),
                            pltpu.SemaphoreType.DMA((2,))],
        ),
        out_shape=jax.ShapeDtypeStruct(xs.shape, jnp.uint32),
        compiler_params=_cparams(("arbitrary",)),
        name="moe_experts",
    )(tile_expert, n_active, next_expert, parity, xs, wg, wu, wd)


CMB_TC = 512


def _combine_kernel(pos1_ref, pos2_ref, pos1_next_ref, pos2_next_ref, x_ref, route_ref, g_ref, ys_ref,
                    o_ref, buf_ref, sem, *, final_norm):
    i = pl.program_id(0)
    slot = i % 2

    def gather(p1_ref, p2_ref, s):
        def start(j8, c):
            for u in range(ROW_UNROLL):
                j = j8 * ROW_UNROLL + u
                for k, p_ref in enumerate((p1_ref, p2_ref)):
                    pltpu.make_async_copy(ys_ref.at[pl.ds(p_ref[0, 0, j], 1), :],
                                          buf_ref.at[s, k, pl.ds(j, 1), :], sem.at[s]).start(priority=k)
            return c
        lax.fori_loop(0, CMB_TC // ROW_UNROLL, start, 0)

    @pl.when(i == 0)
    def _():
        gather(pos1_ref, pos2_ref, 0)

    @pl.when(i + 1 < pl.num_programs(0))
    def _():
        gather(pos1_next_ref, pos2_next_ref, 1 - slot)

    for k in range(2):
        pltpu.make_async_copy(ys_ref.at[pl.ds(0, CMB_TC), :], buf_ref.at[slot, k], sem.at[slot]).wait()

    rec = route_ref[...]
    w1, w2 = rec[:, R_W1:R_W1 + 1], rec[:, R_W2:R_W2 + 1]
    lo1, hi1 = _unpack_pair(buf_ref[slot, 0])
    lo2, hi2 = _unpack_pair(buf_ref[slot, 1])
    out_lo = x_ref[:, :HALF] + w1 * lo1 + w2 * lo2
    out_hi = x_ref[:, HALF:] + w1 * hi1 + w2 * hi2
    if final_norm:
        ms = (jnp.sum(out_lo * out_lo, axis=-1, keepdims=True)
              + jnp.sum(out_hi * out_hi, axis=-1, keepdims=True)) * (1.0 / D_MODEL)
        inv = lax.rsqrt(ms + RMS_EPS)
        out_lo = out_lo * inv * g_ref[:, :HALF]
        out_hi = out_hi * inv * g_ref[:, HALF:]
    o_ref[:, :HALF] = out_lo
    o_ref[:, HALF:] = out_hi


def _combine(x2d, route, ys, pos1, pos2, g_row, final_norm):
    m = x2d.shape[0]
    n_steps = m // CMB_TC
    pos_spec = pl.BlockSpec((1, 1, CMB_TC), lambda i: (i, 0, 0), memory_space=pltpu.SMEM)
    next_spec = pl.BlockSpec((1, 1, CMB_TC), lambda i: (jnp.minimum(i + 1, n_steps - 1), 0, 0),
                             memory_space=pltpu.SMEM)
    pos1, pos2 = pos1.reshape(n_steps, 1, CMB_TC), pos2.reshape(n_steps, 1, CMB_TC)
    return pl.pallas_call(
        functools.partial(_combine_kernel, final_norm=final_norm),
        grid=(n_steps,),
        in_specs=[pos_spec, pos_spec, next_spec, next_spec,
                  pl.BlockSpec((CMB_TC, D_MODEL), lambda i: (i, 0)),
                  pl.BlockSpec((CMB_TC, LANES), lambda i: (i, 0)),
                  pl.BlockSpec((1, D_MODEL), lambda i: (0, 0)),
                  pl.BlockSpec(memory_space=pl.ANY)],
        out_specs=pl.BlockSpec((CMB_TC, D_MODEL), lambda i: (i, 0)),
        out_shape=jax.ShapeDtypeStruct((m, D_MODEL), F32),
        scratch_shapes=[pltpu.VMEM((2, 2, CMB_TC, HALF), jnp.uint32), pltpu.SemaphoreType.DMA((2,))],
        compiler_params=_cparams(("arbitrary",)),
        name="moe_combine",
    )(pos1, pos2, pos1, pos2, x2d, route, g_row, ys)


RL_TN = 512
RL_PIECE = 256
SRC_A_END = 4096
SRC_AB = 16
SRC_COLS = 17936


def _relayout_plan():
    n_rest = (SRC_COLS - SRC_A_END - SRC_AB) // RL_TN
    rest = lambda s: SRC_A_END + SRC_AB + RL_TN * s
    starts = ([rest(s) for s in range(15, n_rest)]
              + [RL_TN * a for a in range(SRC_A_END // RL_TN)]
              + [rest(s) for s in range(15)]
              + [SRC_A_END])
    assert len(starts) * RL_TN == PROJ_COLS and all(s % 16 == 0 and s + RL_TN <= SRC_COLS for s in starts)
    return np.asarray(starts, np.int32)


def _relayout_kernel(start_ref, wt_ref, o_ref, buf_ref, sem, *, layer):
    t = pl.program_id(0)
    n_tiles = pl.num_programs(0)
    slot = t % 2

    def fetch(tile, s):
        first = pl.multiple_of(start_ref[tile], 16)
        pltpu.make_async_copy(wt_ref.at[layer, pl.ds(first, RL_TN), :], buf_ref.at[s], sem.at[s]).start()

    @pl.when(t == 0)
    def _():
        fetch(0, 0)

    @pl.when(t + 1 < n_tiles)
    def _():
        fetch(t + 1, 1 - slot)

    pltpu.make_async_copy(wt_ref.at[layer, pl.ds(0, RL_TN), :], buf_ref.at[slot], sem.at[slot]).wait()

    row = lax.broadcasted_iota(jnp.int32, (RL_TN, RL_PIECE), 0)
    keep = jnp.logical_or(t + 1 < n_tiles, row < SRC_AB)
    for c in range(D_MODEL // RL_PIECE):
        cols = slice(c * RL_PIECE, (c + 1) * RL_PIECE)
        piece = jnp.where(keep, buf_ref[slot, :, cols], 0.0)
        o_ref[cols, :] = piece.T.astype(BF16)


def _prep_w_in(w_all, layer):
    wt = jnp.swapaxes(w_all, 1, 2)
    return pl.pallas_call(
        functools.partial(_relayout_kernel, layer=layer),
        grid_spec=pltpu.PrefetchScalarGridSpec(
            num_scalar_prefetch=1,
            grid=(PROJ_COLS // RL_TN,),
            in_specs=[pl.BlockSpec(memory_space=pl.ANY)],
            out_specs=pl.BlockSpec((D_MODEL, RL_TN), lambda t, starts: (0, t)),
            scratch_shapes=[pltpu.VMEM((2, RL_TN, D_MODEL), F32), pltpu.SemaphoreType.DMA((2,))],
        ),
        out_shape=jax.ShapeDtypeStruct((D_MODEL, PROJ_COLS), BF16),
        compiler_params=_cparams(("arbitrary",)),
        name="w_in_relayout",
    )(jnp.asarray(_relayout_plan()), wt)


def _lane_row(v):
    return jnp.zeros((1, LANES), F32).at[0, :v.shape[0]].set(v.astype(F32))


def _layer(x2d, batch, seq, p, final_g_row):
    proj2 = _in_proj(x2d, p["norm_mix_g"], p["w_in"])
    proj3 = proj2.reshape(batch, seq, PROJ_COLS)
    ydn = _deltanet(proj3, p["dn_conv_w"], p["dn_a_log"], p["dn_dt_bias"], p["dn_norm_g"])
    swa_outs = [_swa_group(proj3, gi) for gi in range(len(SWA_GROUPS))]
    x2d = _merge(x2d, ydn.reshape(batch * seq, DN_WIDTH), proj2, swa_outs, p["sc_conv_w"],
                 p["w_branch_dn"], p["w_branch_sc"], p["w_branch_swa"], p["w_out"], seq)
    hp, route, cnt = _router(x2d, p["norm_ffn_g"], p["w_route_hi"], p["w_route_lo"], p["b_route"])
    plan = _moe_plan(route, cnt)
    xs = _dispatch(hp, plan["pos1"], plan["pos2"], plan["counts"], plan["offs"], plan["n_active"])
    ys = _experts(xs, plan["tile_expert"], plan["n_active"], plan["next_expert"], plan["parity"],
                  p["expert_w_gate"], p["expert_w_up"], p["expert_w_down"], p["layer"])
    is_last = final_g_row is not None
    g_row = final_g_row if is_last else p["norm_ffn_g"]
    return _combine(x2d, route, ys, plan["pos1"], plan["pos2"], g_row, is_last)


def kernel(x, norm_mix_g, w_in, dn_conv_w, dn_a_log, dn_dt_bias, dn_norm_g, sc_conv_w, w_branch_dn, w_branch_sc, w_branch_swa, w_out, norm_ffn_g, router_group_w, router_group_b, router_expert_w, router_expert_b, expert_w_gate, expert_w_up, expert_w_down, final_norm_g):
    batch, seq, _ = x.shape
    depth = w_in.shape[0]
    x2d = x.reshape(batch * seq, D_MODEL)
    for l in range(depth):
        w_route = jnp.concatenate([router_group_w[l], router_expert_w[l]], axis=1)
        w_route = jnp.pad(w_route, ((0, 0), (0, LANES - w_route.shape[1])))
        p = dict(
            norm_mix_g=norm_mix_g[l].reshape(1, D_MODEL),
            w_in=_prep_w_in(w_in, l),
            dn_conv_w=dn_conv_w[l],
            dn_a_log=dn_a_log[l],
            dn_dt_bias=dn_dt_bias[l],
            dn_norm_g=dn_norm_g[l].reshape(1, HEAD_DIM),
            sc_conv_w=sc_conv_w[l],
            w_branch_dn=w_branch_dn[l].astype(BF16),
            w_branch_sc=w_branch_sc[l].astype(BF16),
            w_branch_swa=w_branch_swa[l].astype(BF16),
            w_out=w_out[l].astype(BF16),
            norm_ffn_g=norm_ffn_g[l].reshape(1, D_MODEL),
            w_route_hi=w_route.astype(BF16),
            w_route_lo=(w_route - w_route.astype(BF16).astype(F32)).astype(BF16),
            b_route=_lane_row(jnp.concatenate([router_group_b[l], router_expert_b[l]])),
            expert_w_gate=expert_w_gate,
            expert_w_up=expert_w_up,
            expert_w_down=expert_w_down,
            layer=l,
        )
        final_g_row = final_norm_g.reshape(1, D_MODEL) if l == depth - 1 else None
        x2d = _layer(x2d, batch, seq, p, final_g_row)
    return x2d.reshape(batch, seq, D_MODEL)
```

```python
import functools

import jax
import jax.numpy as jnp
import numpy as np
from jax import lax
from jax.experimental import pallas as pl
from jax.experimental.pallas import tpu as pltpu

F32 = jnp.float32
BF16 = jnp.bfloat16

D_MODEL = 2048
RMS_EPS = 1e-6
L2_EPS = 1e-6

DN_HEADS = 8
HEAD_DIM = 128
DN_WIDTH = DN_HEADS * HEAD_DIM
DN_CONV = 4
SC_WIDTH = 1024
SC_CONV = 3
SWA_GROUPS = ((128, 1), (512, 4), (2048, 16))
SWA_HEADS_PER_GROUP = 4
SWA_HEADS = 12
SWA_GROUP_WIDTH = SWA_HEADS_PER_GROUP * HEAD_DIM
SWA_BLOCK = 128
ALIBI_MAX_BIAS = 8.0
MOE_GROUPS = 4
MOE_EPG = 8
MOE_EXPERTS = 32
MOE_FF = 512

OFF_GATE = 0
OFF_DNQ = 6144
OFF_DNK = OFF_DNQ + DN_WIDTH
OFF_DNV = OFF_DNK + DN_WIDTH
OFF_DNZ = OFF_DNV + DN_WIDTH
OFF_SCC = OFF_DNZ + DN_WIDTH
OFF_SCB = OFF_SCC + SC_WIDTH
OFF_SCX = OFF_SCB + SC_WIDTH
OFF_SWA = OFF_SCX + SC_WIDTH
OFF_AB = OFF_SWA + 3 * SWA_HEADS * HEAD_DIM
PROJ_COLS = 18432

LANES = 128
SUBLANES = 8
SWA_OUT_SLOTS = SWA_HEADS_PER_GROUP + 1
CHUNK = 128
HALO = 16

V7X_VMEM_BYTES = 64 * 1024 * 1024
VMEM_LIMIT = V7X_VMEM_BYTES - 8 * 1024 * 1024


def _cparams(sem):
    return pltpu.CompilerParams(dimension_semantics=sem, vmem_limit_bytes=VMEM_LIMIT)


def _sigmoid(x):
    return 1.0 / (1.0 + jnp.exp(-x))


def _silu(x):
    return x * _sigmoid(x)


def _softplus(x):
    return jnp.maximum(x, 0.0) + jnp.log(1.0 + jnp.exp(-jnp.abs(x)))


IN_TM = 1024
IN_TN = 2048
NORM_ROWS = 64


def _rmsnorm_rows(x, g):
    ms = jnp.mean(x * x, axis=-1, keepdims=True)
    return x * lax.rsqrt(ms + RMS_EPS) * g


def _in_proj_kernel(x_ref, g_ref, w_ref, o_ref, h_ref):
    @pl.when(pl.program_id(1) == 0)
    def _():
        def body(r, c):
            rows = pl.ds(pl.multiple_of(r * NORM_ROWS, NORM_ROWS), NORM_ROWS)
            h_ref[rows, :] = _rmsnorm_rows(x_ref[rows, :], g_ref[...]).astype(BF16)
            return c
        lax.fori_loop(0, IN_TM // NORM_ROWS, body, 0, unroll=4)

    o_ref[...] = jnp.dot(h_ref[...], w_ref[...], preferred_element_type=F32).astype(o_ref.dtype)


def _in_proj(x2d, g_row, w_bf16):
    m = x2d.shape[0]
    return pl.pallas_call(
        _in_proj_kernel,
        grid=(m // IN_TM, PROJ_COLS // IN_TN),
        in_specs=[
            pl.BlockSpec((IN_TM, D_MODEL), lambda i, j: (i, 0)),
            pl.BlockSpec((1, D_MODEL), lambda i, j: (0, 0)),
            pl.BlockSpec((D_MODEL, IN_TN), lambda i, j: (0, j)),
        ],
        out_specs=pl.BlockSpec((IN_TM, IN_TN), lambda i, j: (i, j)),
        out_shape=jax.ShapeDtypeStruct((m, PROJ_COLS), BF16),
        scratch_shapes=[pltpu.VMEM((IN_TM, D_MODEL), BF16)],
        compiler_params=_cparams(("parallel", "arbitrary")),
        name="in_proj",
    )(x2d, g_row, w_bf16)


DN_HB = 8
DN_BB = 2
DN_TS = 512
DN_W = DN_HB * HEAD_DIM
DN_DOUBLINGS = CHUNK.bit_length() - 2
GATE_ROWS = 2 * DN_HEADS


def _dot_nt(a, b):
    return lax.dot_general(a, b, (((1,), (1,)), ((), ())), preferred_element_type=F32)


def _dot_tn(a, b):
    return lax.dot_general(a, b, (((0,), (0,)), ((), ())), preferred_element_type=F32)


def _causal_conv(xx, w, width):
    if width == 4:
        x1 = pltpu.roll(xx, 1, axis=0)
        near = xx * w[3:4, :] + x1 * w[2:3, :]
        far = xx * w[1:2, :] + x1 * w[0:1, :]
        return (near + pltpu.roll(far, 2, axis=0))[HALO:, :]
    acc = None
    for j in range(width):
        shift = width - 1 - j
        xs = xx if shift == 0 else pltpu.roll(xx, shift, axis=0)
        term = xs[HALO:, :] * w[j:j + 1, :]
        acc = term if acc is None else acc + term
    return acc


def _deltanet_kernel(q_ref, k_ref, v_ref, z_ref, ab_ref, cwq_ref, cwk_ref, cwv_ref,
                     alog_ref, dtb_ref, ng_ref, o_ref,
                     xpad_ref, state_ref):
    st = pl.program_id(2)

    @pl.when(st == 0)
    def _():
        xpad_ref[:, 0:HALO, :] = jnp.zeros((3 * DN_BB, HALO, DN_W), BF16)
        state_ref[...] = jnp.zeros_like(state_ref)

    for bb in range(DN_BB):
        for t, ref in enumerate((q_ref, k_ref, v_ref)):
            xpad_ref[3 * bb + t, HALO:, :] = ref[bb]

    row_i = lax.broadcasted_iota(jnp.int32, (CHUNK, CHUNK), 0)
    col_j = lax.broadcasted_iota(jnp.int32, (CHUNK, CHUNK), 1)
    causal = row_i >= col_j
    strict = row_i > col_j
    upper = (row_i <= col_j).astype(F32)
    eye = (row_i == col_j).astype(F32)
    lane = lax.broadcasted_iota(jnp.int32, (CHUNK, LANES), 1)
    gate_row = lax.broadcasted_iota(jnp.int32, (GATE_ROWS, CHUNK), 0)
    neg_decay_rate = -jnp.exp(alog_ref[...])
    cws = (cwq_ref[...], cwk_ref[...], cwv_ref[...])
    ng = ng_ref[...]

    def chunk_body(c, carry):
        r0 = pl.multiple_of(c * CHUNK, CHUNK)
        win = pl.ds(r0, CHUNK + HALO)
        rows = pl.ds(r0, CHUNK)
        conv, gates, z = [], [], []
        for bb in range(DN_BB):
            conv.append([_silu(_causal_conv(xpad_ref[3 * bb + t, win, :].astype(F32), cws[t], DN_CONV))
                         for t in range(3)])
            ab_t = ab_ref[bb, rows, :].astype(F32).T[:GATE_ROWS, :]
            g_cum_t = jnp.dot(neg_decay_rate * _softplus(ab_t + dtb_ref[...]), upper,
                              preferred_element_type=F32, precision=lax.Precision.HIGHEST)
            gates_t = jnp.where(gate_row < DN_HEADS, g_cum_t, _sigmoid(ab_t))
            gates.append(jnp.concatenate([gates_t, jnp.zeros((CHUNK - GATE_ROWS, CHUNK), F32)], axis=0).T)
            z.append(z_ref[bb, rows, :].astype(F32))

        units = [(bb, i) for bb in range(DN_BB) for i in range(DN_HB)]
        sls = [slice(i * HEAD_DIM, (i + 1) * HEAD_DIM) for i in range(DN_HB)]
        dot = functools.partial(jnp.dot, preferred_element_type=F32)

        g_col = [jnp.sum(jnp.where(lane == i, gates[bb], 0.0), axis=-1, keepdims=True) for bb, i in units]
        beta = [jnp.sum(jnp.where(lane == i + DN_HEADS, gates[bb], 0.0), axis=-1, keepdims=True)
                for bb, i in units]
        g_last = [g[CHUNK - 1:CHUNK, :] for g in g_col]
        eg = [jnp.exp(g) for g in g_col]
        ek = [jnp.exp(gl - g) for gl, g in zip(g_last, g_col)]
        g_b = [jnp.broadcast_to(g, (CHUNK, CHUNK)) for g in g_col]
        decay = [jnp.exp(jnp.where(causal, gb - gb.T, -jnp.inf)) for gb in g_b]

        qf = [conv[bb][0][:, sls[i]] for bb, i in units]
        kf = [conv[bb][1][:, sls[i]] for bb, i in units]
        vf = [conv[bb][2][:, sls[i]] for bb, i in units]
        q = [x * lax.rsqrt(jnp.sum(x * x, axis=-1, keepdims=True) + L2_EPS) * (HEAD_DIM ** -0.5) for x in qf]
        k = [x * lax.rsqrt(jnp.sum(x * x, axis=-1, keepdims=True) + L2_EPS) for x in kf]
        kb = [ki * bi for ki, bi in zip(k, beta)]
        a2 = [_dot_nt(jnp.concatenate([qi, kbi], axis=0).astype(BF16), ki.astype(BF16))
              for qi, kbi, ki in zip(q, kb, k)]
        attn = [a[:CHUNK] * d for a, d in zip(a2, decay)]
        n_mat = [jnp.where(strict, -(a[CHUNK:] * d), 0.0) for a, d in zip(a2, decay)]

        u_mat = [eye + n for n in n_mat]
        p_mat = [dot(n.astype(BF16), n.astype(BF16)) for n in n_mat]
        for _ in range(DN_DOUBLINGS - 1):
            up = [dot(jnp.concatenate([u.astype(BF16), p.astype(BF16)], axis=0), p.astype(BF16))
                  for u, p in zip(u_mat, p_mat)]
            u_mat = [u + x[:CHUNK] for u, x in zip(u_mat, up)]
            p_mat = [x[CHUNK:] for x in up]
        u_mat = [u + dot(u.astype(BF16), p.astype(BF16)) for u, p in zip(u_mat, p_mat)]
        uw = [dot(u.astype(BF16), jnp.concatenate([vi * bi, kbi * egi], axis=1).astype(BF16))
              for u, vi, bi, kbi, egi in zip(u_mat, vf, beta, kb, eg)]

        s_old = [state_ref[u] for u in range(len(units))]
        qw = [dot(jnp.concatenate([qi * egi, x[:, HEAD_DIM:]], axis=0).astype(BF16), s.astype(BF16))
              for qi, egi, x, s in zip(q, eg, uw, s_old)]
        v16 = [(x[:, :HEAD_DIM] - y[CHUNK:]).astype(BF16) for x, y in zip(uw, qw)]
        o = [y[:CHUNK] + dot(a.astype(BF16), v) for y, a, v in zip(qw, attn, v16)]
        s_new = [s * jnp.exp(gl) + _dot_tn((ki * eki).astype(BF16), v)
                 for s, gl, ki, eki, v in zip(s_old, g_last, k, ek, v16)]
        for u, (bb, i) in enumerate(units):
            state_ref[u] = s_new[u]
            on = o[u] * lax.rsqrt(jnp.mean(o[u] * o[u], axis=-1, keepdims=True) + RMS_EPS) * ng
            o_ref[bb, rows, sls[i]] = (on * _silu(z[bb][:, sls[i]])).astype(o_ref.dtype)
        return carry

    lax.fori_loop(0, DN_TS // CHUNK, chunk_body, 0)

    tail = pl.ds(DN_TS, HALO)
    head = pl.ds(0, HALO)
    for t in range(3 * DN_BB):
        xpad_ref[t, head, :] = xpad_ref[t, tail, :]


def _gate_rows(v):
    rows = jnp.zeros((GATE_ROWS, LANES), F32)
    return rows.at[:v.shape[0], :].set(jnp.broadcast_to(v.astype(F32)[:, None], (v.shape[0], LANES)))


def _deltanet(proj3, conv_w, a_log, dt_bias, ng_row):
    assert DN_HB == DN_HEADS
    b, s, _ = proj3.shape
    assert b % DN_BB == 0
    gate = pl.BlockSpec((GATE_ROWS, LANES), lambda bi, hg, st: (0, 0))
    qb, kb_, vb, zb = (off // DN_W for off in (OFF_DNQ, OFF_DNK, OFF_DNV, OFF_DNZ))
    act = lambda base: pl.BlockSpec((DN_BB, DN_TS, DN_W), lambda bi, hg, st: (bi, st, base + hg))
    cw = lambda base: pl.BlockSpec((DN_CONV, DN_W), lambda bi, hg, st: (0, base + hg))
    row = pl.BlockSpec((1, LANES), lambda bi, hg, st: (0, 0))
    return pl.pallas_call(
        _deltanet_kernel,
        grid=(b // DN_BB, DN_HEADS // DN_HB, s // DN_TS),
        in_specs=[
            act(qb), act(kb_), act(vb), act(zb),
            pl.BlockSpec((DN_BB, DN_TS, LANES), lambda bi, hg, st: (bi, st, OFF_AB // LANES)),
            cw(0), cw(DN_WIDTH // DN_W), cw(2 * DN_WIDTH // DN_W),
            gate, gate, row,
        ],
        out_specs=pl.BlockSpec((DN_BB, DN_TS, DN_W), lambda bi, hg, st: (bi, st, hg)),
        out_shape=jax.ShapeDtypeStruct((b, s, DN_WIDTH), BF16),
        scratch_shapes=[
            pltpu.VMEM((3 * DN_BB, DN_TS + HALO, DN_W), BF16),
            pltpu.VMEM((DN_BB * DN_HB, HEAD_DIM, HEAD_DIM), F32),
        ],
        compiler_params=_cparams(("parallel", "parallel", "arbitrary")),
        name="deltanet",
    )(proj3, proj3, proj3, proj3, proj3, conv_w, conv_w, conv_w, _gate_rows(a_log), _gate_rows(dt_bias), ng_row)


SWA_CAST_ROWS = 256
SWA_UNROLL = 4


def _swa_kernel(q_ref, k_ref, v_ref, o_ref, *scratch, seq, dil, slopes):
    n_blocks = seq // dil // SWA_BLOCK
    qi = lax.broadcasted_iota(jnp.int32, (SWA_BLOCK, SWA_BLOCK), 0)
    kj = lax.broadcasted_iota(jnp.int32, (SWA_BLOCK, SWA_BLOCK), 1)
    step_cur = (qi - kj).astype(F32)
    step_prev = step_cur + float(SWA_BLOCK)
    ok_cur = qi >= kj
    ok_prev = kj >= qi
    lane = lax.broadcasted_iota(jnp.int32, (SWA_BLOCK, LANES), 1)
    scale = HEAD_DIM ** -0.5
    heads = range(SWA_HEADS_PER_GROUP)
    sls = [slice(hh * HEAD_DIM, (hh + 1) * HEAD_DIM) for hh in heads]
    m_h = [float(slopes[hh]) * float(dil) for hh in heads]

    if dil > 1:
        (f32_ref,) = scratch

        def cast_body(c, carry):
            rows = pl.ds(pl.multiple_of(c * SWA_CAST_ROWS, SWA_CAST_ROWS), SWA_CAST_ROWS)
            for t, ref in enumerate((q_ref, k_ref, v_ref)):
                for hh in heads:
                    f32_ref[t * SWA_HEADS_PER_GROUP + hh, rows, :] = ref[0, rows, sls[hh]].astype(F32)
            return carry
        lax.fori_loop(0, seq // SWA_CAST_ROWS, cast_body, 0)

        def load(t, rows):
            return [f32_ref[t * SWA_HEADS_PER_GROUP + hh, rows, :].astype(BF16) for hh in heads]
    else:
        refs = (q_ref, k_ref, v_ref)

        def load(t, rows):
            return [refs[t][0, rows, sls[hh]] for hh in heads]

    def block_rows(it):
        r = it // n_blocks
        n = it % n_blocks
        base = n * (SWA_BLOCK * dil) + r
        prev_base = jnp.maximum(n - 1, 0) * (SWA_BLOCK * dil) + r
        if dil > 1:
            return n, pl.ds(base, SWA_BLOCK, stride=dil), pl.ds(prev_base, SWA_BLOCK, stride=dil)
        return (n, pl.ds(pl.multiple_of(base, SWA_BLOCK), SWA_BLOCK),
                pl.ds(pl.multiple_of(prev_base, SWA_BLOCK), SWA_BLOCK))

    def blocks_body(it2, carry):
        qn, kc, kp, vc, vp, ok_p, slope, out_rows = [], [], [], [], [], [], [], []
        for u in range(SWA_UNROLL):
            n, rows, prev_rows = block_rows(it2 * SWA_UNROLL + u)
            qn += load(0, rows)
            kc += load(1, rows)
            vc += load(2, rows)
            kp += load(1, prev_rows)
            vp += load(2, prev_rows)
            ok_p += [jnp.logical_and(ok_prev, n > 0)] * len(heads)
            slope += m_h
            out_rows.append(rows)

        s_cur = [jnp.where(ok_cur, _dot_nt(qh, kh) * scale - m * step_cur, -jnp.inf)
                 for qh, kh, m in zip(qn, kc, slope)]
        s_prev = [jnp.where(ok, _dot_nt(qh, kh) * scale - m * step_prev, -jnp.inf)
                  for ok, qh, kh, m in zip(ok_p, qn, kp, slope)]
        mx = [jnp.max(jnp.maximum(a, b), axis=-1, keepdims=True) for a, b in zip(s_cur, s_prev)]
        e_cur = [jnp.exp(a - m) for a, m in zip(s_cur, mx)]
        e_prev = [jnp.exp(b - m) for b, m in zip(s_prev, mx)]
        den = [jnp.sum(a + b, axis=-1, keepdims=True) for a, b in zip(e_cur, e_prev)]
        pv = [jnp.dot(a.astype(BF16), vch, preferred_element_type=F32)
              + jnp.dot(b.astype(BF16), vph, preferred_element_type=F32)
              for a, b, vch, vph in zip(e_cur, e_prev, vc, vp)]
        for u, rows in enumerate(out_rows):
            lse_tile = jnp.zeros((SWA_BLOCK, LANES), F32)
            for hh in heads:
                i = u * len(heads) + hh
                o_ref[0, hh, rows, :] = pv[i] * (1.0 / den[i])
                lse_tile = jnp.where(lane == hh, mx[i] + jnp.log(den[i]), lse_tile)
            o_ref[0, SWA_HEADS_PER_GROUP, rows, :] = lse_tile
        return carry

    lax.fori_loop(0, seq // SWA_BLOCK // SWA_UNROLL, blocks_body, 0)


def _swa_group(proj3, gi):
    batch, seq, _ = proj3.shape
    window, dil = SWA_GROUPS[gi]
    assert window // dil == SWA_BLOCK and seq % (dil * SWA_BLOCK) == 0
    slopes = 2.0 ** (-ALIBI_MAX_BIAS * np.arange(1, SWA_HEADS + 1) / SWA_HEADS)
    slopes = slopes.reshape(len(SWA_GROUPS), SWA_HEADS_PER_GROUP)[gi]

    def spec(which):
        cb = (OFF_SWA + which * SWA_HEADS * HEAD_DIM) // SWA_GROUP_WIDTH + gi
        return pl.BlockSpec((1, seq, SWA_GROUP_WIDTH), lambda bi: (bi, 0, cb))

    scratch = [pltpu.VMEM((3 * SWA_HEADS_PER_GROUP, seq, HEAD_DIM), F32)] if dil > 1 else []
    return pl.pallas_call(
        functools.partial(_swa_kernel, seq=seq, dil=dil, slopes=tuple(float(s) for s in slopes)),
        grid=(batch,),
        in_specs=[spec(0), spec(1), spec(2)],
        out_specs=pl.BlockSpec((1, SWA_OUT_SLOTS, seq, HEAD_DIM), lambda bi: (bi, 0, 0, 0)),
        out_shape=jax.ShapeDtypeStruct((batch, SWA_OUT_SLOTS, seq, HEAD_DIM), F32),
        scratch_shapes=scratch,
        compiler_params=_cparams(("parallel",)),
        name=f"swa_g{gi}",
    )(proj3, proj3, proj3)


MG_TM = 256


def _merge_kernel(x_ref, ydn_ref, gate_ref, scc_ref, scb_ref, scx_ref, scc_h_ref, scx_h_ref,
                  s0_ref, s1_ref, s2_ref, scw_ref, wdn_ref, wsc_ref, wswa_ref, wout_ref, o_ref,
                  *, tiles_per_seq):
    i = pl.program_id(0)
    first = (i % tiles_per_seq) == 0

    cx_cur = scc_ref[...].astype(F32) * scx_ref[...].astype(F32)
    cx_halo = scc_h_ref[...].astype(F32) * scx_h_ref[...].astype(F32)
    cx_halo = jnp.where(first, 0.0, cx_halo)
    cx = jnp.concatenate([cx_halo, cx_cur], axis=0)
    y_sc = scb_ref[...].astype(F32) * _causal_conv(cx, scw_ref[...], SC_CONV)

    outs = (s0_ref, s1_ref, s2_ref)
    lses = [o[0, SWA_HEADS_PER_GROUP] for o in outs]
    mx = jnp.maximum(jnp.maximum(lses[0], lses[1]), lses[2])
    es = [jnp.exp(l - mx) for l in lses]
    inv = 1.0 / (es[0] + es[1] + es[2])
    heads = []
    for hh in range(SWA_HEADS_PER_GROUP):
        acc = None
        for gi in range(3):
            wt = (es[gi] * inv)[:, hh:hh + 1]
            term = wt * outs[gi][0, hh]
            acc = term if acc is None else acc + term
        heads.append(acc)
    y_swa = jnp.concatenate(heads, axis=1)

    gates = gate_ref[...].astype(F32)
    merged = (_sigmoid(gates[:, 0:D_MODEL])
              * jnp.dot(ydn_ref[...], wdn_ref[...], preferred_element_type=F32)
              + _sigmoid(gates[:, D_MODEL:2 * D_MODEL])
              * jnp.dot(y_sc.astype(BF16), wsc_ref[...], preferred_element_type=F32)
              + _sigmoid(gates[:, 2 * D_MODEL:3 * D_MODEL])
              * jnp.dot(y_swa.astype(BF16), wswa_ref[...], preferred_element_type=F32))
    o_ref[...] = x_ref[...] + jnp.dot(merged.astype(BF16), wout_ref[...], preferred_element_type=F32)


def _merge(x2d, ydn2, proj2, swa_outs, sc_conv_w, wdn, wsc, wswa, wout, seq):
    m = x2d.shape[0]
    tiles_per_seq = seq // MG_TM
    halo_blocks = MG_TM // HALO
    rows = lambda w, cb: pl.BlockSpec((MG_TM, w), lambda i: (i, cb))
    halo = lambda cb: pl.BlockSpec((HALO, SC_WIDTH), lambda i: (jnp.maximum(i * halo_blocks - 1, 0), cb))
    full = lambda a: pl.BlockSpec(a.shape, lambda i: (0, 0), pipeline_mode=pl.Buffered(1))
    swa = pl.BlockSpec((1, SWA_OUT_SLOTS, MG_TM, HEAD_DIM),
                       lambda i: (i // tiles_per_seq, 0, i % tiles_per_seq, 0))
    return pl.pallas_call(
        functools.partial(_merge_kernel, tiles_per_seq=tiles_per_seq),
        grid=(m // MG_TM,),
        in_specs=[
            rows(D_MODEL, 0),
            rows(DN_WIDTH, 0),
            rows(3 * D_MODEL, OFF_GATE // (3 * D_MODEL)),
            rows(SC_WIDTH, OFF_SCC // SC_WIDTH), rows(SC_WIDTH, OFF_SCB // SC_WIDTH),
            rows(SC_WIDTH, OFF_SCX // SC_WIDTH),
            halo(OFF_SCC // SC_WIDTH), halo(OFF_SCX // SC_WIDTH),
            swa, swa, swa,
            full(sc_conv_w), full(wdn), full(wsc), full(wswa), full(wout),
        ],
        out_specs=rows(D_MODEL, 0),
        out_shape=jax.ShapeDtypeStruct((m, D_MODEL), F32),
        compiler_params=_cparams(("parallel",)),
        name="merge",
    )(x2d, ydn2, proj2, proj2, proj2, proj2, proj2, proj2, *swa_outs, sc_conv_w, wdn, wsc, wswa, wout)


RT_TM = 1024
ROUTE_LANE0 = MOE_GROUPS
R_E1, R_E2, R_W1, R_W2, R_RANK1, R_RANK2 = range(6)
HALF = D_MODEL // 2


def _pack_pair(lo, hi):
    lo_b = lax.bitcast_convert_type(lo.astype(BF16).astype(F32), jnp.uint32)
    hi_b = lax.bitcast_convert_type(hi.astype(BF16).astype(F32), jnp.uint32)
    return (hi_b & jnp.uint32(0xFFFF0000)) | (lo_b >> jnp.uint32(16))


def _unpack_pair(p):
    lo = lax.bitcast_convert_type(p << jnp.uint32(16), F32)
    hi = lax.bitcast_convert_type(p & jnp.uint32(0xFFFF0000), F32)
    return lo, hi


def _router_kernel(x_ref, g_ref, wrh_ref, wrl_ref, br_ref, hp_ref, route_ref, cnt_ref, carry_ref):
    @pl.when(pl.program_id(0) == 0)
    def _():
        carry_ref[...] = jnp.zeros_like(carry_ref)

    h = _rmsnorm_rows(x_ref[...], g_ref[...])
    hp_ref[...] = _pack_pair(h[:, :HALF], h[:, HALF:])
    h_hi = h.astype(BF16)
    h_lo = (h - h_hi.astype(F32)).astype(BF16)
    w_hi, w_lo = wrh_ref[...], wrl_ref[...]
    dot = functools.partial(jnp.dot, preferred_element_type=F32)
    logits = dot(h_hi, w_hi) + dot(h_lo, w_hi) + dot(h_hi, w_lo) + br_ref[...]
    lane = lax.broadcasted_iota(jnp.int32, logits.shape, 1)
    big = jnp.int32(LANES)
    neg = -jnp.inf

    is_grp = lane < MOE_GROUPS
    gl = jnp.where(is_grp, logits, neg)
    gmax = jnp.max(gl, axis=-1, keepdims=True)
    grp = jnp.min(jnp.where(gl == gmax, lane, big), axis=-1, keepdims=True)
    g_w = 1.0 / jnp.sum(jnp.exp(gl - gmax), axis=-1, keepdims=True)

    lo = ROUTE_LANE0 + grp * MOE_EPG
    in_grp = jnp.logical_and(lane >= lo, lane < lo + MOE_EPG)
    el = jnp.where(in_grp, logits, neg)
    m1 = jnp.max(el, axis=-1, keepdims=True)
    i1 = jnp.min(jnp.where(el == m1, lane, big), axis=-1, keepdims=True)
    el2 = jnp.where(lane == i1, neg, el)
    m2 = jnp.max(el2, axis=-1, keepdims=True)
    i2 = jnp.min(jnp.where(el2 == m2, lane, big), axis=-1, keepdims=True)
    e2 = jnp.exp(m2 - m1)
    w1 = g_w / (1.0 + e2)
    w2 = g_w * e2 / (1.0 + e2)

    tm = logits.shape[0]
    onehot = jnp.where(jnp.logical_or(lane == i1, lane == i2), 1.0, 0.0)
    earlier = (lax.broadcasted_iota(jnp.int32, (tm, tm), 0) > lax.broadcasted_iota(jnp.int32, (tm, tm), 1))
    before = carry_ref[...] + jnp.dot(jnp.where(earlier, 1.0, 0.0).astype(BF16), onehot.astype(BF16),
                                      preferred_element_type=F32)
    rank1 = jnp.sum(jnp.where(lane == i1, before, 0.0), axis=-1, keepdims=True)
    rank2 = jnp.sum(jnp.where(lane == i2, before, 0.0), axis=-1, keepdims=True)
    carry_ref[...] += jnp.sum(onehot, axis=0, keepdims=True)
    cnt_ref[...] = carry_ref[...]

    rec = jnp.zeros_like(logits)
    for ln, val in ((R_E1, (i1 - ROUTE_LANE0).astype(F32)), (R_E2, (i2 - ROUTE_LANE0).astype(F32)),
                    (R_W1, w1), (R_W2, w2), (R_RANK1, rank1), (R_RANK2, rank2)):
        rec = jnp.where(lane == ln, val, rec)
    route_ref[...] = rec


def _router(x2d, g_row, w_route_hi, w_route_lo, b_route):
    m = x2d.shape[0]
    return pl.pallas_call(
        _router_kernel,
        grid=(m // RT_TM,),
        in_specs=[
            pl.BlockSpec((RT_TM, D_MODEL), lambda i: (i, 0)),
            pl.BlockSpec((1, D_MODEL), lambda i: (0, 0)),
            pl.BlockSpec((D_MODEL, LANES), lambda i: (0, 0)),
            pl.BlockSpec((D_MODEL, LANES), lambda i: (0, 0)),
            pl.BlockSpec((1, LANES), lambda i: (0, 0)),
        ],
        out_specs=[pl.BlockSpec((RT_TM, HALF), lambda i: (i, 0)),
                   pl.BlockSpec((RT_TM, LANES), lambda i: (i, 0)),
                   pl.BlockSpec((1, LANES), lambda i: (0, 0))],
        out_shape=[jax.ShapeDtypeStruct((m, HALF), jnp.uint32), jax.ShapeDtypeStruct((m, LANES), F32),
                   jax.ShapeDtypeStruct((1, LANES), F32)],
        scratch_shapes=[pltpu.VMEM((1, LANES), F32)],
        compiler_params=_cparams(("arbitrary",)),
        name="router",
    )(x2d, g_row, w_route_hi, w_route_lo, b_route)


EXP_TM = 512


def _n_row_tiles(n_tokens):
    return 2 * n_tokens // EXP_TM + MOE_EXPERTS


def _positions_kernel(route_ref, offs_ref, pos_ref):
    rec = route_ref[...]
    lane = lax.broadcasted_iota(jnp.int32, rec.shape, 1)
    offs = offs_ref[...]

    def first_row(e):
        return jnp.sum(jnp.where(lane == e.astype(jnp.int32), offs, 0.0), axis=-1, keepdims=True)

    pos1 = first_row(rec[:, R_E1:R_E1 + 1]) + rec[:, R_RANK1:R_RANK1 + 1]
    pos2 = first_row(rec[:, R_E2:R_E2 + 1]) + rec[:, R_RANK2:R_RANK2 + 1]
    pos_ref[...] = jnp.where(lane == 0, pos1, jnp.where(lane == 1, pos2, 0.0)).astype(jnp.int32)


def _moe_plan(route, cnt):
    m = route.shape[0]
    n_tiles = _n_row_tiles(m)
    counts = cnt[0, ROUTE_LANE0:ROUTE_LANE0 + MOE_EXPERTS].astype(jnp.int32)
    padded = (counts + EXP_TM - 1) // EXP_TM * EXP_TM
    ends = jnp.cumsum(padded)
    offs = ends - padded
    pos = pl.pallas_call(
        _positions_kernel,
        grid=(m // RT_TM,),
        in_specs=[pl.BlockSpec((RT_TM, LANES), lambda i: (i, 0)), pl.BlockSpec((1, LANES), lambda i: (0, 0))],
        out_specs=pl.BlockSpec((RT_TM, LANES), lambda i: (i, 0)),
        out_shape=jax.ShapeDtypeStruct((m, LANES), jnp.int32),
        compiler_params=_cparams(("parallel",)),
        name="moe_positions",
    )(route, _lane_row(offs))
    n_active = ends[-1] // EXP_TM
    tile_row = jnp.maximum(jnp.minimum(jnp.arange(n_tiles), n_active - 1), 0) * EXP_TM
    tile_expert = jnp.minimum(jnp.sum(ends[None, :] <= tile_row[:, None], axis=1), MOE_EXPERTS - 1)
    used = counts > 0
    ids = jnp.where(used, jnp.arange(MOE_EXPERTS), MOE_EXPERTS)
    next_used = jnp.concatenate([lax.cummin(ids, reverse=True)[1:], jnp.full((1,), MOE_EXPERTS, ids.dtype)])
    parity = (jnp.cumsum(used.astype(jnp.int32)) - 1) % 2
    i32 = lambda a: a.astype(jnp.int32)
    return dict(pos1=pos[:, 0], pos2=pos[:, 1], counts=counts, offs=i32(offs),
                tile_expert=i32(tile_expert), n_active=i32(n_active).reshape(1),
                next_expert=i32(next_used[tile_expert]), parity=i32(parity[tile_expert]))


DSP_TB = 512
ROW_UNROLL = 8


def _dispatch_kernel(cnt_ref, offs_ref, nact_ref, pos1_ref, pos2_ref, hp_ref, xs_ref, zero_ref, sem, zero_sem):
    n_tiles = xs_ref.shape[0] // EXP_TM

    @pl.when(pl.program_id(0) == 0)
    def _():
        zero_ref[...] = jnp.zeros_like(zero_ref)

        def pad_pieces(e):
            pad = (-cnt_ref[e]) & (EXP_TM - 1)
            first = offs_ref[e] + cnt_ref[e]
            head = jnp.minimum((-first) & (SUBLANES - 1), pad)
            for k in range(SUBLANES - 1):
                yield k < head, pltpu.make_async_copy(zero_ref.at[pl.ds(0, 1), :],
                                                      xs_ref.at[pl.ds(first + k, 1), :], zero_sem)
            body = pad - head
            b = EXP_TM // 2
            while b >= SUBLANES:
                row = pl.multiple_of(first + head + (body & ~(2 * b - 1)), SUBLANES)
                yield (body & b) != 0, pltpu.make_async_copy(zero_ref.at[pl.ds(0, b), :],
                                                             xs_ref.at[pl.ds(row, b), :], zero_sem)
                b //= 2

        def tile_copy(t):
            rows = pl.ds(pl.multiple_of(t * EXP_TM, EXP_TM), EXP_TM)
            return pltpu.make_async_copy(zero_ref, xs_ref.at[rows, :], zero_sem)

        def for_all(method):
            def per_expert(e, c):
                for cond, cp in pad_pieces(e):
                    pl.when(cond)(getattr(cp, method))
                return c
            lax.fori_loop(0, MOE_EXPERTS, per_expert, 0)

            def per_tile(t, c):
                getattr(tile_copy(t), method)()
                return c
            lax.fori_loop(nact_ref[0], n_tiles, per_tile, 0)

        for_all("start")
        for_all("wait")

    def start(j8, c):
        for u in range(ROW_UNROLL):
            j = j8 * ROW_UNROLL + u
            src = hp_ref.at[pl.ds(j, 1), :]
            pltpu.make_async_copy(src, xs_ref.at[pl.ds(pos1_ref[0, 0, j], 1), :], sem).start(priority=0)
            pltpu.make_async_copy(src, xs_ref.at[pl.ds(pos2_ref[0, 0, j], 1), :], sem).start(priority=1)
        return c

    lax.fori_loop(0, DSP_TB // ROW_UNROLL, start, 0)
    for _ in range(2):
        pltpu.make_async_copy(hp_ref, xs_ref.at[pl.ds(0, DSP_TB), :], sem).wait()


def _dispatch(hp, pos1, pos2, counts, offs, n_active):
    m = hp.shape[0]
    n_rows = _n_row_tiles(m) * EXP_TM
    pos_spec = pl.BlockSpec((1, 1, DSP_TB), lambda i, *_: (i, 0, 0), memory_space=pltpu.SMEM)
    return pl.pallas_call(
        _dispatch_kernel,
        grid_spec=pltpu.PrefetchScalarGridSpec(
            num_scalar_prefetch=3,
            grid=(m // DSP_TB,),
            in_specs=[pos_spec, pos_spec, pl.BlockSpec((DSP_TB, HALF), lambda i, *_: (i, 0))],
            out_specs=pl.BlockSpec(memory_space=pl.ANY),
            scratch_shapes=[pltpu.VMEM((EXP_TM, HALF), jnp.uint32), pltpu.SemaphoreType.DMA(()),
                            pltpu.SemaphoreType.DMA(())],
        ),
        out_shape=jax.ShapeDtypeStruct((n_rows, HALF), jnp.uint32),
        compiler_params=_cparams(("arbitrary",)),
        name="moe_dispatch",
    )(counts, offs, n_active, pos1.reshape(m // DSP_TB, 1, DSP_TB), pos2.reshape(m // DSP_TB, 1, DSP_TB), hp)


W_CAST_ROWS = 256


def _cast_rows(src_ref, dst_ref):
    n_rows = dst_ref.shape[0]
    slab = min(W_CAST_ROWS, n_rows)

    def body(r, c):
        rows = pl.ds(pl.multiple_of(r * slab, slab), slab)
        dst_ref[rows, :] = src_ref[rows, :].astype(BF16)
        return c
    lax.fori_loop(0, n_rows // slab, body, 0)


def _expert_kernel(te_ref, nact_ref, next_ref, par_ref, xs_ref, wg_hbm, wu_hbm, wd_hbm, ys_ref,
                   sg_ref, su_ref, sd_ref, wg16_ref, wu16_ref, wd16_ref, sem, *, layer):
    i = pl.program_id(0)
    active = i < nact_ref[0]
    expert = te_ref[i]
    slot = par_ref[i]
    new_expert = jnp.logical_or(i == 0, expert != te_ref[jnp.maximum(i - 1, 0)])

    def copies(e, s):
        return [pltpu.make_async_copy(hbm.at[layer, e], stage.at[s], sem.at[s])
                for hbm, stage in ((wg_hbm, sg_ref), (wu_hbm, su_ref), (wd_hbm, sd_ref))]

    @pl.when(jnp.logical_and(active, i == 0))
    def _():
        for cp in copies(expert, slot):
            cp.start()

    @pl.when(jnp.logical_and(active, new_expert))
    def _():
        for cp in copies(expert, slot):
            cp.wait()

        @pl.when(next_ref[i] < MOE_EXPERTS)
        def _():
            for cp in copies(next_ref[i], 1 - slot):
                cp.start()

        _cast_rows(sg_ref.at[slot], wg16_ref)
        _cast_rows(su_ref.at[slot], wu16_ref)
        _cast_rows(sd_ref.at[slot], wd16_ref)

    @pl.when(active)
    def _():
        lo, hi = _unpack_pair(xs_ref[...])
        lo, hi = lo.astype(BF16), hi.astype(BF16)
        dot = functools.partial(jnp.dot, preferred_element_type=F32)
        gate = dot(lo, wg16_ref[:HALF, :]) + dot(hi, wg16_ref[HALF:, :])
        up = dot(lo, wu16_ref[:HALF, :]) + dot(hi, wu16_ref[HALF:, :])
        y = dot((_silu(gate) * up).astype(BF16), wd16_ref[...])
        ys_ref[...] = _pack_pair(y[:, :HALF], y[:, HALF:])

    @pl.when(jnp.logical_not(active))
    def _():
        ys_ref[...] = jnp.zeros_like(ys_ref)


def _experts(xs, tile_expert, n_active, next_expert, parity, wg, wu, wd, layer):
    n_tiles = xs.shape[0] // EXP_TM
    rows = pl.BlockSpec((EXP_TM, HALF),
                        lambda i, te, na, nx, par: (jnp.maximum(jnp.minimum(i, na[0] - 1), 0), 0))
    out_rows = pl.BlockSpec((EXP_TM, HALF), lambda i, te, na, nx, par: (i, 0))
    hbm = pl.BlockSpec(memory_space=pl.ANY)
    return pl.pallas_call(
        functools.partial(_expert_kernel, layer=layer),
        grid_spec=pltpu.PrefetchScalarGridSpec(
            num_scalar_prefetch=4,
            grid=(n_tiles,),
            in_specs=[rows, hbm, hbm, hbm],
            out_specs=out_rows,
            scratch_shapes=[pltpu.VMEM((2, D_MODEL, MOE_FF), F32), pltpu.VMEM((2, D_MODEL, MOE_FF), F32),
                            pltpu.VMEM((2, MOE_FF, D_MODEL), F32),
                            pltpu.VMEM((D_MODEL, MOE_FF), BF16), pltpu.VMEM((D_MODEL, MOE_FF), BF16),
                            pltpu.VMEM((MOE_FF, D_MODEL), BF16),
                            pltpu.SemaphoreType.DMA((2,))],
        ),
        out_shape=jax.ShapeDtypeStruct(xs.shape, jnp.uint32),
        compiler_params=_cparams(("arbitrary",)),
        name="moe_experts",
    )(tile_expert, n_active, next_expert, parity, xs, wg, wu, wd)


CMB_TC = 512


def _combine_kernel(pos1_ref, pos2_ref, pos1_next_ref, pos2_next_ref, x_ref, route_ref, g_ref, ys_ref,
                    o_ref, buf_ref, sem, *, final_norm):
    i = pl.program_id(0)
    slot = i % 2

    def gather(p1_ref, p2_ref, s):
        def start(j8, c):
            for u in range(ROW_UNROLL):
                j = j8 * ROW_UNROLL + u
                for k, p_ref in enumerate((p1_ref, p2_ref)):
                    pltpu.make_async_copy(ys_ref.at[pl.ds(p_ref[0, 0, j], 1), :],
                                          buf_ref.at[s, k, pl.ds(j, 1), :], sem.at[s]).start(priority=k)
            return c
        lax.fori_loop(0, CMB_TC // ROW_UNROLL, start, 0)

    @pl.when(i == 0)
    def _():
        gather(pos1_ref, pos2_ref, 0)

    @pl.when(i + 1 < pl.num_programs(0))
    def _():
        gather(pos1_next_ref, pos2_next_ref, 1 - slot)

    for k in range(2):
        pltpu.make_async_copy(ys_ref.at[pl.ds(0, CMB_TC), :], buf_ref.at[slot, k], sem.at[slot]).wait()

    rec = route_ref[...]
    w1, w2 = rec[:, R_W1:R_W1 + 1], rec[:, R_W2:R_W2 + 1]
    lo1, hi1 = _unpack_pair(buf_ref[slot, 0])
    lo2, hi2 = _unpack_pair(buf_ref[slot, 1])
    out_lo = x_ref[:, :HALF] + w1 * lo1 + w2 * lo2
    out_hi = x_ref[:, HALF:] + w1 * hi1 + w2 * hi2
    if final_norm:
        ms = (jnp.sum(out_lo * out_lo, axis=-1, keepdims=True)
              + jnp.sum(out_hi * out_hi, axis=-1, keepdims=True)) * (1.0 / D_MODEL)
        inv = lax.rsqrt(ms + RMS_EPS)
        out_lo = out_lo * inv * g_ref[:, :HALF]
        out_hi = out_hi * inv * g_ref[:, HALF:]
    o_ref[:, :HALF] = out_lo
    o_ref[:, HALF:] = out_hi


def _combine(x2d, route, ys, pos1, pos2, g_row, final_norm):
    m = x2d.shape[0]
    n_steps = m // CMB_TC
    pos_spec = pl.BlockSpec((1, 1, CMB_TC), lambda i: (i, 0, 0), memory_space=pltpu.SMEM)
    next_spec = pl.BlockSpec((1, 1, CMB_TC), lambda i: (jnp.minimum(i + 1, n_steps - 1), 0, 0),
                             memory_space=pltpu.SMEM)
    pos1, pos2 = pos1.reshape(n_steps, 1, CMB_TC), pos2.reshape(n_steps, 1, CMB_TC)
    return pl.pallas_call(
        functools.partial(_combine_kernel, final_norm=final_norm),
        grid=(n_steps,),
        in_specs=[pos_spec, pos_spec, next_spec, next_spec,
                  pl.BlockSpec((CMB_TC, D_MODEL), lambda i: (i, 0)),
                  pl.BlockSpec((CMB_TC, LANES), lambda i: (i, 0)),
                  pl.BlockSpec((1, D_MODEL), lambda i: (0, 0)),
                  pl.BlockSpec(memory_space=pl.ANY)],
        out_specs=pl.BlockSpec((CMB_TC, D_MODEL), lambda i: (i, 0)),
        out_shape=jax.ShapeDtypeStruct((m, D_MODEL), F32),
        scratch_shapes=[pltpu.VMEM((2, 2, CMB_TC, HALF), jnp.uint32), pltpu.SemaphoreType.DMA((2,))],
        compiler_params=_cparams(("arbitrary",)),
        name="moe_combine",
    )(pos1, pos2, pos1, pos2, x2d, route, g_row, ys)


RL_TN = 512
RL_PIECE = 256
SRC_A_END = 4096
SRC_AB = 16
SRC_COLS = 17936


def _relayout_plan():
    n_rest = (SRC_COLS - SRC_A_END - SRC_AB) // RL_TN
    rest = lambda s: SRC_A_END + SRC_AB + RL_TN * s
    starts = ([rest(s) for s in range(15, n_rest)]
              + [RL_TN * a for a in range(SRC_A_END // RL_TN)]
              + [rest(s) for s in range(15)]
              + [SRC_A_END])
    assert len(starts) * RL_TN == PROJ_COLS and all(s % 16 == 0 and s + RL_TN <= SRC_COLS for s in starts)
    return np.asarray(starts, np.int32)


def _relayout_kernel(start_ref, wt_ref, o_ref, buf_ref, sem, *, layer):
    t = pl.program_id(0)
    n_tiles = pl.num_programs(0)
    slot = t % 2

    def fetch(tile, s):
        first = pl.multiple_of(start_ref[tile], 16)
        pltpu.make_async_copy(wt_ref.at[layer, pl.ds(first, RL_TN), :], buf_ref.at[s], sem.at[s]).start()

    @pl.when(t == 0)
    def _():
        fetch(0, 0)

    @pl.when(t + 1 < n_tiles)
    def _():
        fetch(t + 1, 1 - slot)

    pltpu.make_async_copy(wt_ref.at[layer, pl.ds(0, RL_TN), :], buf_ref.at[slot], sem.at[slot]).wait()

    row = lax.broadcasted_iota(jnp.int32, (RL_TN, RL_PIECE), 0)
    keep = jnp.logical_or(t + 1 < n_tiles, row < SRC_AB)
    for c in range(D_MODEL // RL_PIECE):
        cols = slice(c * RL_PIECE, (c + 1) * RL_PIECE)
        piece = jnp.where(keep, buf_ref[slot, :, cols], 0.0)
        o_ref[cols, :] = piece.T.astype(BF16)


def _prep_w_in(w_all, layer):
    wt = jnp.swapaxes(w_all, 1, 2)
    return pl.pallas_call(
        functools.partial(_relayout_kernel, layer=layer),
        grid_spec=pltpu.PrefetchScalarGridSpec(
            num_scalar_prefetch=1,
            grid=(PROJ_COLS // RL_TN,),
            in_specs=[pl.BlockSpec(memory_space=pl.ANY)],
            out_specs=pl.BlockSpec((D_MODEL, RL_TN), lambda t, starts: (0, t)),
            scratch_shapes=[pltpu.VMEM((2, RL_TN, D_MODEL), F32), pltpu.SemaphoreType.DMA((2,))],
        ),
        out_shape=jax.ShapeDtypeStruct((D_MODEL, PROJ_COLS), BF16),
        compiler_params=_cparams(("arbitrary",)),
        name="w_in_relayout",
    )(jnp.asarray(_relayout_plan()), wt)


def _lane_row(v):
    return jnp.zeros((1, LANES), F32).at[0, :v.shape[0]].set(v.astype(F32))


def _layer(x2d, batch, seq, p, final_g_row):
    proj2 = _in_proj(x2d, p["norm_mix_g"], p["w_in"])
    proj3 = proj2.reshape(batch, seq, PROJ_COLS)
    ydn = _deltanet(proj3, p["dn_conv_w"], p["dn_a_log"], p["dn_dt_bias"], p["dn_norm_g"])
    swa_outs = [_swa_group(proj3, gi) for gi in range(len(SWA_GROUPS))]
    x2d = _merge(x2d, ydn.reshape(batch * seq, DN_WIDTH), proj2, swa_outs, p["sc_conv_w"],
                 p["w_branch_dn"], p["w_branch_sc"], p["w_branch_swa"], p["w_out"], seq)
    hp, route, cnt = _router(x2d, p["norm_ffn_g"], p["w_route_hi"], p["w_route_lo"], p["b_route"])
    plan = _moe_plan(route, cnt)
    xs = _dispatch(hp, plan["pos1"], plan["pos2"], plan["counts"], plan["offs"], plan["n_active"])
    ys = _experts(xs, plan["tile_expert"], plan["n_active"], plan["next_expert"], plan["parity"],
                  p["expert_w_gate"], p["expert_w_up"], p["expert_w_down"], p["layer"])
    is_last = final_g_row is not None
    g_row = final_g_row if is_last else p["norm_ffn_g"]
    return _combine(x2d, route, ys, plan["pos1"], plan["pos2"], g_row, is_last)


def kernel(x, norm_mix_g, w_in, dn_conv_w, dn_a_log, dn_dt_bias, dn_norm_g, sc_conv_w, w_branch_dn, w_branch_sc, w_branch_swa, w_out, norm_ffn_g, router_group_w, router_group_b, router_expert_w, router_expert_b, expert_w_gate, expert_w_up, expert_w_down, final_norm_g):
    batch, seq, _ = x.shape
    depth = w_in.shape[0]
    x2d = x.reshape(batch * seq, D_MODEL)
    for l in range(depth):
        w_route = jnp.concatenate([router_group_w[l], router_expert_w[l]], axis=1)
        w_route = jnp.pad(w_route, ((0, 0), (0, LANES - w_route.shape[1])))
        p = dict(
            norm_mix_g=norm_mix_g[l].reshape(1, D_MODEL),
            w_in=_prep_w_in(w_in, l),
            dn_conv_w=dn_conv_w[l],
            dn_a_log=dn_a_log[l],
            dn_dt_bias=dn_dt_bias[l],
            dn_norm_g=dn_norm_g[l].reshape(1, HEAD_DIM),
            sc_conv_w=sc_conv_w[l],
            w_branch_dn=w_branch_dn[l].astype(BF16),
            w_branch_sc=w_branch_sc[l].astype(BF16),
            w_branch_swa=w_branch_swa[l].astype(BF16),
            w_out=w_out[l].astype(BF16),
            norm_ffn_g=norm_ffn_g[l].reshape(1, D_MODEL),
            w_route_hi=w_route.astype(BF16),
            w_route_lo=(w_route - w_route.astype(BF16).astype(F32)).astype(BF16),
            b_route=_lane_row(jnp.concatenate([router_group_b[l], router_expert_b[l]])),
            expert_w_gate=expert_w_gate,
            expert_w_up=expert_w_up,
            expert_w_down=expert_w_down,
            layer=l,
        )
        final_g_row = final_norm_g.reshape(1, D_MODEL) if l == depth - 1 else None
        x2d = _layer(x2d, batch, seq, p, final_g_row)
    return x2d.reshape(batch, seq, D_MODEL)
```

```python
import functools

import jax
import jax.numpy as jnp
import numpy as np
from jax import lax
from jax.experimental import pallas as pl
from jax.experimental.pallas import tpu as pltpu

F32 = jnp.float32
BF16 = jnp.bfloat16

D_MODEL = 2048
RMS_EPS = 1e-6
L2_EPS = 1e-6

DN_HEADS = 8
HEAD_DIM = 128
DN_WIDTH = DN_HEADS * HEAD_DIM
DN_CONV = 4
SC_WIDTH = 1024
SC_CONV = 3
SWA_GROUPS = ((128, 1), (512, 4), (2048, 16))
SWA_HEADS_PER_GROUP = 4
SWA_HEADS = 12
SWA_GROUP_WIDTH = SWA_HEADS_PER_GROUP * HEAD_DIM
SWA_BLOCK = 128
ALIBI_MAX_BIAS = 8.0
MOE_GROUPS = 4
MOE_EPG = 8
MOE_EXPERTS = 32
MOE_FF = 512

OFF_GATE = 0
OFF_DNQ = 6144
OFF_DNK = OFF_DNQ + DN_WIDTH
OFF_DNV = OFF_DNK + DN_WIDTH
OFF_DNZ = OFF_DNV + DN_WIDTH
OFF_SCC = OFF_DNZ + DN_WIDTH
OFF_SCB = OFF_SCC + SC_WIDTH
OFF_SCX = OFF_SCB + SC_WIDTH
OFF_SWA = OFF_SCX + SC_WIDTH
OFF_AB = OFF_SWA + 3 * SWA_HEADS * HEAD_DIM
PROJ_COLS = 18432

LANES = 128
SUBLANES = 8
SWA_OUT_SLOTS = SWA_HEADS_PER_GROUP + 1
CHUNK = 128
HALO = 16

V7X_VMEM_BYTES = 64 * 1024 * 1024
VMEM_LIMIT = V7X_VMEM_BYTES - 8 * 1024 * 1024


def _cparams(sem):
    return pltpu.CompilerParams(dimension_semantics=sem, vmem_limit_bytes=VMEM_LIMIT)


def _sigmoid(x):
    return 1.0 / (1.0 + jnp.exp(-x))


def _silu(x):
    return x * _sigmoid(x)


def _softplus(x):
    return jnp.maximum(x, 0.0) + jnp.log(1.0 + jnp.exp(-jnp.abs(x)))


IN_TM = 1024
IN_TN = 2048
NORM_ROWS = 64


def _rmsnorm_rows(x, g):
    ms = jnp.mean(x * x, axis=-1, keepdims=True)
    return x * lax.rsqrt(ms + RMS_EPS) * g


def _in_proj_kernel(x_ref, g_ref, w_ref, o_ref, h_ref):
    @pl.when(pl.program_id(1) == 0)
    def _():
        def body(r, c):
            rows = pl.ds(pl.multiple_of(r * NORM_ROWS, NORM_ROWS), NORM_ROWS)
            h_ref[rows, :] = _rmsnorm_rows(x_ref[rows, :], g_ref[...]).astype(BF16)
            return c
        lax.fori_loop(0, IN_TM // NORM_ROWS, body, 0, unroll=4)

    o_ref[...] = jnp.dot(h_ref[...], w_ref[...], preferred_element_type=F32).astype(o_ref.dtype)


def _in_proj(x2d, g_row, w_bf16):
    m = x2d.shape[0]
    return pl.pallas_call(
        _in_proj_kernel,
        grid=(m // IN_TM, PROJ_COLS // IN_TN),
        in_specs=[
            pl.BlockSpec((IN_TM, D_MODEL), lambda i, j: (i, 0)),
            pl.BlockSpec((1, D_MODEL), lambda i, j: (0, 0)),
            pl.BlockSpec((D_MODEL, IN_TN), lambda i, j: (0, j)),
        ],
        out_specs=pl.BlockSpec((IN_TM, IN_TN), lambda i, j: (i, j)),
        out_shape=jax.ShapeDtypeStruct((m, PROJ_COLS), BF16),
        scratch_shapes=[pltpu.VMEM((IN_TM, D_MODEL), BF16)],
        compiler_params=_cparams(("parallel", "arbitrary")),
        name="in_proj",
    )(x2d, g_row, w_bf16)


DN_HB = 8
DN_BB = 2
DN_TS = 512
DN_W = DN_HB * HEAD_DIM
DN_DOUBLINGS = CHUNK.bit_length() - 2
GATE_ROWS = 2 * DN_HEADS


def _dot_nt(a, b):
    return lax.dot_general(a, b, (((1,), (1,)), ((), ())), preferred_element_type=F32)


def _dot_tn(a, b):
    return lax.dot_general(a, b, (((0,), (0,)), ((), ())), preferred_element_type=F32)


def _causal_conv(xx, w, width):
    if width == 4:
        x1 = pltpu.roll(xx, 1, axis=0)
        near = xx * w[3:4, :] + x1 * w[2:3, :]
        far = xx * w[1:2, :] + x1 * w[0:1, :]
        return (near + pltpu.roll(far, 2, axis=0))[HALO:, :]
    acc = None
    for j in range(width):
        shift = width - 1 - j
        xs = xx if shift == 0 else pltpu.roll(xx, shift, axis=0)
        term = xs[HALO:, :] * w[j:j + 1, :]
        acc = term if acc is None else acc + term
    return acc


def _deltanet_kernel(q_ref, k_ref, v_ref, z_ref, ab_ref, cwq_ref, cwk_ref, cwv_ref,
                     alog_ref, dtb_ref, ng_ref, o_ref,
                     xpad_ref, state_ref):
    st = pl.program_id(2)

    @pl.when(st == 0)
    def _():
        xpad_ref[:, 0:HALO, :] = jnp.zeros((3 * DN_BB, HALO, DN_W), BF16)
        state_ref[...] = jnp.zeros_like(state_ref)

    for bb in range(DN_BB):
        for t, ref in enumerate((q_ref, k_ref, v_ref)):
            xpad_ref[3 * bb + t, HALO:, :] = ref[bb]

    row_i = lax.broadcasted_iota(jnp.int32, (CHUNK, CHUNK), 0)
    col_j = lax.broadcasted_iota(jnp.int32, (CHUNK, CHUNK), 1)
    causal = row_i >= col_j
    strict = row_i > col_j
    upper = (row_i <= col_j).astype(F32)
    eye = (row_i == col_j).astype(F32)
    lane = lax.broadcasted_iota(jnp.int32, (CHUNK, LANES), 1)
    gate_row = lax.broadcasted_iota(jnp.int32, (GATE_ROWS, CHUNK), 0)
    neg_decay_rate = -jnp.exp(alog_ref[...])
    cws = (cwq_ref[...], cwk_ref[...], cwv_ref[...])
    ng = ng_ref[...]

    def chunk_body(c, carry):
        r0 = pl.multiple_of(c * CHUNK, CHUNK)
        win = pl.ds(r0, CHUNK + HALO)
        rows = pl.ds(r0, CHUNK)
        conv, gates, z = [], [], []
        for bb in range(DN_BB):
            conv.append([_silu(_causal_conv(xpad_ref[3 * bb + t, win, :].astype(F32), cws[t], DN_CONV))
                         for t in range(3)])
            ab_t = ab_ref[bb, rows, :].astype(F32).T[:GATE_ROWS, :]
            g_cum_t = jnp.dot(neg_decay_rate * _softplus(ab_t + dtb_ref[...]), upper,
                              preferred_element_type=F32, precision=lax.Precision.HIGHEST)
            gates_t = jnp.where(gate_row < DN_HEADS, g_cum_t, _sigmoid(ab_t))
            gates.append(jnp.concatenate([gates_t, jnp.zeros((CHUNK - GATE_ROWS, CHUNK), F32)], axis=0).T)
            z.append(z_ref[bb, rows, :].astype(F32))

        units = [(bb, i) for bb in range(DN_BB) for i in range(DN_HB)]
        sls = [slice(i * HEAD_DIM, (i + 1) * HEAD_DIM) for i in range(DN_HB)]
        dot = functools.partial(jnp.dot, preferred_element_type=F32)

        g_col = [jnp.sum(jnp.where(lane == i, gates[bb], 0.0), axis=-1, keepdims=True) for bb, i in units]
        beta = [jnp.sum(jnp.where(lane == i + DN_HEADS, gates[bb], 0.0), axis=-1, keepdims=True)
                for bb, i in units]
        g_last = [g[CHUNK - 1:CHUNK, :] for g in g_col]
        eg = [jnp.exp(g) for g in g_col]
        ek = [jnp.exp(gl - g) for gl, g in zip(g_last, g_col)]
        g_b = [jnp.broadcast_to(g, (CHUNK, CHUNK)) for g in g_col]
        decay = [jnp.exp(jnp.where(causal, gb - gb.T, -jnp.inf)) for gb in g_b]

        qf = [conv[bb][0][:, sls[i]] for bb, i in units]
        kf = [conv[bb][1][:, sls[i]] for bb, i in units]
        vf = [conv[bb][2][:, sls[i]] for bb, i in units]
        q = [x * lax.rsqrt(jnp.sum(x * x, axis=-1, keepdims=True) + L2_EPS) * (HEAD_DIM ** -0.5) for x in qf]
        k = [x * lax.rsqrt(jnp.sum(x * x, axis=-1, keepdims=True) + L2_EPS) for x in kf]
        kb = [ki * bi for ki, bi in zip(k, beta)]
        a2 = [_dot_nt(jnp.concatenate([qi, kbi], axis=0).astype(BF16), ki.astype(BF16))
              for qi, kbi, ki in zip(q, kb, k)]
        attn = [a[:CHUNK] * d for a, d in zip(a2, decay)]
        n_mat = [jnp.where(strict, -(a[CHUNK:] * d), 0.0) for a, d in zip(a2, decay)]

        u_mat = [eye + n for n in n_mat]
        p_mat = [dot(n.astype(BF16), n.astype(BF16)) for n in n_mat]
        for _ in range(DN_DOUBLINGS - 1):
            up = [dot(jnp.concatenate([u.astype(BF16), p.astype(BF16)], axis=0), p.astype(BF16))
                  for u, p in zip(u_mat, p_mat)]
            u_mat = [u + x[:CHUNK] for u, x in zip(u_mat, up)]
            p_mat = [x[CHUNK:] for x in up]
        u_mat = [u + dot(u.astype(BF16), p.astype(BF16)) for u, p in zip(u_mat, p_mat)]
        uw = [dot(u.astype(BF16), jnp.concatenate([vi * bi, kbi * egi], axis=1).astype(BF16))
              for u, vi, bi, kbi, egi in zip(u_mat, vf, beta, kb, eg)]

        s_old = [state_ref[u] for u in range(len(units))]
        qw = [dot(jnp.concatenate([qi * egi, x[:, HEAD_DIM:]], axis=0).astype(BF16), s.astype(BF16))
              for qi, egi, x, s in zip(q, eg, uw, s_old)]
        v16 = [(x[:, :HEAD_DIM] - y[CHUNK:]).astype(BF16) for x, y in zip(uw, qw)]
        o = [y[:CHUNK] + dot(a.astype(BF16), v) for y, a, v in zip(qw, attn, v16)]
        s_new = [s * jnp.exp(gl) + _dot_tn((ki * eki).astype(BF16), v)
                 for s, gl, ki, eki, v in zip(s_old, g_last, k, ek, v16)]
        for u, (bb, i) in enumerate(units):
            state_ref[u] = s_new[u]
            on = o[u] * lax.rsqrt(jnp.mean(o[u] * o[u], axis=-1, keepdims=True) + RMS_EPS) * ng
            o_ref[bb, rows, sls[i]] = (on * _silu(z[bb][:, sls[i]])).astype(o_ref.dtype)
        return carry

    lax.fori_loop(0, DN_TS // CHUNK, chunk_body, 0)

    tail = pl.ds(DN_TS, HALO)
    head = pl.ds(0, HALO)
    for t in range(3 * DN_BB):
        xpad_ref[t, head, :] = xpad_ref[t, tail, :]


def _gate_rows(v):
    rows = jnp.zeros((GATE_ROWS, LANES), F32)
    return rows.at[:v.shape[0], :].set(jnp.broadcast_to(v.astype(F32)[:, None], (v.shape[0], LANES)))


def _deltanet(proj3, conv_w, a_log, dt_bias, ng_row):
    assert DN_HB == DN_HEADS
    b, s, _ = proj3.shape
    assert b % DN_BB == 0
    gate = pl.BlockSpec((GATE_ROWS, LANES), lambda bi, hg, st: (0, 0))
    qb, kb_, vb, zb = (off // DN_W for off in (OFF_DNQ, OFF_DNK, OFF_DNV, OFF_DNZ))
    act = lambda base: pl.BlockSpec((DN_BB, DN_TS, DN_W), lambda bi, hg, st: (bi, st, base + hg))
    cw = lambda base: pl.BlockSpec((DN_CONV, DN_W), lambda bi, hg, st: (0, base + hg))
    row = pl.BlockSpec((1, LANES), lambda bi, hg, st: (0, 0))
    return pl.pallas_call(
        _deltanet_kernel,
        grid=(b // DN_BB, DN_HEADS // DN_HB, s // DN_TS),
        in_specs=[
            act(qb), act(kb_), act(vb), act(zb),
            pl.BlockSpec((DN_BB, DN_TS, LANES), lambda bi, hg, st: (bi, st, OFF_AB // LANES)),
            cw(0), cw(DN_WIDTH // DN_W), cw(2 * DN_WIDTH // DN_W),
            gate, gate, row,
        ],
        out_specs=pl.BlockSpec((DN_BB, DN_TS, DN_W), lambda bi, hg, st: (bi, st, hg)),
        out_shape=jax.ShapeDtypeStruct((b, s, DN_WIDTH), BF16),
        scratch_shapes=[
            pltpu.VMEM((3 * DN_BB, DN_TS + HALO, DN_W), BF16),
            pltpu.VMEM((DN_BB * DN_HB, HEAD_DIM, HEAD_DIM), F32),
        ],
        compiler_params=_cparams(("parallel", "parallel", "arbitrary")),
        name="deltanet",
    )(proj3, proj3, proj3, proj3, proj3, conv_w, conv_w, conv_w, _gate_rows(a_log), _gate_rows(dt_bias), ng_row)


SWA_CAST_ROWS = 256
SWA_UNROLL = 4


def _swa_kernel(q_ref, k_ref, v_ref, o_ref, *scratch, seq, dil, slopes):
    n_blocks = seq // dil // SWA_BLOCK
    qi = lax.broadcasted_iota(jnp.int32, (SWA_BLOCK, SWA_BLOCK), 0)
    kj = lax.broadcasted_iota(jnp.int32, (SWA_BLOCK, SWA_BLOCK), 1)
    step_cur = (qi - kj).astype(F32)
    step_prev = step_cur + float(SWA_BLOCK)
    ok_cur = qi >= kj
    ok_prev = kj >= qi
    lane = lax.broadcasted_iota(jnp.int32, (SWA_BLOCK, LANES), 1)
    scale = HEAD_DIM ** -0.5
    heads = range(SWA_HEADS_PER_GROUP)
    sls = [slice(hh * HEAD_DIM, (hh + 1) * HEAD_DIM) for hh in heads]
    m_h = [float(slopes[hh]) * float(dil) for hh in heads]

    if dil > 1:
        (f32_ref,) = scratch

        def cast_body(c, carry):
            rows = pl.ds(pl.multiple_of(c * SWA_CAST_ROWS, SWA_CAST_ROWS), SWA_CAST_ROWS)
            for t, ref in enumerate((q_ref, k_ref, v_ref)):
                for hh in heads:
                    f32_ref[t * SWA_HEADS_PER_GROUP + hh, rows, :] = ref[0, rows, sls[hh]].astype(F32)
            return carry
        lax.fori_loop(0, seq // SWA_CAST_ROWS, cast_body, 0)

        def load(t, rows):
            return [f32_ref[t * SWA_HEADS_PER_GROUP + hh, rows, :].astype(BF16) for hh in heads]
    else:
        refs = (q_ref, k_ref, v_ref)

        def load(t, rows):
            return [refs[t][0, rows, sls[hh]] for hh in heads]

    def block_rows(it):
        r = it // n_blocks
        n = it % n_blocks
        base = n * (SWA_BLOCK * dil) + r
        prev_base = jnp.maximum(n - 1, 0) * (SWA_BLOCK * dil) + r
        if dil > 1:
            return n, pl.ds(base, SWA_BLOCK, stride=dil), pl.ds(prev_base, SWA_BLOCK, stride=dil)
        return (n, pl.ds(pl.multiple_of(base, SWA_BLOCK), SWA_BLOCK),
                pl.ds(pl.multiple_of(prev_base, SWA_BLOCK), SWA_BLOCK))

    def blocks_body(it2, carry):
        qn, kc, kp, vc, vp, ok_p, slope, out_rows = [], [], [], [], [], [], [], []
        for u in range(SWA_UNROLL):
            n, rows, prev_rows = block_rows(it2 * SWA_UNROLL + u)
            qn += load(0, rows)
            kc += load(1, rows)
            vc += load(2, rows)
            kp += load(1, prev_rows)
            vp += load(2, prev_rows)
            ok_p += [jnp.logical_and(ok_prev, n > 0)] * len(heads)
            slope += m_h
            out_rows.append(rows)

        s_cur = [jnp.where(ok_cur, _dot_nt(qh, kh) * scale - m * step_cur, -jnp.inf)
                 for qh, kh, m in zip(qn, kc, slope)]
        s_prev = [jnp.where(ok, _dot_nt(qh, kh) * scale - m * step_prev, -jnp.inf)
                  for ok, qh, kh, m in zip(ok_p, qn, kp, slope)]
        mx = [jnp.max(jnp.maximum(a, b), axis=-1, keepdims=True) for a, b in zip(s_cur, s_prev)]
        e_cur = [jnp.exp(a - m) for a, m in zip(s_cur, mx)]
        e_prev = [jnp.exp(b - m) for b, m in zip(s_prev, mx)]
        den = [jnp.sum(a + b, axis=-1, keepdims=True) for a, b in zip(e_cur, e_prev)]
        pv = [jnp.dot(a.astype(BF16), vch, preferred_element_type=F32)
              + jnp.dot(b.astype(BF16), vph, preferred_element_type=F32)
              for a, b, vch, vph in zip(e_cur, e_prev, vc, vp)]
        for u, rows in enumerate(out_rows):
            lse_tile = jnp.zeros((SWA_BLOCK, LANES), F32)
            for hh in heads:
                i = u * len(heads) + hh
                o_ref[0, hh, rows, :] = pv[i] * (1.0 / den[i])
                lse_tile = jnp.where(lane == hh, mx[i] + jnp.log(den[i]), lse_tile)
            o_ref[0, SWA_HEADS_PER_GROUP, rows, :] = lse_tile
        return carry

    lax.fori_loop(0, seq // SWA_BLOCK // SWA_UNROLL, blocks_body, 0)


def _swa_group(proj3, gi):
    batch, seq, _ = proj3.shape
    window, dil = SWA_GROUPS[gi]
    assert window // dil == SWA_BLOCK and seq % (dil * SWA_BLOCK) == 0
    slopes = 2.0 ** (-ALIBI_MAX_BIAS * np.arange(1, SWA_HEADS + 1) / SWA_HEADS)
    slopes = slopes.reshape(len(SWA_GROUPS), SWA_HEADS_PER_GROUP)[gi]

    def spec(which):
        cb = (OFF_SWA + which * SWA_HEADS * HEAD_DIM) // SWA_GROUP_WIDTH + gi
        return pl.BlockSpec((1, seq, SWA_GROUP_WIDTH), lambda bi: (bi, 0, cb))

    scratch = [pltpu.VMEM((3 * SWA_HEADS_PER_GROUP, seq, HEAD_DIM), F32)] if dil > 1 else []
    return pl.pallas_call(
        functools.partial(_swa_kernel, seq=seq, dil=dil, slopes=tuple(float(s) for s in slopes)),
        grid=(batch,),
        in_specs=[spec(0), spec(1), spec(2)],
        out_specs=pl.BlockSpec((1, SWA_OUT_SLOTS, seq, HEAD_DIM), lambda bi: (bi, 0, 0, 0)),
        out_shape=jax.ShapeDtypeStruct((batch, SWA_OUT_SLOTS, seq, HEAD_DIM), F32),
        scratch_shapes=scratch,
        compiler_params=_cparams(("parallel",)),
        name=f"swa_g{gi}",
    )(proj3, proj3, proj3)


MG_TM = 256


def _merge_kernel(x_ref, ydn_ref, gate_ref, scc_ref, scb_ref, scx_ref, scc_h_ref, scx_h_ref,
                  s0_ref, s1_ref, s2_ref, scw_ref, wdn_ref, wsc_ref, wswa_ref, wout_ref, o_ref,
                  *, tiles_per_seq):
    i = pl.program_id(0)
    first = (i % tiles_per_seq) == 0

    cx_cur = scc_ref[...].astype(F32) * scx_ref[...].astype(F32)
    cx_halo = scc_h_ref[...].astype(F32) * scx_h_ref[...].astype(F32)
    cx_halo = jnp.where(first, 0.0, cx_halo)
    cx = jnp.concatenate([cx_halo, cx_cur], axis=0)
    y_sc = scb_ref[...].astype(F32) * _causal_conv(cx, scw_ref[...], SC_CONV)

    outs = (s0_ref, s1_ref, s2_ref)
    lses = [o[0, SWA_HEADS_PER_GROUP] for o in outs]
    mx = jnp.maximum(jnp.maximum(lses[0], lses[1]), lses[2])
    es = [jnp.exp(l - mx) for l in lses]
    inv = 1.0 / (es[0] + es[1] + es[2])
    heads = []
    for hh in range(SWA_HEADS_PER_GROUP):
        acc = None
        for gi in range(3):
            wt = (es[gi] * inv)[:, hh:hh + 1]
            term = wt * outs[gi][0, hh]
            acc = term if acc is None else acc + term
        heads.append(acc)
    y_swa = jnp.concatenate(heads, axis=1)

    gates = gate_ref[...].astype(F32)
    merged = (_sigmoid(gates[:, 0:D_MODEL])
              * jnp.dot(ydn_ref[...], wdn_ref[...], preferred_element_type=F32)
              + _sigmoid(gates[:, D_MODEL:2 * D_MODEL])
              * jnp.dot(y_sc.astype(BF16), wsc_ref[...], preferred_element_type=F32)
              + _sigmoid(gates[:, 2 * D_MODEL:3 * D_MODEL])
              * jnp.dot(y_swa.astype(BF16), wswa_ref[...], preferred_element_type=F32))
    o_ref[...] = x_ref[...] + jnp.dot(merged.astype(BF16), wout_ref[...], preferred_element_type=F32)


def _merge(x2d, ydn2, proj2, swa_outs, sc_conv_w, wdn, wsc, wswa, wout, seq):
    m = x2d.shape[0]
    tiles_per_seq = seq // MG_TM
    halo_blocks = MG_TM // HALO
    rows = lambda w, cb: pl.BlockSpec((MG_TM, w), lambda i: (i, cb))
    halo = lambda cb: pl.BlockSpec((HALO, SC_WIDTH), lambda i: (jnp.maximum(i * halo_blocks - 1, 0), cb))
    full = lambda a: pl.BlockSpec(a.shape, lambda i: (0, 0), pipeline_mode=pl.Buffered(1))
    swa = pl.BlockSpec((1, SWA_OUT_SLOTS, MG_TM, HEAD_DIM),
                       lambda i: (i // tiles_per_seq, 0, i % tiles_per_seq, 0))
    return pl.pallas_call(
        functools.partial(_merge_kernel, tiles_per_seq=tiles_per_seq),
        grid=(m // MG_TM,),
        in_specs=[
            rows(D_MODEL, 0),
            rows(DN_WIDTH, 0),
            rows(3 * D_MODEL, OFF_GATE // (3 * D_MODEL)),
            rows(SC_WIDTH, OFF_SCC // SC_WIDTH), rows(SC_WIDTH, OFF_SCB // SC_WIDTH),
            rows(SC_WIDTH, OFF_SCX // SC_WIDTH),
            halo(OFF_SCC // SC_WIDTH), halo(OFF_SCX // SC_WIDTH),
            swa, swa, swa,
            full(sc_conv_w), full(wdn), full(wsc), full(wswa), full(wout),
        ],
        out_specs=rows(D_MODEL, 0),
        out_shape=jax.ShapeDtypeStruct((m, D_MODEL), F32),
        compiler_params=_cparams(("parallel",)),
        name="merge",
    )(x2d, ydn2, proj2, proj2, proj2, proj2, proj2, proj2, *swa_outs, sc_conv_w, wdn, wsc, wswa, wout)


RT_TM = 1024
RT_PARTS = 8
ROUTE_LANE0 = MOE_GROUPS
R_E1, R_E2, R_W1, R_W2, R_RANK1, R_RANK2 = range(6)
HALF = D_MODEL // 2


def _pack_pair(lo, hi):
    lo_b = lax.bitcast_convert_type(lo.astype(BF16).astype(F32), jnp.uint32)
    hi_b = lax.bitcast_convert_type(hi.astype(BF16).astype(F32), jnp.uint32)
    return (hi_b & jnp.uint32(0xFFFF0000)) | (lo_b >> jnp.uint32(16))


def _unpack_pair(p):
    lo = lax.bitcast_convert_type(p << jnp.uint32(16), F32)
    hi = lax.bitcast_convert_type(p & jnp.uint32(0xFFFF0000), F32)
    return lo, hi


def _router_kernel(x_ref, g_ref, wrh_ref, wrl_ref, br_ref, hp_ref, route_ref, cnt_ref, carry_ref):
    @pl.when(pl.program_id(0) == 0)
    def _():
        carry_ref[...] = jnp.zeros_like(carry_ref)

    tp = RT_TM // RT_PARTS
    part_rows = [slice(p * tp, (p + 1) * tp) for p in range(RT_PARTS)]
    w_hi, w_lo, bias = wrh_ref[...], wrl_ref[...], br_ref[...]
    dot = functools.partial(jnp.dot, preferred_element_type=F32)
    lane = lax.broadcasted_iota(jnp.int32, (tp, LANES), 1)
    big = jnp.int32(LANES)
    neg = -jnp.inf

    hs = [_rmsnorm_rows(x_ref[rows, :], g_ref[...]) for rows in part_rows]
    for rows, h in zip(part_rows, hs):
        hp_ref[rows, :] = _pack_pair(h[:, :HALF], h[:, HALF:])
    h_hi = [h.astype(BF16) for h in hs]
    h_lo = [(h - hi.astype(F32)).astype(BF16) for h, hi in zip(hs, h_hi)]
    logits = [dot(hi, w_hi) + dot(lo, w_hi) + dot(hi, w_lo) + bias for hi, lo in zip(h_hi, h_lo)]

    def route(lg):
        gl = jnp.where(lane < MOE_GROUPS, lg, neg)
        gmax = jnp.max(gl, axis=-1, keepdims=True)
        grp = jnp.min(jnp.where(gl == gmax, lane, big), axis=-1, keepdims=True)
        g_w = 1.0 / jnp.sum(jnp.exp(gl - gmax), axis=-1, keepdims=True)
        lo = ROUTE_LANE0 + grp * MOE_EPG
        el = jnp.where(jnp.logical_and(lane >= lo, lane < lo + MOE_EPG), lg, neg)
        m1 = jnp.max(el, axis=-1, keepdims=True)
        i1 = jnp.min(jnp.where(el == m1, lane, big), axis=-1, keepdims=True)
        el2 = jnp.where(lane == i1, neg, el)
        m2 = jnp.max(el2, axis=-1, keepdims=True)
        i2 = jnp.min(jnp.where(el2 == m2, lane, big), axis=-1, keepdims=True)
        e2 = jnp.exp(m2 - m1)
        return i1, i2, g_w / (1.0 + e2), g_w * e2 / (1.0 + e2)

    routed = [route(lg) for lg in logits]

    onehot = [jnp.where(jnp.logical_or(lane == i1, lane == i2), 1.0, 0.0) for i1, i2, _, _ in routed]
    earlier = (lax.broadcasted_iota(jnp.int32, (tp, tp), 0) > lax.broadcasted_iota(jnp.int32, (tp, tp), 1))
    earlier = jnp.where(earlier, 1.0, 0.0).astype(BF16)
    within = [dot(earlier, oh.astype(BF16)) for oh in onehot]
    seen = carry_ref[...]
    for rows, (i1, i2, w1, w2), oh, inside in zip(part_rows, routed, onehot, within):
        before = seen + inside
        rank1 = jnp.sum(jnp.where(lane == i1, before, 0.0), axis=-1, keepdims=True)
        rank2 = jnp.sum(jnp.where(lane == i2, before, 0.0), axis=-1, keepdims=True)
        seen = seen + jnp.sum(oh, axis=0, keepdims=True)
        rec = jnp.zeros((tp, LANES), F32)
        for ln, val in ((R_E1, (i1 - ROUTE_LANE0).astype(F32)), (R_E2, (i2 - ROUTE_LANE0).astype(F32)),
                        (R_W1, w1), (R_W2, w2), (R_RANK1, rank1), (R_RANK2, rank2)):
            rec = jnp.where(lane == ln, val, rec)
        route_ref[rows, :] = rec
    carry_ref[...] = seen
    cnt_ref[...] = seen


def _router(x2d, g_row, w_route_hi, w_route_lo, b_route):
    m = x2d.shape[0]
    return pl.pallas_call(
        _router_kernel,
        grid=(m // RT_TM,),
        in_specs=[
            pl.BlockSpec((RT_TM, D_MODEL), lambda i: (i, 0)),
            pl.BlockSpec((1, D_MODEL), lambda i: (0, 0)),
            pl.BlockSpec((D_MODEL, LANES), lambda i: (0, 0)),
            pl.BlockSpec((D_MODEL, LANES), lambda i: (0, 0)),
            pl.BlockSpec((1, LANES), lambda i: (0, 0)),
        ],
        out_specs=[pl.BlockSpec((RT_TM, HALF), lambda i: (i, 0)),
                   pl.BlockSpec((RT_TM, LANES), lambda i: (i, 0)),
                   pl.BlockSpec((1, LANES), lambda i: (0, 0))],
        out_shape=[jax.ShapeDtypeStruct((m, HALF), jnp.uint32), jax.ShapeDtypeStruct((m, LANES), F32),
                   jax.ShapeDtypeStruct((1, LANES), F32)],
        scratch_shapes=[pltpu.VMEM((1, LANES), F32)],
        compiler_params=_cparams(("arbitrary",)),
        name="router",
    )(x2d, g_row, w_route_hi, w_route_lo, b_route)


EXP_TM = 512


def _n_row_tiles(n_tokens):
    return 2 * n_tokens // EXP_TM + MOE_EXPERTS


def _positions_kernel(route_ref, offs_ref, pos_ref):
    rec = route_ref[...]
    lane = lax.broadcasted_iota(jnp.int32, rec.shape, 1)
    offs = offs_ref[...]

    def first_row(e):
        return jnp.sum(jnp.where(lane == e.astype(jnp.int32), offs, 0.0), axis=-1, keepdims=True)

    pos1 = first_row(rec[:, R_E1:R_E1 + 1]) + rec[:, R_RANK1:R_RANK1 + 1]
    pos2 = first_row(rec[:, R_E2:R_E2 + 1]) + rec[:, R_RANK2:R_RANK2 + 1]
    pos_ref[...] = jnp.where(lane == 0, pos1, jnp.where(lane == 1, pos2, 0.0)).astype(jnp.int32)


def _moe_plan(route, cnt):
    m = route.shape[0]
    n_tiles = _n_row_tiles(m)
    counts = cnt[0, ROUTE_LANE0:ROUTE_LANE0 + MOE_EXPERTS].astype(jnp.int32)
    padded = (counts + EXP_TM - 1) // EXP_TM * EXP_TM
    ends = jnp.cumsum(padded)
    offs = ends - padded
    pos = pl.pallas_call(
        _positions_kernel,
        grid=(m // RT_TM,),
        in_specs=[pl.BlockSpec((RT_TM, LANES), lambda i: (i, 0)), pl.BlockSpec((1, LANES), lambda i: (0, 0))],
        out_specs=pl.BlockSpec((RT_TM, LANES), lambda i: (i, 0)),
        out_shape=jax.ShapeDtypeStruct((m, LANES), jnp.int32),
        compiler_params=_cparams(("parallel",)),
        name="moe_positions",
    )(route, _lane_row(offs))
    n_active = ends[-1] // EXP_TM
    tile_row = jnp.maximum(jnp.minimum(jnp.arange(n_tiles), n_active - 1), 0) * EXP_TM
    tile_expert = jnp.minimum(jnp.sum(ends[None, :] <= tile_row[:, None], axis=1), MOE_EXPERTS - 1)
    used = counts > 0
    ids = jnp.where(used, jnp.arange(MOE_EXPERTS), MOE_EXPERTS)
    next_used = jnp.concatenate([lax.cummin(ids, reverse=True)[1:], jnp.full((1,), MOE_EXPERTS, ids.dtype)])
    parity = (jnp.cumsum(used.astype(jnp.int32)) - 1) % 2
    i32 = lambda a: a.astype(jnp.int32)
    return dict(pos1=pos[:, 0], pos2=pos[:, 1], counts=counts, offs=i32(offs),
                tile_expert=i32(tile_expert), n_active=i32(n_active).reshape(1),
                next_expert=i32(next_used[tile_expert]), parity=i32(parity[tile_expert]))


DSP_TB = 512
ROW_UNROLL = 8


def _dispatch_kernel(cnt_ref, offs_ref, nact_ref, pos1_ref, pos2_ref, hp_ref, xs_ref, zero_ref, sem, zero_sem):
    n_tiles = xs_ref.shape[0] // EXP_TM

    @pl.when(pl.program_id(0) == 0)
    def _():
        zero_ref[...] = jnp.zeros_like(zero_ref)

        def pad_pieces(e):
            pad = (-cnt_ref[e]) & (EXP_TM - 1)
            first = offs_ref[e] + cnt_ref[e]
            head = jnp.minimum((-first) & (SUBLANES - 1), pad)
            for k in range(SUBLANES - 1):
                yield k < head, pltpu.make_async_copy(zero_ref.at[pl.ds(0, 1), :],
                                                      xs_ref.at[pl.ds(first + k, 1), :], zero_sem)
            body = pad - head
            b = EXP_TM // 2
            while b >= SUBLANES:
                row = pl.multiple_of(first + head + (body & ~(2 * b - 1)), SUBLANES)
                yield (body & b) != 0, pltpu.make_async_copy(zero_ref.at[pl.ds(0, b), :],
                                                             xs_ref.at[pl.ds(row, b), :], zero_sem)
                b //= 2

        def tile_copy(t):
            rows = pl.ds(pl.multiple_of(t * EXP_TM, EXP_TM), EXP_TM)
            return pltpu.make_async_copy(zero_ref, xs_ref.at[rows, :], zero_sem)

        def for_all(method):
            def per_expert(e, c):
                for cond, cp in pad_pieces(e):
                    pl.when(cond)(getattr(cp, method))
                return c
            lax.fori_loop(0, MOE_EXPERTS, per_expert, 0)

            def per_tile(t, c):
                getattr(tile_copy(t), method)()
                return c
            lax.fori_loop(nact_ref[0], n_tiles, per_tile, 0)

        for_all("start")
        for_all("wait")

    def start(j8, c):
        for u in range(ROW_UNROLL):
            j = j8 * ROW_UNROLL + u
            src = hp_ref.at[pl.ds(j, 1), :]
            pltpu.make_async_copy(src, xs_ref.at[pl.ds(pos1_ref[0, 0, j], 1), :], sem).start(priority=0)
            pltpu.make_async_copy(src, xs_ref.at[pl.ds(pos2_ref[0, 0, j], 1), :], sem).start(priority=1)
        return c

    lax.fori_loop(0, DSP_TB // ROW_UNROLL, start, 0)
    for _ in range(2):
        pltpu.make_async_copy(hp_ref, xs_ref.at[pl.ds(0, DSP_TB), :], sem).wait()


def _dispatch(hp, pos1, pos2, counts, offs, n_active):
    m = hp.shape[0]
    n_rows = _n_row_tiles(m) * EXP_TM
    pos_spec = pl.BlockSpec((1, 1, DSP_TB), lambda i, *_: (i, 0, 0), memory_space=pltpu.SMEM)
    return pl.pallas_call(
        _dispatch_kernel,
        grid_spec=pltpu.PrefetchScalarGridSpec(
            num_scalar_prefetch=3,
            grid=(m // DSP_TB,),
            in_specs=[pos_spec, pos_spec, pl.BlockSpec((DSP_TB, HALF), lambda i, *_: (i, 0))],
            out_specs=pl.BlockSpec(memory_space=pl.ANY),
            scratch_shapes=[pltpu.VMEM((EXP_TM, HALF), jnp.uint32), pltpu.SemaphoreType.DMA(()),
                            pltpu.SemaphoreType.DMA(())],
        ),
        out_shape=jax.ShapeDtypeStruct((n_rows, HALF), jnp.uint32),
        compiler_params=_cparams(("arbitrary",)),
        name="moe_dispatch",
    )(counts, offs, n_active, pos1.reshape(m // DSP_TB, 1, DSP_TB), pos2.reshape(m // DSP_TB, 1, DSP_TB), hp)


W_CAST_ROWS = 256


def _cast_rows(src_ref, dst_ref):
    n_rows = dst_ref.shape[0]
    slab = min(W_CAST_ROWS, n_rows)

    def body(r, c):
        rows = pl.ds(pl.multiple_of(r * slab, slab), slab)
        dst_ref[rows, :] = src_ref[rows, :].astype(BF16)
        return c
    lax.fori_loop(0, n_rows // slab, body, 0)


def _expert_kernel(te_ref, nact_ref, next_ref, par_ref, xs_ref, wg_hbm, wu_hbm, wd_hbm, ys_ref,
                   sg_ref, su_ref, sd_ref, wg16_ref, wu16_ref, wd16_ref, sem, *, layer):
    i = pl.program_id(0)
    active = i < nact_ref[0]
    expert = te_ref[i]
    slot = par_ref[i]
    new_expert = jnp.logical_or(i == 0, expert != te_ref[jnp.maximum(i - 1, 0)])

    def copies(e, s):
        return [pltpu.make_async_copy(hbm.at[layer, e], stage.at[s], sem.at[s])
                for hbm, stage in ((wg_hbm, sg_ref), (wu_hbm, su_ref), (wd_hbm, sd_ref))]

    @pl.when(jnp.logical_and(active, i == 0))
    def _():
        for cp in copies(expert, slot):
            cp.start()

    @pl.when(jnp.logical_and(active, new_expert))
    def _():
        for cp in copies(expert, slot):
            cp.wait()

        @pl.when(next_ref[i] < MOE_EXPERTS)
        def _():
            for cp in copies(next_ref[i], 1 - slot):
                cp.start()

        _cast_rows(sg_ref.at[slot], wg16_ref)
        _cast_rows(su_ref.at[slot], wu16_ref)
        _cast_rows(sd_ref.at[slot], wd16_ref)

    @pl.when(active)
    def _():
        lo, hi = _unpack_pair(xs_ref[...])
        lo, hi = lo.astype(BF16), hi.astype(BF16)
        dot = functools.partial(jnp.dot, preferred_element_type=F32)
        gate = dot(lo, wg16_ref[:HALF, :]) + dot(hi, wg16_ref[HALF:, :])
        up = dot(lo, wu16_ref[:HALF, :]) + dot(hi, wu16_ref[HALF:, :])
        y = dot((_silu(gate) * up).astype(BF16), wd16_ref[...])
        ys_ref[...] = _pack_pair(y[:, :HALF], y[:, HALF:])

    @pl.when(jnp.logical_not(active))
    def _():
        ys_ref[...] = jnp.zeros_like(ys_ref)


def _experts(xs, tile_expert, n_active, next_expert, parity, wg, wu, wd, layer):
    n_tiles = xs.shape[0] // EXP_TM
    rows = pl.BlockSpec((EXP_TM, HALF),
                        lambda i, te, na, nx, par: (jnp.maximum(jnp.minimum(i, na[0] - 1), 0), 0))
    out_rows = pl.BlockSpec((EXP_TM, HALF), lambda i, te, na, nx, par: (i, 0))
    hbm = pl.BlockSpec(memory_space=pl.ANY)
    return pl.pallas_call(
        functools.partial(_expert_kernel, layer=layer),
        grid_spec=pltpu.PrefetchScalarGridSpec(
            num_scalar_prefetch=4,
            grid=(n_tiles,),
            in_specs=[rows, hbm, hbm, hbm],
            out_specs=out_rows,
            scratch_shapes=[pltpu.VMEM((2, D_MODEL, MOE_FF), F32), pltpu.VMEM((2, D_MODEL, MOE_FF), F32),
                            pltpu.VMEM((2, MOE_FF, D_MODEL), F32),
                            pltpu.VMEM((D_MODEL, MOE_FF), BF16), pltpu.VMEM((D_MODEL, MOE_FF), BF16),
                            pltpu.VMEM((MOE_FF, D_MODEL), BF16),
                            pltpu.SemaphoreType.DMA((2,))],
        ),
        out_shape=jax.ShapeDtypeStruct(xs.shape, jnp.uint32),
        compiler_params=_cparams(("arbitrary",)),
        name="moe_experts",
    )(tile_expert, n_active, next_expert, parity, xs, wg, wu, wd)


CMB_TC = 512


def _combine_kernel(pos1_ref, pos2_ref, pos1_next_ref, pos2_next_ref, x_ref, route_ref, g_ref, ys_ref,
                    o_ref, buf_ref, sem, *, final_norm):
    i = pl.program_id(0)
    slot = i % 2

    def gather(p1_ref, p2_ref, s):
        def start(j8, c):
            for u in range(ROW_UNROLL):
                j = j8 * ROW_UNROLL + u
                for k, p_ref in enumerate((p1_ref, p2_ref)):
                    pltpu.make_async_copy(ys_ref.at[pl.ds(p_ref[0, 0, j], 1), :],
                                          buf_ref.at[s, k, pl.ds(j, 1), :], sem.at[s]).start(priority=k)
            return c
        lax.fori_loop(0, CMB_TC // ROW_UNROLL, start, 0)

    @pl.when(i == 0)
    def _():
        gather(pos1_ref, pos2_ref, 0)

    @pl.when(i + 1 < pl.num_programs(0))
    def _():
        gather(pos1_next_ref, pos2_next_ref, 1 - slot)

    for k in range(2):
        pltpu.make_async_copy(ys_ref.at[pl.ds(0, CMB_TC), :], buf_ref.at[slot, k], sem.at[slot]).wait()

    rec = route_ref[...]
    w1, w2 = rec[:, R_W1:R_W1 + 1], rec[:, R_W2:R_W2 + 1]
    lo1, hi1 = _unpack_pair(buf_ref[slot, 0])
    lo2, hi2 = _unpack_pair(buf_ref[slot, 1])
    out_lo = x_ref[:, :HALF] + w1 * lo1 + w2 * lo2
    out_hi = x_ref[:, HALF:] + w1 * hi1 + w2 * hi2
    if final_norm:
        ms = (jnp.sum(out_lo * out_lo, axis=-1, keepdims=True)
              + jnp.sum(out_hi * out_hi, axis=-1, keepdims=True)) * (1.0 / D_MODEL)
        inv = lax.rsqrt(ms + RMS_EPS)
        out_lo = out_lo * inv * g_ref[:, :HALF]
        out_hi = out_hi * inv * g_ref[:, HALF:]
    o_ref[:, :HALF] = out_lo
    o_ref[:, HALF:] = out_hi


def _combine(x2d, route, ys, pos1, pos2, g_row, final_norm):
    m = x2d.shape[0]
    n_steps = m // CMB_TC
    pos_spec = pl.BlockSpec((1, 1, CMB_TC), lambda i: (i, 0, 0), memory_space=pltpu.SMEM)
    next_spec = pl.BlockSpec((1, 1, CMB_TC), lambda i: (jnp.minimum(i + 1, n_steps - 1), 0, 0),
                             memory_space=pltpu.SMEM)
    pos1, pos2 = pos1.reshape(n_steps, 1, CMB_TC), pos2.reshape(n_steps, 1, CMB_TC)
    return pl.pallas_call(
        functools.partial(_combine_kernel, final_norm=final_norm),
        grid=(n_steps,),
        in_specs=[pos_spec, pos_spec, next_spec, next_spec,
                  pl.BlockSpec((CMB_TC, D_MODEL), lambda i: (i, 0)),
                  pl.BlockSpec((CMB_TC, LANES), lambda i: (i, 0)),
                  pl.BlockSpec((1, D_MODEL), lambda i: (0, 0)),
                  pl.BlockSpec(memory_space=pl.ANY)],
        out_specs=pl.BlockSpec((CMB_TC, D_MODEL), lambda i: (i, 0)),
        out_shape=jax.ShapeDtypeStruct((m, D_MODEL), F32),
        scratch_shapes=[pltpu.VMEM((2, 2, CMB_TC, HALF), jnp.uint32), pltpu.SemaphoreType.DMA((2,))],
        compiler_params=_cparams(("arbitrary",)),
        name="moe_combine",
    )(pos1, pos2, pos1, pos2, x2d, route, g_row, ys)


RL_TN = 512
RL_PIECE = 256
SRC_A_END = 4096
SRC_AB = 16
SRC_COLS = 17936


def _relayout_plan():
    n_rest = (SRC_COLS - SRC_A_END - SRC_AB) // RL_TN
    rest = lambda s: SRC_A_END + SRC_AB + RL_TN * s
    starts = ([rest(s) for s in range(15, n_rest)]
              + [RL_TN * a for a in range(SRC_A_END // RL_TN)]
              + [rest(s) for s in range(15)]
              + [SRC_A_END])
    assert len(starts) * RL_TN == PROJ_COLS and all(s % 16 == 0 and s + RL_TN <= SRC_COLS for s in starts)
    return np.asarray(starts, np.int32)


def _relayout_kernel(start_ref, wt_ref, o_ref, buf_ref, sem, *, layer):
    t = pl.program_id(0)
    n_tiles = pl.num_programs(0)
    slot = t % 2

    def fetch(tile, s):
        first = pl.multiple_of(start_ref[tile], 16)
        pltpu.make_async_copy(wt_ref.at[layer, pl.ds(first, RL_TN), :], buf_ref.at[s], sem.at[s]).start()

    @pl.when(t == 0)
    def _():
        fetch(0, 0)

    @pl.when(t + 1 < n_tiles)
    def _():
        fetch(t + 1, 1 - slot)

    pltpu.make_async_copy(wt_ref.at[layer, pl.ds(0, RL_TN), :], buf_ref.at[slot], sem.at[slot]).wait()

    row = lax.broadcasted_iota(jnp.int32, (RL_TN, RL_PIECE), 0)
    keep = jnp.logical_or(t + 1 < n_tiles, row < SRC_AB)
    for c in range(D_MODEL // RL_PIECE):
        cols = slice(c * RL_PIECE, (c + 1) * RL_PIECE)
        piece = jnp.where(keep, buf_ref[slot, :, cols], 0.0)
        o_ref[cols, :] = piece.T.astype(BF16)


def _prep_w_in(w_all, layer):
    wt = jnp.swapaxes(w_all, 1, 2)
    return pl.pallas_call(
        functools.partial(_relayout_kernel, layer=layer),
        grid_spec=pltpu.PrefetchScalarGridSpec(
            num_scalar_prefetch=1,
            grid=(PROJ_COLS // RL_TN,),
            in_specs=[pl.BlockSpec(memory_space=pl.ANY)],
            out_specs=pl.BlockSpec((D_MODEL, RL_TN), lambda t, starts: (0, t)),
            scratch_shapes=[pltpu.VMEM((2, RL_TN, D_MODEL), F32), pltpu.SemaphoreType.DMA((2,))],
        ),
        out_shape=jax.ShapeDtypeStruct((D_MODEL, PROJ_COLS), BF16),
        compiler_params=_cparams(("arbitrary",)),
        name="w_in_relayout",
    )(jnp.asarray(_relayout_plan()), wt)


def _lane_row(v):
    return jnp.zeros((1, LANES), F32).at[0, :v.shape[0]].set(v.astype(F32))


def _layer(x2d, batch, seq, p, final_g_row):
    proj2 = _in_proj(x2d, p["norm_mix_g"], p["w_in"])
    proj3 = proj2.reshape(batch, seq, PROJ_COLS)
    ydn = _deltanet(proj3, p["dn_conv_w"], p["dn_a_log"], p["dn_dt_bias"], p["dn_norm_g"])
    swa_outs = [_swa_group(proj3, gi) for gi in range(len(SWA_GROUPS))]
    x2d = _merge(x2d, ydn.reshape(batch * seq, DN_WIDTH), proj2, swa_outs, p["sc_conv_w"],
                 p["w_branch_dn"], p["w_branch_sc"], p["w_branch_swa"], p["w_out"], seq)
    hp, route, cnt = _router(x2d, p["norm_ffn_g"], p["w_route_hi"], p["w_route_lo"], p["b_route"])
    plan = _moe_plan(route, cnt)
    xs = _dispatch(hp, plan["pos1"], plan["pos2"], plan["counts"], plan["offs"], plan["n_active"])
    ys = _experts(xs, plan["tile_expert"], plan["n_active"], plan["next_expert"], plan["parity"],
                  p["expert_w_gate"], p["expert_w_up"], p["expert_w_down"], p["layer"])
    is_last = final_g_row is not None
    g_row = final_g_row if is_last else p["norm_ffn_g"]
    return _combine(x2d, route, ys, plan["pos1"], plan["pos2"], g_row, is_last)


def kernel(x, norm_mix_g, w_in, dn_conv_w, dn_a_log, dn_dt_bias, dn_norm_g, sc_conv_w, w_branch_dn, w_branch_sc, w_branch_swa, w_out, norm_ffn_g, router_group_w, router_group_b, router_expert_w, router_expert_b, expert_w_gate, expert_w_up, expert_w_down, final_norm_g):
    batch, seq, _ = x.shape
    depth = w_in.shape[0]
    x2d = x.reshape(batch * seq, D_MODEL)
    for l in range(depth):
        w_route = jnp.concatenate([router_group_w[l], router_expert_w[l]], axis=1)
        w_route = jnp.pad(w_route, ((0, 0), (0, LANES - w_route.shape[1])))
        p = dict(
            norm_mix_g=norm_mix_g[l].reshape(1, D_MODEL),
            w_in=_prep_w_in(w_in, l),
            dn_conv_w=dn_conv_w[l],
            dn_a_log=dn_a_log[l],
            dn_dt_bias=dn_dt_bias[l],
            dn_norm_g=dn_norm_g[l].reshape(1, HEAD_DIM),
            sc_conv_w=sc_conv_w[l],
            w_branch_dn=w_branch_dn[l].astype(BF16),
            w_branch_sc=w_branch_sc[l].astype(BF16),
            w_branch_swa=w_branch_swa[l].astype(BF16),
            w_out=w_out[l].astype(BF16),
            norm_ffn_g=norm_ffn_g[l].reshape(1, D_MODEL),
            w_route_hi=w_route.astype(BF16),
            w_route_lo=(w_route - w_route.astype(BF16).astype(F32)).astype(BF16),
            b_route=_lane_row(jnp.concatenate([router_group_b[l], router_expert_b[l]])),
            expert_w_gate=expert_w_gate,
            expert_w_up=expert_w_up,
            expert_w_down=expert_w_down,
            layer=l,
        )
        final_g_row = final_norm_g.reshape(1, D_MODEL) if l == depth - 1 else None
        x2d = _layer(x2d, batch, seq, p, final_g_row)
    return x2d.reshape(batch, seq, D_MODEL)
```

```python
import functools

import jax
import jax.numpy as jnp
import numpy as np
from jax import lax
from jax.experimental import pallas as pl
from jax.experimental.pallas import tpu as pltpu

F32 = jnp.float32
BF16 = jnp.bfloat16

D_MODEL = 2048
RMS_EPS = 1e-6
L2_EPS = 1e-6

DN_HEADS = 8
HEAD_DIM = 128
DN_WIDTH = DN_HEADS * HEAD_DIM
DN_CONV = 4
SC_WIDTH = 1024
SC_CONV = 3
SWA_GROUPS = ((128, 1), (512, 4), (2048, 16))
SWA_HEADS_PER_GROUP = 4
SWA_HEADS = 12
SWA_GROUP_WIDTH = SWA_HEADS_PER_GROUP * HEAD_DIM
SWA_BLOCK = 128
ALIBI_MAX_BIAS = 8.0
MOE_GROUPS = 4
MOE_EPG = 8
MOE_EXPERTS = 32
MOE_FF = 512

OFF_GATE = 0
OFF_DNQ = 6144
OFF_DNK = OFF_DNQ + DN_WIDTH
OFF_DNV = OFF_DNK + DN_WIDTH
OFF_DNZ = OFF_DNV + DN_WIDTH
OFF_SCC = OFF_DNZ + DN_WIDTH
OFF_SCB = OFF_SCC + SC_WIDTH
OFF_SCX = OFF_SCB + SC_WIDTH
OFF_SWA = OFF_SCX + SC_WIDTH
OFF_AB = OFF_SWA + 3 * SWA_HEADS * HEAD_DIM
PROJ_COLS = 18432

LANES = 128
SUBLANES = 8
SWA_OUT_SLOTS = SWA_HEADS_PER_GROUP + 1
CHUNK = 128
HALO = 16

V7X_VMEM_BYTES = 64 * 1024 * 1024
VMEM_LIMIT = V7X_VMEM_BYTES - 8 * 1024 * 1024


def _cparams(sem):
    return pltpu.CompilerParams(dimension_semantics=sem, vmem_limit_bytes=VMEM_LIMIT)


def _sigmoid(x):
    return 1.0 / (1.0 + jnp.exp(-x))


def _silu(x):
    return x * _sigmoid(x)


def _softplus(x):
    return jnp.maximum(x, 0.0) + jnp.log(1.0 + jnp.exp(-jnp.abs(x)))


IN_TM = 1024
IN_TN = 2304
NORM_ROWS = 64


def _rmsnorm_rows(x, g):
    ms = jnp.mean(x * x, axis=-1, keepdims=True)
    return x * lax.rsqrt(ms + RMS_EPS) * g


def _in_proj_kernel(x_ref, g_ref, w_ref, o_ref, h_ref):
    @pl.when(pl.program_id(1) == 0)
    def _():
        def body(r, c):
            rows = pl.ds(pl.multiple_of(r * NORM_ROWS, NORM_ROWS), NORM_ROWS)
            h_ref[rows, :] = _rmsnorm_rows(x_ref[rows, :], g_ref[...]).astype(BF16)
            return c
        lax.fori_loop(0, IN_TM // NORM_ROWS, body, 0, unroll=4)

    o_ref[...] = jnp.dot(h_ref[...], w_ref[...], preferred_element_type=F32).astype(o_ref.dtype)


def _in_proj(x2d, g_row, w_bf16):
    m = x2d.shape[0]
    return pl.pallas_call(
        _in_proj_kernel,
        grid=(m // IN_TM, PROJ_COLS // IN_TN),
        in_specs=[
            pl.BlockSpec((IN_TM, D_MODEL), lambda i, j: (i, 0)),
            pl.BlockSpec((1, D_MODEL), lambda i, j: (0, 0)),
            pl.BlockSpec((D_MODEL, IN_TN), lambda i, j: (0, j)),
        ],
        out_specs=pl.BlockSpec((IN_TM, IN_TN), lambda i, j: (i, j)),
        out_shape=jax.ShapeDtypeStruct((m, PROJ_COLS), BF16),
        scratch_shapes=[pltpu.VMEM((IN_TM, D_MODEL), BF16)],
        compiler_params=_cparams(("parallel", "arbitrary")),
        name="in_proj",
    )(x2d, g_row, w_bf16)


DN_HB = 8
DN_BB = 2
DN_TS = 512
DN_W = DN_HB * HEAD_DIM
DN_DOUBLINGS = CHUNK.bit_length() - 2
GATE_ROWS = 2 * DN_HEADS


def _dot_nt(a, b):
    return lax.dot_general(a, b, (((1,), (1,)), ((), ())), preferred_element_type=F32)


def _dot_tn(a, b):
    return lax.dot_general(a, b, (((0,), (0,)), ((), ())), preferred_element_type=F32)


def _causal_conv(xx, w, width):
    if width == 4:
        x1 = pltpu.roll(xx, 1, axis=0)
        near = xx * w[3:4, :] + x1 * w[2:3, :]
        far = xx * w[1:2, :] + x1 * w[0:1, :]
        return (near + pltpu.roll(far, 2, axis=0))[HALO:, :]
    acc = None
    for j in range(width):
        shift = width - 1 - j
        xs = xx if shift == 0 else pltpu.roll(xx, shift, axis=0)
        term = xs[HALO:, :] * w[j:j + 1, :]
        acc = term if acc is None else acc + term
    return acc


def _deltanet_kernel(q_ref, k_ref, v_ref, z_ref, ab_ref, cwq_ref, cwk_ref, cwv_ref,
                     alog_ref, dtb_ref, ng_ref, o_ref,
                     xpad_ref, state_ref):
    st = pl.program_id(2)

    @pl.when(st == 0)
    def _():
        xpad_ref[:, 0:HALO, :] = jnp.zeros((3 * DN_BB, HALO, DN_W), BF16)
        state_ref[...] = jnp.zeros_like(state_ref)

    for bb in range(DN_BB):
        for t, ref in enumerate((q_ref, k_ref, v_ref)):
            xpad_ref[3 * bb + t, HALO:, :] = ref[bb]

    row_i = lax.broadcasted_iota(jnp.int32, (CHUNK, CHUNK), 0)
    col_j = lax.broadcasted_iota(jnp.int32, (CHUNK, CHUNK), 1)
    causal = row_i >= col_j
    strict = row_i > col_j
    upper = (row_i <= col_j).astype(F32)
    eye = (row_i == col_j).astype(F32)
    lane = lax.broadcasted_iota(jnp.int32, (CHUNK, LANES), 1)
    gate_row = lax.broadcasted_iota(jnp.int32, (GATE_ROWS, CHUNK), 0)
    neg_decay_rate = -jnp.exp(alog_ref[...])
    cws = (cwq_ref[...], cwk_ref[...], cwv_ref[...])
    ng = ng_ref[...]

    def chunk_body(c, carry):
        r0 = pl.multiple_of(c * CHUNK, CHUNK)
        win = pl.ds(r0, CHUNK + HALO)
        rows = pl.ds(r0, CHUNK)
        conv, gates, z = [], [], []
        for bb in range(DN_BB):
            conv.append([_silu(_causal_conv(xpad_ref[3 * bb + t, win, :].astype(F32), cws[t], DN_CONV))
                         for t in range(3)])
            ab_t = ab_ref[bb, rows, :].astype(F32).T[:GATE_ROWS, :]
            g_cum_t = jnp.dot(neg_decay_rate * _softplus(ab_t + dtb_ref[...]), upper,
                              preferred_element_type=F32, precision=lax.Precision.HIGHEST)
            gates_t = jnp.where(gate_row < DN_HEADS, g_cum_t, _sigmoid(ab_t))
            gates.append(jnp.concatenate([gates_t, jnp.zeros((CHUNK - GATE_ROWS, CHUNK), F32)], axis=0).T)
            z.append(z_ref[bb, rows, :].astype(F32))

        units = [(bb, i) for bb in range(DN_BB) for i in range(DN_HB)]
        sls = [slice(i * HEAD_DIM, (i + 1) * HEAD_DIM) for i in range(DN_HB)]
        dot = functools.partial(jnp.dot, preferred_element_type=F32)

        g_col = [jnp.sum(jnp.where(lane == i, gates[bb], 0.0), axis=-1, keepdims=True) for bb, i in units]
        beta = [jnp.sum(jnp.where(lane == i + DN_HEADS, gates[bb], 0.0), axis=-1, keepdims=True)
                for bb, i in units]
        g_last = [g[CHUNK - 1:CHUNK, :] for g in g_col]
        eg = [jnp.exp(g) for g in g_col]
        ek = [jnp.exp(gl - g) for gl, g in zip(g_last, g_col)]
        g_b = [jnp.broadcast_to(g, (CHUNK, CHUNK)) for g in g_col]
        decay = [jnp.exp(jnp.where(causal, gb - gb.T, -jnp.inf)) for gb in g_b]

        qf = [conv[bb][0][:, sls[i]] for bb, i in units]
        kf = [conv[bb][1][:, sls[i]] for bb, i in units]
        vf = [conv[bb][2][:, sls[i]] for bb, i in units]
        q = [x * lax.rsqrt(jnp.sum(x * x, axis=-1, keepdims=True) + L2_EPS) * (HEAD_DIM ** -0.5) for x in qf]
        k = [x * lax.rsqrt(jnp.sum(x * x, axis=-1, keepdims=True) + L2_EPS) for x in kf]
        kb = [ki * bi for ki, bi in zip(k, beta)]
        a2 = [_dot_nt(jnp.concatenate([qi, kbi], axis=0).astype(BF16), ki.astype(BF16))
              for qi, kbi, ki in zip(q, kb, k)]
        attn = [a[:CHUNK] * d for a, d in zip(a2, decay)]
        n_mat = [jnp.where(strict, -(a[CHUNK:] * d), 0.0) for a, d in zip(a2, decay)]

        u_mat = [eye + n for n in n_mat]
        p_mat = [dot(n.astype(BF16), n.astype(BF16)) for n in n_mat]
        for _ in range(DN_DOUBLINGS - 1):
            up = [dot(jnp.concatenate([u.astype(BF16), p.astype(BF16)], axis=0), p.astype(BF16))
                  for u, p in zip(u_mat, p_mat)]
            u_mat = [u + x[:CHUNK] for u, x in zip(u_mat, up)]
            p_mat = [x[CHUNK:] for x in up]
        u_mat = [u + dot(u.astype(BF16), p.astype(BF16)) for u, p in zip(u_mat, p_mat)]
        uw = [dot(u.astype(BF16), jnp.concatenate([vi * bi, kbi * egi], axis=1).astype(BF16))
              for u, vi, bi, kbi, egi in zip(u_mat, vf, beta, kb, eg)]

        s_old = [state_ref[u] for u in range(len(units))]
        qw = [dot(jnp.concatenate([qi * egi, x[:, HEAD_DIM:]], axis=0).astype(BF16), s.astype(BF16))
              for qi, egi, x, s in zip(q, eg, uw, s_old)]
        v16 = [(x[:, :HEAD_DIM] - y[CHUNK:]).astype(BF16) for x, y in zip(uw, qw)]
        o = [y[:CHUNK] + dot(a.astype(BF16), v) for y, a, v in zip(qw, attn, v16)]
        s_new = [s * jnp.exp(gl) + _dot_tn((ki * eki).astype(BF16), v)
                 for s, gl, ki, eki, v in zip(s_old, g_last, k, ek, v16)]
        for u, (bb, i) in enumerate(units):
            state_ref[u] = s_new[u]
            on = o[u] * lax.rsqrt(jnp.mean(o[u] * o[u], axis=-1, keepdims=True) + RMS_EPS) * ng
            o_ref[bb, rows, sls[i]] = (on * _silu(z[bb][:, sls[i]])).astype(o_ref.dtype)
        return carry

    lax.fori_loop(0, DN_TS // CHUNK, chunk_body, 0)

    tail = pl.ds(DN_TS, HALO)
    head = pl.ds(0, HALO)
    for t in range(3 * DN_BB):
        xpad_ref[t, head, :] = xpad_ref[t, tail, :]


def _gate_rows(v):
    rows = jnp.zeros((GATE_ROWS, LANES), F32)
    return rows.at[:v.shape[0], :].set(jnp.broadcast_to(v.astype(F32)[:, None], (v.shape[0], LANES)))


def _deltanet(proj3, conv_w, a_log, dt_bias, ng_row):
    assert DN_HB == DN_HEADS
    b, s, _ = proj3.shape
    assert b % DN_BB == 0
    gate = pl.BlockSpec((GATE_ROWS, LANES), lambda bi, hg, st: (0, 0))
    qb, kb_, vb, zb = (off // DN_W for off in (OFF_DNQ, OFF_DNK, OFF_DNV, OFF_DNZ))
    act = lambda base: pl.BlockSpec((DN_BB, DN_TS, DN_W), lambda bi, hg, st: (bi, st, base + hg))
    cw = lambda base: pl.BlockSpec((DN_CONV, DN_W), lambda bi, hg, st: (0, base + hg))
    row = pl.BlockSpec((1, LANES), lambda bi, hg, st: (0, 0))
    return pl.pallas_call(
        _deltanet_kernel,
        grid=(b // DN_BB, DN_HEADS // DN_HB, s // DN_TS),
        in_specs=[
            act(qb), act(kb_), act(vb), act(zb),
            pl.BlockSpec((DN_BB, DN_TS, LANES), lambda bi, hg, st: (bi, st, OFF_AB // LANES)),
            cw(0), cw(DN_WIDTH // DN_W), cw(2 * DN_WIDTH // DN_W),
            gate, gate, row,
        ],
        out_specs=pl.BlockSpec((DN_BB, DN_TS, DN_W), lambda bi, hg, st: (bi, st, hg)),
        out_shape=jax.ShapeDtypeStruct((b, s, DN_WIDTH), BF16),
        scratch_shapes=[
            pltpu.VMEM((3 * DN_BB, DN_TS + HALO, DN_W), BF16),
            pltpu.VMEM((DN_BB * DN_HB, HEAD_DIM, HEAD_DIM), F32),
        ],
        compiler_params=_cparams(("parallel", "parallel", "arbitrary")),
        name="deltanet",
    )(proj3, proj3, proj3, proj3, proj3, conv_w, conv_w, conv_w, _gate_rows(a_log), _gate_rows(dt_bias), ng_row)


SWA_CAST_ROWS = 256
SWA_UNROLL = 4


def _swa_kernel(q_ref, k_ref, v_ref, o_ref, *scratch, seq, dil, slopes):
    n_blocks = seq // dil // SWA_BLOCK
    qi = lax.broadcasted_iota(jnp.int32, (SWA_BLOCK, SWA_BLOCK), 0)
    kj = lax.broadcasted_iota(jnp.int32, (SWA_BLOCK, SWA_BLOCK), 1)
    step_cur = (qi - kj).astype(F32)
    step_prev = step_cur + float(SWA_BLOCK)
    ok_cur = qi >= kj
    ok_prev = kj >= qi
    lane = lax.broadcasted_iota(jnp.int32, (SWA_BLOCK, LANES), 1)
    scale = HEAD_DIM ** -0.5
    heads = range(SWA_HEADS_PER_GROUP)
    sls = [slice(hh * HEAD_DIM, (hh + 1) * HEAD_DIM) for hh in heads]
    m_h = [float(slopes[hh]) * float(dil) for hh in heads]

    if dil > 1:
        (f32_ref,) = scratch

        def cast_body(c, carry):
            rows = pl.ds(pl.multiple_of(c * SWA_CAST_ROWS, SWA_CAST_ROWS), SWA_CAST_ROWS)
            for t, ref in enumerate((q_ref, k_ref, v_ref)):
                for hh in heads:
                    f32_ref[t * SWA_HEADS_PER_GROUP + hh, rows, :] = ref[0, rows, sls[hh]].astype(F32)
            return carry
        lax.fori_loop(0, seq // SWA_CAST_ROWS, cast_body, 0)

        def load(t, rows):
            return [f32_ref[t * SWA_HEADS_PER_GROUP + hh, rows, :].astype(BF16) for hh in heads]
    else:
        refs = (q_ref, k_ref, v_ref)

        def load(t, rows):
            return [refs[t][0, rows, sls[hh]] for hh in heads]

    def block_rows(it):
        r = it // n_blocks
        n = it % n_blocks
        base = n * (SWA_BLOCK * dil) + r
        prev_base = jnp.maximum(n - 1, 0) * (SWA_BLOCK * dil) + r
        if dil > 1:
            return n, pl.ds(base, SWA_BLOCK, stride=dil), pl.ds(prev_base, SWA_BLOCK, stride=dil)
        return (n, pl.ds(pl.multiple_of(base, SWA_BLOCK), SWA_BLOCK),
                pl.ds(pl.multiple_of(prev_base, SWA_BLOCK), SWA_BLOCK))

    def blocks_body(it2, carry):
        qn, kc, kp, vc, vp, ok_p, slope, out_rows = [], [], [], [], [], [], [], []
        for u in range(SWA_UNROLL):
            n, rows, prev_rows = block_rows(it2 * SWA_UNROLL + u)
            qn += load(0, rows)
            kc += load(1, rows)
            vc += load(2, rows)
            kp += load(1, prev_rows)
            vp += load(2, prev_rows)
            ok_p += [jnp.logical_and(ok_prev, n > 0)] * len(heads)
            slope += m_h
            out_rows.append(rows)

        s_cur = [jnp.where(ok_cur, _dot_nt(qh, kh) * scale - m * step_cur, -jnp.inf)
                 for qh, kh, m in zip(qn, kc, slope)]
        s_prev = [jnp.where(ok, _dot_nt(qh, kh) * scale - m * step_prev, -jnp.inf)
                  for ok, qh, kh, m in zip(ok_p, qn, kp, slope)]
        mx = [jnp.max(jnp.maximum(a, b), axis=-1, keepdims=True) for a, b in zip(s_cur, s_prev)]
        e_cur = [jnp.exp(a - m) for a, m in zip(s_cur, mx)]
        e_prev = [jnp.exp(b - m) for b, m in zip(s_prev, mx)]
        den = [jnp.sum(a + b, axis=-1, keepdims=True) for a, b in zip(e_cur, e_prev)]
        pv = [jnp.dot(a.astype(BF16), vch, preferred_element_type=F32)
              + jnp.dot(b.astype(BF16), vph, preferred_element_type=F32)
              for a, b, vch, vph in zip(e_cur, e_prev, vc, vp)]
        for u, rows in enumerate(out_rows):
            lse_tile = jnp.zeros((SWA_BLOCK, LANES), F32)
            for hh in heads:
                i = u * len(heads) + hh
                o_ref[0, hh, rows, :] = pv[i] * (1.0 / den[i])
                lse_tile = jnp.where(lane == hh, mx[i] + jnp.log(den[i]), lse_tile)
            o_ref[0, SWA_HEADS_PER_GROUP, rows, :] = lse_tile
        return carry

    lax.fori_loop(0, seq // SWA_BLOCK // SWA_UNROLL, blocks_body, 0)


def _swa_group(proj3, gi):
    batch, seq, _ = proj3.shape
    window, dil = SWA_GROUPS[gi]
    assert window // dil == SWA_BLOCK and seq % (dil * SWA_BLOCK) == 0
    slopes = 2.0 ** (-ALIBI_MAX_BIAS * np.arange(1, SWA_HEADS + 1) / SWA_HEADS)
    slopes = slopes.reshape(len(SWA_GROUPS), SWA_HEADS_PER_GROUP)[gi]

    def spec(which):
        cb = (OFF_SWA + which * SWA_HEADS * HEAD_DIM) // SWA_GROUP_WIDTH + gi
        return pl.BlockSpec((1, seq, SWA_GROUP_WIDTH), lambda bi: (bi, 0, cb))

    scratch = [pltpu.VMEM((3 * SWA_HEADS_PER_GROUP, seq, HEAD_DIM), F32)] if dil > 1 else []
    return pl.pallas_call(
        functools.partial(_swa_kernel, seq=seq, dil=dil, slopes=tuple(float(s) for s in slopes)),
        grid=(batch,),
        in_specs=[spec(0), spec(1), spec(2)],
        out_specs=pl.BlockSpec((1, SWA_OUT_SLOTS, seq, HEAD_DIM), lambda bi: (bi, 0, 0, 0)),
        out_shape=jax.ShapeDtypeStruct((batch, SWA_OUT_SLOTS, seq, HEAD_DIM), F32),
        scratch_shapes=scratch,
        compiler_params=_cparams(("parallel",)),
        name=f"swa_g{gi}",
    )(proj3, proj3, proj3)


MG_TM = 256


def _merge_kernel(x_ref, ydn_ref, gate_ref, scc_ref, scb_ref, scx_ref, scc_h_ref, scx_h_ref,
                  s0_ref, s1_ref, s2_ref, scw_ref, wdn_ref, wsc_ref, wswa_ref, wout_ref, o_ref,
                  *, tiles_per_seq):
    i = pl.program_id(0)
    first = (i % tiles_per_seq) == 0

    cx_cur = scc_ref[...].astype(F32) * scx_ref[...].astype(F32)
    cx_halo = scc_h_ref[...].astype(F32) * scx_h_ref[...].astype(F32)
    cx_halo = jnp.where(first, 0.0, cx_halo)
    cx = jnp.concatenate([cx_halo, cx_cur], axis=0)
    y_sc = scb_ref[...].astype(F32) * _causal_conv(cx, scw_ref[...], SC_CONV)

    outs = (s0_ref, s1_ref, s2_ref)
    lses = [o[0, SWA_HEADS_PER_GROUP] for o in outs]
    mx = jnp.maximum(jnp.maximum(lses[0], lses[1]), lses[2])
    es = [jnp.exp(l - mx) for l in lses]
    inv = 1.0 / (es[0] + es[1] + es[2])
    heads = []
    for hh in range(SWA_HEADS_PER_GROUP):
        acc = None
        for gi in range(3):
            wt = (es[gi] * inv)[:, hh:hh + 1]
            term = wt * outs[gi][0, hh]
            acc = term if acc is None else acc + term
        heads.append(acc)
    y_swa = jnp.concatenate(heads, axis=1)

    gates = gate_ref[...].astype(F32)
    merged = (_sigmoid(gates[:, 0:D_MODEL])
              * jnp.dot(ydn_ref[...], wdn_ref[...], preferred_element_type=F32)
              + _sigmoid(gates[:, D_MODEL:2 * D_MODEL])
              * jnp.dot(y_sc.astype(BF16), wsc_ref[...], preferred_element_type=F32)
              + _sigmoid(gates[:, 2 * D_MODEL:3 * D_MODEL])
              * jnp.dot(y_swa.astype(BF16), wswa_ref[...], preferred_element_type=F32))
    o_ref[...] = x_ref[...] + jnp.dot(merged.astype(BF16), wout_ref[...], preferred_element_type=F32)


def _merge(x2d, ydn2, proj2, swa_outs, sc_conv_w, wdn, wsc, wswa, wout, seq):
    m = x2d.shape[0]
    tiles_per_seq = seq // MG_TM
    halo_blocks = MG_TM // HALO
    rows = lambda w, cb: pl.BlockSpec((MG_TM, w), lambda i: (i, cb))
    halo = lambda cb: pl.BlockSpec((HALO, SC_WIDTH), lambda i: (jnp.maximum(i * halo_blocks - 1, 0), cb))
    full = lambda a: pl.BlockSpec(a.shape, lambda i: (0, 0), pipeline_mode=pl.Buffered(1))
    swa = pl.BlockSpec((1, SWA_OUT_SLOTS, MG_TM, HEAD_DIM),
                       lambda i: (i // tiles_per_seq, 0, i % tiles_per_seq, 0))
    return pl.pallas_call(
        functools.partial(_merge_kernel, tiles_per_seq=tiles_per_seq),
        grid=(m // MG_TM,),
        in_specs=[
            rows(D_MODEL, 0),
            rows(DN_WIDTH, 0),
            rows(3 * D_MODEL, OFF_GATE // (3 * D_MODEL)),
            rows(SC_WIDTH, OFF_SCC // SC_WIDTH), rows(SC_WIDTH, OFF_SCB // SC_WIDTH),
            rows(SC_WIDTH, OFF_SCX // SC_WIDTH),
            halo(OFF_SCC // SC_WIDTH), halo(OFF_SCX // SC_WIDTH),
            swa, swa, swa,
            full(sc_conv_w), full(wdn), full(wsc), full(wswa), full(wout),
        ],
        out_specs=rows(D_MODEL, 0),
        out_shape=jax.ShapeDtypeStruct((m, D_MODEL), F32),
        compiler_params=_cparams(("parallel",)),
        name="merge",
    )(x2d, ydn2, proj2, proj2, proj2, proj2, proj2, proj2, *swa_outs, sc_conv_w, wdn, wsc, wswa, wout)


RT_TM = 1024
RT_PARTS = 8
ROUTE_LANE0 = MOE_GROUPS
R_E1, R_E2, R_W1, R_W2, R_RANK1, R_RANK2 = range(6)
HALF = D_MODEL // 2


def _pack_pair(lo, hi):
    lo_b = lax.bitcast_convert_type(lo.astype(BF16).astype(F32), jnp.uint32)
    hi_b = lax.bitcast_convert_type(hi.astype(BF16).astype(F32), jnp.uint32)
    return (hi_b & jnp.uint32(0xFFFF0000)) | (lo_b >> jnp.uint32(16))


def _unpack_pair(p):
    lo = lax.bitcast_convert_type(p << jnp.uint32(16), F32)
    hi = lax.bitcast_convert_type(p & jnp.uint32(0xFFFF0000), F32)
    return lo, hi


def _router_kernel(x_ref, g_ref, wrh_ref, wrl_ref, br_ref, hp_ref, route_ref, cnt_ref, carry_ref):
    @pl.when(pl.program_id(0) == 0)
    def _():
        carry_ref[...] = jnp.zeros_like(carry_ref)

    tp = RT_TM // RT_PARTS
    part_rows = [slice(p * tp, (p + 1) * tp) for p in range(RT_PARTS)]
    w_hi, w_lo, bias = wrh_ref[...], wrl_ref[...], br_ref[...]
    dot = functools.partial(jnp.dot, preferred_element_type=F32)
    lane = lax.broadcasted_iota(jnp.int32, (tp, LANES), 1)
    big = jnp.int32(LANES)
    neg = -jnp.inf

    hs = [_rmsnorm_rows(x_ref[rows, :], g_ref[...]) for rows in part_rows]
    for rows, h in zip(part_rows, hs):
        hp_ref[rows, :] = _pack_pair(h[:, :HALF], h[:, HALF:])
    h_hi = [h.astype(BF16) for h in hs]
    h_lo = [(h - hi.astype(F32)).astype(BF16) for h, hi in zip(hs, h_hi)]
    logits = [dot(hi, w_hi) + dot(lo, w_hi) + dot(hi, w_lo) + bias for hi, lo in zip(h_hi, h_lo)]

    def route(lg):
        gl = jnp.where(lane < MOE_GROUPS, lg, neg)
        gmax = jnp.max(gl, axis=-1, keepdims=True)
        grp = jnp.min(jnp.where(gl == gmax, lane, big), axis=-1, keepdims=True)
        g_w = 1.0 / jnp.sum(jnp.exp(gl - gmax), axis=-1, keepdims=True)
        lo = ROUTE_LANE0 + grp * MOE_EPG
        el = jnp.where(jnp.logical_and(lane >= lo, lane < lo + MOE_EPG), lg, neg)
        m1 = jnp.max(el, axis=-1, keepdims=True)
        i1 = jnp.min(jnp.where(el == m1, lane, big), axis=-1, keepdims=True)
        el2 = jnp.where(lane == i1, neg, el)
        m2 = jnp.max(el2, axis=-1, keepdims=True)
        i2 = jnp.min(jnp.where(el2 == m2, lane, big), axis=-1, keepdims=True)
        e2 = jnp.exp(m2 - m1)
        return i1, i2, g_w / (1.0 + e2), g_w * e2 / (1.0 + e2)

    routed = [route(lg) for lg in logits]

    onehot = [jnp.where(jnp.logical_or(lane == i1, lane == i2), 1.0, 0.0) for i1, i2, _, _ in routed]
    earlier = (lax.broadcasted_iota(jnp.int32, (tp, tp), 0) > lax.broadcasted_iota(jnp.int32, (tp, tp), 1))
    earlier = jnp.where(earlier, 1.0, 0.0).astype(BF16)
    within = [dot(earlier, oh.astype(BF16)) for oh in onehot]
    seen = carry_ref[...]
    for rows, (i1, i2, w1, w2), oh, inside in zip(part_rows, routed, onehot, within):
        before = seen + inside
        rank1 = jnp.sum(jnp.where(lane == i1, before, 0.0), axis=-1, keepdims=True)
        rank2 = jnp.sum(jnp.where(lane == i2, before, 0.0), axis=-1, keepdims=True)
        seen = seen + jnp.sum(oh, axis=0, keepdims=True)
        rec = jnp.zeros((tp, LANES), F32)
        for ln, val in ((R_E1, (i1 - ROUTE_LANE0).astype(F32)), (R_E2, (i2 - ROUTE_LANE0).astype(F32)),
                        (R_W1, w1), (R_W2, w2), (R_RANK1, rank1), (R_RANK2, rank2)):
            rec = jnp.where(lane == ln, val, rec)
        route_ref[rows, :] = rec
    carry_ref[...] = seen
    cnt_ref[...] = seen


def _router(x2d, g_row, w_route_hi, w_route_lo, b_route):
    m = x2d.shape[0]
    return pl.pallas_call(
        _router_kernel,
        grid=(m // RT_TM,),
        in_specs=[
            pl.BlockSpec((RT_TM, D_MODEL), lambda i: (i, 0)),
            pl.BlockSpec((1, D_MODEL), lambda i: (0, 0)),
            pl.BlockSpec((D_MODEL, LANES), lambda i: (0, 0)),
            pl.BlockSpec((D_MODEL, LANES), lambda i: (0, 0)),
            pl.BlockSpec((1, LANES), lambda i: (0, 0)),
        ],
        out_specs=[pl.BlockSpec((RT_TM, HALF), lambda i: (i, 0)),
                   pl.BlockSpec((RT_TM, LANES), lambda i: (i, 0)),
                   pl.BlockSpec((1, LANES), lambda i: (0, 0))],
        out_shape=[jax.ShapeDtypeStruct((m, HALF), jnp.uint32), jax.ShapeDtypeStruct((m, LANES), F32),
                   jax.ShapeDtypeStruct((1, LANES), F32)],
        scratch_shapes=[pltpu.VMEM((1, LANES), F32)],
        compiler_params=_cparams(("arbitrary",)),
        name="router",
    )(x2d, g_row, w_route_hi, w_route_lo, b_route)


EXP_TM = 512


def _n_row_tiles(n_tokens):
    return 2 * n_tokens // EXP_TM + MOE_EXPERTS


def _positions_kernel(route_ref, offs_ref, pos_ref):
    rec = route_ref[...]
    lane = lax.broadcasted_iota(jnp.int32, rec.shape, 1)
    offs = offs_ref[...]

    def first_row(e):
        return jnp.sum(jnp.where(lane == e.astype(jnp.int32), offs, 0.0), axis=-1, keepdims=True)

    pos1 = first_row(rec[:, R_E1:R_E1 + 1]) + rec[:, R_RANK1:R_RANK1 + 1]
    pos2 = first_row(rec[:, R_E2:R_E2 + 1]) + rec[:, R_RANK2:R_RANK2 + 1]
    pos_ref[...] = jnp.where(lane == 0, pos1, jnp.where(lane == 1, pos2, 0.0)).astype(jnp.int32)


def _moe_plan(route, cnt):
    m = route.shape[0]
    n_tiles = _n_row_tiles(m)
    counts = cnt[0, ROUTE_LANE0:ROUTE_LANE0 + MOE_EXPERTS].astype(jnp.int32)
    padded = (counts + EXP_TM - 1) // EXP_TM * EXP_TM
    ends = jnp.cumsum(padded)
    offs = ends - padded
    pos = pl.pallas_call(
        _positions_kernel,
        grid=(m // RT_TM,),
        in_specs=[pl.BlockSpec((RT_TM, LANES), lambda i: (i, 0)), pl.BlockSpec((1, LANES), lambda i: (0, 0))],
        out_specs=pl.BlockSpec((RT_TM, LANES), lambda i: (i, 0)),
        out_shape=jax.ShapeDtypeStruct((m, LANES), jnp.int32),
        compiler_params=_cparams(("parallel",)),
        name="moe_positions",
    )(route, _lane_row(offs))
    n_active = ends[-1] // EXP_TM
    tile_row = jnp.maximum(jnp.minimum(jnp.arange(n_tiles), n_active - 1), 0) * EXP_TM
    tile_expert = jnp.minimum(jnp.sum(ends[None, :] <= tile_row[:, None], axis=1), MOE_EXPERTS - 1)
    used = counts > 0
    ids = jnp.where(used, jnp.arange(MOE_EXPERTS), MOE_EXPERTS)
    next_used = jnp.concatenate([lax.cummin(ids, reverse=True)[1:], jnp.full((1,), MOE_EXPERTS, ids.dtype)])
    parity = (jnp.cumsum(used.astype(jnp.int32)) - 1) % 2
    i32 = lambda a: a.astype(jnp.int32)
    return dict(pos1=pos[:, 0], pos2=pos[:, 1], counts=counts, offs=i32(offs),
                tile_expert=i32(tile_expert), n_active=i32(n_active).reshape(1),
                next_expert=i32(next_used[tile_expert]), parity=i32(parity[tile_expert]))


DSP_TB = 512
ROW_UNROLL = 16


def _dispatch_kernel(cnt_ref, offs_ref, nact_ref, pos1_ref, pos2_ref, hp_ref, xs_ref, zero_ref, sem, zero_sem):
    n_tiles = xs_ref.shape[0] // EXP_TM

    @pl.when(pl.program_id(0) == 0)
    def _():
        zero_ref[...] = jnp.zeros_like(zero_ref)

        def pad_pieces(e):
            pad = (-cnt_ref[e]) & (EXP_TM - 1)
            first = offs_ref[e] + cnt_ref[e]
            head = jnp.minimum((-first) & (SUBLANES - 1), pad)
            for k in range(SUBLANES - 1):
                yield k < head, pltpu.make_async_copy(zero_ref.at[pl.ds(0, 1), :],
                                                      xs_ref.at[pl.ds(first + k, 1), :], zero_sem)
            body = pad - head
            b = EXP_TM // 2
            while b >= SUBLANES:
                row = pl.multiple_of(first + head + (body & ~(2 * b - 1)), SUBLANES)
                yield (body & b) != 0, pltpu.make_async_copy(zero_ref.at[pl.ds(0, b), :],
                                                             xs_ref.at[pl.ds(row, b), :], zero_sem)
                b //= 2

        def tile_copy(t):
            rows = pl.ds(pl.multiple_of(t * EXP_TM, EXP_TM), EXP_TM)
            return pltpu.make_async_copy(zero_ref, xs_ref.at[rows, :], zero_sem)

        def for_all(method):
            def per_expert(e, c):
                for cond, cp in pad_pieces(e):
                    pl.when(cond)(getattr(cp, method))
                return c
            lax.fori_loop(0, MOE_EXPERTS, per_expert, 0)

            def per_tile(t, c):
                getattr(tile_copy(t), method)()
                return c
            lax.fori_loop(nact_ref[0], n_tiles, per_tile, 0)

        for_all("start")
        for_all("wait")

    def start(j8, c):
        for u in range(ROW_UNROLL):
            j = j8 * ROW_UNROLL + u
            src = hp_ref.at[pl.ds(j, 1), :]
            pltpu.make_async_copy(src, xs_ref.at[pl.ds(pos1_ref[0, 0, j], 1), :], sem).start(priority=0)
            pltpu.make_async_copy(src, xs_ref.at[pl.ds(pos2_ref[0, 0, j], 1), :], sem).start(priority=1)
        return c

    lax.fori_loop(0, DSP_TB // ROW_UNROLL, start, 0)
    for _ in range(2):
        pltpu.make_async_copy(hp_ref, xs_ref.at[pl.ds(0, DSP_TB), :], sem).wait()


def _dispatch(hp, pos1, pos2, counts, offs, n_active):
    m = hp.shape[0]
    n_rows = _n_row_tiles(m) * EXP_TM
    pos_spec = pl.BlockSpec((1, 1, DSP_TB), lambda i, *_: (i, 0, 0), memory_space=pltpu.SMEM)
    return pl.pallas_call(
        _dispatch_kernel,
        grid_spec=pltpu.PrefetchScalarGridSpec(
            num_scalar_prefetch=3,
            grid=(m // DSP_TB,),
            in_specs=[pos_spec, pos_spec, pl.BlockSpec((DSP_TB, HALF), lambda i, *_: (i, 0))],
            out_specs=pl.BlockSpec(memory_space=pl.ANY),
            scratch_shapes=[pltpu.VMEM((EXP_TM, HALF), jnp.uint32), pltpu.SemaphoreType.DMA(()),
                            pltpu.SemaphoreType.DMA(())],
        ),
        out_shape=jax.ShapeDtypeStruct((n_rows, HALF), jnp.uint32),
        compiler_params=_cparams(("arbitrary",)),
        name="moe_dispatch",
    )(counts, offs, n_active, pos1.reshape(m // DSP_TB, 1, DSP_TB), pos2.reshape(m // DSP_TB, 1, DSP_TB), hp)


W_CAST_ROWS = 256


def _cast_rows(src_ref, dst_ref):
    n_rows = dst_ref.shape[0]
    slab = min(W_CAST_ROWS, n_rows)

    def body(r, c):
        rows = pl.ds(pl.multiple_of(r * slab, slab), slab)
        dst_ref[rows, :] = src_ref[rows, :].astype(BF16)
        return c
    lax.fori_loop(0, n_rows // slab, body, 0)


def _expert_kernel(te_ref, nact_ref, next_ref, par_ref, xs_ref, wg_hbm, wu_hbm, wd_hbm, ys_ref,
                   sg_ref, su_ref, sd_ref, wg16_ref, wu16_ref, wd16_ref, sem, *, layer):
    i = pl.program_id(0)
    active = i < nact_ref[0]
    expert = te_ref[i]
    slot = par_ref[i]
    new_expert = jnp.logical_or(i == 0, expert != te_ref[jnp.maximum(i - 1, 0)])

    def copies(e, s):
        return [pltpu.make_async_copy(hbm.at[layer, e], stage.at[s], sem.at[s])
                for hbm, stage in ((wg_hbm, sg_ref), (wu_hbm, su_ref), (wd_hbm, sd_ref))]

    @pl.when(jnp.logical_and(active, i == 0))
    def _():
        for cp in copies(expert, slot):
            cp.start()

    @pl.when(jnp.logical_and(active, new_expert))
    def _():
        for cp in copies(expert, slot):
            cp.wait()

        @pl.when(next_ref[i] < MOE_EXPERTS)
        def _():
            for cp in copies(next_ref[i], 1 - slot):
                cp.start()

        _cast_rows(sg_ref.at[slot], wg16_ref)
        _cast_rows(su_ref.at[slot], wu16_ref)
        _cast_rows(sd_ref.at[slot], wd16_ref)

    @pl.when(active)
    def _():
        lo, hi = _unpack_pair(xs_ref[...])
        lo, hi = lo.astype(BF16), hi.astype(BF16)
        dot = functools.partial(jnp.dot, preferred_element_type=F32)
        gate = dot(lo, wg16_ref[:HALF, :]) + dot(hi, wg16_ref[HALF:, :])
        up = dot(lo, wu16_ref[:HALF, :]) + dot(hi, wu16_ref[HALF:, :])
        y = dot((_silu(gate) * up).astype(BF16), wd16_ref[...])
        ys_ref[...] = _pack_pair(y[:, :HALF], y[:, HALF:])

    @pl.when(jnp.logical_not(active))
    def _():
        ys_ref[...] = jnp.zeros_like(ys_ref)


def _experts(xs, tile_expert, n_active, next_expert, parity, wg, wu, wd, layer):
    n_tiles = xs.shape[0] // EXP_TM
    rows = pl.BlockSpec((EXP_TM, HALF),
                        lambda i, te, na, nx, par: (jnp.maximum(jnp.minimum(i, na[0] - 1), 0), 0))
    out_rows = pl.BlockSpec((EXP_TM, HALF), lambda i, te, na, nx, par: (i, 0))
    hbm = pl.BlockSpec(memory_space=pl.ANY)
    return pl.pallas_call(
        functools.partial(_expert_kernel, layer=layer),
        grid_spec=pltpu.PrefetchScalarGridSpec(
            num_scalar_prefetch=4,
            grid=(n_tiles,),
            in_specs=[rows, hbm, hbm, hbm],
            out_specs=out_rows,
            scratch_shapes=[pltpu.VMEM((2, D_MODEL, MOE_FF), F32), pltpu.VMEM((2, D_MODEL, MOE_FF), F32),
                            pltpu.VMEM((2, MOE_FF, D_MODEL), F32),
                            pltpu.VMEM((D_MODEL, MOE_FF), BF16), pltpu.VMEM((D_MODEL, MOE_FF), BF16),
                            pltpu.VMEM((MOE_FF, D_MODEL), BF16),
                            pltpu.SemaphoreType.DMA((2,))],
        ),
        out_shape=jax.ShapeDtypeStruct(xs.shape, jnp.uint32),
        compiler_params=_cparams(("arbitrary",)),
        name="moe_experts",
    )(tile_expert, n_active, next_expert, parity, xs, wg, wu, wd)


CMB_TC = 512


def _combine_kernel(pos1_ref, pos2_ref, pos1_next_ref, pos2_next_ref, x_ref, route_ref, g_ref, ys_ref,
                    o_ref, buf_ref, sem, *, final_norm):
    i = pl.program_id(0)
    slot = i % 2

    def gather(p1_ref, p2_ref, s):
        def start(j8, c):
            for u in range(ROW_UNROLL):
                j = j8 * ROW_UNROLL + u
                for k, p_ref in enumerate((p1_ref, p2_ref)):
                    pltpu.make_async_copy(ys_ref.at[pl.ds(p_ref[0, 0, j], 1), :],
                                          buf_ref.at[s, k, pl.ds(j, 1), :], sem.at[s]).start(priority=k)
            return c
        lax.fori_loop(0, CMB_TC // ROW_UNROLL, start, 0)

    @pl.when(i == 0)
    def _():
        gather(pos1_ref, pos2_ref, 0)

    @pl.when(i + 1 < pl.num_programs(0))
    def _():
        gather(pos1_next_ref, pos2_next_ref, 1 - slot)

    for k in range(2):
        pltpu.make_async_copy(ys_ref.at[pl.ds(0, CMB_TC), :], buf_ref.at[slot, k], sem.at[slot]).wait()

    rec = route_ref[...]
    w1, w2 = rec[:, R_W1:R_W1 + 1], rec[:, R_W2:R_W2 + 1]
    lo1, hi1 = _unpack_pair(buf_ref[slot, 0])
    lo2, hi2 = _unpack_pair(buf_ref[slot, 1])
    out_lo = x_ref[:, :HALF] + w1 * lo1 + w2 * lo2
    out_hi = x_ref[:, HALF:] + w1 * hi1 + w2 * hi2
    if final_norm:
        ms = (jnp.sum(out_lo * out_lo, axis=-1, keepdims=True)
              + jnp.sum(out_hi * out_hi, axis=-1, keepdims=True)) * (1.0 / D_MODEL)
        inv = lax.rsqrt(ms + RMS_EPS)
        out_lo = out_lo * inv * g_ref[:, :HALF]
        out_hi = out_hi * inv * g_ref[:, HALF:]
    o_ref[:, :HALF] = out_lo
    o_ref[:, HALF:] = out_hi


def _combine(x2d, route, ys, pos1, pos2, g_row, final_norm):
    m = x2d.shape[0]
    n_steps = m // CMB_TC
    pos_spec = pl.BlockSpec((1, 1, CMB_TC), lambda i: (i, 0, 0), memory_space=pltpu.SMEM)
    next_spec = pl.BlockSpec((1, 1, CMB_TC), lambda i: (jnp.minimum(i + 1, n_steps - 1), 0, 0),
                             memory_space=pltpu.SMEM)
    pos1, pos2 = pos1.reshape(n_steps, 1, CMB_TC), pos2.reshape(n_steps, 1, CMB_TC)
    return pl.pallas_call(
        functools.partial(_combine_kernel, final_norm=final_norm),
        grid=(n_steps,),
        in_specs=[pos_spec, pos_spec, next_spec, next_spec,
                  pl.BlockSpec((CMB_TC, D_MODEL), lambda i: (i, 0)),
                  pl.BlockSpec((CMB_TC, LANES), lambda i: (i, 0)),
                  pl.BlockSpec((1, D_MODEL), lambda i: (0, 0)),
                  pl.BlockSpec(memory_space=pl.ANY)],
        out_specs=pl.BlockSpec((CMB_TC, D_MODEL), lambda i: (i, 0)),
        out_shape=jax.ShapeDtypeStruct((m, D_MODEL), F32),
        scratch_shapes=[pltpu.VMEM((2, 2, CMB_TC, HALF), jnp.uint32), pltpu.SemaphoreType.DMA((2,))],
        compiler_params=_cparams(("arbitrary",)),
        name="moe_combine",
    )(pos1, pos2, pos1, pos2, x2d, route, g_row, ys)


RL_TN = 512
RL_PIECE = 256
SRC_A_END = 4096
SRC_AB = 16
SRC_COLS = 17936


def _relayout_plan():
    n_rest = (SRC_COLS - SRC_A_END - SRC_AB) // RL_TN
    rest = lambda s: SRC_A_END + SRC_AB + RL_TN * s
    starts = ([rest(s) for s in range(15, n_rest)]
              + [RL_TN * a for a in range(SRC_A_END // RL_TN)]
              + [rest(s) for s in range(15)]
              + [SRC_A_END])
    assert len(starts) * RL_TN == PROJ_COLS and all(s % 16 == 0 and s + RL_TN <= SRC_COLS for s in starts)
    return np.asarray(starts, np.int32)


def _relayout_kernel(start_ref, wt_ref, o_ref, buf_ref, sem, *, layer):
    t = pl.program_id(0)
    n_tiles = pl.num_programs(0)
    slot = t % 2

    def fetch(tile, s):
        first = pl.multiple_of(start_ref[tile], 16)
        pltpu.make_async_copy(wt_ref.at[layer, pl.ds(first, RL_TN), :], buf_ref.at[s], sem.at[s]).start()

    @pl.when(t == 0)
    def _():
        fetch(0, 0)

    @pl.when(t + 1 < n_tiles)
    def _():
        fetch(t + 1, 1 - slot)

    pltpu.make_async_copy(wt_ref.at[layer, pl.ds(0, RL_TN), :], buf_ref.at[slot], sem.at[slot]).wait()

    row = lax.broadcasted_iota(jnp.int32, (RL_TN, RL_PIECE), 0)
    keep = jnp.logical_or(t + 1 < n_tiles, row < SRC_AB)
    for c in range(D_MODEL // RL_PIECE):
        cols = slice(c * RL_PIECE, (c + 1) * RL_PIECE)
        piece = jnp.where(keep, buf_ref[slot, :, cols], 0.0)
        o_ref[cols, :] = piece.T.astype(BF16)


def _prep_w_in(w_all, layer):
    wt = jnp.swapaxes(w_all, 1, 2)
    return pl.pallas_call(
        functools.partial(_relayout_kernel, layer=layer),
        grid_spec=pltpu.PrefetchScalarGridSpec(
            num_scalar_prefetch=1,
            grid=(PROJ_COLS // RL_TN,),
            in_specs=[pl.BlockSpec(memory_space=pl.ANY)],
            out_specs=pl.BlockSpec((D_MODEL, RL_TN), lambda t, starts: (0, t)),
            scratch_shapes=[pltpu.VMEM((2, RL_TN, D_MODEL), F32), pltpu.SemaphoreType.DMA((2,))],
        ),
        out_shape=jax.ShapeDtypeStruct((D_MODEL, PROJ_COLS), BF16),
        compiler_params=_cparams(("arbitrary",)),
        name="w_in_relayout",
    )(jnp.asarray(_relayout_plan()), wt)


def _lane_row(v):
    return jnp.zeros((1, LANES), F32).at[0, :v.shape[0]].set(v.astype(F32))


def _layer(x2d, batch, seq, p, final_g_row):
    proj2 = _in_proj(x2d, p["norm_mix_g"], p["w_in"])
    proj3 = proj2.reshape(batch, seq, PROJ_COLS)
    ydn = _deltanet(proj3, p["dn_conv_w"], p["dn_a_log"], p["dn_dt_bias"], p["dn_norm_g"])
    swa_outs = [_swa_group(proj3, gi) for gi in range(len(SWA_GROUPS))]
    x2d = _merge(x2d, ydn.reshape(batch * seq, DN_WIDTH), proj2, swa_outs, p["sc_conv_w"],
                 p["w_branch_dn"], p["w_branch_sc"], p["w_branch_swa"], p["w_out"], seq)
    hp, route, cnt = _router(x2d, p["norm_ffn_g"], p["w_route_hi"], p["w_route_lo"], p["b_route"])
    plan = _moe_plan(route, cnt)
    xs = _dispatch(hp, plan["pos1"], plan["pos2"], plan["counts"], plan["offs"], plan["n_active"])
    ys = _experts(xs, plan["tile_expert"], plan["n_active"], plan["next_expert"], plan["parity"],
                  p["expert_w_gate"], p["expert_w_up"], p["expert_w_down"], p["layer"])
    is_last = final_g_row is not None
    g_row = final_g_row if is_last else p["norm_ffn_g"]
    return _combine(x2d, route, ys, plan["pos1"], plan["pos2"], g_row, is_last)


def kernel(x, norm_mix_g, w_in, dn_conv_w, dn_a_log, dn_dt_bias, dn_norm_g, sc_conv_w, w_branch_dn, w_branch_sc, w_branch_swa, w_out, norm_ffn_g, router_group_w, router_group_b, router_expert_w, router_expert_b, expert_w_gate, expert_w_up, expert_w_down, final_norm_g):
    batch, seq, _ = x.shape
    depth = w_in.shape[0]
    x2d = x.reshape(batch * seq, D_MODEL)
    for l in range(depth):
        w_route = jnp.concatenate([router_group_w[l], router_expert_w[l]], axis=1)
        w_route = jnp.pad(w_route, ((0, 0), (0, LANES - w_route.shape[1])))
        p = dict(
            norm_mix_g=norm_mix_g[l].reshape(1, D_MODEL),
            w_in=_prep_w_in(w_in, l),
            dn_conv_w=dn_conv_w[l],
            dn_a_log=dn_a_log[l],
            dn_dt_bias=dn_dt_bias[l],
            dn_norm_g=dn_norm_g[l].reshape(1, HEAD_DIM),
            sc_conv_w=sc_conv_w[l],
            w_branch_dn=w_branch_dn[l].astype(BF16),
            w_branch_sc=w_branch_sc[l].astype(BF16),
            w_branch_swa=w_branch_swa[l].astype(BF16),
            w_out=w_out[l].astype(BF16),
            norm_ffn_g=norm_ffn_g[l].reshape(1, D_MODEL),
            w_route_hi=w_route.astype(BF16),
            w_route_lo=(w_route - w_route.astype(BF16).astype(F32)).astype(BF16),
            b_route=_lane_row(jnp.concatenate([router_group_b[l], router_expert_b[l]])),
            expert_w_gate=expert_w_gate,
            expert_w_up=expert_w_up,
            expert_w_down=expert_w_down,
            layer=l,
        )
        final_g_row = final_norm_g.reshape(1, D_MODEL) if l == depth - 1 else None
        x2d = _layer(x2d, batch, seq, p, final_g_row)
    return x2d.reshape(batch, seq, D_MODEL)
```

```python
import functools

import jax
import jax.numpy as jnp
import numpy as np
from jax import lax
from jax.experimental import pallas as pl
from jax.experimental.pallas import tpu as pltpu

F32 = jnp.float32
BF16 = jnp.bfloat16

D_MODEL = 2048
RMS_EPS = 1e-6
L2_EPS = 1e-6

DN_HEADS = 8
HEAD_DIM = 128
DN_WIDTH = DN_HEADS * HEAD_DIM
DN_CONV = 4
SC_WIDTH = 1024
SC_CONV = 3
SWA_GROUPS = ((128, 1), (512, 4), (2048, 16))
SWA_HEADS_PER_GROUP = 4
SWA_HEADS = 12
SWA_GROUP_WIDTH = SWA_HEADS_PER_GROUP * HEAD_DIM
SWA_BLOCK = 128
ALIBI_MAX_BIAS = 8.0
MOE_GROUPS = 4
MOE_EPG = 8
MOE_EXPERTS = 32
MOE_FF = 512

OFF_GATE = 0
OFF_DNQ = 6144
OFF_DNK = OFF_DNQ + DN_WIDTH
OFF_DNV = OFF_DNK + DN_WIDTH
OFF_DNZ = OFF_DNV + DN_WIDTH
OFF_SCC = OFF_DNZ + DN_WIDTH
OFF_SCB = OFF_SCC + SC_WIDTH
OFF_SCX = OFF_SCB + SC_WIDTH
OFF_SWA = OFF_SCX + SC_WIDTH
OFF_AB = OFF_SWA + 3 * SWA_HEADS * HEAD_DIM
PROJ_COLS = 18432

LANES = 128
SUBLANES = 8
SWA_OUT_SLOTS = SWA_HEADS_PER_GROUP + 1
CHUNK = 128
HALO = 16

V7X_VMEM_BYTES = 64 * 1024 * 1024
VMEM_LIMIT = V7X_VMEM_BYTES - 8 * 1024 * 1024


def _cparams(sem):
    return pltpu.CompilerParams(dimension_semantics=sem, vmem_limit_bytes=VMEM_LIMIT)


def _sigmoid(x):
    return 1.0 / (1.0 + jnp.exp(-x))


def _silu(x):
    return x * _sigmoid(x)


def _softplus(x):
    return jnp.maximum(x, 0.0) + jnp.log(1.0 + jnp.exp(-jnp.abs(x)))


IN_TM = 1024
IN_TN = 2304
NORM_ROWS = 64


def _rmsnorm_rows(x, g):
    ms = jnp.mean(x * x, axis=-1, keepdims=True)
    return x * lax.rsqrt(ms + RMS_EPS) * g


def _in_proj_kernel(x_ref, g_ref, w_ref, o_ref, h_ref):
    @pl.when(pl.program_id(1) == 0)
    def _():
        def body(r, c):
            rows = pl.ds(pl.multiple_of(r * NORM_ROWS, NORM_ROWS), NORM_ROWS)
            h_ref[rows, :] = _rmsnorm_rows(x_ref[rows, :], g_ref[...]).astype(BF16)
            return c
        lax.fori_loop(0, IN_TM // NORM_ROWS, body, 0, unroll=4)

    o_ref[...] = jnp.dot(h_ref[...], w_ref[...], preferred_element_type=F32).astype(o_ref.dtype)


def _in_proj(x2d, g_row, w_bf16):
    m = x2d.shape[0]
    return pl.pallas_call(
        _in_proj_kernel,
        grid=(m // IN_TM, PROJ_COLS // IN_TN),
        in_specs=[
            pl.BlockSpec((IN_TM, D_MODEL), lambda i, j: (i, 0)),
            pl.BlockSpec((1, D_MODEL), lambda i, j: (0, 0)),
            pl.BlockSpec((D_MODEL, IN_TN), lambda i, j: (0, j)),
        ],
        out_specs=pl.BlockSpec((IN_TM, IN_TN), lambda i, j: (i, j)),
        out_shape=jax.ShapeDtypeStruct((m, PROJ_COLS), BF16),
        scratch_shapes=[pltpu.VMEM((IN_TM, D_MODEL), BF16)],
        compiler_params=_cparams(("parallel", "arbitrary")),
        name="in_proj",
    )(x2d, g_row, w_bf16)


DN_HB = 8
DN_BB = 2
DN_TS = 512
DN_W = DN_HB * HEAD_DIM
DN_DOUBLINGS = CHUNK.bit_length() - 2
GATE_ROWS = 2 * DN_HEADS


def _dot_nt(a, b):
    return lax.dot_general(a, b, (((1,), (1,)), ((), ())), preferred_element_type=F32)


def _dot_tn(a, b):
    return lax.dot_general(a, b, (((0,), (0,)), ((), ())), preferred_element_type=F32)


def _causal_conv(xx, w, width):
    if width == 4:
        x1 = pltpu.roll(xx, 1, axis=0)
        near = xx * w[3:4, :] + x1 * w[2:3, :]
        far = xx * w[1:2, :] + x1 * w[0:1, :]
        return (near + pltpu.roll(far, 2, axis=0))[HALO:, :]
    acc = None
    for j in range(width):
        shift = width - 1 - j
        xs = xx if shift == 0 else pltpu.roll(xx, shift, axis=0)
        term = xs[HALO:, :] * w[j:j + 1, :]
        acc = term if acc is None else acc + term
    return acc


def _deltanet_kernel(q_ref, k_ref, v_ref, z_ref, ab_ref, cwq_ref, cwk_ref, cwv_ref,
                     alog_ref, dtb_ref, ng_ref, o_ref,
                     xpad_ref, state_ref):
    st = pl.program_id(2)

    @pl.when(st == 0)
    def _():
        xpad_ref[:, 0:HALO, :] = jnp.zeros((3 * DN_BB, HALO, DN_W), BF16)
        state_ref[...] = jnp.zeros_like(state_ref)

    for bb in range(DN_BB):
        for t, ref in enumerate((q_ref, k_ref, v_ref)):
            xpad_ref[3 * bb + t, HALO:, :] = ref[bb]

    row_i = lax.broadcasted_iota(jnp.int32, (CHUNK, CHUNK), 0)
    col_j = lax.broadcasted_iota(jnp.int32, (CHUNK, CHUNK), 1)
    causal = row_i >= col_j
    strict = row_i > col_j
    upper = (row_i <= col_j).astype(F32)
    eye = (row_i == col_j).astype(F32)
    lane = lax.broadcasted_iota(jnp.int32, (CHUNK, LANES), 1)
    gate_row = lax.broadcasted_iota(jnp.int32, (GATE_ROWS, CHUNK), 0)
    neg_decay_rate = -jnp.exp(alog_ref[...])
    cws = (cwq_ref[...], cwk_ref[...], cwv_ref[...])
    ng = ng_ref[...]

    def chunk_body(c, carry):
        r0 = pl.multiple_of(c * CHUNK, CHUNK)
        win = pl.ds(r0, CHUNK + HALO)
        rows = pl.ds(r0, CHUNK)
        conv, gates, z = [], [], []
        for bb in range(DN_BB):
            conv.append([_silu(_causal_conv(xpad_ref[3 * bb + t, win, :].astype(F32), cws[t], DN_CONV))
                         for t in range(3)])
            ab_t = ab_ref[bb, rows, :].astype(F32).T[:GATE_ROWS, :]
            g_cum_t = jnp.dot(neg_decay_rate * _softplus(ab_t + dtb_ref[...]), upper,
                              preferred_element_type=F32, precision=lax.Precision.HIGHEST)
            gates_t = jnp.where(gate_row < DN_HEADS, g_cum_t, _sigmoid(ab_t))
            gates.append(jnp.concatenate([gates_t, jnp.zeros((CHUNK - GATE_ROWS, CHUNK), F32)], axis=0).T)
            z.append(z_ref[bb, rows, :].astype(F32))

        units = [(bb, i) for bb in range(DN_BB) for i in range(DN_HB)]
        sls = [slice(i * HEAD_DIM, (i + 1) * HEAD_DIM) for i in range(DN_HB)]
        dot = functools.partial(jnp.dot, preferred_element_type=F32)

        g_col = [jnp.sum(jnp.where(lane == i, gates[bb], 0.0), axis=-1, keepdims=True) for bb, i in units]
        beta = [jnp.sum(jnp.where(lane == i + DN_HEADS, gates[bb], 0.0), axis=-1, keepdims=True)
                for bb, i in units]
        g_last = [g[CHUNK - 1:CHUNK, :] for g in g_col]
        eg = [jnp.exp(g) for g in g_col]
        ek = [jnp.exp(gl - g) for gl, g in zip(g_last, g_col)]
        g_b = [jnp.broadcast_to(g, (CHUNK, CHUNK)) for g in g_col]
        decay = [jnp.exp(jnp.where(causal, gb - gb.T, -jnp.inf)) for gb in g_b]

        qf = [conv[bb][0][:, sls[i]] for bb, i in units]
        kf = [conv[bb][1][:, sls[i]] for bb, i in units]
        vf = [conv[bb][2][:, sls[i]] for bb, i in units]
        q = [x * lax.rsqrt(jnp.sum(x * x, axis=-1, keepdims=True) + L2_EPS) * (HEAD_DIM ** -0.5) for x in qf]
        k = [x * lax.rsqrt(jnp.sum(x * x, axis=-1, keepdims=True) + L2_EPS) for x in kf]
        kb = [ki * bi for ki, bi in zip(k, beta)]
        a2 = [_dot_nt(jnp.concatenate([qi, kbi], axis=0).astype(BF16), ki.astype(BF16))
              for qi, kbi, ki in zip(q, kb, k)]
        attn = [a[:CHUNK] * d for a, d in zip(a2, decay)]
        n_mat = [jnp.where(strict, -(a[CHUNK:] * d), 0.0) for a, d in zip(a2, decay)]

        u_mat = [eye + n for n in n_mat]
        p_mat = [dot(n.astype(BF16), n.astype(BF16)) for n in n_mat]
        for _ in range(DN_DOUBLINGS - 1):
            up = [dot(jnp.concatenate([u.astype(BF16), p.astype(BF16)], axis=0), p.astype(BF16))
                  for u, p in zip(u_mat, p_mat)]
            u_mat = [u + x[:CHUNK] for u, x in zip(u_mat, up)]
            p_mat = [x[CHUNK:] for x in up]
        u_mat = [u + dot(u.astype(BF16), p.astype(BF16)) for u, p in zip(u_mat, p_mat)]
        uw = [dot(u.astype(BF16), jnp.concatenate([vi * bi, kbi * egi], axis=1).astype(BF16))
              for u, vi, bi, kbi, egi in zip(u_mat, vf, beta, kb, eg)]

        s_old = [state_ref[u] for u in range(len(units))]
        qw = [dot(jnp.concatenate([qi * egi, x[:, HEAD_DIM:]], axis=0).astype(BF16), s.astype(BF16))
              for qi, egi, x, s in zip(q, eg, uw, s_old)]
        v16 = [(x[:, :HEAD_DIM] - y[CHUNK:]).astype(BF16) for x, y in zip(uw, qw)]
        o = [y[:CHUNK] + dot(a.astype(BF16), v) for y, a, v in zip(qw, attn, v16)]
        s_new = [s * jnp.exp(gl) + _dot_tn((ki * eki).astype(BF16), v)
                 for s, gl, ki, eki, v in zip(s_old, g_last, k, ek, v16)]
        for u, (bb, i) in enumerate(units):
            state_ref[u] = s_new[u]
            on = o[u] * lax.rsqrt(jnp.mean(o[u] * o[u], axis=-1, keepdims=True) + RMS_EPS) * ng
            o_ref[bb, rows, sls[i]] = (on * _silu(z[bb][:, sls[i]])).astype(o_ref.dtype)
        return carry

    lax.fori_loop(0, DN_TS // CHUNK, chunk_body, 0)

    tail = pl.ds(DN_TS, HALO)
    head = pl.ds(0, HALO)
    for t in range(3 * DN_BB):
        xpad_ref[t, head, :] = xpad_ref[t, tail, :]


def _gate_rows(v):
    rows = jnp.zeros((GATE_ROWS, LANES), F32)
    return rows.at[:v.shape[0], :].set(jnp.broadcast_to(v.astype(F32)[:, None], (v.shape[0], LANES)))


def _deltanet(proj3, conv_w, a_log, dt_bias, ng_row):
    assert DN_HB == DN_HEADS
    b, s, _ = proj3.shape
    assert b % DN_BB == 0
    gate = pl.BlockSpec((GATE_ROWS, LANES), lambda bi, hg, st: (0, 0))
    qb, kb_, vb, zb = (off // DN_W for off in (OFF_DNQ, OFF_DNK, OFF_DNV, OFF_DNZ))
    act = lambda base: pl.BlockSpec((DN_BB, DN_TS, DN_W), lambda bi, hg, st: (bi, st, base + hg))
    cw = lambda base: pl.BlockSpec((DN_CONV, DN_W), lambda bi, hg, st: (0, base + hg))
    row = pl.BlockSpec((1, LANES), lambda bi, hg, st: (0, 0))
    return pl.pallas_call(
        _deltanet_kernel,
        grid=(b // DN_BB, DN_HEADS // DN_HB, s // DN_TS),
        in_specs=[
            act(qb), act(kb_), act(vb), act(zb),
            pl.BlockSpec((DN_BB, DN_TS, LANES), lambda bi, hg, st: (bi, st, OFF_AB // LANES)),
            cw(0), cw(DN_WIDTH // DN_W), cw(2 * DN_WIDTH // DN_W),
            gate, gate, row,
        ],
        out_specs=pl.BlockSpec((DN_BB, DN_TS, DN_W), lambda bi, hg, st: (bi, st, hg)),
        out_shape=jax.ShapeDtypeStruct((b, s, DN_WIDTH), BF16),
        scratch_shapes=[
            pltpu.VMEM((3 * DN_BB, DN_TS + HALO, DN_W), BF16),
            pltpu.VMEM((DN_BB * DN_HB, HEAD_DIM, HEAD_DIM), F32),
        ],
        compiler_params=_cparams(("parallel", "parallel", "arbitrary")),
        name="deltanet",
    )(proj3, proj3, proj3, proj3, proj3, conv_w, conv_w, conv_w, _gate_rows(a_log), _gate_rows(dt_bias), ng_row)


SWA_CAST_ROWS = 256
SWA_UNROLL = 4


def _swa_kernel(q_ref, k_ref, v_ref, o_ref, *scratch, seq, dil, slopes):
    n_blocks = seq // dil // SWA_BLOCK
    qi = lax.broadcasted_iota(jnp.int32, (SWA_BLOCK, SWA_BLOCK), 0)
    kj = lax.broadcasted_iota(jnp.int32, (SWA_BLOCK, SWA_BLOCK), 1)
    step_cur = (qi - kj).astype(F32)
    step_prev = step_cur + float(SWA_BLOCK)
    ok_cur = qi >= kj
    ok_prev = kj >= qi
    lane = lax.broadcasted_iota(jnp.int32, (SWA_BLOCK, LANES), 1)
    scale = HEAD_DIM ** -0.5
    heads = range(SWA_HEADS_PER_GROUP)
    sls = [slice(hh * HEAD_DIM, (hh + 1) * HEAD_DIM) for hh in heads]
    m_h = [float(slopes[hh]) * float(dil) for hh in heads]

    if dil > 1:
        (f32_ref,) = scratch

        def cast_body(c, carry):
            rows = pl.ds(pl.multiple_of(c * SWA_CAST_ROWS, SWA_CAST_ROWS), SWA_CAST_ROWS)
            for t, ref in enumerate((q_ref, k_ref, v_ref)):
                for hh in heads:
                    f32_ref[t * SWA_HEADS_PER_GROUP + hh, rows, :] = ref[0, rows, sls[hh]].astype(F32)
            return carry
        lax.fori_loop(0, seq // SWA_CAST_ROWS, cast_body, 0)

        def load(t, rows):
            return [f32_ref[t * SWA_HEADS_PER_GROUP + hh, rows, :].astype(BF16) for hh in heads]
    else:
        refs = (q_ref, k_ref, v_ref)

        def load(t, rows):
            return [refs[t][0, rows, sls[hh]] for hh in heads]

    def block_rows(it):
        r = it // n_blocks
        n = it % n_blocks
        base = n * (SWA_BLOCK * dil) + r
        prev_base = jnp.maximum(n - 1, 0) * (SWA_BLOCK * dil) + r
        if dil > 1:
            return n, pl.ds(base, SWA_BLOCK, stride=dil), pl.ds(prev_base, SWA_BLOCK, stride=dil)
        return (n, pl.ds(pl.multiple_of(base, SWA_BLOCK), SWA_BLOCK),
                pl.ds(pl.multiple_of(prev_base, SWA_BLOCK), SWA_BLOCK))

    def blocks_body(it2, carry):
        qn, kc, kp, vc, vp, ok_p, slope, out_rows = [], [], [], [], [], [], [], []
        for u in range(SWA_UNROLL):
            n, rows, prev_rows = block_rows(it2 * SWA_UNROLL + u)
            qn += load(0, rows)
            kc += load(1, rows)
            vc += load(2, rows)
            kp += load(1, prev_rows)
            vp += load(2, prev_rows)
            ok_p += [jnp.logical_and(ok_prev, n > 0)] * len(heads)
            slope += m_h
            out_rows.append(rows)

        s_cur = [jnp.where(ok_cur, _dot_nt(qh, kh) * scale - m * step_cur, -jnp.inf)
                 for qh, kh, m in zip(qn, kc, slope)]
        s_prev = [jnp.where(ok, _dot_nt(qh, kh) * scale - m * step_prev, -jnp.inf)
                  for ok, qh, kh, m in zip(ok_p, qn, kp, slope)]
        mx = [jnp.max(jnp.maximum(a, b), axis=-1, keepdims=True) for a, b in zip(s_cur, s_prev)]
        e_cur = [jnp.exp(a - m) for a, m in zip(s_cur, mx)]
        e_prev = [jnp.exp(b - m) for b, m in zip(s_prev, mx)]
        den = [jnp.sum(a + b, axis=-1, keepdims=True) for a, b in zip(e_cur, e_prev)]
        pv = [jnp.dot(a.astype(BF16), vch, preferred_element_type=F32)
              + jnp.dot(b.astype(BF16), vph, preferred_element_type=F32)
              for a, b, vch, vph in zip(e_cur, e_prev, vc, vp)]
        for u, rows in enumerate(out_rows):
            lse_tile = jnp.zeros((SWA_BLOCK, LANES), F32)
            for hh in heads:
                i = u * len(heads) + hh
                o_ref[0, hh, rows, :] = pv[i] * (1.0 / den[i])
                lse_tile = jnp.where(lane == hh, mx[i] + jnp.log(den[i]), lse_tile)
            o_ref[0, SWA_HEADS_PER_GROUP, rows, :] = lse_tile
        return carry

    lax.fori_loop(0, seq // SWA_BLOCK // SWA_UNROLL, blocks_body, 0)


def _swa_group(proj3, gi):
    batch, seq, _ = proj3.shape
    window, dil = SWA_GROUPS[gi]
    assert window // dil == SWA_BLOCK and seq % (dil * SWA_BLOCK) == 0
    slopes = 2.0 ** (-ALIBI_MAX_BIAS * np.arange(1, SWA_HEADS + 1) / SWA_HEADS)
    slopes = slopes.reshape(len(SWA_GROUPS), SWA_HEADS_PER_GROUP)[gi]

    def spec(which):
        cb = (OFF_SWA + which * SWA_HEADS * HEAD_DIM) // SWA_GROUP_WIDTH + gi
        return pl.BlockSpec((1, seq, SWA_GROUP_WIDTH), lambda bi: (bi, 0, cb))

    scratch = [pltpu.VMEM((3 * SWA_HEADS_PER_GROUP, seq, HEAD_DIM), F32)] if dil > 1 else []
    return pl.pallas_call(
        functools.partial(_swa_kernel, seq=seq, dil=dil, slopes=tuple(float(s) for s in slopes)),
        grid=(batch,),
        in_specs=[spec(0), spec(1), spec(2)],
        out_specs=pl.BlockSpec((1, SWA_OUT_SLOTS, seq, HEAD_DIM), lambda bi: (bi, 0, 0, 0)),
        out_shape=jax.ShapeDtypeStruct((batch, SWA_OUT_SLOTS, seq, HEAD_DIM), F32),
        scratch_shapes=scratch,
        compiler_params=_cparams(("parallel",)),
        name=f"swa_g{gi}",
    )(proj3, proj3, proj3)


MG_TM = 256


def _merge_kernel(x_ref, ydn_ref, gate_ref, scc_ref, scb_ref, scx_ref, scc_h_ref, scx_h_ref,
                  s0_ref, s1_ref, s2_ref, scw_ref, wdn_ref, wsc_ref, wswa_ref, wout_ref, o_ref,
                  *, tiles_per_seq):
    i = pl.program_id(0)
    first = (i % tiles_per_seq) == 0

    cx_cur = scc_ref[...].astype(F32) * scx_ref[...].astype(F32)
    cx_halo = scc_h_ref[...].astype(F32) * scx_h_ref[...].astype(F32)
    cx_halo = jnp.where(first, 0.0, cx_halo)
    cx = jnp.concatenate([cx_halo, cx_cur], axis=0)
    y_sc = scb_ref[...].astype(F32) * _causal_conv(cx, scw_ref[...], SC_CONV)

    outs = (s0_ref, s1_ref, s2_ref)
    lses = [o[0, SWA_HEADS_PER_GROUP] for o in outs]
    mx = jnp.maximum(jnp.maximum(lses[0], lses[1]), lses[2])
    es = [jnp.exp(l - mx) for l in lses]
    inv = 1.0 / (es[0] + es[1] + es[2])
    heads = []
    for hh in range(SWA_HEADS_PER_GROUP):
        acc = None
        for gi in range(3):
            wt = (es[gi] * inv)[:, hh:hh + 1]
            term = wt * outs[gi][0, hh]
            acc = term if acc is None else acc + term
        heads.append(acc)
    y_swa = jnp.concatenate(heads, axis=1)

    gates = gate_ref[...].astype(F32)
    merged = (_sigmoid(gates[:, 0:D_MODEL])
              * jnp.dot(ydn_ref[...], wdn_ref[...], preferred_element_type=F32)
              + _sigmoid(gates[:, D_MODEL:2 * D_MODEL])
              * jnp.dot(y_sc.astype(BF16), wsc_ref[...], preferred_element_type=F32)
              + _sigmoid(gates[:, 2 * D_MODEL:3 * D_MODEL])
              * jnp.dot(y_swa.astype(BF16), wswa_ref[...], preferred_element_type=F32))
    o_ref[...] = x_ref[...] + jnp.dot(merged.astype(BF16), wout_ref[...], preferred_element_type=F32)


def _merge(x2d, ydn2, proj2, swa_outs, sc_conv_w, wdn, wsc, wswa, wout, seq):
    m = x2d.shape[0]
    tiles_per_seq = seq // MG_TM
    halo_blocks = MG_TM // HALO
    rows = lambda w, cb: pl.BlockSpec((MG_TM, w), lambda i: (i, cb))
    halo = lambda cb: pl.BlockSpec((HALO, SC_WIDTH), lambda i: (jnp.maximum(i * halo_blocks - 1, 0), cb))
    full = lambda a: pl.BlockSpec(a.shape, lambda i: (0, 0), pipeline_mode=pl.Buffered(1))
    swa = pl.BlockSpec((1, SWA_OUT_SLOTS, MG_TM, HEAD_DIM),
                       lambda i: (i // tiles_per_seq, 0, i % tiles_per_seq, 0))
    return pl.pallas_call(
        functools.partial(_merge_kernel, tiles_per_seq=tiles_per_seq),
        grid=(m // MG_TM,),
        in_specs=[
            rows(D_MODEL, 0),
            rows(DN_WIDTH, 0),
            rows(3 * D_MODEL, OFF_GATE // (3 * D_MODEL)),
            rows(SC_WIDTH, OFF_SCC // SC_WIDTH), rows(SC_WIDTH, OFF_SCB // SC_WIDTH),
            rows(SC_WIDTH, OFF_SCX // SC_WIDTH),
            halo(OFF_SCC // SC_WIDTH), halo(OFF_SCX // SC_WIDTH),
            swa, swa, swa,
            full(sc_conv_w), full(wdn), full(wsc), full(wswa), full(wout),
        ],
        out_specs=rows(D_MODEL, 0),
        out_shape=jax.ShapeDtypeStruct((m, D_MODEL), F32),
        compiler_params=_cparams(("parallel",)),
        name="merge",
    )(x2d, ydn2, proj2, proj2, proj2, proj2, proj2, proj2, *swa_outs, sc_conv_w, wdn, wsc, wswa, wout)


RT_TM = 1024
RT_PARTS = 8
ROUTE_LANE0 = MOE_GROUPS
R_E1, R_E2, R_W1, R_W2, R_RANK1, R_RANK2 = range(6)
HALF = D_MODEL // 2


def _pack_pair(lo, hi):
    lo_b = lax.bitcast_convert_type(lo.astype(BF16).astype(F32), jnp.uint32)
    hi_b = lax.bitcast_convert_type(hi.astype(BF16).astype(F32), jnp.uint32)
    return (hi_b & jnp.uint32(0xFFFF0000)) | (lo_b >> jnp.uint32(16))


def _unpack_pair(p):
    lo = lax.bitcast_convert_type(p << jnp.uint32(16), F32)
    hi = lax.bitcast_convert_type(p & jnp.uint32(0xFFFF0000), F32)
    return lo, hi


def _router_kernel(x_ref, g_ref, wrh_ref, wrl_ref, br_ref, hp_ref, route_ref, cnt_ref, carry_ref):
    @pl.when(pl.program_id(0) == 0)
    def _():
        carry_ref[...] = jnp.zeros_like(carry_ref)

    tp = RT_TM // RT_PARTS
    part_rows = [slice(p * tp, (p + 1) * tp) for p in range(RT_PARTS)]
    w_hi, w_lo, bias = wrh_ref[...], wrl_ref[...], br_ref[...]
    dot = functools.partial(jnp.dot, preferred_element_type=F32)
    lane = lax.broadcasted_iota(jnp.int32, (tp, LANES), 1)
    big = jnp.int32(LANES)
    neg = -jnp.inf

    hs = [_rmsnorm_rows(x_ref[rows, :], g_ref[...]) for rows in part_rows]
    for rows, h in zip(part_rows, hs):
        hp_ref[rows, :] = _pack_pair(h[:, :HALF], h[:, HALF:])
    h_hi = [h.astype(BF16) for h in hs]
    h_lo = [(h - hi.astype(F32)).astype(BF16) for h, hi in zip(hs, h_hi)]
    logits = [dot(hi, w_hi) + dot(lo, w_hi) + dot(hi, w_lo) + bias for hi, lo in zip(h_hi, h_lo)]

    def route(lg):
        gl = jnp.where(lane < MOE_GROUPS, lg, neg)
        gmax = jnp.max(gl, axis=-1, keepdims=True)
        grp = jnp.min(jnp.where(gl == gmax, lane, big), axis=-1, keepdims=True)
        g_w = 1.0 / jnp.sum(jnp.exp(gl - gmax), axis=-1, keepdims=True)
        lo = ROUTE_LANE0 + grp * MOE_EPG
        el = jnp.where(jnp.logical_and(lane >= lo, lane < lo + MOE_EPG), lg, neg)
        m1 = jnp.max(el, axis=-1, keepdims=True)
        i1 = jnp.min(jnp.where(el == m1, lane, big), axis=-1, keepdims=True)
        el2 = jnp.where(lane == i1, neg, el)
        m2 = jnp.max(el2, axis=-1, keepdims=True)
        i2 = jnp.min(jnp.where(el2 == m2, lane, big), axis=-1, keepdims=True)
        e2 = jnp.exp(m2 - m1)
        return i1, i2, g_w / (1.0 + e2), g_w * e2 / (1.0 + e2)

    routed = [route(lg) for lg in logits]

    onehot = [jnp.where(jnp.logical_or(lane == i1, lane == i2), 1.0, 0.0) for i1, i2, _, _ in routed]
    earlier = (lax.broadcasted_iota(jnp.int32, (tp, tp), 0) > lax.broadcasted_iota(jnp.int32, (tp, tp), 1))
    earlier = jnp.where(earlier, 1.0, 0.0).astype(BF16)
    within = [dot(earlier, oh.astype(BF16)) for oh in onehot]
    seen = carry_ref[...]
    for rows, (i1, i2, w1, w2), oh, inside in zip(part_rows, routed, onehot, within):
        before = seen + inside
        rank1 = jnp.sum(jnp.where(lane == i1, before, 0.0), axis=-1, keepdims=True)
        rank2 = jnp.sum(jnp.where(lane == i2, before, 0.0), axis=-1, keepdims=True)
        seen = seen + jnp.sum(oh, axis=0, keepdims=True)
        rec = jnp.zeros((tp, LANES), F32)
        for ln, val in ((R_E1, (i1 - ROUTE_LANE0).astype(F32)), (R_E2, (i2 - ROUTE_LANE0).astype(F32)),
                        (R_W1, w1), (R_W2, w2), (R_RANK1, rank1), (R_RANK2, rank2)):
            rec = jnp.where(lane == ln, val, rec)
        route_ref[rows, :] = rec
    carry_ref[...] = seen
    cnt_ref[...] = seen


def _router(x2d, g_row, w_route_hi, w_route_lo, b_route):
    m = x2d.shape[0]
    return pl.pallas_call(
        _router_kernel,
        grid=(m // RT_TM,),
        in_specs=[
            pl.BlockSpec((RT_TM, D_MODEL), lambda i: (i, 0)),
            pl.BlockSpec((1, D_MODEL), lambda i: (0, 0)),
            pl.BlockSpec((D_MODEL, LANES), lambda i: (0, 0)),
            pl.BlockSpec((D_MODEL, LANES), lambda i: (0, 0)),
            pl.BlockSpec((1, LANES), lambda i: (0, 0)),
        ],
        out_specs=[pl.BlockSpec((RT_TM, HALF), lambda i: (i, 0)),
                   pl.BlockSpec((RT_TM, LANES), lambda i: (i, 0)),
                   pl.BlockSpec((1, LANES), lambda i: (0, 0))],
        out_shape=[jax.ShapeDtypeStruct((m, HALF), jnp.uint32), jax.ShapeDtypeStruct((m, LANES), F32),
                   jax.ShapeDtypeStruct((1, LANES), F32)],
        scratch_shapes=[pltpu.VMEM((1, LANES), F32)],
        compiler_params=_cparams(("arbitrary",)),
        name="router",
    )(x2d, g_row, w_route_hi, w_route_lo, b_route)


EXP_TM = 512


def _n_row_tiles(n_tokens):
    return 2 * n_tokens // EXP_TM + MOE_EXPERTS


def _positions_kernel(route_ref, offs_ref, pos_ref):
    rec = route_ref[...]
    lane = lax.broadcasted_iota(jnp.int32, rec.shape, 1)
    offs = offs_ref[...]

    def first_row(e):
        return jnp.sum(jnp.where(lane == e.astype(jnp.int32), offs, 0.0), axis=-1, keepdims=True)

    pos1 = first_row(rec[:, R_E1:R_E1 + 1]) + rec[:, R_RANK1:R_RANK1 + 1]
    pos2 = first_row(rec[:, R_E2:R_E2 + 1]) + rec[:, R_RANK2:R_RANK2 + 1]
    pos_ref[...] = jnp.where(lane == 0, pos1, jnp.where(lane == 1, pos2, 0.0)).astype(jnp.int32)


def _moe_plan(route, cnt):
    m = route.shape[0]
    n_tiles = _n_row_tiles(m)
    counts = cnt[0, ROUTE_LANE0:ROUTE_LANE0 + MOE_EXPERTS].astype(jnp.int32)
    padded = (counts + EXP_TM - 1) // EXP_TM * EXP_TM
    ends = jnp.cumsum(padded)
    offs = ends - padded
    pos = pl.pallas_call(
        _positions_kernel,
        grid=(m // RT_TM,),
        in_specs=[pl.BlockSpec((RT_TM, LANES), lambda i: (i, 0)), pl.BlockSpec((1, LANES), lambda i: (0, 0))],
        out_specs=pl.BlockSpec((RT_TM, LANES), lambda i: (i, 0)),
        out_shape=jax.ShapeDtypeStruct((m, LANES), jnp.int32),
        compiler_params=_cparams(("parallel",)),
        name="moe_positions",
    )(route, _lane_row(offs))
    n_active = ends[-1] // EXP_TM
    tile_row = jnp.maximum(jnp.minimum(jnp.arange(n_tiles), n_active - 1), 0) * EXP_TM
    tile_expert = jnp.minimum(jnp.sum(ends[None, :] <= tile_row[:, None], axis=1), MOE_EXPERTS - 1)
    used = counts > 0
    ids = jnp.where(used, jnp.arange(MOE_EXPERTS), MOE_EXPERTS)
    next_used = jnp.concatenate([lax.cummin(ids, reverse=True)[1:], jnp.full((1,), MOE_EXPERTS, ids.dtype)])
    parity = (jnp.cumsum(used.astype(jnp.int32)) - 1) % 2
    i32 = lambda a: a.astype(jnp.int32)
    return dict(pos1=pos[:, 0], pos2=pos[:, 1], counts=counts, offs=i32(offs),
                tile_expert=i32(tile_expert), n_active=i32(n_active).reshape(1),
                next_expert=i32(next_used[tile_expert]), parity=i32(parity[tile_expert]))


DSP_TB = 512
ROW_UNROLL = 16


def _dispatch_kernel(cnt_ref, offs_ref, nact_ref, pos1_ref, pos2_ref, hp_ref, xs_ref, zero_ref, sem, zero_sem):
    n_tiles = xs_ref.shape[0] // EXP_TM
    step = pl.program_id(0)
    slot = step % 2

    @pl.when(pl.program_id(0) == 0)
    def _():
        zero_ref[...] = jnp.zeros_like(zero_ref)

        def pad_pieces(e):
            pad = (-cnt_ref[e]) & (EXP_TM - 1)
            first = offs_ref[e] + cnt_ref[e]
            head = jnp.minimum((-first) & (SUBLANES - 1), pad)
            for k in range(SUBLANES - 1):
                yield k < head, pltpu.make_async_copy(zero_ref.at[pl.ds(0, 1), :],
                                                      xs_ref.at[pl.ds(first + k, 1), :], zero_sem)
            body = pad - head
            b = EXP_TM // 2
            while b >= SUBLANES:
                row = pl.multiple_of(first + head + (body & ~(2 * b - 1)), SUBLANES)
                yield (body & b) != 0, pltpu.make_async_copy(zero_ref.at[pl.ds(0, b), :],
                                                             xs_ref.at[pl.ds(row, b), :], zero_sem)
                b //= 2

        def tile_copy(t):
            rows = pl.ds(pl.multiple_of(t * EXP_TM, EXP_TM), EXP_TM)
            return pltpu.make_async_copy(zero_ref, xs_ref.at[rows, :], zero_sem)

        def for_all(method):
            def per_expert(e, c):
                for cond, cp in pad_pieces(e):
                    pl.when(cond)(getattr(cp, method))
                return c
            lax.fori_loop(0, MOE_EXPERTS, per_expert, 0)

            def per_tile(t, c):
                getattr(tile_copy(t), method)()
                return c
            lax.fori_loop(nact_ref[0], n_tiles, per_tile, 0)

        for_all("start")
        for_all("wait")

    def start(j8, c):
        for u in range(ROW_UNROLL):
            j = j8 * ROW_UNROLL + u
            src = hp_ref.at[pl.ds(step * DSP_TB + j, 1), :]
            pltpu.make_async_copy(src, xs_ref.at[pl.ds(pos1_ref[0, 0, j], 1), :], sem.at[slot]).start(priority=0)
            pltpu.make_async_copy(src, xs_ref.at[pl.ds(pos2_ref[0, 0, j], 1), :], sem.at[slot]).start(priority=1)
        return c

    lax.fori_loop(0, DSP_TB // ROW_UNROLL, start, 0)

    def wait_step(s):
        for _ in range(2):
            pltpu.make_async_copy(hp_ref.at[pl.ds(0, DSP_TB), :], xs_ref.at[pl.ds(0, DSP_TB), :], sem.at[s]).wait()

    @pl.when(step > 0)
    def _():
        wait_step(1 - slot)

    @pl.when(step + 1 == pl.num_programs(0))
    def _():
        wait_step(slot)


def _dispatch(hp, pos1, pos2, counts, offs, n_active):
    m = hp.shape[0]
    n_rows = _n_row_tiles(m) * EXP_TM
    pos_spec = pl.BlockSpec((1, 1, DSP_TB), lambda i, *_: (i, 0, 0), memory_space=pltpu.SMEM)
    return pl.pallas_call(
        _dispatch_kernel,
        grid_spec=pltpu.PrefetchScalarGridSpec(
            num_scalar_prefetch=3,
            grid=(m // DSP_TB,),
            in_specs=[pos_spec, pos_spec, pl.BlockSpec(memory_space=pl.ANY)],
            out_specs=pl.BlockSpec(memory_space=pl.ANY),
            scratch_shapes=[pltpu.VMEM((EXP_TM, HALF), jnp.uint32), pltpu.SemaphoreType.DMA((2,)),
                            pltpu.SemaphoreType.DMA(())],
        ),
        out_shape=jax.ShapeDtypeStruct((n_rows, HALF), jnp.uint32),
        compiler_params=_cparams(("arbitrary",)),
        name="moe_dispatch",
    )(counts, offs, n_active, pos1.reshape(m // DSP_TB, 1, DSP_TB), pos2.reshape(m // DSP_TB, 1, DSP_TB), hp)


W_CAST_ROWS = 256


def _cast_rows(src_ref, dst_ref):
    n_rows = dst_ref.shape[0]
    slab = min(W_CAST_ROWS, n_rows)

    def body(r, c):
        rows = pl.ds(pl.multiple_of(r * slab, slab), slab)
        dst_ref[rows, :] = src_ref[rows, :].astype(BF16)
        return c
    lax.fori_loop(0, n_rows // slab, body, 0)


def _expert_kernel(te_ref, nact_ref, next_ref, par_ref, xs_ref, wg_hbm, wu_hbm, wd_hbm, ys_ref,
                   sg_ref, su_ref, sd_ref, wg16_ref, wu16_ref, wd16_ref, sem, *, layer):
    i = pl.program_id(0)
    active = i < nact_ref[0]
    expert = te_ref[i]
    slot = par_ref[i]
    new_expert = jnp.logical_or(i == 0, expert != te_ref[jnp.maximum(i - 1, 0)])

    def copies(e, s):
        return [pltpu.make_async_copy(hbm.at[layer, e], stage.at[s], sem.at[s])
                for hbm, stage in ((wg_hbm, sg_ref), (wu_hbm, su_ref), (wd_hbm, sd_ref))]

    @pl.when(jnp.logical_and(active, i == 0))
    def _():
        for cp in copies(expert, slot):
            cp.start()

    @pl.when(jnp.logical_and(active, new_expert))
    def _():
        for cp in copies(expert, slot):
            cp.wait()

        @pl.when(next_ref[i] < MOE_EXPERTS)
        def _():
            for cp in copies(next_ref[i], 1 - slot):
                cp.start()

        _cast_rows(sg_ref.at[slot], wg16_ref)
        _cast_rows(su_ref.at[slot], wu16_ref)
        _cast_rows(sd_ref.at[slot], wd16_ref)

    @pl.when(active)
    def _():
        lo, hi = _unpack_pair(xs_ref[...])
        lo, hi = lo.astype(BF16), hi.astype(BF16)
        dot = functools.partial(jnp.dot, preferred_element_type=F32)
        gate = dot(lo, wg16_ref[:HALF, :]) + dot(hi, wg16_ref[HALF:, :])
        up = dot(lo, wu16_ref[:HALF, :]) + dot(hi, wu16_ref[HALF:, :])
        y = dot((_silu(gate) * up).astype(BF16), wd16_ref[...])
        ys_ref[...] = _pack_pair(y[:, :HALF], y[:, HALF:])

    @pl.when(jnp.logical_not(active))
    def _():
        ys_ref[...] = jnp.zeros_like(ys_ref)


def _experts(xs, tile_expert, n_active, next_expert, parity, wg, wu, wd, layer):
    n_tiles = xs.shape[0] // EXP_TM
    rows = pl.BlockSpec((EXP_TM, HALF),
                        lambda i, te, na, nx, par: (jnp.maximum(jnp.minimum(i, na[0] - 1), 0), 0))
    out_rows = pl.BlockSpec((EXP_TM, HALF), lambda i, te, na, nx, par: (i, 0))
    hbm = pl.BlockSpec(memory_space=pl.ANY)
    return pl.pallas_call(
        functools.partial(_expert_kernel, layer=layer),
        grid_spec=pltpu.PrefetchScalarGridSpec(
            num_scalar_prefetch=4,
            grid=(n_tiles,),
            in_specs=[rows, hbm, hbm, hbm],
            out_specs=out_rows,
            scratch_shapes=[pltpu.VMEM((2, D_MODEL, MOE_FF), F32), pltpu.VMEM((2, D_MODEL, MOE_FF), F32),
                            pltpu.VMEM((2, MOE_FF, D_MODEL), F32),
                            pltpu.VMEM((D_MODEL, MOE_FF), BF16), pltpu.VMEM((D_MODEL, MOE_FF), BF16),
                            pltpu.VMEM((MOE_FF, D_MODEL), BF16),
                            pltpu.SemaphoreType.DMA((2,))],
        ),
        out_shape=jax.ShapeDtypeStruct(xs.shape, jnp.uint32),
        compiler_params=_cparams(("arbitrary",)),
        name="moe_experts",
    )(tile_expert, n_active, next_expert, parity, xs, wg, wu, wd)


CMB_TC = 512


def _combine_kernel(pos1_ref, pos2_ref, pos1_next_ref, pos2_next_ref, x_ref, route_ref, g_ref, ys_ref,
                    o_ref, buf_ref, sem, *, final_norm):
    i = pl.program_id(0)
    slot = i % 2

    def gather(p1_ref, p2_ref, s):
        def start(j8, c):
            for u in range(ROW_UNROLL):
                j = j8 * ROW_UNROLL + u
                for k, p_ref in enumerate((p1_ref, p2_ref)):
                    pltpu.make_async_copy(ys_ref.at[pl.ds(p_ref[0, 0, j], 1), :],
                                          buf_ref.at[s, k, pl.ds(j, 1), :], sem.at[s]).start(priority=k)
            return c
        lax.fori_loop(0, CMB_TC // ROW_UNROLL, start, 0)

    @pl.when(i == 0)
    def _():
        gather(pos1_ref, pos2_ref, 0)

    @pl.when(i + 1 < pl.num_programs(0))
    def _():
        gather(pos1_next_ref, pos2_next_ref, 1 - slot)

    for k in range(2):
        pltpu.make_async_copy(ys_ref.at[pl.ds(0, CMB_TC), :], buf_ref.at[slot, k], sem.at[slot]).wait()

    rec = route_ref[...]
    w1, w2 = rec[:, R_W1:R_W1 + 1], rec[:, R_W2:R_W2 + 1]
    lo1, hi1 = _unpack_pair(buf_ref[slot, 0])
    lo2, hi2 = _unpack_pair(buf_ref[slot, 1])
    out_lo = x_ref[:, :HALF] + w1 * lo1 + w2 * lo2
    out_hi = x_ref[:, HALF:] + w1 * hi1 + w2 * hi2
    if final_norm:
        ms = (jnp.sum(out_lo * out_lo, axis=-1, keepdims=True)
              + jnp.sum(out_hi * out_hi, axis=-1, keepdims=True)) * (1.0 / D_MODEL)
        inv = lax.rsqrt(ms + RMS_EPS)
        out_lo = out_lo * inv * g_ref[:, :HALF]
        out_hi = out_hi * inv * g_ref[:, HALF:]
    o_ref[:, :HALF] = out_lo
    o_ref[:, HALF:] = out_hi


def _combine(x2d, route, ys, pos1, pos2, g_row, final_norm):
    m = x2d.shape[0]
    n_steps = m // CMB_TC
    pos_spec = pl.BlockSpec((1, 1, CMB_TC), lambda i: (i, 0, 0), memory_space=pltpu.SMEM)
    next_spec = pl.BlockSpec((1, 1, CMB_TC), lambda i: (jnp.minimum(i + 1, n_steps - 1), 0, 0),
                             memory_space=pltpu.SMEM)
    pos1, pos2 = pos1.reshape(n_steps, 1, CMB_TC), pos2.reshape(n_steps, 1, CMB_TC)
    return pl.pallas_call(
        functools.partial(_combine_kernel, final_norm=final_norm),
        grid=(n_steps,),
        in_specs=[pos_spec, pos_spec, next_spec, next_spec,
                  pl.BlockSpec((CMB_TC, D_MODEL), lambda i: (i, 0)),
                  pl.BlockSpec((CMB_TC, LANES), lambda i: (i, 0)),
                  pl.BlockSpec((1, D_MODEL), lambda i: (0, 0)),
                  pl.BlockSpec(memory_space=pl.ANY)],
        out_specs=pl.BlockSpec((CMB_TC, D_MODEL), lambda i: (i, 0)),
        out_shape=jax.ShapeDtypeStruct((m, D_MODEL), F32),
        scratch_shapes=[pltpu.VMEM((2, 2, CMB_TC, HALF), jnp.uint32), pltpu.SemaphoreType.DMA((2,))],
        compiler_params=_cparams(("arbitrary",)),
        name="moe_combine",
    )(pos1, pos2, pos1, pos2, x2d, route, g_row, ys)


RL_TN = 512
RL_PIECE = 256
SRC_A_END = 4096
SRC_AB = 16
SRC_COLS = 17936


def _relayout_plan():
    n_rest = (SRC_COLS - SRC_A_END - SRC_AB) // RL_TN
    rest = lambda s: SRC_A_END + SRC_AB + RL_TN * s
    starts = ([rest(s) for s in range(15, n_rest)]
              + [RL_TN * a for a in range(SRC_A_END // RL_TN)]
              + [rest(s) for s in range(15)]
              + [SRC_A_END])
    assert len(starts) * RL_TN == PROJ_COLS and all(s % 16 == 0 and s + RL_TN <= SRC_COLS for s in starts)
    return np.asarray(starts, np.int32)


def _relayout_kernel(start_ref, wt_ref, o_ref, buf_ref, sem, *, layer):
    t = pl.program_id(0)
    n_tiles = pl.num_programs(0)
    slot = t % 2

    def fetch(tile, s):
        first = pl.multiple_of(start_ref[tile], 16)
        pltpu.make_async_copy(wt_ref.at[layer, pl.ds(first, RL_TN), :], buf_ref.at[s], sem.at[s]).start()

    @pl.when(t == 0)
    def _():
        fetch(0, 0)

    @pl.when(t + 1 < n_tiles)
    def _():
        fetch(t + 1, 1 - slot)

    pltpu.make_async_copy(wt_ref.at[layer, pl.ds(0, RL_TN), :], buf_ref.at[slot], sem.at[slot]).wait()

    row = lax.broadcasted_iota(jnp.int32, (RL_TN, RL_PIECE), 0)
    keep = jnp.logical_or(t + 1 < n_tiles, row < SRC_AB)
    for c in range(D_MODEL // RL_PIECE):
        cols = slice(c * RL_PIECE, (c + 1) * RL_PIECE)
        piece = jnp.where(keep, buf_ref[slot, :, cols], 0.0)
        o_ref[cols, :] = piece.T.astype(BF16)


def _prep_w_in(w_all, layer):
    wt = jnp.swapaxes(w_all, 1, 2)
    return pl.pallas_call(
        functools.partial(_relayout_kernel, layer=layer),
        grid_spec=pltpu.PrefetchScalarGridSpec(
            num_scalar_prefetch=1,
            grid=(PROJ_COLS // RL_TN,),
            in_specs=[pl.BlockSpec(memory_space=pl.ANY)],
            out_specs=pl.BlockSpec((D_MODEL, RL_TN), lambda t, starts: (0, t)),
            scratch_shapes=[pltpu.VMEM((2, RL_TN, D_MODEL), F32), pltpu.SemaphoreType.DMA((2,))],
        ),
        out_shape=jax.ShapeDtypeStruct((D_MODEL, PROJ_COLS), BF16),
        compiler_params=_cparams(("arbitrary",)),
        name="w_in_relayout",
    )(jnp.asarray(_relayout_plan()), wt)


def _lane_row(v):
    return jnp.zeros((1, LANES), F32).at[0, :v.shape[0]].set(v.astype(F32))


def _layer(x2d, batch, seq, p, final_g_row):
    proj2 = _in_proj(x2d, p["norm_mix_g"], p["w_in"])
    proj3 = proj2.reshape(batch, seq, PROJ_COLS)
    ydn = _deltanet(proj3, p["dn_conv_w"], p["dn_a_log"], p["dn_dt_bias"], p["dn_norm_g"])
    swa_outs = [_swa_group(proj3, gi) for gi in range(len(SWA_GROUPS))]
    x2d = _merge(x2d, ydn.reshape(batch * seq, DN_WIDTH), proj2, swa_outs, p["sc_conv_w"],
                 p["w_branch_dn"], p["w_branch_sc"], p["w_branch_swa"], p["w_out"], seq)
    hp, route, cnt = _router(x2d, p["norm_ffn_g"], p["w_route_hi"], p["w_route_lo"], p["b_route"])
    plan = _moe_plan(route, cnt)
    xs = _dispatch(hp, plan["pos1"], plan["pos2"], plan["counts"], plan["offs"], plan["n_active"])
    ys = _experts(xs, plan["tile_expert"], plan["n_active"], plan["next_expert"], plan["parity"],
                  p["expert_w_gate"], p["expert_w_up"], p["expert_w_down"], p["layer"])
    is_last = final_g_row is not None
    g_row = final_g_row if is_last else p["norm_ffn_g"]
    return _combine(x2d, route, ys, plan["pos1"], plan["pos2"], g_row, is_last)


def kernel(x, norm_mix_g, w_in, dn_conv_w, dn_a_log, dn_dt_bias, dn_norm_g, sc_conv_w, w_branch_dn, w_branch_sc, w_branch_swa, w_out, norm_ffn_g, router_group_w, router_group_b, router_expert_w, router_expert_b, expert_w_gate, expert_w_up, expert_w_down, final_norm_g):
    batch, seq, _ = x.shape
    depth = w_in.shape[0]
    x2d = x.reshape(batch * seq, D_MODEL)
    for l in range(depth):
        w_route = jnp.concatenate([router_group_w[l], router_expert_w[l]], axis=1)
        w_route = jnp.pad(w_route, ((0, 0), (0, LANES - w_route.shape[1])))
        p = dict(
            norm_mix_g=norm_mix_g[l].reshape(1, D_MODEL),
            w_in=_prep_w_in(w_in, l),
            dn_conv_w=dn_conv_w[l],
            dn_a_log=dn_a_log[l],
            dn_dt_bias=dn_dt_bias[l],
            dn_norm_g=dn_norm_g[l].reshape(1, HEAD_DIM),
            sc_conv_w=sc_conv_w[l],
            w_branch_dn=w_branch_dn[l].astype(BF16),
            w_branch_sc=w_branch_sc[l].astype(BF16),
            w_branch_swa=w_branch_swa[l].astype(BF16),
            w_out=w_out[l].astype(BF16),
            norm_ffn_g=norm_ffn_g[l].reshape(1, D_MODEL),
            w_route_hi=w_route.astype(BF16),
            w_route_lo=(w_route - w_route.astype(BF16).astype(F32)).astype(BF16),
            b_route=_lane_row(jnp.concatenate([router_group_b[l], router_expert_b[l]])),
            expert_w_gate=expert_w_gate,
            expert_w_up=expert_w_up,
            expert_w_down=expert_w_down,
            layer=l,
        )
        final_g_row = final_norm_g.reshape(1, D_MODEL) if l == depth - 1 else None
        x2d = _layer(x2d, batch, seq, p, final_g_row)
    return x2d.reshape(batch, seq, D_MODEL)
```
